```python
import numpy as np
import jax
import jax.numpy as jnp
from jax import lax

D_MODEL = 1024
BATCH = 8
SEQ = 2048
DEPTH = 1

CHUNK = 64
POOL_WIDTH = D_MODEL
POOL_WINDOWS = (2, 4, 8, 16)
POOL_GROUPS = len(POOL_WINDOWS)
POOL_GROUP_WIDTH = POOL_WIDTH // POOL_GROUPS
GLA_HEADS = 4
GLA_KEY_DIM = D_MODEL // 2
GLA_VALUE_DIM = D_MODEL
GLA_HEAD_K = GLA_KEY_DIM // GLA_HEADS
GLA_HEAD_V = GLA_VALUE_DIM // GLA_HEADS
GLA_GATE_RANK = 16
GLA_TAU = 16.0
N_EXPERTS = 32
TOP_K = 4
D_EXPERT = D_MODEL
SWIGLU_ALPHA = 1.702
SWIGLU_LIMIT = 7.0
EXPERT_BLOCK = 256
LN_EPS = 1e-5
RMS_EPS = 1e-6
DEEPNORM_ALPHA = (2.0 * DEPTH) ** 0.25
DEEPNORM_BETA = (8.0 * DEPTH) ** -0.25
IN_WIDTHS = (POOL_WIDTH, GLA_KEY_DIM, GLA_KEY_DIM, GLA_VALUE_DIM, GLA_VALUE_DIM, GLA_GATE_RANK, D_MODEL, D_MODEL)
IN_TOTAL = sum(IN_WIDTHS)
IN_OFFSETS = tuple(int(o) for o in np.cumsum(IN_WIDTHS)[:-1])
V_START = POOL_WIDTH + 2 * GLA_KEY_DIM
V_END = V_START + GLA_VALUE_DIM

kernel_name = 'hybrid_pool_gla_moe_deepnorm_block'


def layer_norm(x, gain, bias):
    x32 = x.astype(jnp.float32)
    mu = jnp.mean(x32, axis=-1, keepdims=True)
    var = jnp.mean(jnp.square(x32 - mu), axis=-1, keepdims=True)
    return ((x32 - mu) * lax.rsqrt(var + LN_EPS) * gain + bias).astype(x.dtype)


def multiscale_pool(a):
    bsz, seq, _ = a.shape
    a32 = a.astype(jnp.float32)
    csum = jnp.concatenate([jnp.zeros((bsz, 1, POOL_WIDTH), jnp.float32), jnp.cumsum(a32, axis=1)], axis=1)
    t = jnp.arange(seq)
    groups = []
    for g, w in enumerate(POOL_WINDOWS):
        lo_c, hi_c = g * POOL_GROUP_WIDTH, (g + 1) * POOL_GROUP_WIDTH
        cs = csum[:, :, lo_c:hi_c]
        start = jnp.maximum(t + 1 - w, 0)
        win_sum = cs[:, 1:, :] - cs[:, start, :]
        count = jnp.minimum(t + 1, w).astype(jnp.float32)
        groups.append(win_sum / count[None, :, None] - a32[:, :, lo_c:hi_c])
    return jnp.stack(groups, axis=2)


def gla_chunked(q, k, v, log_a):
    bsz, seq, heads, dk = q.shape
    dv = v.shape[-1]
    n_chunks = seq // CHUNK

    def to_chunks(t):
        return t.astype(jnp.float32).reshape(bsz, n_chunks, CHUNK, heads, -1).transpose(1, 0, 3, 2, 4)

    qc = to_chunks(q) * (dk ** -0.5)
    kc, vc, gc = to_chunks(k), to_chunks(v), to_chunks(log_a)
    causal = jnp.tril(jnp.ones((CHUNK, CHUNK), dtype=bool))[:, :, None]

    def step(state, inp):
        qi, ki, vi, gi = inp
        b = jnp.cumsum(gi, axis=-2)
        diff = b[:, :, :, None, :] - b[:, :, None, :, :]
        decay = jnp.where(causal, jnp.exp(jnp.where(causal, diff, 0.0)), 0.0)
        scores = jnp.einsum('bhid,bhjd,bhijd->bhij', qi, ki, decay)
        o = jnp.einsum('bhij,bhje->bhie', scores, vi) + jnp.einsum('bhid,bhde->bhie', qi * jnp.exp(b), state)
        b_last = b[:, :, -1:, :]
        state = jnp.exp(b_last[:, :, 0, :])[..., None] * state + jnp.einsum('bhjd,bhje->bhde', ki * jnp.exp(b_last - b), vi)
        return state, o

    state0 = jnp.zeros((bsz, heads, dk, dv), jnp.float32)
    _, out = lax.scan(step, state0, (qc, kc, vc, gc))
    return out.transpose(1, 0, 3, 2, 4).reshape(bsz, seq, heads, dv)


def token_mixer(u, w_in, w_pool_group, pool_scale, w_branch_a, w_alpha_up, b_alpha, gla_norm_gain, w_branch_b, w_out):
    bsz, seq, _ = u.shape
    proj = u @ w_in
    a, q, k, v, r, alpha_low, gate_a, gate_b = jnp.split(proj, IN_OFFSETS, axis=-1)
    pooled = multiscale_pool(a).astype(u.dtype)
    ya = jnp.einsum('bsgc,gcd->bsgd', pooled, w_pool_group).reshape(bsz, seq, POOL_WIDTH) * pool_scale
    ya = ya @ w_branch_a
    log_a = jax.nn.log_sigmoid((alpha_low @ w_alpha_up + b_alpha).astype(jnp.float32)) / GLA_TAU
    o = gla_chunked(q.reshape(bsz, seq, GLA_HEADS, GLA_HEAD_K), k.reshape(bsz, seq, GLA_HEADS, GLA_HEAD_K),
                    v.reshape(bsz, seq, GLA_HEADS, GLA_HEAD_V), log_a.reshape(bsz, seq, GLA_HEADS, GLA_HEAD_K))
    o = o * lax.rsqrt(jnp.mean(jnp.square(o), axis=-1, keepdims=True) + RMS_EPS) * gla_norm_gain
    yb = (o.reshape(bsz, seq, GLA_VALUE_DIM).astype(u.dtype) * jax.nn.silu(r)) @ w_branch_b
    merged = jax.nn.sigmoid(gate_a) * ya + jax.nn.sigmoid(gate_b) * yb
    return merged @ w_out


def routed_moe(u, w_router, b_router, w_gate_up, b_gate_up, w_down, b_down):
    bsz, seq, d = u.shape
    n_tok = bsz * seq
    xt = u.reshape(n_tok, d)
    logits = (xt @ w_router + b_router).astype(jnp.float32)
    top_val, top_idx = lax.top_k(logits, TOP_K)
    weights = jax.nn.softmax(top_val, axis=-1)
    n_assign = n_tok * TOP_K
    flat_e = top_idx.reshape(-1).astype(jnp.int32)
    order = jnp.argsort(flat_e)
    sorted_e = flat_e[order]
    counts = jnp.bincount(flat_e, length=N_EXPERTS)
    padded = (counts + EXPERT_BLOCK - 1) // EXPERT_BLOCK * EXPERT_BLOCK
    pad_end = jnp.cumsum(padded)
    pad_start = pad_end - padded
    grp_start = jnp.cumsum(counts) - counts
    dest = pad_start[sorted_e] + (jnp.arange(n_assign) - grp_start[sorted_e])
    n_rows = (n_assign + N_EXPERTS * (EXPERT_BLOCK - 1) + EXPERT_BLOCK - 1) // EXPERT_BLOCK * EXPERT_BLOCK
    n_blocks = n_rows // EXPERT_BLOCK
    row_token = jnp.full((n_rows,), n_tok, jnp.int32).at[dest].set((order // TOP_K).astype(jnp.int32))
    block_expert = jnp.minimum(jnp.searchsorted(pad_end, jnp.arange(n_blocks) * EXPERT_BLOCK, side='right'), N_EXPERTS - 1)
    x_pad = jnp.concatenate([xt, jnp.zeros((1, d), xt.dtype)], axis=0)
    x_rows = x_pad[row_token].reshape(n_blocks, EXPERT_BLOCK, d)

    def expert_block(args):
        xb, e = args
        h = xb @ w_gate_up[e] + b_gate_up[e]
        gate = jnp.minimum(h[:, ::2], SWIGLU_LIMIT)
        up = jnp.clip(h[:, 1::2], -SWIGLU_LIMIT, SWIGLU_LIMIT)
        glu = gate * jax.nn.sigmoid(gate * SWIGLU_ALPHA)
        return ((up + 1.0) * glu) @ w_down[e] + b_down[e]

    y_rows = lax.map(expert_block, (x_rows, block_expert)).reshape(n_rows, d)
    row_of_assign = jnp.zeros((n_assign,), jnp.int32).at[order].set(dest.astype(jnp.int32))
    y_sel = y_rows[row_of_assign].reshape(n_tok, TOP_K, d)
    y = jnp.einsum('tk,tkd->td', weights.astype(y_sel.dtype), y_sel)
    return y.reshape(bsz, seq, d)


def setup_inputs(seed: int = 0) -> dict:
    key = jax.random.key(seed)
    ks = jax.random.split(key, 24)

    def nrm(k, shape, scale):
        return jax.random.normal(k, shape, jnp.float32) * scale

    v_col_scale = jnp.ones((IN_TOTAL,), jnp.float32).at[V_START:V_END].set(DEEPNORM_BETA)
    return {
        'x': nrm(ks[0], (BATCH, SEQ, D_MODEL), 1.0),
        'c': nrm(ks[1], (BATCH, D_MODEL), 1.0),
        'w_ada': nrm(ks[2], (DEPTH, D_MODEL, 6 * D_MODEL), 0.5 * D_MODEL ** -0.5),
        'b_ada': nrm(ks[3], (DEPTH, 6 * D_MODEL), 0.02),
        'w_in': nrm(ks[4], (DEPTH, D_MODEL, IN_TOTAL), D_MODEL ** -0.5) * v_col_scale,
        'w_pool_group': nrm(ks[5], (DEPTH, POOL_GROUPS, POOL_GROUP_WIDTH, POOL_GROUP_WIDTH), POOL_GROUP_WIDTH ** -0.5),
        'pool_scale': 1.0 + nrm(ks[6], (DEPTH, POOL_WIDTH), 0.1),
        'w_branch_a': nrm(ks[7], (DEPTH, POOL_WIDTH, D_MODEL), POOL_WIDTH ** -0.5 * DEEPNORM_BETA),
        'w_alpha_up': nrm(ks[8], (DEPTH, GLA_GATE_RANK, GLA_KEY_DIM), GLA_GATE_RANK ** -0.5),
        'b_alpha': nrm(ks[9], (DEPTH, GLA_KEY_DIM), 0.1),
        'gla_norm_gain': 1.0 + nrm(ks[10], (DEPTH, GLA_HEAD_V), 0.1),
        'w_branch_b': nrm(ks[11], (DEPTH, GLA_VALUE_DIM, D_MODEL), GLA_VALUE_DIM ** -0.5 * DEEPNORM_BETA),
        'w_out': nrm(ks[12], (DEPTH, D_MODEL, D_MODEL), D_MODEL ** -0.5 * DEEPNORM_BETA),
        'ln1_gain': 1.0 + nrm(ks[13], (DEPTH, D_MODEL), 0.05),
        'ln1_bias': nrm(ks[14], (DEPTH, D_MODEL), 0.02),
        'w_router': nrm(ks[15], (DEPTH, D_MODEL, N_EXPERTS), D_MODEL ** -0.5),
        'b_router': nrm(ks[16], (DEPTH, N_EXPERTS), 0.01),
        'w_gate_up': nrm(ks[17], (DEPTH, N_EXPERTS, D_MODEL, 2 * D_EXPERT), D_MODEL ** -0.5 * DEEPNORM_BETA),
        'b_gate_up': nrm(ks[18], (DEPTH, N_EXPERTS, 2 * D_EXPERT), 0.02),
        'w_down': nrm(ks[19], (DEPTH, N_EXPERTS, D_EXPERT, D_MODEL), D_EXPERT ** -0.5 * DEEPNORM_BETA),
        'b_down': nrm(ks[20], (DEPTH, N_EXPERTS, D_MODEL), 0.02),
        'ln2_gain': 1.0 + nrm(ks[21], (DEPTH, D_MODEL), 0.05),
        'ln2_bias': nrm(ks[22], (DEPTH, D_MODEL), 0.02),
    }


def reference(x, c, w_ada, b_ada, w_in, w_pool_group, pool_scale, w_branch_a, w_alpha_up, b_alpha,
              gla_norm_gain, w_branch_b, w_out, ln1_gain, ln1_bias, w_router, b_router,
              w_gate_up, b_gate_up, w_down, b_down, ln2_gain, ln2_bias):
    for l in range(DEPTH):
        mod = jax.nn.silu(c) @ w_ada[l] + b_ada[l]
        sh_m, sc_m, g_m, sh_f, sc_f, g_f = jnp.split(mod[:, None, :], 6, axis=-1)
        u = x * (1.0 + sc_m) + sh_m
        y = token_mixer(u, w_in[l], w_pool_group[l], pool_scale[l], w_branch_a[l], w_alpha_up[l], b_alpha[l],
                        gla_norm_gain[l], w_branch_b[l], w_out[l])
        x = layer_norm(DEEPNORM_ALPHA * x + g_m * y, ln1_gain[l], ln1_bias[l])
        u = x * (1.0 + sc_f) + sh_f
        y = routed_moe(u, w_router[l], b_router[l], w_gate_up[l], b_gate_up[l], w_down[l], b_down[l])
        x = layer_norm(DEEPNORM_ALPHA * x + g_f * y, ln2_gain[l], ln2_bias[l])
    return x
```

```python
import functools

import jax
import jax.numpy as jnp
from jax import lax
from jax.experimental import pallas as pl
from jax.experimental.pallas import tpu as pltpu

D_MODEL = 1024
CHUNK = 64
SUB = 16
N_SUB = CHUNK // SUB
POOL_WINDOWS = (2, 4, 8, 16)
POOL_GROUP_WIDTH = D_MODEL // len(POOL_WINDOWS)
POOL_HALO = 16
GLA_HEADS = 4
GLA_KEY_DIM = D_MODEL // 2
GLA_HEAD_K = GLA_KEY_DIM // GLA_HEADS
GLA_HEAD_V = D_MODEL // GLA_HEADS
GLA_GATE_RANK = 16
GLA_TAU = 16.0
N_EXPERTS = 32
TOP_K = 4
SWIGLU_ALPHA = 1.702
SWIGLU_LIMIT = 7.0
EXPERT_BLOCK = 256
LN_EPS = 1e-5
RMS_EPS = 1e-6
DEPTH = 1
DEEPNORM_ALPHA = (2.0 * DEPTH) ** 0.25

LANES = 128
RANK_PAD = LANES
_W = (D_MODEL, GLA_KEY_DIM, GLA_KEY_DIM, D_MODEL, D_MODEL, RANK_PAD, D_MODEL, D_MODEL)
_OFF = tuple(sum(_W[:i]) for i in range(len(_W) + 1))
IN_PADDED = _OFF[-1]
EXP_CAP = 60.0
NEG_BIG = -1e30

SEQ_TILE = 256
VMEM_LIMIT = 56 * 1024 * 1024

F32 = jnp.float32
BF16 = jnp.bfloat16
HI = lax.Precision.HIGHEST


def _dot(a, b):
    return jnp.dot(a, b, preferred_element_type=F32)


def _dot_nt(a, b):
    return lax.dot_general(a, b, (((1,), (1,)), ((), ())), preferred_element_type=F32)


def _dot_tn(a, b):
    return lax.dot_general(a, b, (((0,), (0,)), ((), ())), preferred_element_type=F32)


def _ada_kernel(c_ref, w_ref, b_ref, o_ref):
    c = c_ref[...]
    s = c * jax.nn.sigmoid(c)
    o_ref[...] = jnp.dot(s, w_ref[...], precision=HI, preferred_element_type=F32) + b_ref[...]


def _ada(c, w_ada, b_ada):
    bsz, d = c.shape
    n = w_ada.shape[1]
    tn = 1024
    return pl.pallas_call(
        _ada_kernel,
        grid=(n // tn,),
        in_specs=[
            pl.BlockSpec((bsz, d), lambda j: (0, 0)),
            pl.BlockSpec((d, tn), lambda j: (0, j)),
            pl.BlockSpec((1, tn), lambda j: (0, j)),
        ],
        out_specs=pl.BlockSpec((bsz, tn), lambda j: (0, j)),
        out_shape=jax.ShapeDtypeStruct((bsz, n), F32),
        name="ada",
    )(c, w_ada, b_ada.reshape(1, n))


def _layer_norm(z, gain, bias):
    mu = jnp.mean(z, axis=-1, keepdims=True)
    zc = z - mu
    var = jnp.mean(zc * zc, axis=-1, keepdims=True)
    return zc * lax.rsqrt(var + LN_EPS) * gain + bias


def _mixer_kernel(x_ref, mod_ref, w_in_ref, w_pool_ref, pool_scale_ref, w_a_ref, w_al_ref, b_al_ref,
                  gain_ref, w_b_ref, w_out_ref, ln_g_ref, ln_b_ref, w_r_ref, b_r_ref,
                  x1_ref, u2_ref, route_ref, counts_ref,
                  a_ext, s_ref, cnt_ref, o_ref):
    b_idx = pl.program_id(0)
    s_idx = pl.program_id(1)
    tile = x_ref.shape[1]

    @pl.when(s_idx == 0)
    def _():
        s_ref[...] = jnp.zeros_like(s_ref)
        a_ext[0:POOL_HALO, :] = jnp.zeros((POOL_HALO, D_MODEL), F32)

    @pl.when((b_idx == 0) & (s_idx == 0))
    def _():
        cnt_ref[...] = jnp.zeros_like(cnt_ref)

    mod = mod_ref[0]
    sh_m, sc_m, g_m = mod[0:1], mod[1:2], mod[2:3]
    sh_f, sc_f, g_f = mod[3:4], mod[4:5], mod[5:6]
    x = x_ref[0]
    u = (x * (1.0 + sc_m) + sh_m).astype(BF16)

    def proj(i):
        return _dot(u, w_in_ref[:, _OFF[i]:_OFF[i + 1]])

    a = proj(0)
    a_ext[POOL_HALO:POOL_HALO + tile, :] = a
    t_glob = s_idx * tile + lax.broadcasted_iota(jnp.int32, (tile, 1), 0)
    mapped = []
    for g, w in enumerate(POOL_WINDOWS):
        lo, hi = g * POOL_GROUP_WIDTH, (g + 1) * POOL_GROUP_WIDTH
        win = a[:, lo:hi]
        for k in range(1, w):
            win = win + a_ext[pl.ds(POOL_HALO - k, tile), lo:hi]
        inv_cnt = 1.0 / jnp.minimum(t_glob + 1, w).astype(F32)
        pooled = win * inv_cnt - a[:, lo:hi]
        mapped.append(_dot(pooled.astype(BF16), w_pool_ref[g]))
    a_ext[0:POOL_HALO, :] = a[tile - POOL_HALO:tile, :]
    ya = jnp.concatenate(mapped, axis=1) * pool_scale_ref[...]
    ya = _dot(ya.astype(BF16), w_a_ref[...])

    q = proj(1) * (GLA_HEAD_K ** -0.5)
    k_all = proj(2)
    v_all = proj(3)
    alpha_low = proj(5)
    z = _dot(alpha_low.astype(BF16), w_al_ref[...]) + b_al_ref[...]
    log_a = (jnp.minimum(z, 0.0) - jnp.log1p(jnp.exp(-jnp.abs(z)))) * (1.0 / GLA_TAU)

    r2 = lax.broadcasted_iota(jnp.int32, (2 * CHUNK, CHUNK), 0)
    c2 = lax.broadcasted_iota(jnp.int32, (2 * CHUNK, CHUNK), 1)
    cum_bound = jnp.where(r2 < CHUNK, r2 + 1, jnp.bitwise_and(r2 - CHUNK, -SUB))
    cum_mat = (c2 < cum_bound).astype(F32)
    ri = lax.broadcasted_iota(jnp.int32, (CHUNK, CHUNK), 0)
    ci = lax.broadcasted_iota(jnp.int32, (CHUNK, CHUNK), 1)
    causal = ci <= ri

    for c in range(tile // CHUNK):
        rows = slice(c * CHUNK, (c + 1) * CHUNK)
        cum = jnp.dot(cum_mat, log_a[rows], precision=HI, preferred_element_type=F32)
        b_cum, b_ref_pt = cum[0:CHUNK], cum[CHUNK:2 * CHUNK]
        for h in range(GLA_HEADS):
            ks = slice(h * GLA_HEAD_K, (h + 1) * GLA_HEAD_K)
            vs = slice(h * GLA_HEAD_V, (h + 1) * GLA_HEAD_V)
            qh, kh, vh = q[rows, ks], k_all[rows, ks], v_all[rows, vs].astype(BF16)
            bh, rh = b_cum[:, ks], b_ref_pt[:, ks]
            b_last = bh[CHUNK - 1:CHUNK, :]
            q_dec = (qh * jnp.exp(bh - rh)).astype(BF16)
            k_dec = jnp.concatenate(
                [(kh * jnp.exp(jnp.minimum(rh[i * SUB:i * SUB + 1, :] - bh, EXP_CAP))).astype(BF16)
                 for i in range(N_SUB)], axis=0)
            s_all = _dot_nt(q_dec, k_dec)
            scores = jnp.concatenate(
                [s_all[i * SUB:(i + 1) * SUB, i * CHUNK:(i + 1) * CHUNK] for i in range(N_SUB)], axis=0)
            scores = jnp.where(causal, scores, 0.0).astype(BF16)
            state_t = s_ref[h]
            o = _dot(scores, vh) + _dot_nt((qh * jnp.exp(bh)).astype(BF16), state_t.astype(BF16))
            k_carry = (kh * jnp.exp(b_last - bh)).astype(BF16)
            s_ref[h] = state_t * jnp.exp(b_last) + _dot_tn(vh, k_carry)
            o = o * lax.rsqrt(jnp.mean(o * o, axis=-1, keepdims=True) + RMS_EPS) * gain_ref[...]
            o_ref[rows, vs] = o

    r = proj(4)
    yb = _dot((o_ref[...] * (r * jax.nn.sigmoid(r))).astype(BF16), w_b_ref[...])

    merged = jax.nn.sigmoid(proj(6)) * ya + jax.nn.sigmoid(proj(7)) * yb
    y = _dot(merged.astype(BF16), w_out_ref[...])
    x1 = _layer_norm(DEEPNORM_ALPHA * x + g_m * y, ln_g_ref[...], ln_b_ref[...])
    x1_ref[0] = x1
    u2 = x1 * (1.0 + sc_f) + sh_f
    u2_ref[0] = u2

    logits = jnp.dot(u2, w_r_ref[...], precision=HI, preferred_element_type=F32) + b_r_ref[...]
    lane = lax.broadcasted_iota(jnp.int32, (tile, LANES), 1).astype(F32)
    work = logits
    sel = jnp.zeros((tile, LANES), F32)
    vals, idxs = [], []
    for _ in range(TOP_K):
        m = jnp.max(work, axis=1, keepdims=True)
        idx = jnp.min(jnp.where(work == m, lane, float(LANES)), axis=1, keepdims=True)
        hit = lane == idx
        vals.append(m)
        idxs.append(idx)
        sel = jnp.where(hit, 1.0, sel)
        work = jnp.where(hit, -jnp.inf, work)
    exps = [jnp.exp(v - vals[0]) for v in vals]
    inv_den = 1.0 / (exps[0] + exps[1] + exps[2] + exps[3])

    rt = lax.broadcasted_iota(jnp.int32, (tile, tile), 0)
    ct = lax.broadcasted_iota(jnp.int32, (tile, tile), 1)
    before = (ct < rt).astype(BF16)
    base = _dot(before, sel.astype(BF16)) + cnt_ref[0:1, :]
    route = jnp.zeros((tile, LANES), F32)
    for kk in range(TOP_K):
        rank = jnp.sum(jnp.where(lane == idxs[kk], base, 0.0), axis=1, keepdims=True)
        route = jnp.where(lane == kk, idxs[kk], route)
        route = jnp.where(lane == TOP_K + kk, rank, route)
        route = jnp.where(lane == 2 * TOP_K + kk, exps[kk] * inv_den, route)
    route_ref[0] = route
    cnt_ref[...] = cnt_ref[...] + jnp.sum(sel, axis=0, keepdims=True)
    counts_ref[...] = cnt_ref[...]


def _mixer(x, mod, w_in_p, w_pool, pool_scale, w_a, w_al_p, b_alpha, gain, w_b, w_out, ln_g, ln_b,
           w_r_p, b_r_p):
    bsz, seq, d = x.shape
    tile = SEQ_TILE
    n_s = seq // tile

    def const(shape):
        nd = len(shape)
        return pl.BlockSpec(shape, lambda b, s: (0,) * nd, pipeline_mode=pl.Buffered(1))

    tok = lambda width: pl.BlockSpec((1, tile, width), lambda b, s: (b, s, 0))
    return pl.pallas_call(
        _mixer_kernel,
        grid=(bsz, n_s),
        in_specs=[
            tok(d),
            pl.BlockSpec((1, 6, d), lambda b, s: (b, 0, 0)),
            const(w_in_p.shape), const(w_pool.shape), const(pool_scale.shape), const(w_a.shape),
            const(w_al_p.shape), const(b_alpha.shape), const(gain.shape), const(w_b.shape),
            const(w_out.shape), const(ln_g.shape), const(ln_b.shape), const(w_r_p.shape),
            const(b_r_p.shape),
        ],
        out_specs=[tok(d), tok(d), tok(LANES), pl.BlockSpec((8, LANES), lambda b, s: (0, 0))],
        out_shape=[
            jax.ShapeDtypeStruct((bsz, seq, d), F32),
            jax.ShapeDtypeStruct((bsz, seq, d), F32),
            jax.ShapeDtypeStruct((bsz, seq, LANES), F32),
            jax.ShapeDtypeStruct((8, LANES), F32),
        ],
        scratch_shapes=[
            pltpu.VMEM((POOL_HALO + tile, d), F32),
            pltpu.VMEM((GLA_HEADS, GLA_HEAD_V, GLA_HEAD_K), F32),
            pltpu.VMEM((8, LANES), F32),
            pltpu.VMEM((tile, d), F32),
        ],
        compiler_params=pltpu.CompilerParams(
            dimension_semantics=("arbitrary", "arbitrary"), vmem_limit_bytes=VMEM_LIMIT),
        name="mixer",
    )(x, mod, w_in_p, w_pool, pool_scale, w_a, w_al_p, b_alpha, gain, w_b, w_out, ln_g, ln_b,
      w_r_p, b_r_p)


def _moe_kernel(be_ref, nb_ref, tok_cur_ref, tok_nxt_ref, dst_ref,
                u2_hbm, wg_ref, wu_ref, bg_ref, bu_ref, wd_ref, bd_ref,
                ysel_hbm,
                xbuf, ybuf, sem_in, sem_out):
    i = pl.program_id(0)
    nb = nb_ref[0]
    slot = lax.rem(i, 2)
    other = 1 - slot

    def gather(tok_ref, s):
        for r in range(EXPERT_BLOCK):
            pltpu.make_async_copy(u2_hbm.at[pl.ds(tok_ref[0, 0, r], 1)], xbuf.at[s, pl.ds(r, 1)],
                                  sem_in.at[s]).start()

    def wait_gather(s):
        pltpu.make_async_copy(u2_hbm.at[pl.ds(0, EXPERT_BLOCK)], xbuf.at[s], sem_in.at[s]).wait()

    def wait_scatter(s):
        pltpu.make_async_copy(ybuf.at[s], ysel_hbm.at[pl.ds(0, EXPERT_BLOCK)], sem_out.at[s]).wait()

    @pl.when(i == 0)
    def _():
        gather(tok_cur_ref, 0)
        n_real = ysel_hbm.shape[0] - 2 * EXPERT_BLOCK
        ybuf[1] = jnp.zeros((EXPERT_BLOCK, D_MODEL), F32)
        for j in range(2):
            pltpu.make_async_copy(ybuf.at[1], ysel_hbm.at[pl.ds(n_real + j * EXPERT_BLOCK, EXPERT_BLOCK)],
                                  sem_out.at[1]).start()
        for j in range(2):
            wait_scatter(1)

    @pl.when(i + 1 < nb)
    def _():
        gather(tok_nxt_ref, other)

    @pl.when(i < nb)
    def _():
        wait_gather(slot)

        @pl.when(i >= 2)
        def _():
            wait_scatter(slot)

        xb = xbuf[slot].astype(BF16)
        gate = jnp.minimum(_dot(xb, wg_ref[0]) + bg_ref[0], SWIGLU_LIMIT)
        up = jnp.clip(_dot(xb, wu_ref[0]) + bu_ref[0], -SWIGLU_LIMIT, SWIGLU_LIMIT)
        glu = gate * jax.nn.sigmoid(gate * SWIGLU_ALPHA)
        ybuf[slot] = _dot(((up + 1.0) * glu).astype(BF16), wd_ref[0]) + bd_ref[0]
        for r in range(EXPERT_BLOCK):
            pltpu.make_async_copy(ybuf.at[slot, pl.ds(r, 1)], ysel_hbm.at[pl.ds(dst_ref[0, 0, r], 1)],
                                  sem_out.at[slot]).start()

    @pl.when(i == nb - 1)
    def _():
        wait_scatter(slot)

        @pl.when(i >= 1)
        def _():
            wait_scatter(other)


def _moe(block_expert, n_active, row_token, row_dst, u2, wg, wu, bg, bu, wd, bd, n_slots):
    n_blocks = block_expert.shape[0]
    d = u2.shape[1]
    f = wg.shape[2]
    tok3 = row_token.reshape(n_blocks, 1, EXPERT_BLOCK)
    dst3 = row_dst.reshape(n_blocks, 1, EXPERT_BLOCK)
    smem_blk = lambda fn: pl.BlockSpec((1, 1, EXPERT_BLOCK), fn, memory_space=pltpu.SMEM)
    grid_spec = pltpu.PrefetchScalarGridSpec(
        num_scalar_prefetch=2,
        grid=(n_blocks,),
        in_specs=[
            smem_blk(lambda i, be, nb: (i, 0, 0)),
            smem_blk(lambda i, be, nb: (jnp.minimum(i + 1, n_blocks - 1), 0, 0)),
            smem_blk(lambda i, be, nb: (i, 0, 0)),
            pl.BlockSpec(memory_space=pl.ANY),
            pl.BlockSpec((1, d, f), lambda i, be, nb: (be[i], 0, 0)),
            pl.BlockSpec((1, d, f), lambda i, be, nb: (be[i], 0, 0)),
            pl.BlockSpec((1, 1, f), lambda i, be, nb: (be[i], 0, 0)),
            pl.BlockSpec((1, 1, f), lambda i, be, nb: (be[i], 0, 0)),
            pl.BlockSpec((1, f, d), lambda i, be, nb: (be[i], 0, 0)),
            pl.BlockSpec((1, 1, d), lambda i, be, nb: (be[i], 0, 0)),
        ],
        out_specs=pl.BlockSpec(memory_space=pl.ANY),
        scratch_shapes=[
            pltpu.VMEM((2, EXPERT_BLOCK, d), F32),
            pltpu.VMEM((2, EXPERT_BLOCK, d), F32),
            pltpu.SemaphoreType.DMA((2,)),
            pltpu.SemaphoreType.DMA((2,)),
        ],
    )
    return pl.pallas_call(
        _moe_kernel,
        grid_spec=grid_spec,
        out_shape=jax.ShapeDtypeStruct((n_slots, d), F32),
        compiler_params=pltpu.CompilerParams(
            dimension_semantics=("arbitrary",), vmem_limit_bytes=VMEM_LIMIT),
        name="moe",
    )(block_expert, n_active, tok3, tok3, dst3, u2, wg, wu, bg, bu, wd, bd)


def _combine_kernel(x1_ref, mod_ref, route_ref, y0_ref, y1_ref, y2_ref, y3_ref, ln_g_ref, ln_b_ref, o_ref):
    g_f = mod_ref[0][5:6]
    route = route_ref[...]
    y = jnp.zeros(x1_ref.shape, F32)
    for kk, y_ref in enumerate((y0_ref, y1_ref, y2_ref, y3_ref)):
        y = y + route[:, 2 * TOP_K + kk:2 * TOP_K + kk + 1] * y_ref[...]
    o_ref[...] = _layer_norm(DEEPNORM_ALPHA * x1_ref[...] + g_f * y, ln_g_ref[...], ln_b_ref[...])


def _combine(x1, mod, route, ysel, ln_g, ln_b, seq):
    n_tok, d = x1.shape
    tile = 512
    n_t = n_tok // tile
    per_seq = seq // tile
    row = lambda width: pl.BlockSpec((tile, width), lambda i: (i, 0))
    ysel_k = lambda kk: pl.BlockSpec((tile, d), lambda i: (kk * n_t + i, 0))
    vec = pl.BlockSpec((1, d), lambda i: (0, 0))
    return pl.pallas_call(
        _combine_kernel,
        grid=(n_t,),
        in_specs=[row(d), pl.BlockSpec((1, 6, d), lambda i: (i // per_seq, 0, 0)), row(LANES),
                  ysel_k(0), ysel_k(1), ysel_k(2), ysel_k(3), vec, vec],
        out_specs=row(d),
        out_shape=jax.ShapeDtypeStruct((n_tok, d), F32),
        compiler_params=pltpu.CompilerParams(dimension_semantics=("arbitrary",)),
        name="combine",
    )(x1, mod, route, ysel, ysel, ysel, ysel, ln_g, ln_b)


def _pad_in_proj(w_in):
    off = [0, D_MODEL, D_MODEL + GLA_KEY_DIM, D_MODEL + 2 * GLA_KEY_DIM, 2 * D_MODEL + 2 * GLA_KEY_DIM,
           3 * D_MODEL + 2 * GLA_KEY_DIM]
    rank_end = off[-1] + GLA_GATE_RANK
    pad = jnp.zeros((w_in.shape[0], RANK_PAD - GLA_GATE_RANK), w_in.dtype)
    return jnp.concatenate([w_in[:, :rank_end], pad, w_in[:, rank_end:]], axis=1)


def _layer(x, c, w_ada, b_ada, w_in, w_pool_group, pool_scale, w_branch_a, w_alpha_up, b_alpha,
           gla_norm_gain, w_branch_b, w_out, ln1_gain, ln1_bias, w_router, b_router,
           w_gate_up, b_gate_up, w_down, b_down, ln2_gain, ln2_bias):
    bsz, seq, d = x.shape
    n_tok = bsz * seq
    n_assign = n_tok * TOP_K
    row2 = lambda v: v.reshape(1, -1)

    mod = _ada(c, w_ada, b_ada).reshape(bsz, 6, d)

    w_in_p = _pad_in_proj(w_in).astype(BF16)
    w_al_p = jnp.concatenate(
        [w_alpha_up, jnp.zeros((RANK_PAD - GLA_GATE_RANK, GLA_KEY_DIM), w_alpha_up.dtype)], axis=0).astype(BF16)
    w_r_p = jnp.concatenate([w_router, jnp.zeros((d, LANES - N_EXPERTS), w_router.dtype)], axis=1)
    b_r_p = jnp.concatenate([b_router, jnp.full((LANES - N_EXPERTS,), NEG_BIG, b_router.dtype)]).reshape(1, LANES)
    x1, u2, route, counts = _mixer(
        x, mod, w_in_p, w_pool_group.astype(BF16), row2(pool_scale), w_branch_a.astype(BF16), w_al_p,
        row2(b_alpha), row2(gla_norm_gain), w_branch_b.astype(BF16), w_out.astype(BF16), row2(ln1_gain),
        row2(ln1_bias), w_r_p, b_r_p)

    route = route.reshape(n_tok, LANES)
    top_idx = route[:, 0:TOP_K].astype(jnp.int32)
    rank = route[:, TOP_K:2 * TOP_K].astype(jnp.int32)
    counts = counts[0, :N_EXPERTS].astype(jnp.int32)
    padded = (counts + EXPERT_BLOCK - 1) // EXPERT_BLOCK * EXPERT_BLOCK
    pad_end = jnp.cumsum(padded)
    pad_start = pad_end - padded
    n_rows = (n_assign + N_EXPERTS * (EXPERT_BLOCK - 1) + EXPERT_BLOCK - 1) // EXPERT_BLOCK * EXPERT_BLOCK
    n_blocks = n_rows // EXPERT_BLOCK
    n_active = (pad_end[-1] // EXPERT_BLOCK).astype(jnp.int32)
    dest = pad_start[top_idx] + rank
    slot = jnp.arange(TOP_K, dtype=jnp.int32)[None, :] * n_tok + jnp.arange(n_tok, dtype=jnp.int32)[:, None]
    row_slot = jnp.full((n_rows,), -1, jnp.int32).at[dest.reshape(-1)].set(slot.reshape(-1))
    dump = n_assign + jnp.arange(n_rows, dtype=jnp.int32) % (2 * EXPERT_BLOCK)
    row_token = jnp.where(row_slot < 0, 0, row_slot % n_tok)
    row_dst = jnp.where(row_slot < 0, dump, row_slot)
    blk = jnp.arange(n_blocks, dtype=jnp.int32)
    block_expert = jnp.minimum(jnp.searchsorted(pad_end, blk * EXPERT_BLOCK, side='right'), N_EXPERTS - 1)
    block_expert = jnp.where(blk < n_active, block_expert, block_expert[n_active - 1]).astype(jnp.int32)

    f = w_down.shape[1]
    wg = w_gate_up[:, :, 0::2].astype(BF16)
    wu = w_gate_up[:, :, 1::2].astype(BF16)
    bg = b_gate_up[:, 0::2].reshape(N_EXPERTS, 1, f)
    bu = b_gate_up[:, 1::2].reshape(N_EXPERTS, 1, f)
    ysel = _moe(block_expert, n_active.reshape(1), row_token, row_dst, u2.reshape(n_tok, d), wg, wu, bg, bu,
                w_down.astype(BF16), b_down.reshape(N_EXPERTS, 1, d), n_assign + 2 * EXPERT_BLOCK)

    out = _combine(x1.reshape(n_tok, d), mod, route, ysel, row2(ln2_gain), row2(ln2_bias), seq)
    return out.reshape(bsz, seq, d)


def kernel(x, c, w_ada, b_ada, w_in, w_pool_group, pool_scale, w_branch_a, w_alpha_up, b_alpha, gla_norm_gain,
           w_branch_b, w_out, ln1_gain, ln1_bias, w_router, b_router, w_gate_up, b_gate_up, w_down, b_down,
           ln2_gain, ln2_bias):
    for l in range(DEPTH):
        x = _layer(x, c, w_ada[l], b_ada[l], w_in[l], w_pool_group[l], pool_scale[l], w_branch_a[l],
                   w_alpha_up[l], b_alpha[l], gla_norm_gain[l], w_branch_b[l], w_out[l], ln1_gain[l],
                   ln1_bias[l], w_router[l], b_router[l], w_gate_up[l], b_gate_up[l], w_down[l], b_down[l],
                   ln2_gain[l], ln2_bias[l])
    return x
```

```python
import jax
import jax.numpy as jnp
from jax import lax
from jax.experimental import pallas as pl
from jax.experimental.pallas import tpu as pltpu

D_MODEL = 1024
CHUNK = 64
SUB = 16
N_SUB = CHUNK // SUB
POOL_WINDOWS = (2, 4, 8, 16)
POOL_GROUP_WIDTH = D_MODEL // len(POOL_WINDOWS)
POOL_HALO = 16
GLA_HEADS = 4
GLA_KEY_DIM = D_MODEL // 2
GLA_HEAD_K = GLA_KEY_DIM // GLA_HEADS
GLA_HEAD_V = D_MODEL // GLA_HEADS
GLA_GATE_RANK = 16
GLA_TAU = 16.0
N_EXPERTS = 32
TOP_K = 4
SWIGLU_ALPHA = 1.702
SWIGLU_LIMIT = 7.0
EXPERT_BLOCK = 256
LN_EPS = 1e-5
RMS_EPS = 1e-6
DEPTH = 1
DEEPNORM_ALPHA = (2.0 * DEPTH) ** 0.25

LANES = 128
MXU_COLS = 256
RANK_PAD = LANES
_W = (D_MODEL, GLA_KEY_DIM, GLA_KEY_DIM, D_MODEL, D_MODEL, RANK_PAD, D_MODEL, D_MODEL)
_OFF = tuple(sum(_W[:i]) for i in range(len(_W) + 1))
EXP_CAP = 60.0
NEG_BIG = -1e30

SEQ_TILE = 256
DISPATCH_TILE = 256
COMBINE_TILE = 256
DMA_UNROLL = 8
VMEM_LIMIT = 56 * 1024 * 1024

F32 = jnp.float32
BF16 = jnp.bfloat16
HI = lax.Precision.HIGHEST


def _dot(a, b):
    return jnp.dot(a, b, preferred_element_type=F32)


def _dot_nt(a, b):
    return lax.dot_general(a, b, (((1,), (1,)), ((), ())), preferred_element_type=F32)


def _dot_tn(a, b):
    return lax.dot_general(a, b, (((0,), (0,)), ((), ())), preferred_element_type=F32)


ROW_SUB = D_MODEL // LANES


def _load_rows(ref, lead=()):
    return jnp.concatenate([ref[lead + (slice(None), j, slice(None))] for j in range(ROW_SUB)], axis=1)


def _store_rows(ref, val):
    for j in range(ROW_SUB):
        ref[:, j, :] = val[:, j * LANES:(j + 1) * LANES]


def _ada_kernel(c_ref, w_ref, b_ref, o_ref):
    c = c_ref[...]
    s = c * jax.nn.sigmoid(c)
    o_ref[...] = jnp.dot(s, w_ref[...], precision=HI, preferred_element_type=F32) + b_ref[...]


def _ada(c, w_ada, b_ada):
    bsz, d = c.shape
    n = w_ada.shape[1]
    tn = 1024
    return pl.pallas_call(
        _ada_kernel,
        grid=(n // tn,),
        in_specs=[
            pl.BlockSpec((bsz, d), lambda j: (0, 0)),
            pl.BlockSpec((d, tn), lambda j: (0, j)),
            pl.BlockSpec((1, tn), lambda j: (0, j)),
        ],
        out_specs=pl.BlockSpec((bsz, tn), lambda j: (0, j)),
        out_shape=jax.ShapeDtypeStruct((bsz, n), F32),
        name="ada",
    )(c, w_ada, b_ada.reshape(1, n))


def _layer_norm(z, gain, bias):
    mu = jnp.mean(z, axis=-1, keepdims=True)
    zc = z - mu
    var = jnp.mean(zc * zc, axis=-1, keepdims=True)
    return zc * lax.rsqrt(var + LN_EPS) * gain + bias


def _mixer_kernel(x_ref, mod_ref, w_in_ref, w_pool_ref, pool_scale_ref, w_a_ref, w_al_ref, b_al_ref,
                  gain_ref, w_b_ref, w_out_ref, ln_g_ref, ln_b_ref, w_r_ref, b_r_ref,
                  x1_ref, u2_ref, route_ref, counts_ref,
                  a_ext, s_ref, cnt_ref, o_ref):
    b_idx = pl.program_id(0)
    s_idx = pl.program_id(1)
    tile = x_ref.shape[1]

    @pl.when(s_idx == 0)
    def _():
        s_ref[...] = jnp.zeros_like(s_ref)
        a_ext[0:POOL_HALO, :] = jnp.zeros((POOL_HALO, D_MODEL), F32)

    @pl.when((b_idx == 0) & (s_idx == 0))
    def _():
        cnt_ref[...] = jnp.zeros_like(cnt_ref)

    mod = mod_ref[0]
    sh_m, sc_m, g_m = mod[0:1], mod[1:2], mod[2:3]
    sh_f, sc_f = mod[3:4], mod[4:5]
    x = x_ref[0]
    u = (x * (1.0 + sc_m) + sh_m).astype(BF16)

    def proj(i):
        return _dot(u, w_in_ref[:, _OFF[i]:_OFF[i + 1]])

    a = proj(0)
    a_ext[POOL_HALO:POOL_HALO + tile, :] = a
    t_glob = s_idx * tile + lax.broadcasted_iota(jnp.int32, (tile, 1), 0)
    mapped = []
    for g, w in enumerate(POOL_WINDOWS):
        lo, hi = g * POOL_GROUP_WIDTH, (g + 1) * POOL_GROUP_WIDTH
        win = a[:, lo:hi]
        for k in range(1, w):
            win = win + a_ext[pl.ds(POOL_HALO - k, tile), lo:hi]
        inv_cnt = 1.0 / jnp.minimum(t_glob + 1, w).astype(F32)
        pooled = win * inv_cnt - a[:, lo:hi]
        mapped.append(_dot(pooled.astype(BF16), w_pool_ref[g]))
    a_ext[0:POOL_HALO, :] = a[tile - POOL_HALO:tile, :]
    ya = jnp.concatenate(mapped, axis=1) * pool_scale_ref[...]
    ya = _dot(ya.astype(BF16), w_a_ref[...])

    q = proj(1) * (GLA_HEAD_K ** -0.5)
    k_all = proj(2)
    v_all = proj(3)
    alpha_low = proj(5)
    z = _dot(alpha_low.astype(BF16), w_al_ref[...]) + b_al_ref[...]
    log_a = (jnp.minimum(z, 0.0) - jnp.log1p(jnp.exp(-jnp.abs(z)))) * (1.0 / GLA_TAU)

    r2 = lax.broadcasted_iota(jnp.int32, (2 * CHUNK, CHUNK), 0)
    c2 = lax.broadcasted_iota(jnp.int32, (2 * CHUNK, CHUNK), 1)
    cum_bound = jnp.where(r2 < CHUNK, r2 + 1, jnp.bitwise_and(r2 - CHUNK, -SUB))
    cum_mat = (c2 < cum_bound).astype(F32)
    ri = lax.broadcasted_iota(jnp.int32, (CHUNK, CHUNK), 0)
    ci = lax.broadcasted_iota(jnp.int32, (CHUNK, CHUNK), 1)
    causal = ci <= ri

    for c in range(tile // CHUNK):
        rows = slice(c * CHUNK, (c + 1) * CHUNK)
        cum = jnp.dot(cum_mat, log_a[rows], precision=HI, preferred_element_type=F32)
        b_cum, b_ref_pt = cum[0:CHUNK], cum[CHUNK:2 * CHUNK]
        for h in range(GLA_HEADS):
            ks = slice(h * GLA_HEAD_K, (h + 1) * GLA_HEAD_K)
            vs = slice(h * GLA_HEAD_V, (h + 1) * GLA_HEAD_V)
            qh, kh, vh = q[rows, ks], k_all[rows, ks], v_all[rows, vs].astype(BF16)
            bh, rh = b_cum[:, ks], b_ref_pt[:, ks]
            b_last = bh[CHUNK - 1:CHUNK, :]
            q_dec = (qh * jnp.exp(bh - rh)).astype(BF16)
            k_dec = jnp.concatenate(
                [(kh * jnp.exp(jnp.minimum(rh[i * SUB:i * SUB + 1, :] - bh, EXP_CAP))).astype(BF16)
                 for i in range(N_SUB)], axis=0)
            s_all = _dot_nt(q_dec, k_dec)
            scores = jnp.concatenate(
                [s_all[i * SUB:(i + 1) * SUB, i * CHUNK:(i + 1) * CHUNK] for i in range(N_SUB)], axis=0)
            scores = jnp.where(causal, scores, 0.0).astype(BF16)
            state_t = s_ref[h]
            o = _dot(scores, vh) + _dot_nt((qh * jnp.exp(bh)).astype(BF16), state_t.astype(BF16))
            k_carry = (kh * jnp.exp(b_last - bh)).astype(BF16)
            s_ref[h] = state_t * jnp.exp(b_last) + _dot_tn(vh, k_carry)
            o = o * lax.rsqrt(jnp.mean(o * o, axis=-1, keepdims=True) + RMS_EPS) * gain_ref[...]
            o_ref[rows, vs] = o

    r = proj(4)
    yb = _dot((o_ref[...] * (r * jax.nn.sigmoid(r))).astype(BF16), w_b_ref[...])

    merged = jax.nn.sigmoid(proj(6)) * ya + jax.nn.sigmoid(proj(7)) * yb
    y = _dot(merged.astype(BF16), w_out_ref[...])
    x1 = _layer_norm(DEEPNORM_ALPHA * x + g_m * y, ln_g_ref[...], ln_b_ref[...])
    x1_ref[0] = x1
    u2 = x1 * (1.0 + sc_f) + sh_f
    _store_rows(u2_ref, u2)

    logits = jnp.dot(u2, w_r_ref[...], precision=HI, preferred_element_type=F32) + b_r_ref[...]
    lane = lax.broadcasted_iota(jnp.int32, (tile, LANES), 1).astype(F32)
    work = logits
    sel = jnp.zeros((tile, LANES), F32)
    vals, idxs = [], []
    for _ in range(TOP_K):
        m = jnp.max(work, axis=1, keepdims=True)
        idx = jnp.min(jnp.where(work == m, lane, float(LANES)), axis=1, keepdims=True)
        hit = lane == idx
        vals.append(m)
        idxs.append(idx)
        sel = jnp.where(hit, 1.0, sel)
        work = jnp.where(hit, -jnp.inf, work)
    exps = [jnp.exp(v - vals[0]) for v in vals]
    inv_den = 1.0 / (exps[0] + exps[1] + exps[2] + exps[3])

    rt = lax.broadcasted_iota(jnp.int32, (tile, tile), 0)
    ct = lax.broadcasted_iota(jnp.int32, (tile, tile), 1)
    before = (ct < rt).astype(BF16)
    base = _dot(before, sel.astype(BF16)) + cnt_ref[0:1, :]
    route = jnp.zeros((tile, LANES), F32)
    for kk in range(TOP_K):
        rank = jnp.sum(jnp.where(lane == idxs[kk], base, 0.0), axis=1, keepdims=True)
        route = jnp.where(lane == kk, idxs[kk], route)
        route = jnp.where(lane == TOP_K + kk, rank, route)
        route = jnp.where(lane == 2 * TOP_K + kk, exps[kk] * inv_den, route)
    route_ref[0] = route
    cnt_ref[...] = cnt_ref[...] + jnp.sum(sel, axis=0, keepdims=True)
    counts_ref[...] = cnt_ref[...]


def _mixer(x, mod, w_in_p, w_pool, pool_scale, w_a, w_al_p, b_alpha, gain, w_b, w_out, ln_g, ln_b,
           w_r_p, b_r_p):
    bsz, seq, d = x.shape
    tile = SEQ_TILE
    n_s = seq // tile

    def const(shape):
        nd = len(shape)
        return pl.BlockSpec(shape, lambda b, s: (0,) * nd, pipeline_mode=pl.Buffered(1))

    tok = lambda width: pl.BlockSpec((1, tile, width), lambda b, s: (b, s, 0))
    return pl.pallas_call(
        _mixer_kernel,
        grid=(bsz, n_s),
        in_specs=[
            tok(d),
            pl.BlockSpec((1, 6, d), lambda b, s: (b, 0, 0)),
            const(w_in_p.shape), const(w_pool.shape), const(pool_scale.shape), const(w_a.shape),
            const(w_al_p.shape), const(b_alpha.shape), const(gain.shape), const(w_b.shape),
            const(w_out.shape), const(ln_g.shape), const(ln_b.shape), const(w_r_p.shape),
            const(b_r_p.shape),
        ],
        out_specs=[tok(d), pl.BlockSpec((tile, ROW_SUB, LANES), lambda b, s: (b * n_s + s, 0, 0)),
                   tok(LANES), pl.BlockSpec((8, LANES), lambda b, s: (0, 0))],
        out_shape=[
            jax.ShapeDtypeStruct((bsz, seq, d), F32),
            jax.ShapeDtypeStruct((bsz * seq, ROW_SUB, LANES), F32),
            jax.ShapeDtypeStruct((bsz, seq, LANES), F32),
            jax.ShapeDtypeStruct((8, LANES), F32),
        ],
        scratch_shapes=[
            pltpu.VMEM((POOL_HALO + tile, d), F32),
            pltpu.VMEM((GLA_HEADS, GLA_HEAD_V, GLA_HEAD_K), F32),
            pltpu.VMEM((8, LANES), F32),
            pltpu.VMEM((tile, d), F32),
        ],
        compiler_params=pltpu.CompilerParams(
            dimension_semantics=("arbitrary", "arbitrary"), vmem_limit_bytes=VMEM_LIMIT),
        name="mixer",
    )(x, mod, w_in_p, w_pool, pool_scale, w_a, w_al_p, b_alpha, gain, w_b, w_out, ln_g, ln_b,
      w_r_p, b_r_p)


def _dispatch_kernel(fill_ref, dest_ref, u2_ref, rows_hbm, zbuf, sem_fill, sem_rows):
    i = pl.program_id(0)
    tile = u2_ref.shape[0]

    @pl.when(i == 0)
    def _():
        zbuf[...] = jnp.zeros_like(zbuf)

        def fill(start):
            cp = pltpu.make_async_copy(zbuf, rows_hbm.at[pl.ds(start, EXPERT_BLOCK)], sem_fill.at[0])
            cp.start()
            cp.wait()

        for e in range(N_EXPERTS):
            fill(fill_ref[e])

        def unused(blk, carry):
            fill(blk * EXPERT_BLOCK)
            return carry

        lax.fori_loop(fill_ref[N_EXPERTS], rows_hbm.shape[0] // EXPERT_BLOCK, unused, 0)

    def body(r, carry):
        for kk in range(TOP_K):
            pltpu.make_async_copy(u2_ref.at[r], rows_hbm.at[dest_ref[0, 0, r * TOP_K + kk]],
                                  sem_rows.at[0]).start()
        return carry

    lax.fori_loop(0, tile, body, 0, unroll=DMA_UNROLL)
    for _ in range(TOP_K):
        pltpu.make_async_copy(u2_ref, rows_hbm.at[pl.ds(0, tile)], sem_rows.at[0]).wait()


def _dispatch(fill_start, dest, u2_rows, n_rows):
    n_tok = u2_rows.shape[0]
    tile = DISPATCH_TILE
    n_t = n_tok // tile
    grid_spec = pltpu.PrefetchScalarGridSpec(
        num_scalar_prefetch=1,
        grid=(n_t,),
        in_specs=[
            pl.BlockSpec((1, 1, tile * TOP_K), lambda i, fs: (i, 0, 0), memory_space=pltpu.SMEM),
            pl.BlockSpec((tile, ROW_SUB, LANES), lambda i, fs: (i, 0, 0)),
        ],
        out_specs=pl.BlockSpec(memory_space=pl.ANY),
        scratch_shapes=[
            pltpu.VMEM((EXPERT_BLOCK, ROW_SUB, LANES), F32),
            pltpu.SemaphoreType.DMA((1,)),
            pltpu.SemaphoreType.DMA((1,)),
        ],
    )
    return pl.pallas_call(
        _dispatch_kernel,
        grid_spec=grid_spec,
        out_shape=jax.ShapeDtypeStruct((n_rows + EXPERT_BLOCK, ROW_SUB, LANES), F32),
        compiler_params=pltpu.CompilerParams(dimension_semantics=("arbitrary",)),
        name="dispatch",
    )(fill_start, dest.reshape(n_t, 1, tile * TOP_K), u2_rows)


def _moe_kernel(be_ref, nb_ref, x_ref, wgu_ref, bg_ref, bu_ref, wd_ref, bd_ref, y_ref,
                wg_s, wu_s, wd_s):
    i = pl.program_id(0)
    new_expert = (i == 0) | (be_ref[i] != be_ref[jnp.maximum(i - 1, 0)])

    @pl.when(new_expert & (i < nb_ref[0]))
    def _():
        src = lax.broadcasted_iota(jnp.int32, (MXU_COLS, MXU_COLS), 0)
        col = lax.broadcasted_iota(jnp.int32, (MXU_COLS, MXU_COLS), 1)
        half = MXU_COLS // 2
        want = jnp.where(col < half, 2 * col, 2 * (col - half) + 1)
        unzip = (src == want).astype(BF16)
        for g in range(wgu_ref.shape[2] // MXU_COLS):
            blk = wgu_ref[0, :, g * MXU_COLS:(g + 1) * MXU_COLS].astype(BF16)
            sep = _dot(blk, unzip)
            wg_s[:, g * half:(g + 1) * half] = sep[:, :half].astype(BF16)
            wu_s[:, g * half:(g + 1) * half] = sep[:, half:].astype(BF16)
        wd_s[...] = wd_ref[0].astype(BF16)

    @pl.when(i < nb_ref[0])
    def _():
        xb = _load_rows(x_ref).astype(BF16)
        gate = jnp.minimum(_dot(xb, wg_s[...]) + bg_ref[0], SWIGLU_LIMIT)
        up = jnp.clip(_dot(xb, wu_s[...]) + bu_ref[0], -SWIGLU_LIMIT, SWIGLU_LIMIT)
        glu = gate * jax.nn.sigmoid(gate * SWIGLU_ALPHA)
        _store_rows(y_ref, _dot(((up + 1.0) * glu).astype(BF16), wd_s[...]) + bd_ref[0])

    @pl.when(i >= nb_ref[0])
    def _():
        y_ref[...] = jnp.zeros_like(y_ref)


def _moe(block_expert, n_active, x_rows, w_gate_up, bg, bu, w_down, bd):
    n_blocks = block_expert.shape[0]
    _, d, f2 = w_gate_up.shape
    f = f2 // 2
    rows_in = pl.BlockSpec((EXPERT_BLOCK, ROW_SUB, LANES), lambda i, be, nb: (jnp.minimum(i, nb[0] - 1), 0, 0))
    rows_out = pl.BlockSpec((EXPERT_BLOCK, ROW_SUB, LANES), lambda i, be, nb: (i, 0, 0))
    per_expert = lambda shape: pl.BlockSpec((1,) + shape, lambda i, be, nb: (be[i], 0, 0))
    grid_spec = pltpu.PrefetchScalarGridSpec(
        num_scalar_prefetch=2,
        grid=(n_blocks,),
        in_specs=[rows_in, per_expert((d, f2)), per_expert((1, f)), per_expert((1, f)), per_expert((f, d)),
                  per_expert((1, d))],
        out_specs=rows_out,
        scratch_shapes=[pltpu.VMEM((d, f), BF16), pltpu.VMEM((d, f), BF16), pltpu.VMEM((f, d), BF16)],
    )
    return pl.pallas_call(
        _moe_kernel,
        grid_spec=grid_spec,
        out_shape=jax.ShapeDtypeStruct((n_blocks * EXPERT_BLOCK, ROW_SUB, LANES), F32),
        compiler_params=pltpu.CompilerParams(
            dimension_semantics=("arbitrary",), vmem_limit_bytes=VMEM_LIMIT),
        name="moe",
    )(block_expert, n_active, x_rows, w_gate_up, bg, bu, w_down, bd)


def _combine_kernel(dest_cur_ref, dest_nxt_ref, x1_ref, mod_ref, route_ref, ln_g_ref, ln_b_ref, y_hbm,
                    o_ref, ybuf, sem):
    i = pl.program_id(0)
    n_steps = pl.num_programs(0)
    tile = x1_ref.shape[0]
    slot = lax.rem(i, 2)

    def gather(dest_ref, s):
        def body(r, carry):
            for kk in range(TOP_K):
                pltpu.make_async_copy(y_hbm.at[dest_ref[0, 0, r * TOP_K + kk]], ybuf.at[s, kk, r],
                                      sem.at[s]).start()
            return carry
        lax.fori_loop(0, tile, body, 0, unroll=DMA_UNROLL)

    @pl.when(i == 0)
    def _():
        gather(dest_cur_ref, 0)

    @pl.when(i + 1 < n_steps)
    def _():
        gather(dest_nxt_ref, 1 - slot)

    for kk in range(TOP_K):
        pltpu.make_async_copy(y_hbm.at[pl.ds(0, tile)], ybuf.at[slot, kk], sem.at[slot]).wait()

    g_f = mod_ref[0][5:6]
    route = route_ref[...]
    y = jnp.zeros(x1_ref.shape, F32)
    for kk in range(TOP_K):
        y = y + route[:, 2 * TOP_K + kk:2 * TOP_K + kk + 1] * _load_rows(ybuf, (slot, kk))
    o_ref[...] = _layer_norm(DEEPNORM_ALPHA * x1_ref[...] + g_f * y, ln_g_ref[...], ln_b_ref[...])


def _combine(dest, x1, mod, route, ln_g, ln_b, y_rows, seq):
    n_tok, d = x1.shape
    tile = COMBINE_TILE
    n_t = n_tok // tile
    per_seq = seq // tile
    dest3 = dest.reshape(n_t, 1, tile * TOP_K)
    smem_blk = lambda fn: pl.BlockSpec((1, 1, tile * TOP_K), fn, memory_space=pltpu.SMEM)
    row = lambda width: pl.BlockSpec((tile, width), lambda i: (i, 0))
    vec = pl.BlockSpec((1, d), lambda i: (0, 0))
    return pl.pallas_call(
        _combine_kernel,
        grid=(n_t,),
        in_specs=[smem_blk(lambda i: (i, 0, 0)), smem_blk(lambda i: (jnp.minimum(i + 1, n_t - 1), 0, 0)),
                  row(d), pl.BlockSpec((1, 6, d), lambda i: (i // per_seq, 0, 0)), row(LANES), vec, vec,
                  pl.BlockSpec(memory_space=pl.ANY)],
        out_specs=row(d),
        out_shape=jax.ShapeDtypeStruct((n_tok, d), F32),
        scratch_shapes=[pltpu.VMEM((2, TOP_K, tile, ROW_SUB, LANES), F32), pltpu.SemaphoreType.DMA((2,))],
        compiler_params=pltpu.CompilerParams(
            dimension_semantics=("arbitrary",), vmem_limit_bytes=VMEM_LIMIT),
        name="combine",
    )(dest3, dest3, x1, mod, route, ln_g, ln_b, y_rows)


def _pad_in_proj(w_in):
    rank_end = 3 * D_MODEL + 2 * GLA_KEY_DIM + GLA_GATE_RANK
    pad = jnp.zeros((w_in.shape[0], RANK_PAD - GLA_GATE_RANK), w_in.dtype)
    return jnp.concatenate([w_in[:, :rank_end], pad, w_in[:, rank_end:]], axis=1)


def _layer(x, c, w_ada, b_ada, w_in, w_pool_group, pool_scale, w_branch_a, w_alpha_up, b_alpha,
           gla_norm_gain, w_branch_b, w_out, ln1_gain, ln1_bias, w_router, b_router,
           w_gate_up, b_gate_up, w_down, b_down, ln2_gain, ln2_bias):
    bsz, seq, d = x.shape
    n_tok = bsz * seq
    n_assign = n_tok * TOP_K
    row2 = lambda v: v.reshape(1, -1)

    mod = _ada(c, w_ada, b_ada).reshape(bsz, 6, d)

    w_in_p = _pad_in_proj(w_in).astype(BF16)
    w_al_p = jnp.concatenate(
        [w_alpha_up, jnp.zeros((RANK_PAD - GLA_GATE_RANK, GLA_KEY_DIM), w_alpha_up.dtype)], axis=0).astype(BF16)
    w_r_p = jnp.concatenate([w_router, jnp.zeros((d, LANES - N_EXPERTS), w_router.dtype)], axis=1)
    b_r_p = jnp.concatenate([b_router, jnp.full((LANES - N_EXPERTS,), NEG_BIG, b_router.dtype)]).reshape(1, LANES)
    x1, u2_rows, route, counts = _mixer(
        x, mod, w_in_p, w_pool_group.astype(BF16), row2(pool_scale), w_branch_a.astype(BF16), w_al_p,
        row2(b_alpha), row2(gla_norm_gain), w_branch_b.astype(BF16), w_out.astype(BF16), row2(ln1_gain),
        row2(ln1_bias), w_r_p, b_r_p)

    route = route.reshape(n_tok, LANES)
    top_idx = route[:, 0:TOP_K].astype(jnp.int32)
    rank = route[:, TOP_K:2 * TOP_K].astype(jnp.int32)
    counts = counts[0, :N_EXPERTS].astype(jnp.int32)
    padded = (counts + EXPERT_BLOCK - 1) // EXPERT_BLOCK * EXPERT_BLOCK
    pad_end = jnp.cumsum(padded)
    pad_start = pad_end - padded
    n_rows = (n_assign + N_EXPERTS * (EXPERT_BLOCK - 1) + EXPERT_BLOCK - 1) // EXPERT_BLOCK * EXPERT_BLOCK
    n_blocks = n_rows // EXPERT_BLOCK
    n_active = (pad_end[-1] // EXPERT_BLOCK).astype(jnp.int32)
    dest = (pad_start[top_idx] + rank).reshape(-1)
    blk_row = jnp.arange(n_blocks, dtype=jnp.int32)[:, None] * EXPERT_BLOCK
    block_expert = jnp.minimum(jnp.sum(pad_end[None, :] <= blk_row, axis=1), N_EXPERTS - 1).astype(jnp.int32)

    fill_start = jnp.concatenate([pad_start + counts, n_active.reshape(1)]).astype(jnp.int32)
    x_rows = _dispatch(fill_start, dest, u2_rows, n_rows)
    f = w_down.shape[1]
    bg = b_gate_up[:, 0::2].reshape(N_EXPERTS, 1, f)
    bu = b_gate_up[:, 1::2].reshape(N_EXPERTS, 1, f)
    y_rows = _moe(block_expert, n_active.reshape(1), x_rows, w_gate_up, bg, bu, w_down,
                  b_down.reshape(N_EXPERTS, 1, d))
    out = _combine(dest, x1.reshape(n_tok, d), mod, route, row2(ln2_gain), row2(ln2_bias), y_rows, seq)
    return out.reshape(bsz, seq, d)


def kernel(x, c, w_ada, b_ada, w_in, w_pool_group, pool_scale, w_branch_a, w_alpha_up, b_alpha, gla_norm_gain,
           w_branch_b, w_out, ln1_gain, ln1_bias, w_router, b_router, w_gate_up, b_gate_up, w_down, b_down,
           ln2_gain, ln2_bias):
    for l in range(DEPTH):
        x = _layer(x, c, w_ada[l], b_ada[l], w_in[l], w_pool_group[l], pool_scale[l], w_branch_a[l],
                   w_alpha_up[l], b_alpha[l], gla_norm_gain[l], w_branch_b[l], w_out[l], ln1_gain[l],
                   ln1_bias[l], w_router[l], b_router[l], w_gate_up[l], b_gate_up[l], w_down[l], b_down[l],
                   ln2_gain[l], ln2_bias[l])
    return x
```

```python
import jax
import jax.numpy as jnp
from jax import lax
from jax.experimental import pallas as pl
from jax.experimental.pallas import tpu as pltpu

D_MODEL = 1024
CHUNK = 64
SUB = 16
N_SUB = CHUNK // SUB
POOL_WINDOWS = (2, 4, 8, 16)
POOL_GROUP_WIDTH = D_MODEL // len(POOL_WINDOWS)
POOL_HALO = 16
GLA_HEADS = 4
GLA_KEY_DIM = D_MODEL // 2
GLA_HEAD_K = GLA_KEY_DIM // GLA_HEADS
GLA_HEAD_V = D_MODEL // GLA_HEADS
GLA_GATE_RANK = 16
GLA_TAU = 16.0
N_EXPERTS = 32
TOP_K = 4
SWIGLU_ALPHA = 1.702
SWIGLU_LIMIT = 7.0
EXPERT_BLOCK = 256
LN_EPS = 1e-5
RMS_EPS = 1e-6
DEPTH = 1
DEEPNORM_ALPHA = (2.0 * DEPTH) ** 0.25

LANES = 128
MXU_COLS = 256
RANK_PAD = LANES
_W = (D_MODEL, GLA_KEY_DIM, GLA_KEY_DIM, D_MODEL, D_MODEL, RANK_PAD, D_MODEL, D_MODEL)
_OFF = tuple(sum(_W[:i]) for i in range(len(_W) + 1))
EXP_CAP = 60.0
NEG_BIG = -1e30

SEQ_TILE = 256
DISPATCH_TILE = 256
COMBINE_TILE = 256
DMA_UNROLL = 8
VMEM_LIMIT = 56 * 1024 * 1024

F32 = jnp.float32
BF16 = jnp.bfloat16
HI = lax.Precision.HIGHEST


def _dot(a, b):
    return jnp.dot(a, b, preferred_element_type=F32)


def _dot_nt(a, b):
    return lax.dot_general(a, b, (((1,), (1,)), ((), ())), preferred_element_type=F32)


def _dot_tn(a, b):
    return lax.dot_general(a, b, (((0,), (0,)), ((), ())), preferred_element_type=F32)


ROW_SUB = D_MODEL // LANES


def _load_rows(ref, n_rows, lead=()):
    return jnp.concatenate(
        [ref[lead + (pl.ds(j, n_rows, stride=ROW_SUB), slice(None))] for j in range(ROW_SUB)], axis=1)


def _store_rows(ref, val):
    for j in range(ROW_SUB):
        ref[pl.ds(j, val.shape[0], stride=ROW_SUB), :] = val[:, j * LANES:(j + 1) * LANES]


def _ada_kernel(c_ref, w_ref, b_ref, o_ref):
    c = c_ref[...]
    s = c * jax.nn.sigmoid(c)
    o_ref[...] = jnp.dot(s, w_ref[...], precision=HI, preferred_element_type=F32) + b_ref[...]


def _ada(c, w_ada, b_ada):
    bsz, d = c.shape
    n = w_ada.shape[1]
    tn = 1024
    return pl.pallas_call(
        _ada_kernel,
        grid=(n // tn,),
        in_specs=[
            pl.BlockSpec((bsz, d), lambda j: (0, 0)),
            pl.BlockSpec((d, tn), lambda j: (0, j)),
            pl.BlockSpec((1, tn), lambda j: (0, j)),
        ],
        out_specs=pl.BlockSpec((bsz, tn), lambda j: (0, j)),
        out_shape=jax.ShapeDtypeStruct((bsz, n), F32),
        name="ada",
    )(c, w_ada, b_ada.reshape(1, n))


def _layer_norm(z, gain, bias):
    mu = jnp.mean(z, axis=-1, keepdims=True)
    zc = z - mu
    var = jnp.mean(zc * zc, axis=-1, keepdims=True)
    return zc * lax.rsqrt(var + LN_EPS) * gain + bias


def _mixer_kernel(x_ref, mod_ref, w_in_ref, w_pool_ref, pool_scale_ref, w_a_ref, w_al_ref, b_al_ref,
                  gain_ref, w_b_ref, w_out_ref, ln_g_ref, ln_b_ref, w_r_ref, b_r_ref,
                  x1_ref, u2_ref, route_ref, counts_ref,
                  a_ext, s_ref, cnt_ref, o_ref):
    b_idx = pl.program_id(0)
    s_idx = pl.program_id(1)
    tile = x_ref.shape[1]

    @pl.when(s_idx == 0)
    def _():
        s_ref[...] = jnp.zeros_like(s_ref)
        a_ext[0:POOL_HALO, :] = jnp.zeros((POOL_HALO, D_MODEL), F32)

    @pl.when((b_idx == 0) & (s_idx == 0))
    def _():
        cnt_ref[...] = jnp.zeros_like(cnt_ref)

    mod = mod_ref[0]
    sh_m, sc_m, g_m = mod[0:1], mod[1:2], mod[2:3]
    sh_f, sc_f = mod[3:4], mod[4:5]
    x = x_ref[0]
    u = (x * (1.0 + sc_m) + sh_m).astype(BF16)

    def proj(i):
        return _dot(u, w_in_ref[:, _OFF[i]:_OFF[i + 1]])

    a = proj(0)
    a_ext[POOL_HALO:POOL_HALO + tile, :] = a
    t_glob = s_idx * tile + lax.broadcasted_iota(jnp.int32, (tile, 1), 0)
    mapped = []
    for g, w in enumerate(POOL_WINDOWS):
        lo, hi = g * POOL_GROUP_WIDTH, (g + 1) * POOL_GROUP_WIDTH
        win = a[:, lo:hi]
        for k in range(1, w):
            win = win + a_ext[pl.ds(POOL_HALO - k, tile), lo:hi]
        inv_cnt = 1.0 / jnp.minimum(t_glob + 1, w).astype(F32)
        pooled = win * inv_cnt - a[:, lo:hi]
        mapped.append(_dot(pooled.astype(BF16), w_pool_ref[g]))
    a_ext[0:POOL_HALO, :] = a[tile - POOL_HALO:tile, :]
    ya = jnp.concatenate(mapped, axis=1) * pool_scale_ref[...]
    ya = _dot(ya.astype(BF16), w_a_ref[...])

    q = proj(1) * (GLA_HEAD_K ** -0.5)
    k_all = proj(2)
    v_all = proj(3)
    alpha_low = proj(5)
    z = _dot(alpha_low.astype(BF16), w_al_ref[...]) + b_al_ref[...]
    log_a = (jnp.minimum(z, 0.0) - jnp.log1p(jnp.exp(-jnp.abs(z)))) * (1.0 / GLA_TAU)

    r2 = lax.broadcasted_iota(jnp.int32, (2 * CHUNK, CHUNK), 0)
    c2 = lax.broadcasted_iota(jnp.int32, (2 * CHUNK, CHUNK), 1)
    cum_bound = jnp.where(r2 < CHUNK, r2 + 1, jnp.bitwise_and(r2 - CHUNK, -SUB))
    cum_mat = (c2 < cum_bound).astype(F32)
    ri = lax.broadcasted_iota(jnp.int32, (CHUNK, CHUNK), 0)
    ci = lax.broadcasted_iota(jnp.int32, (CHUNK, CHUNK), 1)
    causal = ci <= ri

    for c in range(tile // CHUNK):
        rows = slice(c * CHUNK, (c + 1) * CHUNK)
        cum = jnp.dot(cum_mat, log_a[rows], precision=HI, preferred_element_type=F32)
        b_cum, b_ref_pt = cum[0:CHUNK], cum[CHUNK:2 * CHUNK]
        for h in range(GLA_HEADS):
            ks = slice(h * GLA_HEAD_K, (h + 1) * GLA_HEAD_K)
            vs = slice(h * GLA_HEAD_V, (h + 1) * GLA_HEAD_V)
            qh, kh, vh = q[rows, ks], k_all[rows, ks], v_all[rows, vs].astype(BF16)
            bh, rh = b_cum[:, ks], b_ref_pt[:, ks]
            b_last = bh[CHUNK - 1:CHUNK, :]
            q_dec = (qh * jnp.exp(bh - rh)).astype(BF16)
            k_dec = jnp.concatenate(
                [(kh * jnp.exp(jnp.minimum(rh[i * SUB:i * SUB + 1, :] - bh, EXP_CAP))).astype(BF16)
                 for i in range(N_SUB)], axis=0)
            s_all = _dot_nt(q_dec, k_dec)
            scores = jnp.concatenate(
                [s_all[i * SUB:(i + 1) * SUB, i * CHUNK:(i + 1) * CHUNK] for i in range(N_SUB)], axis=0)
            scores = jnp.where(causal, scores, 0.0).astype(BF16)
            state_t = s_ref[h]
            o = _dot(scores, vh) + _dot_nt((qh * jnp.exp(bh)).astype(BF16), state_t.astype(BF16))
            k_carry = (kh * jnp.exp(b_last - bh)).astype(BF16)
            s_ref[h] = state_t * jnp.exp(b_last) + _dot_tn(vh, k_carry)
            o = o * lax.rsqrt(jnp.mean(o * o, axis=-1, keepdims=True) + RMS_EPS) * gain_ref[...]
            o_ref[rows, vs] = o

    r = proj(4)
    yb = _dot((o_ref[...] * (r * jax.nn.sigmoid(r))).astype(BF16), w_b_ref[...])

    merged = jax.nn.sigmoid(proj(6)) * ya + jax.nn.sigmoid(proj(7)) * yb
    y = _dot(merged.astype(BF16), w_out_ref[...])
    x1 = _layer_norm(DEEPNORM_ALPHA * x + g_m * y, ln_g_ref[...], ln_b_ref[...])
    x1_ref[0] = x1
    u2 = x1 * (1.0 + sc_f) + sh_f
    _store_rows(u2_ref, u2)

    logits = jnp.dot(u2, w_r_ref[...], precision=HI, preferred_element_type=F32) + b_r_ref[...]
    lane = lax.broadcasted_iota(jnp.int32, (tile, LANES), 1).astype(F32)
    work = logits
    sel = jnp.zeros((tile, LANES), F32)
    vals, idxs = [], []
    for _ in range(TOP_K):
        m = jnp.max(work, axis=1, keepdims=True)
        idx = jnp.min(jnp.where(work == m, lane, float(LANES)), axis=1, keepdims=True)
        hit = lane == idx
        vals.append(m)
        idxs.append(idx)
        sel = jnp.where(hit, 1.0, sel)
        work = jnp.where(hit, -jnp.inf, work)
    exps = [jnp.exp(v - vals[0]) for v in vals]
    inv_den = 1.0 / (exps[0] + exps[1] + exps[2] + exps[3])

    rt = lax.broadcasted_iota(jnp.int32, (tile, tile), 0)
    ct = lax.broadcasted_iota(jnp.int32, (tile, tile), 1)
    before = (ct < rt).astype(BF16)
    base = _dot(before, sel.astype(BF16)) + cnt_ref[0:1, :]
    route = jnp.zeros((tile, LANES), F32)
    for kk in range(TOP_K):
        rank = jnp.sum(jnp.where(lane == idxs[kk], base, 0.0), axis=1, keepdims=True)
        route = jnp.where(lane == kk, idxs[kk], route)
        route = jnp.where(lane == TOP_K + kk, rank, route)
        route = jnp.where(lane == 2 * TOP_K + kk, exps[kk] * inv_den, route)
    route_ref[0] = route
    cnt_ref[...] = cnt_ref[...] + jnp.sum(sel, axis=0, keepdims=True)
    counts_ref[...] = cnt_ref[...]


def _mixer(x, mod, w_in_p, w_pool, pool_scale, w_a, w_al_p, b_alpha, gain, w_b, w_out, ln_g, ln_b,
           w_r_p, b_r_p):
    bsz, seq, d = x.shape
    tile = SEQ_TILE
    n_s = seq // tile

    def const(shape):
        nd = len(shape)
        return pl.BlockSpec(shape, lambda b, s: (0,) * nd, pipeline_mode=pl.Buffered(1))

    tok = lambda width: pl.BlockSpec((1, tile, width), lambda b, s: (b, s, 0))
    return pl.pallas_call(
        _mixer_kernel,
        grid=(bsz, n_s),
        in_specs=[
            tok(d),
            pl.BlockSpec((1, 6, d), lambda b, s: (b, 0, 0)),
            const(w_in_p.shape), const(w_pool.shape), const(pool_scale.shape), const(w_a.shape),
            const(w_al_p.shape), const(b_alpha.shape), const(gain.shape), const(w_b.shape),
            const(w_out.shape), const(ln_g.shape), const(ln_b.shape), const(w_r_p.shape),
            const(b_r_p.shape),
        ],
        out_specs=[tok(d), pl.BlockSpec((tile * ROW_SUB, LANES), lambda b, s: (b * n_s + s, 0)),
                   tok(LANES), pl.BlockSpec((8, LANES), lambda b, s: (0, 0))],
        out_shape=[
            jax.ShapeDtypeStruct((bsz, seq, d), F32),
            jax.ShapeDtypeStruct((bsz * seq * ROW_SUB, LANES), F32),
            jax.ShapeDtypeStruct((bsz, seq, LANES), F32),
            jax.ShapeDtypeStruct((8, LANES), F32),
        ],
        scratch_shapes=[
            pltpu.VMEM((POOL_HALO + tile, d), F32),
            pltpu.VMEM((GLA_HEADS, GLA_HEAD_V, GLA_HEAD_K), F32),
            pltpu.VMEM((8, LANES), F32),
            pltpu.VMEM((tile, d), F32),
        ],
        compiler_params=pltpu.CompilerParams(
            dimension_semantics=("arbitrary", "arbitrary"), vmem_limit_bytes=VMEM_LIMIT),
        name="mixer",
    )(x, mod, w_in_p, w_pool, pool_scale, w_a, w_al_p, b_alpha, gain, w_b, w_out, ln_g, ln_b,
      w_r_p, b_r_p)


def _row_at(ref, row8):
    return ref.at[pl.ds(pl.multiple_of(row8, ROW_SUB), ROW_SUB)]


def _dispatch_kernel(fill_ref, dest_ref, u2_ref, rows_hbm, zbuf, sem_fill, sem_rows):
    i = pl.program_id(0)
    tile = u2_ref.shape[0] // ROW_SUB
    blk8 = EXPERT_BLOCK * ROW_SUB

    @pl.when(i == 0)
    def _():
        zbuf[...] = jnp.zeros_like(zbuf)

        def fill(start):
            dst = rows_hbm.at[pl.ds(pl.multiple_of(start * ROW_SUB, ROW_SUB), blk8)]
            cp = pltpu.make_async_copy(zbuf, dst, sem_fill.at[0])
            cp.start()
            cp.wait()

        for e in range(N_EXPERTS):
            fill(fill_ref[e])

        def unused(blk, carry):
            fill(blk * EXPERT_BLOCK)
            return carry

        lax.fori_loop(fill_ref[N_EXPERTS], rows_hbm.shape[0] // blk8, unused, 0)

    def body(r, carry):
        for kk in range(TOP_K):
            pltpu.make_async_copy(_row_at(u2_ref, r * ROW_SUB), _row_at(rows_hbm, dest_ref[0, 0, r * TOP_K + kk]),
                                  sem_rows.at[0]).start(priority=kk % 2)
        return carry

    lax.fori_loop(0, tile, body, 0, unroll=DMA_UNROLL)
    for _ in range(TOP_K):
        pltpu.make_async_copy(u2_ref, rows_hbm.at[pl.ds(0, tile * ROW_SUB)], sem_rows.at[0]).wait()


def _dispatch(fill_start, dest8, u2_rows, n_rows):
    n_tok = u2_rows.shape[0] // ROW_SUB
    tile = DISPATCH_TILE
    n_t = n_tok // tile
    grid_spec = pltpu.PrefetchScalarGridSpec(
        num_scalar_prefetch=1,
        grid=(n_t,),
        in_specs=[
            pl.BlockSpec((1, 1, tile * TOP_K), lambda i, fs: (i, 0, 0), memory_space=pltpu.SMEM),
            pl.BlockSpec((tile * ROW_SUB, LANES), lambda i, fs: (i, 0)),
        ],
        out_specs=pl.BlockSpec(memory_space=pl.ANY),
        scratch_shapes=[
            pltpu.VMEM((EXPERT_BLOCK * ROW_SUB, LANES), F32),
            pltpu.SemaphoreType.DMA((1,)),
            pltpu.SemaphoreType.DMA((1,)),
        ],
    )
    return pl.pallas_call(
        _dispatch_kernel,
        grid_spec=grid_spec,
        out_shape=jax.ShapeDtypeStruct(((n_rows + EXPERT_BLOCK) * ROW_SUB, LANES), F32),
        compiler_params=pltpu.CompilerParams(dimension_semantics=("arbitrary",)),
        name="dispatch",
    )(fill_start, dest8.reshape(n_t, 1, tile * TOP_K), u2_rows)


def _moe_kernel(be_ref, nb_ref, x_ref, wgu_ref, bg_ref, bu_ref, wd_ref, bd_ref, y_ref,
                wg_s, wu_s, wd_s):
    i = pl.program_id(0)
    new_expert = (i == 0) | (be_ref[i] != be_ref[jnp.maximum(i - 1, 0)])

    @pl.when(new_expert & (i < nb_ref[0]))
    def _():
        src = lax.broadcasted_iota(jnp.int32, (MXU_COLS, MXU_COLS), 0)
        col = lax.broadcasted_iota(jnp.int32, (MXU_COLS, MXU_COLS), 1)
        half = MXU_COLS // 2
        want = jnp.where(col < half, 2 * col, 2 * (col - half) + 1)
        unzip = (src == want).astype(BF16)
        for g in range(wgu_ref.shape[2] // MXU_COLS):
            blk = wgu_ref[0, :, g * MXU_COLS:(g + 1) * MXU_COLS].astype(BF16)
            sep = _dot(blk, unzip)
            wg_s[:, g * half:(g + 1) * half] = sep[:, :half].astype(BF16)
            wu_s[:, g * half:(g + 1) * half] = sep[:, half:].astype(BF16)
        wd_s[...] = wd_ref[0].astype(BF16)

    @pl.when(i < nb_ref[0])
    def _():
        xb = _load_rows(x_ref, EXPERT_BLOCK).astype(BF16)
        gate = jnp.minimum(_dot(xb, wg_s[...]) + bg_ref[0], SWIGLU_LIMIT)
        up = jnp.clip(_dot(xb, wu_s[...]) + bu_ref[0], -SWIGLU_LIMIT, SWIGLU_LIMIT)
        glu = gate * jax.nn.sigmoid(gate * SWIGLU_ALPHA)
        _store_rows(y_ref, _dot(((up + 1.0) * glu).astype(BF16), wd_s[...]) + bd_ref[0])

    @pl.when(i >= nb_ref[0])
    def _():
        y_ref[...] = jnp.zeros_like(y_ref)


def _moe(block_expert, n_active, x_rows, w_gate_up, bg, bu, w_down, bd):
    n_blocks = block_expert.shape[0]
    _, d, f2 = w_gate_up.shape
    f = f2 // 2
    blk8 = EXPERT_BLOCK * ROW_SUB
    rows_in = pl.BlockSpec((blk8, LANES), lambda i, be, nb: (jnp.maximum(jnp.minimum(i, nb[0] - 1), 0), 0))
    rows_out = pl.BlockSpec((blk8, LANES), lambda i, be, nb: (i, 0))
    per_expert = lambda shape: pl.BlockSpec((1,) + shape, lambda i, be, nb: (be[i], 0, 0))
    grid_spec = pltpu.PrefetchScalarGridSpec(
        num_scalar_prefetch=2,
        grid=(n_blocks,),
        in_specs=[rows_in, per_expert((d, f2)), per_expert((1, f)), per_expert((1, f)), per_expert((f, d)),
                  per_expert((1, d))],
        out_specs=rows_out,
        scratch_shapes=[pltpu.VMEM((d, f), BF16), pltpu.VMEM((d, f), BF16), pltpu.VMEM((f, d), BF16)],
    )
    return pl.pallas_call(
        _moe_kernel,
        grid_spec=grid_spec,
        out_shape=jax.ShapeDtypeStruct((n_blocks * blk8, LANES), F32),
        compiler_params=pltpu.CompilerParams(
            dimension_semantics=("arbitrary",), vmem_limit_bytes=VMEM_LIMIT),
        name="moe",
    )(block_expert, n_active, x_rows, w_gate_up, bg, bu, w_down, bd)


def _combine_kernel(dest_cur_ref, dest_nxt_ref, x1_ref, mod_ref, route_ref, ln_g_ref, ln_b_ref, y_hbm,
                    o_ref, ybuf, sem):
    i = pl.program_id(0)
    n_steps = pl.num_programs(0)
    tile = x1_ref.shape[0]
    slot = lax.rem(i, 2)

    def gather(dest_ref, s):
        def body(r, carry):
            for kk in range(TOP_K):
                pltpu.make_async_copy(_row_at(y_hbm, dest_ref[0, 0, r * TOP_K + kk]),
                                      _row_at(ybuf.at[s, kk], r * ROW_SUB), sem.at[s]).start(priority=kk % 2)
            return carry
        lax.fori_loop(0, tile, body, 0, unroll=DMA_UNROLL)

    @pl.when(i == 0)
    def _():
        gather(dest_cur_ref, 0)

    @pl.when(i + 1 < n_steps)
    def _():
        gather(dest_nxt_ref, 1 - slot)

    for kk in range(TOP_K):
        pltpu.make_async_copy(y_hbm.at[pl.ds(0, tile * ROW_SUB)], ybuf.at[slot, kk], sem.at[slot]).wait()

    g_f = mod_ref[0][5:6]
    route = route_ref[...]
    y = jnp.zeros(x1_ref.shape, F32)
    for kk in range(TOP_K):
        y = y + route[:, 2 * TOP_K + kk:2 * TOP_K + kk + 1] * _load_rows(ybuf, tile, (slot, kk))
    o_ref[...] = _layer_norm(DEEPNORM_ALPHA * x1_ref[...] + g_f * y, ln_g_ref[...], ln_b_ref[...])


def _combine(dest8, x1, mod, route, ln_g, ln_b, y_rows, seq):
    n_tok, d = x1.shape
    tile = COMBINE_TILE
    n_t = n_tok // tile
    per_seq = seq // tile
    dest3 = dest8.reshape(n_t, 1, tile * TOP_K)
    smem_blk = lambda fn: pl.BlockSpec((1, 1, tile * TOP_K), fn, memory_space=pltpu.SMEM)
    row = lambda width: pl.BlockSpec((tile, width), lambda i: (i, 0))
    vec = pl.BlockSpec((1, d), lambda i: (0, 0))
    return pl.pallas_call(
        _combine_kernel,
        grid=(n_t,),
        in_specs=[smem_blk(lambda i: (i, 0, 0)), smem_blk(lambda i: (jnp.minimum(i + 1, n_t - 1), 0, 0)),
                  row(d), pl.BlockSpec((1, 6, d), lambda i: (i // per_seq, 0, 0)), row(LANES), vec, vec,
                  pl.BlockSpec(memory_space=pl.ANY)],
        out_specs=row(d),
        out_shape=jax.ShapeDtypeStruct((n_tok, d), F32),
        scratch_shapes=[pltpu.VMEM((2, TOP_K, tile * ROW_SUB, LANES), F32), pltpu.SemaphoreType.DMA((2,))],
        compiler_params=pltpu.CompilerParams(
            dimension_semantics=("arbitrary",), vmem_limit_bytes=VMEM_LIMIT),
        name="combine",
    )(dest3, dest3, x1, mod, route, ln_g, ln_b, y_rows)


def _pad_in_proj(w_in):
    rank_end = 3 * D_MODEL + 2 * GLA_KEY_DIM + GLA_GATE_RANK
    pad = jnp.zeros((w_in.shape[0], RANK_PAD - GLA_GATE_RANK), w_in.dtype)
    return jnp.concatenate([w_in[:, :rank_end], pad, w_in[:, rank_end:]], axis=1)


def _layer(x, c, w_ada, b_ada, w_in, w_pool_group, pool_scale, w_branch_a, w_alpha_up, b_alpha,
           gla_norm_gain, w_branch_b, w_out, ln1_gain, ln1_bias, w_router, b_router,
           w_gate_up, b_gate_up, w_down, b_down, ln2_gain, ln2_bias):
    bsz, seq, d = x.shape
    n_tok = bsz * seq
    n_assign = n_tok * TOP_K
    row2 = lambda v: v.reshape(1, -1)

    mod = _ada(c, w_ada, b_ada).reshape(bsz, 6, d)

    w_in_p = _pad_in_proj(w_in).astype(BF16)
    w_al_p = jnp.concatenate(
        [w_alpha_up, jnp.zeros((RANK_PAD - GLA_GATE_RANK, GLA_KEY_DIM), w_alpha_up.dtype)], axis=0).astype(BF16)
    w_r_p = jnp.concatenate([w_router, jnp.zeros((d, LANES - N_EXPERTS), w_router.dtype)], axis=1)
    b_r_p = jnp.concatenate([b_router, jnp.full((LANES - N_EXPERTS,), NEG_BIG, b_router.dtype)]).reshape(1, LANES)
    x1, u2_rows, route, counts = _mixer(
        x, mod, w_in_p, w_pool_group.astype(BF16), row2(pool_scale), w_branch_a.astype(BF16), w_al_p,
        row2(b_alpha), row2(gla_norm_gain), w_branch_b.astype(BF16), w_out.astype(BF16), row2(ln1_gain),
        row2(ln1_bias), w_r_p, b_r_p)

    route = route.reshape(n_tok, LANES)
    top_idx = route[:, 0:TOP_K].astype(jnp.int32)
    rank = route[:, TOP_K:2 * TOP_K].astype(jnp.int32)
    counts = counts[0, :N_EXPERTS].astype(jnp.int32)
    padded = (counts + EXPERT_BLOCK - 1) // EXPERT_BLOCK * EXPERT_BLOCK
    pad_end = jnp.cumsum(padded)
    pad_start = pad_end - padded
    n_rows = (n_assign + N_EXPERTS * (EXPERT_BLOCK - 1) + EXPERT_BLOCK - 1) // EXPERT_BLOCK * EXPERT_BLOCK
    n_blocks = n_rows // EXPERT_BLOCK
    n_active = (pad_end[-1] // EXPERT_BLOCK).astype(jnp.int32)
    dest8 = ((pad_start[top_idx] + rank) * ROW_SUB).reshape(-1)
    blk_row = jnp.arange(n_blocks, dtype=jnp.int32)[:, None] * EXPERT_BLOCK
    block_expert = jnp.minimum(jnp.sum(pad_end[None, :] <= blk_row, axis=1), N_EXPERTS - 1).astype(jnp.int32)

    fill_start = jnp.concatenate([pad_start + counts, n_active.reshape(1)]).astype(jnp.int32)
    x_rows = _dispatch(fill_start, dest8, u2_rows, n_rows)
    f = w_down.shape[1]
    bg = b_gate_up[:, 0::2].reshape(N_EXPERTS, 1, f)
    bu = b_gate_up[:, 1::2].reshape(N_EXPERTS, 1, f)
    y_rows = _moe(block_expert, n_active.reshape(1), x_rows, w_gate_up, bg, bu, w_down,
                  b_down.reshape(N_EXPERTS, 1, d))
    out = _combine(dest8, x1.reshape(n_tok, d), mod, route, row2(ln2_gain), row2(ln2_bias), y_rows, seq)
    return out.reshape(bsz, seq, d)


def kernel(x, c, w_ada, b_ada, w_in, w_pool_group, pool_scale, w_branch_a, w_alpha_up, b_alpha, gla_norm_gain,
           w_branch_b, w_out, ln1_gain, ln1_bias, w_router, b_router, w_gate_up, b_gate_up, w_down, b_down,
           ln2_gain, ln2_bias):
    for l in range(DEPTH):
        x = _layer(x, c, w_ada[l], b_ada[l], w_in[l], w_pool_group[l], pool_scale[l], w_branch_a[l],
                   w_alpha_up[l], b_alpha[l], gla_norm_gain[l], w_branch_b[l], w_out[l], ln1_gain[l],
                   ln1_bias[l], w_router[l], b_router[l], w_gate_up[l], b_gate_up[l], w_down[l], b_down[l],
                   ln2_gain[l], ln2_bias[l])
    return x
```

```python
import jax
import jax.numpy as jnp
from jax import lax
from jax.experimental import pallas as pl
from jax.experimental.pallas import tpu as pltpu

D_MODEL = 1024
CHUNK = 64
SUB = 16
N_SUB = CHUNK // SUB
POOL_WINDOWS = (2, 4, 8, 16)
POOL_GROUP_WIDTH = D_MODEL // len(POOL_WINDOWS)
POOL_HALO = 16
GLA_HEADS = 4
GLA_KEY_DIM = D_MODEL // 2
GLA_HEAD_K = GLA_KEY_DIM // GLA_HEADS
GLA_HEAD_V = D_MODEL // GLA_HEADS
GLA_GATE_RANK = 16
GLA_TAU = 16.0
N_EXPERTS = 32
TOP_K = 4
SWIGLU_ALPHA = 1.702
SWIGLU_LIMIT = 7.0
EXPERT_BLOCK = 256
LN_EPS = 1e-5
RMS_EPS = 1e-6
DEPTH = 1
DEEPNORM_ALPHA = (2.0 * DEPTH) ** 0.25

LANES = 128
MXU_COLS = 256
RANK_PAD = LANES
_W = (D_MODEL, GLA_KEY_DIM, GLA_KEY_DIM, D_MODEL, D_MODEL, RANK_PAD, D_MODEL, D_MODEL)
_OFF = tuple(sum(_W[:i]) for i in range(len(_W) + 1))
EXP_CAP = 60.0
ROUTE_ROWS = 16

SEQ_TILE = 256
DISPATCH_TILE = 256
COMBINE_TILE = 256
DMA_UNROLL = 8
VMEM_LIMIT = 56 * 1024 * 1024

F32 = jnp.float32
BF16 = jnp.bfloat16
HI = lax.Precision.HIGHEST


def _dot(a, b):
    return jnp.dot(a, b, preferred_element_type=F32)


def _dot_nt(a, b):
    return lax.dot_general(a, b, (((1,), (1,)), ((), ())), preferred_element_type=F32)


def _dot_tn(a, b):
    return lax.dot_general(a, b, (((0,), (0,)), ((), ())), preferred_element_type=F32)


def _sigmoid(v):
    return 0.5 * jnp.tanh(0.5 * v) + 0.5


ROW_SUB = D_MODEL // LANES


def _load_rows(ref, n_rows, lead=()):
    return jnp.concatenate(
        [ref[lead + (pl.ds(j, n_rows, stride=ROW_SUB), slice(None))] for j in range(ROW_SUB)], axis=1)


def _store_rows(ref, val):
    for j in range(ROW_SUB):
        ref[pl.ds(j, val.shape[0], stride=ROW_SUB), :] = val[:, j * LANES:(j + 1) * LANES]


def _ada_kernel(c_ref, w_ref, b_ref, o_ref):
    c = c_ref[...]
    s = c * jax.nn.sigmoid(c)
    o_ref[...] = jnp.dot(s, w_ref[...], precision=HI, preferred_element_type=F32) + b_ref[...]


def _ada(c, w_ada, b_ada):
    bsz, d = c.shape
    n = w_ada.shape[1]
    tn = 1024
    return pl.pallas_call(
        _ada_kernel,
        grid=(n // tn,),
        in_specs=[
            pl.BlockSpec((bsz, d), lambda j: (0, 0)),
            pl.BlockSpec((d, tn), lambda j: (0, j)),
            pl.BlockSpec((1, tn), lambda j: (0, j)),
        ],
        out_specs=pl.BlockSpec((bsz, tn), lambda j: (0, j)),
        out_shape=jax.ShapeDtypeStruct((bsz, n), F32),
        name="ada",
    )(c, w_ada, b_ada.reshape(1, n))


def _layer_norm(z, gain, bias):
    mu = jnp.mean(z, axis=-1, keepdims=True)
    zc = z - mu
    var = jnp.mean(zc * zc, axis=-1, keepdims=True)
    return zc * lax.rsqrt(var + LN_EPS) * gain + bias


def _mixer_kernel(x_ref, mod_ref, w_in_ref, w_pool_ref, pool_scale_ref, w_a_ref, w_al_ref, b_al_ref,
                  gain_ref, w_b_ref, w_out_ref, ln_g_ref, ln_b_ref, w_r_ref, b_r_ref,
                  x1_ref, u2_ref, route_ref, counts_ref,
                  a_ext, s_ref, cnt_ref, o_ref):
    b_idx = pl.program_id(0)
    s_idx = pl.program_id(1)
    tile = x_ref.shape[1]

    @pl.when(s_idx == 0)
    def _():
        s_ref[...] = jnp.zeros_like(s_ref)
        a_ext[0:POOL_HALO, :] = jnp.zeros((POOL_HALO, D_MODEL), F32)

    @pl.when((b_idx == 0) & (s_idx == 0))
    def _():
        cnt_ref[...] = jnp.zeros_like(cnt_ref)

    mod = mod_ref[0]
    sh_m, sc_m, g_m = mod[0:1], mod[1:2], mod[2:3]
    sh_f, sc_f = mod[3:4], mod[4:5]
    x = x_ref[0]
    u = (x * (1.0 + sc_m) + sh_m).astype(BF16)

    def proj(i):
        return _dot(u, w_in_ref[:, _OFF[i]:_OFF[i + 1]])

    a = proj(0)
    a_ext[POOL_HALO:POOL_HALO + tile, :] = a
    t_glob = s_idx * tile + lax.broadcasted_iota(jnp.int32, (tile, 1), 0)
    mapped = []
    for g, w in enumerate(POOL_WINDOWS):
        lo, hi = g * POOL_GROUP_WIDTH, (g + 1) * POOL_GROUP_WIDTH
        win = a_ext[:, lo:hi]
        k = 1
        while k < w:
            win = win + pltpu.roll(win, k, 0)
            k *= 2
        inv_cnt = 1.0 / jnp.minimum(t_glob + 1, w).astype(F32)
        pooled = win[POOL_HALO:, :] * inv_cnt - a[:, lo:hi]
        mapped.append(_dot(pooled.astype(BF16), w_pool_ref[g]))
    a_ext[0:POOL_HALO, :] = a[tile - POOL_HALO:tile, :]
    ya = jnp.concatenate(mapped, axis=1) * pool_scale_ref[...]
    ya = _dot(ya.astype(BF16), w_a_ref[...])

    q = proj(1) * (GLA_HEAD_K ** -0.5)
    k_all = proj(2)
    v_all = proj(3)
    alpha_low = proj(5)
    z = _dot(alpha_low.astype(BF16), w_al_ref[...]) + b_al_ref[...]
    log_a = (jnp.minimum(z, 0.0) - jnp.log1p(jnp.exp(-jnp.abs(z)))) * (1.0 / GLA_TAU)

    ri = lax.broadcasted_iota(jnp.int32, (CHUNK, CHUNK), 0)
    ci = lax.broadcasted_iota(jnp.int32, (CHUNK, CHUNK), 1)
    causal = ci <= ri
    cum_mat = causal.astype(BF16)
    la_hi = log_a.astype(BF16)
    la_split = jnp.concatenate([la_hi, (log_a - la_hi.astype(F32)).astype(BF16)], axis=1)

    for c in range(tile // CHUNK):
        rows = slice(c * CHUNK, (c + 1) * CHUNK)
        cum = _dot(cum_mat, la_split[rows])
        b_cum = cum[:, :GLA_KEY_DIM] + cum[:, GLA_KEY_DIM:]
        ref_pts = [jnp.zeros((1, GLA_KEY_DIM), F32)] + [b_cum[i * SUB - 1:i * SUB, :] for i in range(1, N_SUB)]
        b_ref_pt = jnp.concatenate([jnp.broadcast_to(p, (SUB, GLA_KEY_DIM)) for p in ref_pts], axis=0)
        for h in range(GLA_HEADS):
            ks = slice(h * GLA_HEAD_K, (h + 1) * GLA_HEAD_K)
            vs = slice(h * GLA_HEAD_V, (h + 1) * GLA_HEAD_V)
            qh, kh, vh = q[rows, ks], k_all[rows, ks], v_all[rows, vs].astype(BF16)
            bh, rh = b_cum[:, ks], b_ref_pt[:, ks]
            b_last = bh[CHUNK - 1:CHUNK, :]
            q_dec = (qh * jnp.exp(bh - rh)).astype(BF16)
            k_dec = jnp.concatenate(
                [(kh * jnp.exp(jnp.minimum(ref_pts[i][:, ks] - bh, EXP_CAP))).astype(BF16)
                 for i in range(N_SUB)], axis=0)
            s_all = _dot_nt(q_dec, k_dec)
            scores = jnp.concatenate(
                [s_all[i * SUB:(i + 1) * SUB, i * CHUNK:(i + 1) * CHUNK] for i in range(N_SUB)], axis=0)
            scores = jnp.where(causal, scores, 0.0).astype(BF16)
            state_t = s_ref[h]
            o = _dot(scores, vh) + _dot_nt((qh * jnp.exp(bh)).astype(BF16), state_t.astype(BF16))
            k_carry = (kh * jnp.exp(b_last - bh)).astype(BF16)
            s_ref[h] = state_t * jnp.exp(b_last) + _dot_tn(vh, k_carry)
            o = o * lax.rsqrt(jnp.mean(o * o, axis=-1, keepdims=True) + RMS_EPS) * gain_ref[...]
            o_ref[rows, vs] = o

    r = proj(4)
    yb = _dot((o_ref[...] * (r * _sigmoid(r))).astype(BF16), w_b_ref[...])

    merged = _sigmoid(proj(6)) * ya + _sigmoid(proj(7)) * yb
    y = _dot(merged.astype(BF16), w_out_ref[...])
    x1 = _layer_norm(DEEPNORM_ALPHA * x + g_m * y, ln_g_ref[...], ln_b_ref[...])
    x1_ref[0] = x1
    u2 = x1 * (1.0 + sc_f) + sh_f
    _store_rows(u2_ref, u2)

    u2_hi = u2.astype(BF16)
    u2_lo = (u2 - u2_hi.astype(F32)).astype(BF16)
    part = _dot_nt(w_r_ref[...], u2_hi)
    logits = (part[:N_EXPERTS] + part[N_EXPERTS:] + _dot_nt(w_r_ref[0:N_EXPERTS, :], u2_lo)
              + b_r_ref[...])
    erow = lax.broadcasted_iota(jnp.int32, (N_EXPERTS, tile), 0).astype(F32)
    work = logits
    sel = jnp.zeros((N_EXPERTS, tile), F32)
    vals, hits = [], []
    for _ in range(TOP_K):
        m = jnp.max(work, axis=0, keepdims=True)
        idx = jnp.min(jnp.where(work == m, erow, float(N_EXPERTS)), axis=0, keepdims=True)
        hit = erow == idx
        vals.append(m)
        hits.append((idx, hit))
        sel = jnp.where(hit, 1.0, sel)
        work = jnp.where(hit, -jnp.inf, work)
    exps = [jnp.exp(v - vals[0]) for v in vals]
    inv_den = 1.0 / (exps[0] + exps[1] + exps[2] + exps[3])

    rt = lax.broadcasted_iota(jnp.int32, (tile, tile), 0)
    ct = lax.broadcasted_iota(jnp.int32, (tile, tile), 1)
    before = (rt < ct).astype(BF16)
    base = _dot(sel.astype(BF16), before) + cnt_ref[:, 0:1]
    orow = lax.broadcasted_iota(jnp.int32, (ROUTE_ROWS, tile), 0)
    route = jnp.zeros((ROUTE_ROWS, tile), F32)
    for kk in range(TOP_K):
        idx, hit = hits[kk]
        rank = jnp.sum(jnp.where(hit, base, 0.0), axis=0, keepdims=True)
        route = jnp.where(orow == kk, idx, route)
        route = jnp.where(orow == TOP_K + kk, rank, route)
        route = jnp.where(orow == 2 * TOP_K + kk, exps[kk] * inv_den, route)
    route_ref[...] = route
    cnt_ref[...] = cnt_ref[...] + jnp.sum(sel, axis=1, keepdims=True)
    counts_ref[...] = cnt_ref[...]


def _mixer(x, mod, w_in_p, w_pool, pool_scale, w_a, w_al_p, b_alpha, gain, w_b, w_out, ln_g, ln_b,
           w_r_p, b_r_p):
    bsz, seq, d = x.shape
    tile = SEQ_TILE
    n_s = seq // tile

    def const(shape):
        nd = len(shape)
        return pl.BlockSpec(shape, lambda b, s: (0,) * nd, pipeline_mode=pl.Buffered(1))

    tok = lambda width: pl.BlockSpec((1, tile, width), lambda b, s: (b, s, 0))
    return pl.pallas_call(
        _mixer_kernel,
        grid=(bsz, n_s),
        in_specs=[
            tok(d),
            pl.BlockSpec((1, 6, d), lambda b, s: (b, 0, 0)),
            const(w_in_p.shape), const(w_pool.shape), const(pool_scale.shape), const(w_a.shape),
            const(w_al_p.shape), const(b_alpha.shape), const(gain.shape), const(w_b.shape),
            const(w_out.shape), const(ln_g.shape), const(ln_b.shape), const(w_r_p.shape),
            const(b_r_p.shape),
        ],
        out_specs=[tok(d), pl.BlockSpec((tile * ROW_SUB, LANES), lambda b, s: (b * n_s + s, 0)),
                   pl.BlockSpec((ROUTE_ROWS, tile), lambda b, s: (0, b * n_s + s)),
                   pl.BlockSpec((N_EXPERTS, LANES), lambda b, s: (0, 0))],
        out_shape=[
            jax.ShapeDtypeStruct((bsz, seq, d), F32),
            jax.ShapeDtypeStruct((bsz * seq * ROW_SUB, LANES), F32),
            jax.ShapeDtypeStruct((ROUTE_ROWS, bsz * seq), F32),
            jax.ShapeDtypeStruct((N_EXPERTS, LANES), F32),
        ],
        scratch_shapes=[
            pltpu.VMEM((POOL_HALO + tile, d), F32),
            pltpu.VMEM((GLA_HEADS, GLA_HEAD_V, GLA_HEAD_K), F32),
            pltpu.VMEM((N_EXPERTS, LANES), F32),
            pltpu.VMEM((tile, d), F32),
        ],
        compiler_params=pltpu.CompilerParams(
            dimension_semantics=("arbitrary", "arbitrary"), vmem_limit_bytes=VMEM_LIMIT),
        name="mixer",
    )(x, mod, w_in_p, w_pool, pool_scale, w_a, w_al_p, b_alpha, gain, w_b, w_out, ln_g, ln_b,
      w_r_p, b_r_p)


def _row_at(ref, row8):
    return ref.at[pl.ds(pl.multiple_of(row8, ROW_SUB), ROW_SUB)]


def _dispatch_kernel(fill_ref, dest_ref, u2_ref, rows_hbm, zbuf, sem_fill, sem_rows):
    i = pl.program_id(0)
    tile = u2_ref.shape[0] // ROW_SUB
    blk8 = EXPERT_BLOCK * ROW_SUB

    @pl.when(i == 0)
    def _():
        zbuf[...] = jnp.zeros_like(zbuf)

        def fill(start):
            dst = rows_hbm.at[pl.ds(pl.multiple_of(start * ROW_SUB, ROW_SUB), blk8)]
            cp = pltpu.make_async_copy(zbuf, dst, sem_fill.at[0])
            cp.start()
            cp.wait()

        for e in range(N_EXPERTS):
            fill(fill_ref[e])

        def unused(blk, carry):
            fill(blk * EXPERT_BLOCK)
            return carry

        lax.fori_loop(fill_ref[N_EXPERTS], rows_hbm.shape[0] // blk8, unused, 0)

    def body(r, carry):
        for kk in range(TOP_K):
            pltpu.make_async_copy(_row_at(u2_ref, r * ROW_SUB), _row_at(rows_hbm, dest_ref[0, 0, r * TOP_K + kk]),
                                  sem_rows.at[0]).start(priority=kk % 2)
        return carry

    lax.fori_loop(0, tile, body, 0, unroll=DMA_UNROLL)
    for _ in range(TOP_K):
        pltpu.make_async_copy(u2_ref, rows_hbm.at[pl.ds(0, tile * ROW_SUB)], sem_rows.at[0]).wait()


def _dispatch(fill_start, dest8, u2_rows, n_rows):
    n_tok = u2_rows.shape[0] // ROW_SUB
    tile = DISPATCH_TILE
    n_t = n_tok // tile
    grid_spec = pltpu.PrefetchScalarGridSpec(
        num_scalar_prefetch=1,
        grid=(n_t,),
        in_specs=[
            pl.BlockSpec((1, 1, tile * TOP_K), lambda i, fs: (i, 0, 0), memory_space=pltpu.SMEM),
            pl.BlockSpec((tile * ROW_SUB, LANES), lambda i, fs: (i, 0)),
        ],
        out_specs=pl.BlockSpec(memory_space=pl.ANY),
        scratch_shapes=[
            pltpu.VMEM((EXPERT_BLOCK * ROW_SUB, LANES), F32),
            pltpu.SemaphoreType.DMA((1,)),
            pltpu.SemaphoreType.DMA((1,)),
        ],
    )
    return pl.pallas_call(
        _dispatch_kernel,
        grid_spec=grid_spec,
        out_shape=jax.ShapeDtypeStruct(((n_rows + EXPERT_BLOCK) * ROW_SUB, LANES), F32),
        compiler_params=pltpu.CompilerParams(dimension_semantics=("arbitrary",)),
        name="dispatch",
    )(fill_start, dest8.reshape(n_t, 1, tile * TOP_K), u2_rows)


def _moe_kernel(be_ref, nb_ref, x_ref, wgu_ref, bg_ref, bu_ref, wd_ref, bd_ref, y_ref,
                wg_s, wu_s, wd_s):
    i = pl.program_id(0)
    new_expert = (i == 0) | (be_ref[i] != be_ref[jnp.maximum(i - 1, 0)])

    @pl.when(new_expert & (i < nb_ref[0]))
    def _():
        src = lax.broadcasted_iota(jnp.int32, (MXU_COLS, MXU_COLS), 0)
        col = lax.broadcasted_iota(jnp.int32, (MXU_COLS, MXU_COLS), 1)
        half = MXU_COLS // 2
        want = jnp.where(col < half, 2 * col, 2 * (col - half) + 1)
        unzip = (src == want).astype(BF16)
        for g in range(wgu_ref.shape[2] // MXU_COLS):
            blk = wgu_ref[0, :, g * MXU_COLS:(g + 1) * MXU_COLS].astype(BF16)
            sep = _dot(blk, unzip)
            wg_s[:, g * half:(g + 1) * half] = sep[:, :half].astype(BF16)
            wu_s[:, g * half:(g + 1) * half] = sep[:, half:].astype(BF16)
        wd_s[...] = wd_ref[0].astype(BF16)

    @pl.when(i < nb_ref[0])
    def _():
        xb = _load_rows(x_ref, EXPERT_BLOCK).astype(BF16)
        gate = jnp.minimum(_dot(xb, wg_s[...]) + bg_ref[0], SWIGLU_LIMIT)
        up = jnp.clip(_dot(xb, wu_s[...]) + bu_ref[0], -SWIGLU_LIMIT, SWIGLU_LIMIT)
        glu = gate * _sigmoid(gate * SWIGLU_ALPHA)
        _store_rows(y_ref, _dot(((up + 1.0) * glu).astype(BF16), wd_s[...]) + bd_ref[0])

    @pl.when(i >= nb_ref[0])
    def _():
        y_ref[...] = jnp.zeros_like(y_ref)


def _moe(block_expert, n_active, x_rows, w_gate_up, bg, bu, w_down, bd):
    n_blocks = block_expert.shape[0]
    _, d, f2 = w_gate_up.shape
    f = f2 // 2
    blk8 = EXPERT_BLOCK * ROW_SUB
    rows_in = pl.BlockSpec((blk8, LANES), lambda i, be, nb: (jnp.maximum(jnp.minimum(i, nb[0] - 1), 0), 0))
    rows_out = pl.BlockSpec((blk8, LANES), lambda i, be, nb: (i, 0))
    per_expert = lambda shape: pl.BlockSpec((1,) + shape, lambda i, be, nb: (be[i], 0, 0))
    grid_spec = pltpu.PrefetchScalarGridSpec(
        num_scalar_prefetch=2,
        grid=(n_blocks,),
        in_specs=[rows_in, per_expert((d, f2)), per_expert((1, f)), per_expert((1, f)), per_expert((f, d)),
                  per_expert((1, d))],
        out_specs=rows_out,
        scratch_shapes=[pltpu.VMEM((d, f), BF16), pltpu.VMEM((d, f), BF16), pltpu.VMEM((f, d), BF16)],
    )
    return pl.pallas_call(
        _moe_kernel,
        grid_spec=grid_spec,
        out_shape=jax.ShapeDtypeStruct((n_blocks * blk8, LANES), F32),
        compiler_params=pltpu.CompilerParams(
            dimension_semantics=("arbitrary",), vmem_limit_bytes=VMEM_LIMIT),
        name="moe",
    )(block_expert, n_active, x_rows, w_gate_up, bg, bu, w_down, bd)


def _combine_kernel(dest_cur_ref, dest_nxt_ref, x1_ref, mod_ref, route_ref, ln_g_ref, ln_b_ref, y_hbm,
                    o_ref, ybuf, sem):
    i = pl.program_id(0)
    n_steps = pl.num_programs(0)
    tile = x1_ref.shape[0]
    slot = lax.rem(i, 2)

    def gather(dest_ref, s):
        def body(r, carry):
            for kk in range(TOP_K):
                pltpu.make_async_copy(_row_at(y_hbm, dest_ref[0, 0, r * TOP_K + kk]),
                                      _row_at(ybuf.at[s, kk], r * ROW_SUB), sem.at[s]).start(priority=kk % 2)
            return carry
        lax.fori_loop(0, tile, body, 0, unroll=DMA_UNROLL)

    @pl.when(i == 0)
    def _():
        gather(dest_cur_ref, 0)

    @pl.when(i + 1 < n_steps)
    def _():
        gather(dest_nxt_ref, 1 - slot)

    for kk in range(TOP_K):
        pltpu.make_async_copy(y_hbm.at[pl.ds(0, tile * ROW_SUB)], ybuf.at[slot, kk], sem.at[slot]).wait()

    g_f = mod_ref[0][5:6]
    weight = route_ref[...]
    y = jnp.zeros(x1_ref.shape, F32)
    for kk in range(TOP_K):
        y = y + weight[:, kk:kk + 1] * _load_rows(ybuf, tile, (slot, kk))
    o_ref[...] = _layer_norm(DEEPNORM_ALPHA * x1_ref[...] + g_f * y, ln_g_ref[...], ln_b_ref[...])


def _combine(dest8, x1, mod, route, ln_g, ln_b, y_rows, seq):
    n_tok, d = x1.shape
    tile = COMBINE_TILE
    n_t = n_tok // tile
    per_seq = seq // tile
    dest3 = dest8.reshape(n_t, 1, tile * TOP_K)
    smem_blk = lambda fn: pl.BlockSpec((1, 1, tile * TOP_K), fn, memory_space=pltpu.SMEM)
    row = lambda width: pl.BlockSpec((tile, width), lambda i: (i, 0))
    vec = pl.BlockSpec((1, d), lambda i: (0, 0))
    return pl.pallas_call(
        _combine_kernel,
        grid=(n_t,),
        in_specs=[smem_blk(lambda i: (i, 0, 0)), smem_blk(lambda i: (jnp.minimum(i + 1, n_t - 1), 0, 0)),
                  row(d), pl.BlockSpec((1, 6, d), lambda i: (i // per_seq, 0, 0)), row(TOP_K), vec, vec,
                  pl.BlockSpec(memory_space=pl.ANY)],
        out_specs=row(d),
        out_shape=jax.ShapeDtypeStruct((n_tok, d), F32),
        scratch_shapes=[pltpu.VMEM((2, TOP_K, tile * ROW_SUB, LANES), F32), pltpu.SemaphoreType.DMA((2,))],
        compiler_params=pltpu.CompilerParams(
            dimension_semantics=("arbitrary",), vmem_limit_bytes=VMEM_LIMIT),
        name="combine",
    )(dest3, dest3, x1, mod, route, ln_g, ln_b, y_rows)


def _pad_in_proj(w_in):
    rank_end = 3 * D_MODEL + 2 * GLA_KEY_DIM + GLA_GATE_RANK
    pad = jnp.zeros((w_in.shape[0], RANK_PAD - GLA_GATE_RANK), w_in.dtype)
    return jnp.concatenate([w_in[:, :rank_end], pad, w_in[:, rank_end:]], axis=1)


def _layer(x, c, w_ada, b_ada, w_in, w_pool_group, pool_scale, w_branch_a, w_alpha_up, b_alpha,
           gla_norm_gain, w_branch_b, w_out, ln1_gain, ln1_bias, w_router, b_router,
           w_gate_up, b_gate_up, w_down, b_down, ln2_gain, ln2_bias):
    bsz, seq, d = x.shape
    n_tok = bsz * seq
    n_assign = n_tok * TOP_K
    row2 = lambda v: v.reshape(1, -1)

    mod = _ada(c, w_ada, b_ada).reshape(bsz, 6, d)

    w_in_p = _pad_in_proj(w_in).astype(BF16)
    w_al_p = jnp.concatenate(
        [w_alpha_up, jnp.zeros((RANK_PAD - GLA_GATE_RANK, GLA_KEY_DIM), w_alpha_up.dtype)], axis=0).astype(BF16)
    w_r_t = w_router.T
    w_r_hi = w_r_t.astype(BF16)
    w_r_split = jnp.concatenate([w_r_hi, (w_r_t - w_r_hi.astype(F32)).astype(BF16)], axis=0)
    x1, u2_rows, route_t, counts = _mixer(
        x, mod, w_in_p, w_pool_group.astype(BF16), row2(pool_scale), w_branch_a.astype(BF16), w_al_p,
        row2(b_alpha), row2(gla_norm_gain), w_branch_b.astype(BF16), w_out.astype(BF16), row2(ln1_gain),
        row2(ln1_bias), w_r_split, b_router.reshape(N_EXPERTS, 1))

    top_idx = route_t[0:TOP_K].T.astype(jnp.int32)
    rank = route_t[TOP_K:2 * TOP_K].T.astype(jnp.int32)
    route = route_t[2 * TOP_K:3 * TOP_K].T
    counts = counts[:, 0].astype(jnp.int32)
    padded = (counts + EXPERT_BLOCK - 1) // EXPERT_BLOCK * EXPERT_BLOCK
    pad_end = jnp.cumsum(padded)
    pad_start = pad_end - padded
    n_rows = (n_assign + N_EXPERTS * (EXPERT_BLOCK - 1) + EXPERT_BLOCK - 1) // EXPERT_BLOCK * EXPERT_BLOCK
    n_blocks = n_rows // EXPERT_BLOCK
    n_active = (pad_end[-1] // EXPERT_BLOCK).astype(jnp.int32)
    dest8 = ((pad_start[top_idx] + rank) * ROW_SUB).reshape(-1)
    blk_row = jnp.arange(n_blocks, dtype=jnp.int32)[:, None] * EXPERT_BLOCK
    block_expert = jnp.minimum(jnp.sum(pad_end[None, :] <= blk_row, axis=1), N_EXPERTS - 1).astype(jnp.int32)

    fill_start = jnp.concatenate([pad_start + counts, n_active.reshape(1)]).astype(jnp.int32)
    x_rows = _dispatch(fill_start, dest8, u2_rows, n_rows)
    f = w_down.shape[1]
    bg = b_gate_up[:, 0::2].reshape(N_EXPERTS, 1, f)
    bu = b_gate_up[:, 1::2].reshape(N_EXPERTS, 1, f)
    y_rows = _moe(block_expert, n_active.reshape(1), x_rows, w_gate_up, bg, bu, w_down,
                  b_down.reshape(N_EXPERTS, 1, d))
    out = _combine(dest8, x1.reshape(n_tok, d), mod, route, row2(ln2_gain), row2(ln2_bias), y_rows, seq)
    return out.reshape(bsz, seq, d)


def kernel(x, c, w_ada, b_ada, w_in, w_pool_group, pool_scale, w_branch_a, w_alpha_up, b_alpha, gla_norm_gain,
           w_branch_b, w_out, ln1_gain, ln1_bias, w_router, b_router, w_gate_up, b_gate_up, w_down, b_down,
           ln2_gain, ln2_bias):
    for l in range(DEPTH):
        x = _layer(x, c, w_ada[l], b_ada[l], w_in[l], w_pool_group[l], pool_scale[l], w_branch_a[l],
                   w_alpha_up[l], b_alpha[l], gla_norm_gain[l], w_branch_b[l], w_out[l], ln1_gain[l],
                   ln1_bias[l], w_router[l], b_router[l], w_gate_up[l], b_gate_up[l], w_down[l], b_down[l],
                   ln2_gain[l], ln2_bias[l])
    return x
```

```python
import jax
import jax.numpy as jnp
from jax import lax
from jax.experimental import pallas as pl
from jax.experimental.pallas import tpu as pltpu

D_MODEL = 1024
CHUNK = 64
SUB = 16
N_SUB = CHUNK // SUB
POOL_WINDOWS = (2, 4, 8, 16)
POOL_GROUP_WIDTH = D_MODEL // len(POOL_WINDOWS)
POOL_HALO = 16
GLA_HEADS = 4
GLA_KEY_DIM = D_MODEL // 2
GLA_HEAD_K = GLA_KEY_DIM // GLA_HEADS
GLA_HEAD_V = D_MODEL // GLA_HEADS
GLA_GATE_RANK = 16
GLA_TAU = 16.0
N_EXPERTS = 32
TOP_K = 4
SWIGLU_ALPHA = 1.702
SWIGLU_LIMIT = 7.0
EXPERT_BLOCK = 256
LN_EPS = 1e-5
RMS_EPS = 1e-6
DEPTH = 1
DEEPNORM_ALPHA = (2.0 * DEPTH) ** 0.25

LANES = 128
MXU_COLS = 256
RANK_PAD = LANES
_W = (D_MODEL, GLA_KEY_DIM, GLA_KEY_DIM, D_MODEL, D_MODEL, RANK_PAD, D_MODEL, D_MODEL)
_OFF = tuple(sum(_W[:i]) for i in range(len(_W) + 1))
EXP_CAP = 60.0
ROUTE_ROWS = 16

SEQ_TILE = 512
DISPATCH_TILE = 256
COMBINE_TILE = 256
DMA_UNROLL = 8
VMEM_LIMIT = 56 * 1024 * 1024

F32 = jnp.float32
BF16 = jnp.bfloat16
HI = lax.Precision.HIGHEST


def _dot(a, b):
    return jnp.dot(a, b, preferred_element_type=F32)


def _dot_nt(a, b):
    return lax.dot_general(a, b, (((1,), (1,)), ((), ())), preferred_element_type=F32)


def _dot_tn(a, b):
    return lax.dot_general(a, b, (((0,), (0,)), ((), ())), preferred_element_type=F32)


def _sigmoid(v):
    return 0.5 * jnp.tanh(0.5 * v) + 0.5


ROW_SUB = D_MODEL // LANES


def _load_rows(ref, n_rows, lead=()):
    return jnp.concatenate(
        [ref[lead + (pl.ds(j, n_rows, stride=ROW_SUB), slice(None))] for j in range(ROW_SUB)], axis=1)


def _store_rows(ref, val):
    for j in range(ROW_SUB):
        ref[pl.ds(j, val.shape[0], stride=ROW_SUB), :] = val[:, j * LANES:(j + 1) * LANES]


def _ada_kernel(c_ref, w_ref, b_ref, o_ref):
    c = c_ref[...]
    s = c * jax.nn.sigmoid(c)
    o_ref[...] = jnp.dot(s, w_ref[...], precision=HI, preferred_element_type=F32) + b_ref[...]


def _ada(c, w_ada, b_ada):
    bsz, d = c.shape
    n = w_ada.shape[1]
    tn = 1024
    return pl.pallas_call(
        _ada_kernel,
        grid=(n // tn,),
        in_specs=[
            pl.BlockSpec((bsz, d), lambda j: (0, 0)),
            pl.BlockSpec((d, tn), lambda j: (0, j)),
            pl.BlockSpec((1, tn), lambda j: (0, j)),
        ],
        out_specs=pl.BlockSpec((bsz, tn), lambda j: (0, j)),
        out_shape=jax.ShapeDtypeStruct((bsz, n), F32),
        name="ada",
    )(c, w_ada, b_ada.reshape(1, n))


def _layer_norm(z, gain, bias):
    mu = jnp.mean(z, axis=-1, keepdims=True)
    zc = z - mu
    var = jnp.mean(zc * zc, axis=-1, keepdims=True)
    return zc * lax.rsqrt(var + LN_EPS) * gain + bias


def _mixer_kernel(x_ref, mod_ref, w_in_ref, w_pool_ref, pool_scale_ref, w_a_ref, w_al_ref, b_al_ref,
                  gain_ref, w_b_ref, w_out_ref, ln_g_ref, ln_b_ref, w_r_ref, b_r_ref,
                  x1_ref, u2_ref, route_ref, counts_ref,
                  a_ext, s_ref, cnt_ref, o_ref):
    b_idx = pl.program_id(0)
    s_idx = pl.program_id(1)
    tile = x_ref.shape[1]

    @pl.when(s_idx == 0)
    def _():
        s_ref[...] = jnp.zeros_like(s_ref)
        a_ext[0:POOL_HALO, :] = jnp.zeros((POOL_HALO, D_MODEL), F32)

    @pl.when((b_idx == 0) & (s_idx == 0))
    def _():
        cnt_ref[...] = jnp.zeros_like(cnt_ref)

    mod = mod_ref[0]
    sh_m, sc_m, g_m = mod[0:1], mod[1:2], mod[2:3]
    sh_f, sc_f = mod[3:4], mod[4:5]
    x = x_ref[0]
    u = (x * (1.0 + sc_m) + sh_m).astype(BF16)

    def proj(i):
        return _dot(u, w_in_ref[:, _OFF[i]:_OFF[i + 1]])

    a = proj(0)
    a_ext[POOL_HALO:POOL_HALO + tile, :] = a
    t_glob = s_idx * tile + lax.broadcasted_iota(jnp.int32, (tile, 1), 0)
    mapped = []
    for g, w in enumerate(POOL_WINDOWS):
        lo, hi = g * POOL_GROUP_WIDTH, (g + 1) * POOL_GROUP_WIDTH
        win = a_ext[:, lo:hi]
        k = 1
        while k < w:
            win = win + pltpu.roll(win, k, 0)
            k *= 2
        inv_cnt = 1.0 / jnp.minimum(t_glob + 1, w).astype(F32)
        pooled = win[POOL_HALO:, :] * inv_cnt - a[:, lo:hi]
        mapped.append(_dot(pooled.astype(BF16), w_pool_ref[g]))
    a_ext[0:POOL_HALO, :] = a[tile - POOL_HALO:tile, :]
    ya = jnp.concatenate(mapped, axis=1) * pool_scale_ref[...]
    ya = _dot(ya.astype(BF16), w_a_ref[...])

    q = proj(1) * (GLA_HEAD_K ** -0.5)
    k_all = proj(2)
    v_all = proj(3)
    alpha_low = proj(5)
    z = _dot(alpha_low.astype(BF16), w_al_ref[...]) + b_al_ref[...]
    log_a = (jnp.minimum(z, 0.0) - jnp.log1p(jnp.exp(-jnp.abs(z)))) * (1.0 / GLA_TAU)

    ri = lax.broadcasted_iota(jnp.int32, (CHUNK, CHUNK), 0)
    ci = lax.broadcasted_iota(jnp.int32, (CHUNK, CHUNK), 1)
    causal = ci <= ri
    cum_mat = causal.astype(BF16)
    la_hi = log_a.astype(BF16)
    la_split = jnp.concatenate([la_hi, (log_a - la_hi.astype(F32)).astype(BF16)], axis=1)

    n_chunks = tile // CHUNK
    pairs = [(c, h) for c in range(n_chunks) for h in range(GLA_HEADS)]
    rows_of = lambda c: slice(c * CHUNK, (c + 1) * CHUNK)
    ks_of = lambda h: slice(h * GLA_HEAD_K, (h + 1) * GLA_HEAD_K)
    vs_of = lambda h: slice(h * GLA_HEAD_V, (h + 1) * GLA_HEAD_V)

    b_cum, ref_pts, b_ref_pt = [], [], []
    for c in range(n_chunks):
        cum = _dot(cum_mat, la_split[rows_of(c)])
        b_cum.append(cum[:, :GLA_KEY_DIM] + cum[:, GLA_KEY_DIM:])
        ref_pts.append([jnp.zeros((1, GLA_KEY_DIM), F32)]
                       + [b_cum[c][i * SUB - 1:i * SUB, :] for i in range(1, N_SUB)])
        b_ref_pt.append(jnp.concatenate([jnp.broadcast_to(p, (SUB, GLA_KEY_DIM)) for p in ref_pts[c]], axis=0))

    s_all = {}
    for c, h in pairs:
        qh, kh, bh = q[rows_of(c), ks_of(h)], k_all[rows_of(c), ks_of(h)], b_cum[c][:, ks_of(h)]
        q_dec = (qh * jnp.exp(bh - b_ref_pt[c][:, ks_of(h)])).astype(BF16)
        k_dec = jnp.concatenate(
            [(kh * jnp.exp(jnp.minimum(ref_pts[c][i][:, ks_of(h)] - bh, EXP_CAP))).astype(BF16)
             for i in range(N_SUB)], axis=0)
        s_all[c, h] = _dot_nt(q_dec, k_dec)

    o_intra, kv, decay_last = {}, {}, {}
    for c, h in pairs:
        kh, bh = k_all[rows_of(c), ks_of(h)], b_cum[c][:, ks_of(h)]
        vh = v_all[rows_of(c), vs_of(h)].astype(BF16)
        scores = jnp.concatenate(
            [s_all[c, h][i * SUB:(i + 1) * SUB, i * CHUNK:(i + 1) * CHUNK] for i in range(N_SUB)], axis=0)
        o_intra[c, h] = _dot(jnp.where(causal, scores, 0.0).astype(BF16), vh)
        b_last = bh[CHUNK - 1:CHUNK, :]
        kv[c, h] = _dot_tn(vh, (kh * jnp.exp(b_last - bh)).astype(BF16))
        decay_last[c, h] = jnp.exp(b_last)

    o_inter = {}
    state_t = [s_ref[h] for h in range(GLA_HEADS)]
    for c, h in pairs:
        q_in = (q[rows_of(c), ks_of(h)] * jnp.exp(b_cum[c][:, ks_of(h)])).astype(BF16)
        o_inter[c, h] = _dot_nt(q_in, state_t[h].astype(BF16))
        state_t[h] = state_t[h] * decay_last[c, h] + kv[c, h]
    for h in range(GLA_HEADS):
        s_ref[h] = state_t[h]

    for c, h in pairs:
        o = o_intra[c, h] + o_inter[c, h]
        o = o * lax.rsqrt(jnp.mean(o * o, axis=-1, keepdims=True) + RMS_EPS) * gain_ref[...]
        o_ref[rows_of(c), vs_of(h)] = o

    r = proj(4)
    yb = _dot((o_ref[...] * (r * _sigmoid(r))).astype(BF16), w_b_ref[...])

    merged = _sigmoid(proj(6)) * ya + _sigmoid(proj(7)) * yb
    y = _dot(merged.astype(BF16), w_out_ref[...])
    x1 = _layer_norm(DEEPNORM_ALPHA * x + g_m * y, ln_g_ref[...], ln_b_ref[...])
    x1_ref[0] = x1
    u2 = x1 * (1.0 + sc_f) + sh_f
    _store_rows(u2_ref, u2)

    u2_hi = u2.astype(BF16)
    u2_lo = (u2 - u2_hi.astype(F32)).astype(BF16)
    part = _dot_nt(w_r_ref[...], u2_hi)
    logits = (part[:N_EXPERTS] + part[N_EXPERTS:] + _dot_nt(w_r_ref[0:N_EXPERTS, :], u2_lo)
              + b_r_ref[...])
    erow = lax.broadcasted_iota(jnp.int32, (N_EXPERTS, tile), 0).astype(F32)
    work = logits
    sel = jnp.zeros((N_EXPERTS, tile), F32)
    vals, hits = [], []
    for _ in range(TOP_K):
        m = jnp.max(work, axis=0, keepdims=True)
        idx = jnp.min(jnp.where(work == m, erow, float(N_EXPERTS)), axis=0, keepdims=True)
        hit = erow == idx
        vals.append(m)
        hits.append((idx, hit))
        sel = jnp.where(hit, 1.0, sel)
        work = jnp.where(hit, -jnp.inf, work)
    exps = [jnp.exp(v - vals[0]) for v in vals]
    inv_den = 1.0 / (exps[0] + exps[1] + exps[2] + exps[3])

    rt = lax.broadcasted_iota(jnp.int32, (tile, tile), 0)
    ct = lax.broadcasted_iota(jnp.int32, (tile, tile), 1)
    before = (rt < ct).astype(BF16)
    base = _dot(sel.astype(BF16), before) + cnt_ref[:, 0:1]
    orow = lax.broadcasted_iota(jnp.int32, (ROUTE_ROWS, tile), 0)
    route = jnp.zeros((ROUTE_ROWS, tile), F32)
    for kk in range(TOP_K):
        idx, hit = hits[kk]
        rank = jnp.sum(jnp.where(hit, base, 0.0), axis=0, keepdims=True)
        route = jnp.where(orow == kk, idx, route)
        route = jnp.where(orow == TOP_K + kk, rank, route)
        route = jnp.where(orow == 2 * TOP_K + kk, exps[kk] * inv_den, route)
    route_ref[...] = route
    cnt_ref[...] = cnt_ref[...] + jnp.sum(sel, axis=1, keepdims=True)
    counts_ref[...] = cnt_ref[...]


def _mixer(x, mod, w_in_p, w_pool, pool_scale, w_a, w_al_p, b_alpha, gain, w_b, w_out, ln_g, ln_b,
           w_r_p, b_r_p):
    bsz, seq, d = x.shape
    tile = SEQ_TILE
    n_s = seq // tile

    def const(shape):
        nd = len(shape)
        return pl.BlockSpec(shape, lambda b, s: (0,) * nd, pipeline_mode=pl.Buffered(1))

    tok = lambda width: pl.BlockSpec((1, tile, width), lambda b, s: (b, s, 0))
    return pl.pallas_call(
        _mixer_kernel,
        grid=(bsz, n_s),
        in_specs=[
            tok(d),
            pl.BlockSpec((1, 6, d), lambda b, s: (b, 0, 0)),
            const(w_in_p.shape), const(w_pool.shape), const(pool_scale.shape), const(w_a.shape),
            const(w_al_p.shape), const(b_alpha.shape), const(gain.shape), const(w_b.shape),
            const(w_out.shape), const(ln_g.shape), const(ln_b.shape), const(w_r_p.shape),
            const(b_r_p.shape),
        ],
        out_specs=[tok(d), pl.BlockSpec((tile * ROW_SUB, LANES), lambda b, s: (b * n_s + s, 0)),
                   pl.BlockSpec((ROUTE_ROWS, tile), lambda b, s: (0, b * n_s + s)),
                   pl.BlockSpec((N_EXPERTS, LANES), lambda b, s: (0, 0))],
        out_shape=[
            jax.ShapeDtypeStruct((bsz, seq, d), F32),
            jax.ShapeDtypeStruct((bsz * seq * ROW_SUB, LANES), F32),
            jax.ShapeDtypeStruct((ROUTE_ROWS, bsz * seq), F32),
            jax.ShapeDtypeStruct((N_EXPERTS, LANES), F32),
        ],
        scratch_shapes=[
            pltpu.VMEM((POOL_HALO + tile, d), F32),
            pltpu.VMEM((GLA_HEADS, GLA_HEAD_V, GLA_HEAD_K), F32),
            pltpu.VMEM((N_EXPERTS, LANES), F32),
            pltpu.VMEM((tile, d), F32),
        ],
        compiler_params=pltpu.CompilerParams(
            dimension_semantics=("arbitrary", "arbitrary"), vmem_limit_bytes=VMEM_LIMIT),
        name="mixer",
    )(x, mod, w_in_p, w_pool, pool_scale, w_a, w_al_p, b_alpha, gain, w_b, w_out, ln_g, ln_b,
      w_r_p, b_r_p)


def _row_at(ref, row8):
    return ref.at[pl.ds(pl.multiple_of(row8, ROW_SUB), ROW_SUB)]


def _dispatch_kernel(fill_ref, dest_ref, u2_ref, rows_hbm, zbuf, sem_fill, sem_rows):
    i = pl.program_id(0)
    tile = u2_ref.shape[0] // ROW_SUB
    blk8 = EXPERT_BLOCK * ROW_SUB

    @pl.when(i == 0)
    def _():
        zbuf[...] = jnp.zeros_like(zbuf)

        def fill(start):
            dst = rows_hbm.at[pl.ds(pl.multiple_of(start * ROW_SUB, ROW_SUB), blk8)]
            cp = pltpu.make_async_copy(zbuf, dst, sem_fill.at[0])
            cp.start()
            cp.wait()

        for e in range(N_EXPERTS):
            fill(fill_ref[e])

        def unused(blk, carry):
            fill(blk * EXPERT_BLOCK)
            return carry

        lax.fori_loop(fill_ref[N_EXPERTS], rows_hbm.shape[0] // blk8, unused, 0)

    def body(r, carry):
        for kk in range(TOP_K):
            pltpu.make_async_copy(_row_at(u2_ref, r * ROW_SUB), _row_at(rows_hbm, dest_ref[0, 0, r * TOP_K + kk]),
                                  sem_rows.at[0]).start(priority=kk % 2)
        return carry

    lax.fori_loop(0, tile, body, 0, unroll=DMA_UNROLL)
    for _ in range(TOP_K):
        pltpu.make_async_copy(u2_ref, rows_hbm.at[pl.ds(0, tile * ROW_SUB)], sem_rows.at[0]).wait()


def _dispatch(fill_start, dest8, u2_rows, n_rows):
    n_tok = u2_rows.shape[0] // ROW_SUB
    tile = DISPATCH_TILE
    n_t = n_tok // tile
    grid_spec = pltpu.PrefetchScalarGridSpec(
        num_scalar_prefetch=1,
        grid=(n_t,),
        in_specs=[
            pl.BlockSpec((1, 1, tile * TOP_K), lambda i, fs: (i, 0, 0), memory_space=pltpu.SMEM),
            pl.BlockSpec((tile * ROW_SUB, LANES), lambda i, fs: (i, 0)),
        ],
        out_specs=pl.BlockSpec(memory_space=pl.ANY),
        scratch_shapes=[
            pltpu.VMEM((EXPERT_BLOCK * ROW_SUB, LANES), F32),
            pltpu.SemaphoreType.DMA((1,)),
            pltpu.SemaphoreType.DMA((1,)),
        ],
    )
    return pl.pallas_call(
        _dispatch_kernel,
        grid_spec=grid_spec,
        out_shape=jax.ShapeDtypeStruct(((n_rows + EXPERT_BLOCK) * ROW_SUB, LANES), F32),
        compiler_params=pltpu.CompilerParams(dimension_semantics=("arbitrary",)),
        name="dispatch",
    )(fill_start, dest8.reshape(n_t, 1, tile * TOP_K), u2_rows)


def _moe_kernel(be_ref, nb_ref, x_ref, wgu_ref, bg_ref, bu_ref, wd_ref, bd_ref, y_ref,
                wg_s, wu_s, wd_s):
    i = pl.program_id(0)
    new_expert = (i == 0) | (be_ref[i] != be_ref[jnp.maximum(i - 1, 0)])

    @pl.when(new_expert & (i < nb_ref[0]))
    def _():
        src = lax.broadcasted_iota(jnp.int32, (MXU_COLS, MXU_COLS), 0)
        col = lax.broadcasted_iota(jnp.int32, (MXU_COLS, MXU_COLS), 1)
        half = MXU_COLS // 2
        want = jnp.where(col < half, 2 * col, 2 * (col - half) + 1)
        unzip = (src == want).astype(BF16)
        for g in range(wgu_ref.shape[2] // MXU_COLS):
            blk = wgu_ref[0, :, g * MXU_COLS:(g + 1) * MXU_COLS].astype(BF16)
            sep = _dot(blk, unzip)
            wg_s[:, g * half:(g + 1) * half] = sep[:, :half].astype(BF16)
            wu_s[:, g * half:(g + 1) * half] = sep[:, half:].astype(BF16)
        wd_s[...] = wd_ref[0].astype(BF16)

    @pl.when(i < nb_ref[0])
    def _():
        xb = _load_rows(x_ref, EXPERT_BLOCK).astype(BF16)
        gate = jnp.minimum(_dot(xb, wg_s[...]) + bg_ref[0], SWIGLU_LIMIT)
        up = jnp.clip(_dot(xb, wu_s[...]) + bu_ref[0], -SWIGLU_LIMIT, SWIGLU_LIMIT)
        glu = gate * _sigmoid(gate * SWIGLU_ALPHA)
        _store_rows(y_ref, _dot(((up + 1.0) * glu).astype(BF16), wd_s[...]) + bd_ref[0])

    @pl.when(i >= nb_ref[0])
    def _():
        y_ref[...] = jnp.zeros_like(y_ref)


def _moe(block_expert, n_active, x_rows, w_gate_up, bg, bu, w_down, bd):
    n_blocks = block_expert.shape[0]
    _, d, f2 = w_gate_up.shape
    f = f2 // 2
    blk8 = EXPERT_BLOCK * ROW_SUB
    rows_in = pl.BlockSpec((blk8, LANES), lambda i, be, nb: (jnp.maximum(jnp.minimum(i, nb[0] - 1), 0), 0))
    rows_out = pl.BlockSpec((blk8, LANES), lambda i, be, nb: (i, 0))
    per_expert = lambda shape: pl.BlockSpec((1,) + shape, lambda i, be, nb: (be[i], 0, 0))
    grid_spec = pltpu.PrefetchScalarGridSpec(
        num_scalar_prefetch=2,
        grid=(n_blocks,),
        in_specs=[rows_in, per_expert((d, f2)), per_expert((1, f)), per_expert((1, f)), per_expert((f, d)),
                  per_expert((1, d))],
        out_specs=rows_out,
        scratch_shapes=[pltpu.VMEM((d, f), BF16), pltpu.VMEM((d, f), BF16), pltpu.VMEM((f, d), BF16)],
    )
    return pl.pallas_call(
        _moe_kernel,
        grid_spec=grid_spec,
        out_shape=jax.ShapeDtypeStruct((n_blocks * blk8, LANES), F32),
        compiler_params=pltpu.CompilerParams(
            dimension_semantics=("arbitrary",), vmem_limit_bytes=VMEM_LIMIT),
        name="moe",
    )(block_expert, n_active, x_rows, w_gate_up, bg, bu, w_down, bd)


def _combine_kernel(dest_cur_ref, dest_nxt_ref, x1_ref, mod_ref, route_ref, ln_g_ref, ln_b_ref, y_hbm,
                    o_ref, ybuf, sem):
    i = pl.program_id(0)
    n_steps = pl.num_programs(0)
    tile = x1_ref.shape[0]
    slot = lax.rem(i, 2)

    def gather(dest_ref, s):
        def body(r, carry):
            for kk in range(TOP_K):
                pltpu.make_async_copy(_row_at(y_hbm, dest_ref[0, 0, r * TOP_K + kk]),
                                      _row_at(ybuf.at[s, kk], r * ROW_SUB), sem.at[s]).start(priority=kk % 2)
            return carry
        lax.fori_loop(0, tile, body, 0, unroll=DMA_UNROLL)

    @pl.when(i == 0)
    def _():
        gather(dest_cur_ref, 0)

    @pl.when(i + 1 < n_steps)
    def _():
        gather(dest_nxt_ref, 1 - slot)

    for kk in range(TOP_K):
        pltpu.make_async_copy(y_hbm.at[pl.ds(0, tile * ROW_SUB)], ybuf.at[slot, kk], sem.at[slot]).wait()

    g_f = mod_ref[0][5:6]
    weight = route_ref[...]
    y = jnp.zeros(x1_ref.shape, F32)
    for kk in range(TOP_K):
        y = y + weight[:, kk:kk + 1] * _load_rows(ybuf, tile, (slot, kk))
    o_ref[...] = _layer_norm(DEEPNORM_ALPHA * x1_ref[...] + g_f * y, ln_g_ref[...], ln_b_ref[...])


def _combine(dest8, x1, mod, route, ln_g, ln_b, y_rows, seq):
    n_tok, d = x1.shape
    tile = COMBINE_TILE
    n_t = n_tok // tile
    per_seq = seq // tile
    dest3 = dest8.reshape(n_t, 1, tile * TOP_K)
    smem_blk = lambda fn: pl.BlockSpec((1, 1, tile * TOP_K), fn, memory_space=pltpu.SMEM)
    row = lambda width: pl.BlockSpec((tile, width), lambda i: (i, 0))
    vec = pl.BlockSpec((1, d), lambda i: (0, 0))
    return pl.pallas_call(
        _combine_kernel,
        grid=(n_t,),
        in_specs=[smem_blk(lambda i: (i, 0, 0)), smem_blk(lambda i: (jnp.minimum(i + 1, n_t - 1), 0, 0)),
                  row(d), pl.BlockSpec((1, 6, d), lambda i: (i // per_seq, 0, 0)), row(TOP_K), vec, vec,
                  pl.BlockSpec(memory_space=pl.ANY)],
        out_specs=row(d),
        out_shape=jax.ShapeDtypeStruct((n_tok, d), F32),
        scratch_shapes=[pltpu.VMEM((2, TOP_K, tile * ROW_SUB, LANES), F32), pltpu.SemaphoreType.DMA((2,))],
        compiler_params=pltpu.CompilerParams(
            dimension_semantics=("arbitrary",), vmem_limit_bytes=VMEM_LIMIT),
        name="combine",
    )(dest3, dest3, x1, mod, route, ln_g, ln_b, y_rows)


def _pad_in_proj(w_in):
    rank_end = 3 * D_MODEL + 2 * GLA_KEY_DIM + GLA_GATE_RANK
    pad = jnp.zeros((w_in.shape[0], RANK_PAD - GLA_GATE_RANK), w_in.dtype)
    return jnp.concatenate([w_in[:, :rank_end], pad, w_in[:, rank_end:]], axis=1)


def _layer(x, c, w_ada, b_ada, w_in, w_pool_group, pool_scale, w_branch_a, w_alpha_up, b_alpha,
           gla_norm_gain, w_branch_b, w_out, ln1_gain, ln1_bias, w_router, b_router,
           w_gate_up, b_gate_up, w_down, b_down, ln2_gain, ln2_bias):
    bsz, seq, d = x.shape
    n_tok = bsz * seq
    n_assign = n_tok * TOP_K
    row2 = lambda v: v.reshape(1, -1)

    mod = _ada(c, w_ada, b_ada).reshape(bsz, 6, d)

    w_in_p = _pad_in_proj(w_in).astype(BF16)
    w_al_p = jnp.concatenate(
        [w_alpha_up, jnp.zeros((RANK_PAD - GLA_GATE_RANK, GLA_KEY_DIM), w_alpha_up.dtype)], axis=0).astype(BF16)
    w_r_t = w_router.T
    w_r_hi = w_r_t.astype(BF16)
    w_r_split = jnp.concatenate([w_r_hi, (w_r_t - w_r_hi.astype(F32)).astype(BF16)], axis=0)
    x1, u2_rows, route_t, counts = _mixer(
        x, mod, w_in_p, w_pool_group.astype(BF16), row2(pool_scale), w_branch_a.astype(BF16), w_al_p,
        row2(b_alpha), row2(gla_norm_gain), w_branch_b.astype(BF16), w_out.astype(BF16), row2(ln1_gain),
        row2(ln1_bias), w_r_split, b_router.reshape(N_EXPERTS, 1))

    top_idx = route_t[0:TOP_K].T.astype(jnp.int32)
    rank = route_t[TOP_K:2 * TOP_K].T.astype(jnp.int32)
    route = route_t[2 * TOP_K:3 * TOP_K].T
    counts = counts[:, 0].astype(jnp.int32)
    padded = (counts + EXPERT_BLOCK - 1) // EXPERT_BLOCK * EXPERT_BLOCK
    pad_end = jnp.cumsum(padded)
    pad_start = pad_end - padded
    n_rows = (n_assign + N_EXPERTS * (EXPERT_BLOCK - 1) + EXPERT_BLOCK - 1) // EXPERT_BLOCK * EXPERT_BLOCK
    n_blocks = n_rows // EXPERT_BLOCK
    n_active = (pad_end[-1] // EXPERT_BLOCK).astype(jnp.int32)
    dest8 = ((pad_start[top_idx] + rank) * ROW_SUB).reshape(-1)
    blk_row = jnp.arange(n_blocks, dtype=jnp.int32)[:, None] * EXPERT_BLOCK
    block_expert = jnp.minimum(jnp.sum(pad_end[None, :] <= blk_row, axis=1), N_EXPERTS - 1).astype(jnp.int32)

    fill_start = jnp.concatenate([pad_start + counts, n_active.reshape(1)]).astype(jnp.int32)
    x_rows = _dispatch(fill_start, dest8, u2_rows, n_rows)
    f = w_down.shape[1]
    bg = b_gate_up[:, 0::2].reshape(N_EXPERTS, 1, f)
    bu = b_gate_up[:, 1::2].reshape(N_EXPERTS, 1, f)
    y_rows = _moe(block_expert, n_active.reshape(1), x_rows, w_gate_up, bg, bu, w_down,
                  b_down.reshape(N_EXPERTS, 1, d))
    out = _combine(dest8, x1.reshape(n_tok, d), mod, route, row2(ln2_gain), row2(ln2_bias), y_rows, seq)
    return out.reshape(bsz, seq, d)


def kernel(x, c, w_ada, b_ada, w_in, w_pool_group, pool_scale, w_branch_a, w_alpha_up, b_alpha, gla_norm_gain,
           w_branch_b, w_out, ln1_gain, ln1_bias, w_router, b_router, w_gate_up, b_gate_up, w_down, b_down,
           ln2_gain, ln2_bias):
    for l in range(DEPTH):
        x = _layer(x, c, w_ada[l], b_ada[l], w_in[l], w_pool_group[l], pool_scale[l], w_branch_a[l],
                   w_alpha_up[l], b_alpha[l], gla_norm_gain[l], w_branch_b[l], w_out[l], ln1_gain[l],
                   ln1_bias[l], w_router[l], b_router[l], w_gate_up[l], b_gate_up[l], w_down[l], b_down[l],
                   ln2_gain[l], ln2_bias[l])
    return x
```

```python
import jax
import jax.numpy as jnp
from jax import lax
from jax.experimental import pallas as pl
from jax.experimental.pallas import tpu as pltpu

D_MODEL = 1024
CHUNK = 64
SUB = 16
N_SUB = CHUNK // SUB
POOL_WINDOWS = (2, 4, 8, 16)
POOL_GROUP_WIDTH = D_MODEL // len(POOL_WINDOWS)
POOL_HALO = 16
GLA_HEADS = 4
GLA_KEY_DIM = D_MODEL // 2
GLA_HEAD_K = GLA_KEY_DIM // GLA_HEADS
GLA_HEAD_V = D_MODEL // GLA_HEADS
GLA_GATE_RANK = 16
GLA_TAU = 16.0
N_EXPERTS = 32
TOP_K = 4
SWIGLU_ALPHA = 1.702
SWIGLU_LIMIT = 7.0
EXPERT_BLOCK = 256
LN_EPS = 1e-5
RMS_EPS = 1e-6
DEPTH = 1
DEEPNORM_ALPHA = (2.0 * DEPTH) ** 0.25

LANES = 128
MXU_COLS = 256
RANK_PAD = LANES
_W = (D_MODEL, GLA_KEY_DIM, GLA_KEY_DIM, D_MODEL, D_MODEL, RANK_PAD, D_MODEL, D_MODEL)
_OFF = tuple(sum(_W[:i]) for i in range(len(_W) + 1))
EXP_CAP = 60.0
ROUTE_ROWS = 16

SEQ_TILE = 512
DISPATCH_TILE = 256
COMBINE_TILE = 256
DMA_UNROLL = 8
VMEM_LIMIT = 56 * 1024 * 1024

F32 = jnp.float32
BF16 = jnp.bfloat16
HI = lax.Precision.HIGHEST


def _dot(a, b):
    return jnp.dot(a, b, preferred_element_type=F32)


def _dot_nt(a, b):
    return lax.dot_general(a, b, (((1,), (1,)), ((), ())), preferred_element_type=F32)


def _dot_tn(a, b):
    return lax.dot_general(a, b, (((0,), (0,)), ((), ())), preferred_element_type=F32)


def _sigmoid(v):
    return 0.5 * jnp.tanh(0.5 * v) + 0.5


ROW_SUB = D_MODEL // LANES


def _load_rows(ref, n_rows, lead=()):
    return jnp.concatenate(
        [ref[lead + (pl.ds(j, n_rows, stride=ROW_SUB), slice(None))] for j in range(ROW_SUB)], axis=1)


def _store_rows(ref, val):
    for j in range(ROW_SUB):
        ref[pl.ds(j, val.shape[0], stride=ROW_SUB), :] = val[:, j * LANES:(j + 1) * LANES]


def _ada_kernel(c_ref, w_ref, b_ref, o_ref):
    c = c_ref[...]
    s = c * jax.nn.sigmoid(c)
    o_ref[...] = jnp.dot(s, w_ref[...], precision=HI, preferred_element_type=F32) + b_ref[...]


def _ada(c, w_ada, b_ada):
    bsz, d = c.shape
    n = w_ada.shape[1]
    tn = 1024
    return pl.pallas_call(
        _ada_kernel,
        grid=(n // tn,),
        in_specs=[
            pl.BlockSpec((bsz, d), lambda j: (0, 0)),
            pl.BlockSpec((d, tn), lambda j: (0, j)),
            pl.BlockSpec((1, tn), lambda j: (0, j)),
        ],
        out_specs=pl.BlockSpec((bsz, tn), lambda j: (0, j)),
        out_shape=jax.ShapeDtypeStruct((bsz, n), F32),
        name="ada",
    )(c, w_ada, b_ada.reshape(1, n))


def _layer_norm(z, gain, bias):
    mu = jnp.mean(z, axis=-1, keepdims=True)
    zc = z - mu
    var = jnp.mean(zc * zc, axis=-1, keepdims=True)
    return zc * lax.rsqrt(var + LN_EPS) * gain + bias


def _mixer_kernel(x_ref, mod_ref, w_in_ref, w_pool_ref, pool_scale_ref, w_a_ref, w_al_ref, b_al_ref,
                  gain_ref, w_b_ref, w_out_ref, ln_g_ref, ln_b_ref, w_r_ref, b_r_ref,
                  x1_ref, u2_ref, route_ref, counts_ref,
                  a_ext, s_ref, cnt_ref, o_ref):
    b_idx = pl.program_id(0)
    s_idx = pl.program_id(1)
    tile = x_ref.shape[1]

    @pl.when(s_idx == 0)
    def _():
        s_ref[...] = jnp.zeros_like(s_ref)
        a_ext[0:POOL_HALO, :] = jnp.zeros((POOL_HALO, D_MODEL), F32)

    @pl.when((b_idx == 0) & (s_idx == 0))
    def _():
        cnt_ref[...] = jnp.zeros_like(cnt_ref)

    mod = mod_ref[0]
    sh_m, sc_m, g_m = mod[0:1], mod[1:2], mod[2:3]
    sh_f, sc_f = mod[3:4], mod[4:5]
    x = x_ref[0]
    u = (x * (1.0 + sc_m) + sh_m).astype(BF16)

    def proj(i):
        return _dot(u, w_in_ref[:, _OFF[i]:_OFF[i + 1]])

    a = proj(0)
    a_ext[POOL_HALO:POOL_HALO + tile, :] = a
    t_glob = s_idx * tile + lax.broadcasted_iota(jnp.int32, (tile, 1), 0)
    mapped = []
    for g, w in enumerate(POOL_WINDOWS):
        lo, hi = g * POOL_GROUP_WIDTH, (g + 1) * POOL_GROUP_WIDTH
        win = a_ext[:, lo:hi]
        k = 1
        while k < w:
            win = win + pltpu.roll(win, k, 0)
            k *= 2
        inv_cnt = 1.0 / jnp.minimum(t_glob + 1, w).astype(F32)
        pooled = win[POOL_HALO:, :] * inv_cnt - a[:, lo:hi]
        mapped.append(_dot(pooled.astype(BF16), w_pool_ref[g]))
    a_ext[0:POOL_HALO, :] = a[tile - POOL_HALO:tile, :]
    ya = jnp.concatenate(mapped, axis=1) * pool_scale_ref[...]
    ya = _dot(ya.astype(BF16), w_a_ref[...])

    q = proj(1) * (GLA_HEAD_K ** -0.5)
    k_all = proj(2)
    v_all = proj(3)
    alpha_low = proj(5)
    z = _dot(alpha_low.astype(BF16), w_al_ref[...]) + b_al_ref[...]
    log_a = (jnp.minimum(z, 0.0) - jnp.log1p(jnp.exp(-jnp.abs(z)))) * (1.0 / GLA_TAU)

    ri = lax.broadcasted_iota(jnp.int32, (CHUNK, CHUNK), 0)
    ci = lax.broadcasted_iota(jnp.int32, (CHUNK, CHUNK), 1)
    causal = ci <= ri
    cum_mat = causal.astype(BF16)
    la_hi = log_a.astype(BF16)
    la_split = jnp.concatenate([la_hi, (log_a - la_hi.astype(F32)).astype(BF16)], axis=1)

    n_chunks = tile // CHUNK
    pairs = [(c, h) for c in range(n_chunks) for h in range(GLA_HEADS)]
    rows_of = lambda c: slice(c * CHUNK, (c + 1) * CHUNK)
    ks_of = lambda h: slice(h * GLA_HEAD_K, (h + 1) * GLA_HEAD_K)
    vs_of = lambda h: slice(h * GLA_HEAD_V, (h + 1) * GLA_HEAD_V)

    b_cum, ref_pts, b_ref_pt = [], [], []
    for c in range(n_chunks):
        cum = _dot(cum_mat, la_split[rows_of(c)])
        b_cum.append(cum[:, :GLA_KEY_DIM] + cum[:, GLA_KEY_DIM:])
        ref_pts.append([jnp.zeros((1, GLA_KEY_DIM), F32)]
                       + [b_cum[c][i * SUB - 1:i * SUB, :] for i in range(1, N_SUB)])
        b_ref_pt.append(jnp.concatenate([jnp.broadcast_to(p, (SUB, GLA_KEY_DIM)) for p in ref_pts[c]], axis=0))

    s_all = {}
    for c, h in pairs:
        qh, kh, bh = q[rows_of(c), ks_of(h)], k_all[rows_of(c), ks_of(h)], b_cum[c][:, ks_of(h)]
        q_dec = (qh * jnp.exp(bh - b_ref_pt[c][:, ks_of(h)])).astype(BF16)
        k_dec = jnp.concatenate(
            [(kh * jnp.exp(jnp.minimum(ref_pts[c][i][:, ks_of(h)] - bh, EXP_CAP))).astype(BF16)
             for i in range(N_SUB)], axis=0)
        s_all[c, h] = _dot_nt(q_dec, k_dec)

    o_intra, kv, decay_last = {}, {}, {}
    for c, h in pairs:
        kh, bh = k_all[rows_of(c), ks_of(h)], b_cum[c][:, ks_of(h)]
        vh = v_all[rows_of(c), vs_of(h)].astype(BF16)
        scores = jnp.concatenate(
            [s_all[c, h][i * SUB:(i + 1) * SUB, i * CHUNK:(i + 1) * CHUNK] for i in range(N_SUB)], axis=0)
        o_intra[c, h] = _dot(jnp.where(causal, scores, 0.0).astype(BF16), vh)
        b_last = bh[CHUNK - 1:CHUNK, :]
        kv[c, h] = _dot_tn(vh, (kh * jnp.exp(b_last - bh)).astype(BF16))
        decay_last[c, h] = jnp.exp(b_last)

    o_inter = {}
    state_t = [s_ref[h] for h in range(GLA_HEADS)]
    for c, h in pairs:
        q_in = (q[rows_of(c), ks_of(h)] * jnp.exp(b_cum[c][:, ks_of(h)])).astype(BF16)
        o_inter[c, h] = _dot_nt(q_in, state_t[h].astype(BF16))
        state_t[h] = state_t[h] * decay_last[c, h] + kv[c, h]
    for h in range(GLA_HEADS):
        s_ref[h] = state_t[h]

    for c, h in pairs:
        o = o_intra[c, h] + o_inter[c, h]
        o = o * lax.rsqrt(jnp.mean(o * o, axis=-1, keepdims=True) + RMS_EPS) * gain_ref[...]
        o_ref[rows_of(c), vs_of(h)] = o

    r = proj(4)
    yb = _dot((o_ref[...] * (r * _sigmoid(r))).astype(BF16), w_b_ref[...])

    merged = _sigmoid(proj(6)) * ya + _sigmoid(proj(7)) * yb
    y = _dot(merged.astype(BF16), w_out_ref[...])
    x1 = _layer_norm(DEEPNORM_ALPHA * x + g_m * y, ln_g_ref[...], ln_b_ref[...])
    x1_ref[0] = x1
    u2 = x1 * (1.0 + sc_f) + sh_f
    _store_rows(u2_ref, u2)

    u2_hi = u2.astype(BF16)
    u2_lo = (u2 - u2_hi.astype(F32)).astype(BF16)
    part = _dot_nt(w_r_ref[...], u2_hi)
    logits = (part[:N_EXPERTS] + part[N_EXPERTS:] + _dot_nt(w_r_ref[0:N_EXPERTS, :], u2_lo)
              + b_r_ref[...])
    erow = lax.broadcasted_iota(jnp.int32, (N_EXPERTS, tile), 0).astype(F32)
    work = logits
    sel = jnp.zeros((N_EXPERTS, tile), F32)
    vals, hits = [], []
    for _ in range(TOP_K):
        m = jnp.max(work, axis=0, keepdims=True)
        idx = jnp.min(jnp.where(work == m, erow, float(N_EXPERTS)), axis=0, keepdims=True)
        hit = erow == idx
        vals.append(m)
        hits.append((idx, hit))
        sel = jnp.where(hit, 1.0, sel)
        work = jnp.where(hit, -jnp.inf, work)
    exps = [jnp.exp(v - vals[0]) for v in vals]
    inv_den = 1.0 / (exps[0] + exps[1] + exps[2] + exps[3])

    rt = lax.broadcasted_iota(jnp.int32, (tile, tile), 0)
    ct = lax.broadcasted_iota(jnp.int32, (tile, tile), 1)
    before = (rt < ct).astype(BF16)
    base = _dot(sel.astype(BF16), before) + cnt_ref[:, 0:1]
    orow = lax.broadcasted_iota(jnp.int32, (ROUTE_ROWS, tile), 0)
    route = jnp.zeros((ROUTE_ROWS, tile), F32)
    for kk in range(TOP_K):
        idx, hit = hits[kk]
        rank = jnp.sum(jnp.where(hit, base, 0.0), axis=0, keepdims=True)
        route = jnp.where(orow == kk, idx, route)
        route = jnp.where(orow == TOP_K + kk, rank, route)
        route = jnp.where(orow == 2 * TOP_K + kk, exps[kk] * inv_den, route)
    route_ref[...] = route
    cnt_ref[...] = cnt_ref[...] + jnp.sum(sel, axis=1, keepdims=True)
    counts_ref[...] = cnt_ref[...]


def _mixer(x, mod, w_in_p, w_pool, pool_scale, w_a, w_al_p, b_alpha, gain, w_b, w_out, ln_g, ln_b,
           w_r_p, b_r_p):
    bsz, seq, d = x.shape
    tile = SEQ_TILE
    n_s = seq // tile

    def const(shape):
        nd = len(shape)
        return pl.BlockSpec(shape, lambda b, s: (0,) * nd, pipeline_mode=pl.Buffered(1))

    tok = lambda width: pl.BlockSpec((1, tile, width), lambda b, s: (b, s, 0))
    return pl.pallas_call(
        _mixer_kernel,
        grid=(bsz, n_s),
        in_specs=[
            tok(d),
            pl.BlockSpec((1, 6, d), lambda b, s: (b, 0, 0)),
            const(w_in_p.shape), const(w_pool.shape), const(pool_scale.shape), const(w_a.shape),
            const(w_al_p.shape), const(b_alpha.shape), const(gain.shape), const(w_b.shape),
            const(w_out.shape), const(ln_g.shape), const(ln_b.shape), const(w_r_p.shape),
            const(b_r_p.shape),
        ],
        out_specs=[tok(d), pl.BlockSpec((tile * ROW_SUB, LANES), lambda b, s: (b * n_s + s, 0)),
                   pl.BlockSpec((ROUTE_ROWS, tile), lambda b, s: (0, b * n_s + s)),
                   pl.BlockSpec((N_EXPERTS, LANES), lambda b, s: (0, 0))],
        out_shape=[
            jax.ShapeDtypeStruct((bsz, seq, d), F32),
            jax.ShapeDtypeStruct((bsz * seq * ROW_SUB, LANES), F32),
            jax.ShapeDtypeStruct((ROUTE_ROWS, bsz * seq), F32),
            jax.ShapeDtypeStruct((N_EXPERTS, LANES), F32),
        ],
        scratch_shapes=[
            pltpu.VMEM((POOL_HALO + tile, d), F32),
            pltpu.VMEM((GLA_HEADS, GLA_HEAD_V, GLA_HEAD_K), F32),
            pltpu.VMEM((N_EXPERTS, LANES), F32),
            pltpu.VMEM((tile, d), F32),
        ],
        compiler_params=pltpu.CompilerParams(
            dimension_semantics=("arbitrary", "arbitrary"), vmem_limit_bytes=VMEM_LIMIT),
        name="mixer",
    )(x, mod, w_in_p, w_pool, pool_scale, w_a, w_al_p, b_alpha, gain, w_b, w_out, ln_g, ln_b,
      w_r_p, b_r_p)


def _row_at(ref, row8):
    return ref.at[pl.ds(pl.multiple_of(row8, ROW_SUB), ROW_SUB)]


def _dispatch_kernel(fill_ref, dest_ref, u2_ref, rows_hbm, zbuf, sem_fill, sem_rows):
    i = pl.program_id(0)
    tile = u2_ref.shape[0] // ROW_SUB
    blk8 = EXPERT_BLOCK * ROW_SUB

    @pl.when(i == 0)
    def _():
        zbuf[...] = jnp.zeros_like(zbuf)

        def fill(start):
            dst = rows_hbm.at[pl.ds(pl.multiple_of(start * ROW_SUB, ROW_SUB), blk8)]
            cp = pltpu.make_async_copy(zbuf, dst, sem_fill.at[0])
            cp.start()
            cp.wait()

        for e in range(N_EXPERTS):
            fill(fill_ref[e])

        def unused(blk, carry):
            fill(blk * EXPERT_BLOCK)
            return carry

        lax.fori_loop(fill_ref[N_EXPERTS], rows_hbm.shape[0] // blk8, unused, 0)

    def body(r, carry):
        for kk in range(TOP_K):
            pltpu.make_async_copy(_row_at(u2_ref, r * ROW_SUB), _row_at(rows_hbm, dest_ref[0, 0, r * TOP_K + kk]),
                                  sem_rows.at[0]).start(priority=kk % 2)
        return carry

    lax.fori_loop(0, tile, body, 0, unroll=DMA_UNROLL)
    for _ in range(TOP_K):
        pltpu.make_async_copy(u2_ref, rows_hbm.at[pl.ds(0, tile * ROW_SUB)], sem_rows.at[0]).wait()


def _dispatch(fill_start, dest8, u2_rows, n_rows):
    n_tok = u2_rows.shape[0] // ROW_SUB
    tile = DISPATCH_TILE
    n_t = n_tok // tile
    grid_spec = pltpu.PrefetchScalarGridSpec(
        num_scalar_prefetch=1,
        grid=(n_t,),
        in_specs=[
            pl.BlockSpec((1, 1, tile * TOP_K), lambda i, fs: (i, 0, 0), memory_space=pltpu.SMEM),
            pl.BlockSpec((tile * ROW_SUB, LANES), lambda i, fs: (i, 0)),
        ],
        out_specs=pl.BlockSpec(memory_space=pl.ANY),
        scratch_shapes=[
            pltpu.VMEM((EXPERT_BLOCK * ROW_SUB, LANES), F32),
            pltpu.SemaphoreType.DMA((1,)),
            pltpu.SemaphoreType.DMA((1,)),
        ],
    )
    return pl.pallas_call(
        _dispatch_kernel,
        grid_spec=grid_spec,
        out_shape=jax.ShapeDtypeStruct(((n_rows + EXPERT_BLOCK) * ROW_SUB, LANES), F32),
        compiler_params=pltpu.CompilerParams(dimension_semantics=("arbitrary",)),
        name="dispatch",
    )(fill_start, dest8.reshape(n_t, 1, tile * TOP_K), u2_rows)


def _moe_kernel(be_ref, nxt_ref, par_ref, nb_ref, x_ref, bg_ref, bu_ref, bd_ref, wgu_hbm, wd_hbm, y_ref,
                wgu_buf, wd_buf, wg_s, wu_s, wd_s, sem_gu, sem_d):
    i = pl.program_id(0)
    new_expert = (i == 0) | (be_ref[i] != be_ref[jnp.maximum(i - 1, 0)])

    def fetch(e, s):
        return (pltpu.make_async_copy(wgu_hbm.at[e], wgu_buf.at[s], sem_gu.at[s]),
                pltpu.make_async_copy(wd_hbm.at[e], wd_buf.at[s], sem_d.at[s]))

    @pl.when(new_expert & (i < nb_ref[0]))
    def _():
        slot = par_ref[i]

        @pl.when(i == 0)
        def _():
            for cp in fetch(be_ref[0], slot):
                cp.start()

        @pl.when(nxt_ref[i] >= 0)
        def _():
            for cp in fetch(nxt_ref[i], 1 - slot):
                cp.start()

        for cp in fetch(be_ref[i], slot):
            cp.wait()
        src = lax.broadcasted_iota(jnp.int32, (MXU_COLS, MXU_COLS), 0)
        col = lax.broadcasted_iota(jnp.int32, (MXU_COLS, MXU_COLS), 1)
        half = MXU_COLS // 2
        want = jnp.where(col < half, 2 * col, 2 * (col - half) + 1)
        unzip = (src == want).astype(BF16)
        for g in range(wgu_buf.shape[2] // MXU_COLS):
            blk = wgu_buf[slot, :, g * MXU_COLS:(g + 1) * MXU_COLS].astype(BF16)
            sep = _dot(blk, unzip)
            wg_s[:, g * half:(g + 1) * half] = sep[:, :half].astype(BF16)
            wu_s[:, g * half:(g + 1) * half] = sep[:, half:].astype(BF16)
        wd_s[...] = wd_buf[slot].astype(BF16)

    @pl.when(i < nb_ref[0])
    def _():
        xb = _load_rows(x_ref, EXPERT_BLOCK).astype(BF16)
        gate = jnp.minimum(_dot(xb, wg_s[...]) + bg_ref[0], SWIGLU_LIMIT)
        up = jnp.clip(_dot(xb, wu_s[...]) + bu_ref[0], -SWIGLU_LIMIT, SWIGLU_LIMIT)
        glu = gate * _sigmoid(gate * SWIGLU_ALPHA)
        _store_rows(y_ref, _dot(((up + 1.0) * glu).astype(BF16), wd_s[...]) + bd_ref[0])

    @pl.when(i >= nb_ref[0])
    def _():
        y_ref[...] = jnp.zeros_like(y_ref)


def _moe(block_expert, next_expert, slot_parity, n_active, x_rows, w_gate_up, bg, bu, w_down, bd):
    n_blocks = block_expert.shape[0]
    _, d, f2 = w_gate_up.shape
    f = f2 // 2
    blk8 = EXPERT_BLOCK * ROW_SUB
    rows_in = pl.BlockSpec((blk8, LANES),
                           lambda i, be, nx, pr, nb: (jnp.maximum(jnp.minimum(i, nb[0] - 1), 0), 0))
    rows_out = pl.BlockSpec((blk8, LANES), lambda i, be, nx, pr, nb: (i, 0))
    per_expert = lambda shape: pl.BlockSpec((1,) + shape, lambda i, be, nx, pr, nb: (be[i], 0, 0))
    hbm = pl.BlockSpec(memory_space=pl.ANY)
    grid_spec = pltpu.PrefetchScalarGridSpec(
        num_scalar_prefetch=4,
        grid=(n_blocks,),
        in_specs=[rows_in, per_expert((1, f)), per_expert((1, f)), per_expert((1, d)), hbm, hbm],
        out_specs=rows_out,
        scratch_shapes=[pltpu.VMEM((2, d, f2), F32), pltpu.VMEM((2, f, d), F32),
                        pltpu.VMEM((d, f), BF16), pltpu.VMEM((d, f), BF16), pltpu.VMEM((f, d), BF16),
                        pltpu.SemaphoreType.DMA((2,)), pltpu.SemaphoreType.DMA((2,))],
    )
    return pl.pallas_call(
        _moe_kernel,
        grid_spec=grid_spec,
        out_shape=jax.ShapeDtypeStruct((n_blocks * blk8, LANES), F32),
        compiler_params=pltpu.CompilerParams(
            dimension_semantics=("arbitrary",), vmem_limit_bytes=VMEM_LIMIT),
        name="moe",
    )(block_expert, next_expert, slot_parity, n_active, x_rows, bg, bu, bd, w_gate_up, w_down)


def _combine_kernel(dest_cur_ref, dest_nxt_ref, x1_ref, mod_ref, route_ref, ln_g_ref, ln_b_ref, y_hbm,
                    o_ref, ybuf, sem):
    i = pl.program_id(0)
    n_steps = pl.num_programs(0)
    tile = x1_ref.shape[0]
    slot = lax.rem(i, 2)

    def gather(dest_ref, s):
        def body(r, carry):
            for kk in range(TOP_K):
                pltpu.make_async_copy(_row_at(y_hbm, dest_ref[0, 0, r * TOP_K + kk]),
                                      _row_at(ybuf.at[s, kk], r * ROW_SUB), sem.at[s]).start(priority=kk % 2)
            return carry
        lax.fori_loop(0, tile, body, 0, unroll=DMA_UNROLL)

    @pl.when(i == 0)
    def _():
        gather(dest_cur_ref, 0)

    @pl.when(i + 1 < n_steps)
    def _():
        gather(dest_nxt_ref, 1 - slot)

    for kk in range(TOP_K):
        pltpu.make_async_copy(y_hbm.at[pl.ds(0, tile * ROW_SUB)], ybuf.at[slot, kk], sem.at[slot]).wait()

    g_f = mod_ref[0][5:6]
    weight = route_ref[...]
    y = jnp.zeros(x1_ref.shape, F32)
    for kk in range(TOP_K):
        y = y + weight[:, kk:kk + 1] * _load_rows(ybuf, tile, (slot, kk))
    o_ref[...] = _layer_norm(DEEPNORM_ALPHA * x1_ref[...] + g_f * y, ln_g_ref[...], ln_b_ref[...])


def _combine(dest8, x1, mod, route, ln_g, ln_b, y_rows, seq):
    n_tok, d = x1.shape
    tile = COMBINE_TILE
    n_t = n_tok // tile
    per_seq = seq // tile
    dest3 = dest8.reshape(n_t, 1, tile * TOP_K)
    smem_blk = lambda fn: pl.BlockSpec((1, 1, tile * TOP_K), fn, memory_space=pltpu.SMEM)
    row = lambda width: pl.BlockSpec((tile, width), lambda i: (i, 0))
    vec = pl.BlockSpec((1, d), lambda i: (0, 0))
    return pl.pallas_call(
        _combine_kernel,
        grid=(n_t,),
        in_specs=[smem_blk(lambda i: (i, 0, 0)), smem_blk(lambda i: (jnp.minimum(i + 1, n_t - 1), 0, 0)),
                  row(d), pl.BlockSpec((1, 6, d), lambda i: (i // per_seq, 0, 0)), row(TOP_K), vec, vec,
                  pl.BlockSpec(memory_space=pl.ANY)],
        out_specs=row(d),
        out_shape=jax.ShapeDtypeStruct((n_tok, d), F32),
        scratch_shapes=[pltpu.VMEM((2, TOP_K, tile * ROW_SUB, LANES), F32), pltpu.SemaphoreType.DMA((2,))],
        compiler_params=pltpu.CompilerParams(
            dimension_semantics=("arbitrary",), vmem_limit_bytes=VMEM_LIMIT),
        name="combine",
    )(dest3, dest3, x1, mod, route, ln_g, ln_b, y_rows)


def _pad_in_proj(w_in):
    rank_end = 3 * D_MODEL + 2 * GLA_KEY_DIM + GLA_GATE_RANK
    pad = jnp.zeros((w_in.shape[0], RANK_PAD - GLA_GATE_RANK), w_in.dtype)
    return jnp.concatenate([w_in[:, :rank_end], pad, w_in[:, rank_end:]], axis=1)


def _layer(x, c, w_ada, b_ada, w_in, w_pool_group, pool_scale, w_branch_a, w_alpha_up, b_alpha,
           gla_norm_gain, w_branch_b, w_out, ln1_gain, ln1_bias, w_router, b_router,
           w_gate_up, b_gate_up, w_down, b_down, ln2_gain, ln2_bias):
    bsz, seq, d = x.shape
    n_tok = bsz * seq
    n_assign = n_tok * TOP_K
    row2 = lambda v: v.reshape(1, -1)

    mod = _ada(c, w_ada, b_ada).reshape(bsz, 6, d)

    w_in_p = _pad_in_proj(w_in).astype(BF16)
    w_al_p = jnp.concatenate(
        [w_alpha_up, jnp.zeros((RANK_PAD - GLA_GATE_RANK, GLA_KEY_DIM), w_alpha_up.dtype)], axis=0).astype(BF16)
    w_r_t = w_router.T
    w_r_hi = w_r_t.astype(BF16)
    w_r_split = jnp.concatenate([w_r_hi, (w_r_t - w_r_hi.astype(F32)).astype(BF16)], axis=0)
    x1, u2_rows, route_t, counts = _mixer(
        x, mod, w_in_p, w_pool_group.astype(BF16), row2(pool_scale), w_branch_a.astype(BF16), w_al_p,
        row2(b_alpha), row2(gla_norm_gain), w_branch_b.astype(BF16), w_out.astype(BF16), row2(ln1_gain),
        row2(ln1_bias), w_r_split, b_router.reshape(N_EXPERTS, 1))

    top_idx = route_t[0:TOP_K].T.astype(jnp.int32)
    rank = route_t[TOP_K:2 * TOP_K].T.astype(jnp.int32)
    route = route_t[2 * TOP_K:3 * TOP_K].T
    counts = counts[:, 0].astype(jnp.int32)
    padded = (counts + EXPERT_BLOCK - 1) // EXPERT_BLOCK * EXPERT_BLOCK
    pad_end = jnp.cumsum(padded)
    pad_start = pad_end - padded
    n_rows = (n_assign + N_EXPERTS * (EXPERT_BLOCK - 1) + EXPERT_BLOCK - 1) // EXPERT_BLOCK * EXPERT_BLOCK
    n_blocks = n_rows // EXPERT_BLOCK
    n_active = (pad_end[-1] // EXPERT_BLOCK).astype(jnp.int32)
    dest8 = ((pad_start[top_idx] + rank) * ROW_SUB).reshape(-1)
    blk_row = jnp.arange(n_blocks, dtype=jnp.int32)[:, None] * EXPERT_BLOCK
    block_expert = jnp.minimum(jnp.sum(pad_end[None, :] <= blk_row, axis=1), N_EXPERTS - 1).astype(jnp.int32)

    fill_start = jnp.concatenate([pad_start + counts, n_active.reshape(1)]).astype(jnp.int32)
    x_rows = _dispatch(fill_start, dest8, u2_rows, n_rows)
    f = w_down.shape[1]
    bg = b_gate_up[:, 0::2].reshape(N_EXPERTS, 1, f)
    bu = b_gate_up[:, 1::2].reshape(N_EXPERTS, 1, f)
    eid = jnp.arange(N_EXPERTS, dtype=jnp.int32)
    owns = counts > 0
    later = jnp.where((eid[None, :] > eid[:, None]) & owns[None, :], eid[None, :], N_EXPERTS)
    next_of = jnp.min(later, axis=1)
    next_of = jnp.where(next_of == N_EXPERTS, -1, next_of).astype(jnp.int32)
    parity_of = ((jnp.cumsum(owns.astype(jnp.int32)) - 1) % 2).astype(jnp.int32)
    y_rows = _moe(block_expert, next_of[block_expert], parity_of[block_expert], n_active.reshape(1), x_rows,
                  w_gate_up, bg, bu, w_down, b_down.reshape(N_EXPERTS, 1, d))
    out = _combine(dest8, x1.reshape(n_tok, d), mod, route, row2(ln2_gain), row2(ln2_bias), y_rows, seq)
    return out.reshape(bsz, seq, d)


def kernel(x, c, w_ada, b_ada, w_in, w_pool_group, pool_scale, w_branch_a, w_alpha_up, b_alpha, gla_norm_gain,
           w_branch_b, w_out, ln1_gain, ln1_bias, w_router, b_router, w_gate_up, b_gate_up, w_down, b_down,
           ln2_gain, ln2_bias):
    for l in range(DEPTH):
        x = _layer(x, c, w_ada[l], b_ada[l], w_in[l], w_pool_group[l], pool_scale[l], w_branch_a[l],
                   w_alpha_up[l], b_alpha[l], gla_norm_gain[l], w_branch_b[l], w_out[l], ln1_gain[l],
                   ln1_bias[l], w_router[l], b_router[l], w_gate_up[l], b_gate_up[l], w_down[l], b_down[l],
                   ln2_gain[l], ln2_bias[l])
    return x
```

```python
import jax
import jax.numpy as jnp
from jax import lax
from jax.experimental import pallas as pl
from jax.experimental.pallas import tpu as pltpu

D_MODEL = 1024
CHUNK = 64
SUB = 16
N_SUB = CHUNK // SUB
POOL_WINDOWS = (2, 4, 8, 16)
POOL_GROUP_WIDTH = D_MODEL // len(POOL_WINDOWS)
POOL_HALO = 16
GLA_HEADS = 4
GLA_KEY_DIM = D_MODEL // 2
GLA_HEAD_K = GLA_KEY_DIM // GLA_HEADS
GLA_HEAD_V = D_MODEL // GLA_HEADS
GLA_GATE_RANK = 16
GLA_TAU = 16.0
N_EXPERTS = 32
TOP_K = 4
SWIGLU_ALPHA = 1.702
SWIGLU_LIMIT = 7.0
EXPERT_BLOCK = 512
EXPERT_HALF = EXPERT_BLOCK // 2
LN_EPS = 1e-5
RMS_EPS = 1e-6
DEPTH = 1
DEEPNORM_ALPHA = (2.0 * DEPTH) ** 0.25

LANES = 128
MXU_COLS = 256
RANK_PAD = LANES
_W = (D_MODEL, GLA_KEY_DIM, GLA_KEY_DIM, D_MODEL, D_MODEL, RANK_PAD, D_MODEL, D_MODEL)
_OFF = tuple(sum(_W[:i]) for i in range(len(_W) + 1))
EXP_CAP = 60.0
ROUTE_ROWS = 16

SEQ_TILE = 512
DISPATCH_TILE = 256
COMBINE_TILE = 256
DMA_UNROLL = 8
VMEM_LIMIT = 56 * 1024 * 1024

F32 = jnp.float32
BF16 = jnp.bfloat16
HI = lax.Precision.HIGHEST


def _dot(a, b):
    return jnp.dot(a, b, preferred_element_type=F32)


def _dot_nt(a, b):
    return lax.dot_general(a, b, (((1,), (1,)), ((), ())), preferred_element_type=F32)


def _dot_tn(a, b):
    return lax.dot_general(a, b, (((0,), (0,)), ((), ())), preferred_element_type=F32)


def _sigmoid(v):
    return 0.5 * jnp.tanh(0.5 * v) + 0.5


ROW_SUB = D_MODEL // LANES


def _load_rows(ref, n_rows, lead=()):
    return jnp.concatenate(
        [ref[lead + (pl.ds(j, n_rows, stride=ROW_SUB), slice(None))] for j in range(ROW_SUB)], axis=1)


def _store_rows(ref, val):
    for j in range(ROW_SUB):
        ref[pl.ds(j, val.shape[0], stride=ROW_SUB), :] = val[:, j * LANES:(j + 1) * LANES]


def _ada_kernel(c_ref, w_ref, b_ref, o_ref):
    c = c_ref[...]
    s = c * jax.nn.sigmoid(c)
    o_ref[...] = jnp.dot(s, w_ref[...], precision=HI, preferred_element_type=F32) + b_ref[...]


def _ada(c, w_ada, b_ada):
    bsz, d = c.shape
    n = w_ada.shape[1]
    tn = 1024
    return pl.pallas_call(
        _ada_kernel,
        grid=(n // tn,),
        in_specs=[
            pl.BlockSpec((bsz, d), lambda j: (0, 0)),
            pl.BlockSpec((d, tn), lambda j: (0, j)),
            pl.BlockSpec((1, tn), lambda j: (0, j)),
        ],
        out_specs=pl.BlockSpec((bsz, tn), lambda j: (0, j)),
        out_shape=jax.ShapeDtypeStruct((bsz, n), F32),
        name="ada",
    )(c, w_ada, b_ada.reshape(1, n))


def _layer_norm(z, gain, bias):
    mu = jnp.mean(z, axis=-1, keepdims=True)
    zc = z - mu
    var = jnp.mean(zc * zc, axis=-1, keepdims=True)
    return zc * lax.rsqrt(var + LN_EPS) * gain + bias


def _mixer_kernel(x_ref, mod_ref, w_in_ref, w_pool_ref, pool_scale_ref, w_a_ref, w_al_ref, b_al_ref,
                  gain_ref, w_b_ref, w_out_ref, ln_g_ref, ln_b_ref, w_r_ref, b_r_ref,
                  x1_ref, u2_ref, route_ref, counts_ref,
                  a_ext, s_ref, cnt_ref, o_ref):
    b_idx = pl.program_id(0)
    s_idx = pl.program_id(1)
    tile = x_ref.shape[1]

    @pl.when(s_idx == 0)
    def _():
        s_ref[...] = jnp.zeros_like(s_ref)
        a_ext[0:POOL_HALO, :] = jnp.zeros((POOL_HALO, D_MODEL), F32)

    @pl.when((b_idx == 0) & (s_idx == 0))
    def _():
        cnt_ref[...] = jnp.zeros_like(cnt_ref)

    mod = mod_ref[0]
    sh_m, sc_m, g_m = mod[0:1], mod[1:2], mod[2:3]
    sh_f, sc_f = mod[3:4], mod[4:5]
    x = x_ref[0]
    u = (x * (1.0 + sc_m) + sh_m).astype(BF16)

    def proj(i):
        return _dot(u, w_in_ref[:, _OFF[i]:_OFF[i + 1]])

    a = proj(0)
    a_ext[POOL_HALO:POOL_HALO + tile, :] = a
    t_glob = s_idx * tile + lax.broadcasted_iota(jnp.int32, (tile, 1), 0)
    mapped = []
    for g, w in enumerate(POOL_WINDOWS):
        lo, hi = g * POOL_GROUP_WIDTH, (g + 1) * POOL_GROUP_WIDTH
        win = a_ext[:, lo:hi]
        k = 1
        while k < w:
            win = win + pltpu.roll(win, k, 0)
            k *= 2
        inv_cnt = 1.0 / jnp.minimum(t_glob + 1, w).astype(F32)
        pooled = win[POOL_HALO:, :] * inv_cnt - a[:, lo:hi]
        mapped.append(_dot(pooled.astype(BF16), w_pool_ref[g]))
    a_ext[0:POOL_HALO, :] = a[tile - POOL_HALO:tile, :]
    ya = jnp.concatenate(mapped, axis=1) * pool_scale_ref[...]
    ya = _dot(ya.astype(BF16), w_a_ref[...])

    q = proj(1) * (GLA_HEAD_K ** -0.5)
    k_all = proj(2)
    v_all = proj(3)
    alpha_low = proj(5)
    z = _dot(alpha_low.astype(BF16), w_al_ref[...]) + b_al_ref[...]
    log_a = (jnp.minimum(z, 0.0) - jnp.log1p(jnp.exp(-jnp.abs(z)))) * (1.0 / GLA_TAU)

    ri = lax.broadcasted_iota(jnp.int32, (CHUNK, CHUNK), 0)
    ci = lax.broadcasted_iota(jnp.int32, (CHUNK, CHUNK), 1)
    causal = ci <= ri
    cum_mat = causal.astype(BF16)
    la_hi = log_a.astype(BF16)
    la_split = jnp.concatenate([la_hi, (log_a - la_hi.astype(F32)).astype(BF16)], axis=1)

    n_chunks = tile // CHUNK
    pairs = [(c, h) for c in range(n_chunks) for h in range(GLA_HEADS)]
    rows_of = lambda c: slice(c * CHUNK, (c + 1) * CHUNK)
    ks_of = lambda h: slice(h * GLA_HEAD_K, (h + 1) * GLA_HEAD_K)
    vs_of = lambda h: slice(h * GLA_HEAD_V, (h + 1) * GLA_HEAD_V)

    b_cum, ref_pts, b_ref_pt = [], [], []
    for c in range(n_chunks):
        cum = _dot(cum_mat, la_split[rows_of(c)])
        b_cum.append(cum[:, :GLA_KEY_DIM] + cum[:, GLA_KEY_DIM:])
        ref_pts.append([jnp.zeros((1, GLA_KEY_DIM), F32)]
                       + [b_cum[c][i * SUB - 1:i * SUB, :] for i in range(1, N_SUB)])
        b_ref_pt.append(jnp.concatenate([jnp.broadcast_to(p, (SUB, GLA_KEY_DIM)) for p in ref_pts[c]], axis=0))

    s_all = {}
    for c, h in pairs:
        qh, kh, bh = q[rows_of(c), ks_of(h)], k_all[rows_of(c), ks_of(h)], b_cum[c][:, ks_of(h)]
        q_dec = (qh * jnp.exp(bh - b_ref_pt[c][:, ks_of(h)])).astype(BF16)
        k_dec = jnp.concatenate(
            [(kh * jnp.exp(jnp.minimum(ref_pts[c][i][:, ks_of(h)] - bh, EXP_CAP))).astype(BF16)
             for i in range(N_SUB)], axis=0)
        s_all[c, h] = _dot_nt(q_dec, k_dec)

    o_intra, kv, decay_last = {}, {}, {}
    for c, h in pairs:
        kh, bh = k_all[rows_of(c), ks_of(h)], b_cum[c][:, ks_of(h)]
        vh = v_all[rows_of(c), vs_of(h)].astype(BF16)
        scores = jnp.concatenate(
            [s_all[c, h][i * SUB:(i + 1) * SUB, i * CHUNK:(i + 1) * CHUNK] for i in range(N_SUB)], axis=0)
        o_intra[c, h] = _dot(jnp.where(causal, scores, 0.0).astype(BF16), vh)
        b_last = bh[CHUNK - 1:CHUNK, :]
        kv[c, h] = _dot_tn(vh, (kh * jnp.exp(b_last - bh)).astype(BF16))
        decay_last[c, h] = jnp.exp(b_last)

    o_inter = {}
    state_t = [s_ref[h] for h in range(GLA_HEADS)]
    for c, h in pairs:
        q_in = (q[rows_of(c), ks_of(h)] * jnp.exp(b_cum[c][:, ks_of(h)])).astype(BF16)
        o_inter[c, h] = _dot_nt(q_in, state_t[h].astype(BF16))
        state_t[h] = state_t[h] * decay_last[c, h] + kv[c, h]
    for h in range(GLA_HEADS):
        s_ref[h] = state_t[h]

    for c, h in pairs:
        o = o_intra[c, h] + o_inter[c, h]
        o = o * lax.rsqrt(jnp.mean(o * o, axis=-1, keepdims=True) + RMS_EPS) * gain_ref[...]
        o_ref[rows_of(c), vs_of(h)] = o

    r = proj(4)
    yb = _dot((o_ref[...] * (r * _sigmoid(r))).astype(BF16), w_b_ref[...])

    merged = _sigmoid(proj(6)) * ya + _sigmoid(proj(7)) * yb
    y = _dot(merged.astype(BF16), w_out_ref[...])
    x1 = _layer_norm(DEEPNORM_ALPHA * x + g_m * y, ln_g_ref[...], ln_b_ref[...])
    x1_ref[0] = x1
    u2 = x1 * (1.0 + sc_f) + sh_f
    _store_rows(u2_ref, u2)

    u2_hi = u2.astype(BF16)
    u2_lo = (u2 - u2_hi.astype(F32)).astype(BF16)
    part = _dot_nt(w_r_ref[...], u2_hi)
    logits = (part[:N_EXPERTS] + part[N_EXPERTS:] + _dot_nt(w_r_ref[0:N_EXPERTS, :], u2_lo)
              + b_r_ref[...])
    erow = lax.broadcasted_iota(jnp.int32, (N_EXPERTS, tile), 0).astype(F32)
    work = logits
    sel = jnp.zeros((N_EXPERTS, tile), F32)
    vals, hits = [], []
    for _ in range(TOP_K):
        m = jnp.max(work, axis=0, keepdims=True)
        idx = jnp.min(jnp.where(work == m, erow, float(N_EXPERTS)), axis=0, keepdims=True)
        hit = erow == idx
        vals.append(m)
        hits.append((idx, hit))
        sel = jnp.where(hit, 1.0, sel)
        work = jnp.where(hit, -jnp.inf, work)
    exps = [jnp.exp(v - vals[0]) for v in vals]
    inv_den = 1.0 / (exps[0] + exps[1] + exps[2] + exps[3])

    rt = lax.broadcasted_iota(jnp.int32, (tile, tile), 0)
    ct = lax.broadcasted_iota(jnp.int32, (tile, tile), 1)
    before = (rt < ct).astype(BF16)
    base = _dot(sel.astype(BF16), before) + cnt_ref[:, 0:1]
    orow = lax.broadcasted_iota(jnp.int32, (ROUTE_ROWS, tile), 0)
    route = jnp.zeros((ROUTE_ROWS, tile), F32)
    for kk in range(TOP_K):
        idx, hit = hits[kk]
        rank = jnp.sum(jnp.where(hit, base, 0.0), axis=0, keepdims=True)
        route = jnp.where(orow == kk, idx, route)
        route = jnp.where(orow == TOP_K + kk, rank, route)
        route = jnp.where(orow == 2 * TOP_K + kk, exps[kk] * inv_den, route)
    route_ref[...] = route
    cnt_ref[...] = cnt_ref[...] + jnp.sum(sel, axis=1, keepdims=True)
    counts_ref[...] = cnt_ref[...]


def _mixer(x, mod, w_in_p, w_pool, pool_scale, w_a, w_al_p, b_alpha, gain, w_b, w_out, ln_g, ln_b,
           w_r_p, b_r_p):
    bsz, seq, d = x.shape
    tile = SEQ_TILE
    n_s = seq // tile

    def const(shape):
        nd = len(shape)
        return pl.BlockSpec(shape, lambda b, s: (0,) * nd, pipeline_mode=pl.Buffered(1))

    tok = lambda width: pl.BlockSpec((1, tile, width), lambda b, s: (b, s, 0))
    return pl.pallas_call(
        _mixer_kernel,
        grid=(bsz, n_s),
        in_specs=[
            tok(d),
            pl.BlockSpec((1, 6, d), lambda b, s: (b, 0, 0)),
            const(w_in_p.shape), const(w_pool.shape), const(pool_scale.shape), const(w_a.shape),
            const(w_al_p.shape), const(b_alpha.shape), const(gain.shape), const(w_b.shape),
            const(w_out.shape), const(ln_g.shape), const(ln_b.shape), const(w_r_p.shape),
            const(b_r_p.shape),
        ],
        out_specs=[tok(d), pl.BlockSpec((tile * ROW_SUB, LANES), lambda b, s: (b * n_s + s, 0)),
                   pl.BlockSpec((ROUTE_ROWS, tile), lambda b, s: (0, b * n_s + s)),
                   pl.BlockSpec((N_EXPERTS, LANES), lambda b, s: (0, 0))],
        out_shape=[
            jax.ShapeDtypeStruct((bsz, seq, d), F32),
            jax.ShapeDtypeStruct((bsz * seq * ROW_SUB, LANES), F32),
            jax.ShapeDtypeStruct((ROUTE_ROWS, bsz * seq), F32),
            jax.ShapeDtypeStruct((N_EXPERTS, LANES), F32),
        ],
        scratch_shapes=[
            pltpu.VMEM((POOL_HALO + tile, d), F32),
            pltpu.VMEM((GLA_HEADS, GLA_HEAD_V, GLA_HEAD_K), F32),
            pltpu.VMEM((N_EXPERTS, LANES), F32),
            pltpu.VMEM((tile, d), F32),
        ],
        compiler_params=pltpu.CompilerParams(
            dimension_semantics=("arbitrary", "arbitrary"), vmem_limit_bytes=VMEM_LIMIT),
        name="mixer",
    )(x, mod, w_in_p, w_pool, pool_scale, w_a, w_al_p, b_alpha, gain, w_b, w_out, ln_g, ln_b,
      w_r_p, b_r_p)


def _row_at(ref, row8):
    return ref.at[pl.ds(pl.multiple_of(row8, ROW_SUB), ROW_SUB)]


def _dispatch_kernel(fill_ref, dest_ref, u2_ref, rows_hbm, zbuf, sem_fill, sem_rows):
    i = pl.program_id(0)
    tile = u2_ref.shape[0] // ROW_SUB
    blk8 = EXPERT_BLOCK * ROW_SUB

    @pl.when(i == 0)
    def _():
        zbuf[...] = jnp.zeros_like(zbuf)

        def fill(start):
            dst = rows_hbm.at[pl.ds(pl.multiple_of(start * ROW_SUB, ROW_SUB), blk8)]
            cp = pltpu.make_async_copy(zbuf, dst, sem_fill.at[0])
            cp.start()
            cp.wait()

        for e in range(N_EXPERTS):
            fill(fill_ref[e])

        def unused(blk, carry):
            fill(blk * EXPERT_BLOCK)
            return carry

        lax.fori_loop(fill_ref[N_EXPERTS], rows_hbm.shape[0] // blk8, unused, 0)

    def body(r, carry):
        for kk in range(TOP_K):
            pltpu.make_async_copy(_row_at(u2_ref, r * ROW_SUB), _row_at(rows_hbm, dest_ref[0, 0, r * TOP_K + kk]),
                                  sem_rows.at[0]).start(priority=kk % 2)
        return carry

    lax.fori_loop(0, tile, body, 0, unroll=DMA_UNROLL)
    for _ in range(TOP_K):
        pltpu.make_async_copy(u2_ref, rows_hbm.at[pl.ds(0, tile * ROW_SUB)], sem_rows.at[0]).wait()


def _dispatch(fill_start, dest8, u2_rows, n_rows):
    n_tok = u2_rows.shape[0] // ROW_SUB
    tile = DISPATCH_TILE
    n_t = n_tok // tile
    grid_spec = pltpu.PrefetchScalarGridSpec(
        num_scalar_prefetch=1,
        grid=(n_t,),
        in_specs=[
            pl.BlockSpec((1, 1, tile * TOP_K), lambda i, fs: (i, 0, 0), memory_space=pltpu.SMEM),
            pl.BlockSpec((tile * ROW_SUB, LANES), lambda i, fs: (i, 0)),
        ],
        out_specs=pl.BlockSpec(memory_space=pl.ANY),
        scratch_shapes=[
            pltpu.VMEM((EXPERT_BLOCK * ROW_SUB, LANES), F32),
            pltpu.SemaphoreType.DMA((1,)),
            pltpu.SemaphoreType.DMA((1,)),
        ],
    )
    return pl.pallas_call(
        _dispatch_kernel,
        grid_spec=grid_spec,
        out_shape=jax.ShapeDtypeStruct(((n_rows + EXPERT_BLOCK) * ROW_SUB, LANES), F32),
        compiler_params=pltpu.CompilerParams(dimension_semantics=("arbitrary",)),
        name="dispatch",
    )(fill_start, dest8.reshape(n_t, 1, tile * TOP_K), u2_rows)


def _moe_kernel(be_ref, nxt_ref, par_ref, full_ref, nb_ref, x_ref, bg_ref, bu_ref, bd_ref, wgu_hbm, wd_hbm, y_ref,
                wgu_buf, wd_buf, wg_s, wu_s, wd_s, sem_gu, sem_d):
    i = pl.program_id(0)
    new_expert = (i == 0) | (be_ref[i] != be_ref[jnp.maximum(i - 1, 0)])

    def fetch(e, s):
        return (pltpu.make_async_copy(wgu_hbm.at[e], wgu_buf.at[s], sem_gu.at[s]),
                pltpu.make_async_copy(wd_hbm.at[e], wd_buf.at[s], sem_d.at[s]))

    @pl.when(new_expert & (i < nb_ref[0]))
    def _():
        slot = par_ref[i]

        @pl.when(i == 0)
        def _():
            for cp in fetch(be_ref[0], slot):
                cp.start()

        @pl.when(nxt_ref[i] >= 0)
        def _():
            for cp in fetch(nxt_ref[i], 1 - slot):
                cp.start()

        for cp in fetch(be_ref[i], slot):
            cp.wait()
        src = lax.broadcasted_iota(jnp.int32, (MXU_COLS, MXU_COLS), 0)
        col = lax.broadcasted_iota(jnp.int32, (MXU_COLS, MXU_COLS), 1)
        half = MXU_COLS // 2
        want = jnp.where(col < half, 2 * col, 2 * (col - half) + 1)
        unzip = (src == want).astype(BF16)
        for g in range(wgu_buf.shape[2] // MXU_COLS):
            blk = wgu_buf[slot, :, g * MXU_COLS:(g + 1) * MXU_COLS].astype(BF16)
            sep = _dot(blk, unzip)
            wg_s[:, g * half:(g + 1) * half] = sep[:, :half].astype(BF16)
            wu_s[:, g * half:(g + 1) * half] = sep[:, half:].astype(BF16)
        wd_s[...] = wd_buf[slot].astype(BF16)

    def expert_mlp(n_rows):
        xb = _load_rows(x_ref, n_rows).astype(BF16)
        gate = jnp.minimum(_dot(xb, wg_s[...]) + bg_ref[0], SWIGLU_LIMIT)
        up = jnp.clip(_dot(xb, wu_s[...]) + bu_ref[0], -SWIGLU_LIMIT, SWIGLU_LIMIT)
        glu = gate * _sigmoid(gate * SWIGLU_ALPHA)
        _store_rows(y_ref, _dot(((up + 1.0) * glu).astype(BF16), wd_s[...]) + bd_ref[0])

    active = i < nb_ref[0]
    both_halves = full_ref[i] == 1

    @pl.when(active & both_halves)
    def _():
        expert_mlp(EXPERT_BLOCK)

    @pl.when(active & jnp.logical_not(both_halves))
    def _():
        expert_mlp(EXPERT_HALF)
        y_ref[EXPERT_HALF * ROW_SUB:, :] = jnp.zeros((EXPERT_HALF * ROW_SUB, LANES), F32)

    @pl.when(jnp.logical_not(active))
    def _():
        y_ref[...] = jnp.zeros_like(y_ref)


def _moe(block_expert, next_expert, slot_parity, both_halves, n_active, x_rows, w_gate_up, bg, bu, w_down, bd):
    n_blocks = block_expert.shape[0]
    _, d, f2 = w_gate_up.shape
    f = f2 // 2
    blk8 = EXPERT_BLOCK * ROW_SUB
    rows_in = pl.BlockSpec((blk8, LANES),
                           lambda i, be, nx, pr, fl, nb: (jnp.maximum(jnp.minimum(i, nb[0] - 1), 0), 0))
    rows_out = pl.BlockSpec((blk8, LANES), lambda i, be, nx, pr, fl, nb: (i, 0))
    per_expert = lambda shape: pl.BlockSpec((1,) + shape, lambda i, be, nx, pr, fl, nb: (be[i], 0, 0))
    hbm = pl.BlockSpec(memory_space=pl.ANY)
    grid_spec = pltpu.PrefetchScalarGridSpec(
        num_scalar_prefetch=5,
        grid=(n_blocks,),
        in_specs=[rows_in, per_expert((1, f)), per_expert((1, f)), per_expert((1, d)), hbm, hbm],
        out_specs=rows_out,
        scratch_shapes=[pltpu.VMEM((2, d, f2), F32), pltpu.VMEM((2, f, d), F32),
                        pltpu.VMEM((d, f), BF16), pltpu.VMEM((d, f), BF16), pltpu.VMEM((f, d), BF16),
                        pltpu.SemaphoreType.DMA((2,)), pltpu.SemaphoreType.DMA((2,))],
    )
    return pl.pallas_call(
        _moe_kernel,
        grid_spec=grid_spec,
        out_shape=jax.ShapeDtypeStruct((n_blocks * blk8, LANES), F32),
        compiler_params=pltpu.CompilerParams(
            dimension_semantics=("arbitrary",), vmem_limit_bytes=VMEM_LIMIT),
        name="moe",
    )(block_expert, next_expert, slot_parity, both_halves, n_active, x_rows, bg, bu, bd, w_gate_up, w_down)


def _combine_kernel(dest_cur_ref, dest_nxt_ref, x1_ref, mod_ref, route_ref, ln_g_ref, ln_b_ref, y_hbm,
                    o_ref, ybuf, sem):
    i = pl.program_id(0)
    n_steps = pl.num_programs(0)
    tile = x1_ref.shape[0]
    slot = lax.rem(i, 2)

    def gather(dest_ref, s):
        def body(r, carry):
            for kk in range(TOP_K):
                pltpu.make_async_copy(_row_at(y_hbm, dest_ref[0, 0, r * TOP_K + kk]),
                                      _row_at(ybuf.at[s, kk], r * ROW_SUB), sem.at[s]).start(priority=kk % 2)
            return carry
        lax.fori_loop(0, tile, body, 0, unroll=DMA_UNROLL)

    @pl.when(i == 0)
    def _():
        gather(dest_cur_ref, 0)

    @pl.when(i + 1 < n_steps)
    def _():
        gather(dest_nxt_ref, 1 - slot)

    for kk in range(TOP_K):
        pltpu.make_async_copy(y_hbm.at[pl.ds(0, tile * ROW_SUB)], ybuf.at[slot, kk], sem.at[slot]).wait()

    g_f = mod_ref[0][5:6]
    weight = route_ref[...]
    y = jnp.zeros(x1_ref.shape, F32)
    for kk in range(TOP_K):
        y = y + weight[:, kk:kk + 1] * _load_rows(ybuf, tile, (slot, kk))
    o_ref[...] = _layer_norm(DEEPNORM_ALPHA * x1_ref[...] + g_f * y, ln_g_ref[...], ln_b_ref[...])


def _combine(dest8, x1, mod, route, ln_g, ln_b, y_rows, seq):
    n_tok, d = x1.shape
    tile = COMBINE_TILE
    n_t = n_tok // tile
    per_seq = seq // tile
    dest3 = dest8.reshape(n_t, 1, tile * TOP_K)
    smem_blk = lambda fn: pl.BlockSpec((1, 1, tile * TOP_K), fn, memory_space=pltpu.SMEM)
    row = lambda width: pl.BlockSpec((tile, width), lambda i: (i, 0))
    vec = pl.BlockSpec((1, d), lambda i: (0, 0))
    return pl.pallas_call(
        _combine_kernel,
        grid=(n_t,),
        in_specs=[smem_blk(lambda i: (i, 0, 0)), smem_blk(lambda i: (jnp.minimum(i + 1, n_t - 1), 0, 0)),
                  row(d), pl.BlockSpec((1, 6, d), lambda i: (i // per_seq, 0, 0)), row(TOP_K), vec, vec,
                  pl.BlockSpec(memory_space=pl.ANY)],
        out_specs=row(d),
        out_shape=jax.ShapeDtypeStruct((n_tok, d), F32),
        scratch_shapes=[pltpu.VMEM((2, TOP_K, tile * ROW_SUB, LANES), F32), pltpu.SemaphoreType.DMA((2,))],
        compiler_params=pltpu.CompilerParams(
            dimension_semantics=("arbitrary",), vmem_limit_bytes=VMEM_LIMIT),
        name="combine",
    )(dest3, dest3, x1, mod, route, ln_g, ln_b, y_rows)


def _pad_in_proj(w_in):
    rank_end = 3 * D_MODEL + 2 * GLA_KEY_DIM + GLA_GATE_RANK
    pad = jnp.zeros((w_in.shape[0], RANK_PAD - GLA_GATE_RANK), BF16)
    return jnp.concatenate([w_in[:, :rank_end].astype(BF16), pad, w_in[:, rank_end:].astype(BF16)], axis=1)


def _layer(x, c, w_ada, b_ada, w_in, w_pool_group, pool_scale, w_branch_a, w_alpha_up, b_alpha,
           gla_norm_gain, w_branch_b, w_out, ln1_gain, ln1_bias, w_router, b_router,
           w_gate_up, b_gate_up, w_down, b_down, ln2_gain, ln2_bias):
    bsz, seq, d = x.shape
    n_tok = bsz * seq
    n_assign = n_tok * TOP_K
    row2 = lambda v: v.reshape(1, -1)

    mod = _ada(c, w_ada, b_ada).reshape(bsz, 6, d)

    w_in_p = _pad_in_proj(w_in)
    w_al_p = jnp.concatenate(
        [w_alpha_up, jnp.zeros((RANK_PAD - GLA_GATE_RANK, GLA_KEY_DIM), w_alpha_up.dtype)], axis=0).astype(BF16)
    w_r_t = w_router.T
    w_r_hi = w_r_t.astype(BF16)
    w_r_split = jnp.concatenate([w_r_hi, (w_r_t - w_r_hi.astype(F32)).astype(BF16)], axis=0)
    x1, u2_rows, route_t, counts = _mixer(
        x, mod, w_in_p, w_pool_group.astype(BF16), row2(pool_scale), w_branch_a.astype(BF16), w_al_p,
        row2(b_alpha), row2(gla_norm_gain), w_branch_b.astype(BF16), w_out.astype(BF16), row2(ln1_gain),
        row2(ln1_bias), w_r_split, b_router.reshape(N_EXPERTS, 1))

    top_idx = route_t[0:TOP_K].T.astype(jnp.int32)
    rank = route_t[TOP_K:2 * TOP_K].T.astype(jnp.int32)
    route = route_t[2 * TOP_K:3 * TOP_K].T
    counts = counts[:, 0].astype(jnp.int32)
    padded = (counts + EXPERT_BLOCK - 1) // EXPERT_BLOCK * EXPERT_BLOCK
    pad_end = jnp.cumsum(padded)
    pad_start = pad_end - padded
    n_rows = (n_assign + N_EXPERTS * (EXPERT_BLOCK - 1) + EXPERT_BLOCK - 1) // EXPERT_BLOCK * EXPERT_BLOCK
    n_blocks = n_rows // EXPERT_BLOCK
    n_active = (pad_end[-1] // EXPERT_BLOCK).astype(jnp.int32)
    start_of = jnp.sum(jnp.where(top_idx[..., None] == jnp.arange(N_EXPERTS, dtype=jnp.int32), pad_start, 0),
                       axis=-1)
    dest8 = ((start_of + rank) * ROW_SUB).reshape(-1)
    blk_row = jnp.arange(n_blocks, dtype=jnp.int32)[:, None] * EXPERT_BLOCK
    block_expert = jnp.minimum(jnp.sum(pad_end[None, :] <= blk_row, axis=1), N_EXPERTS - 1).astype(jnp.int32)

    fill_start = jnp.concatenate([pad_start + counts, n_active.reshape(1)]).astype(jnp.int32)
    x_rows = _dispatch(fill_start, dest8, u2_rows, n_rows)
    f = w_down.shape[1]
    bg = b_gate_up[:, 0::2].reshape(N_EXPERTS, 1, f)
    bu = b_gate_up[:, 1::2].reshape(N_EXPERTS, 1, f)
    eid = jnp.arange(N_EXPERTS, dtype=jnp.int32)
    owns = counts > 0
    later = jnp.where((eid[None, :] > eid[:, None]) & owns[None, :], eid[None, :], N_EXPERTS)
    next_of = jnp.min(later, axis=1)
    next_of = jnp.where(next_of == N_EXPERTS, -1, next_of).astype(jnp.int32)
    parity_of = ((jnp.cumsum(owns.astype(jnp.int32)) - 1) % 2).astype(jnp.int32)
    is_block_expert = block_expert[:, None] == eid[None, :]
    per_block = lambda v: jnp.sum(jnp.where(is_block_expert, v[None, :], 0), axis=1).astype(jnp.int32)
    left = per_block(pad_start + counts) - blk_row[:, 0]
    both_halves = (left > EXPERT_HALF).astype(jnp.int32)
    y_rows = _moe(block_expert, per_block(next_of), per_block(parity_of), both_halves, n_active.reshape(1),
                  x_rows, w_gate_up, bg, bu, w_down, b_down.reshape(N_EXPERTS, 1, d))
    out = _combine(dest8, x1.reshape(n_tok, d), mod, route, row2(ln2_gain), row2(ln2_bias), y_rows, seq)
    return out.reshape(bsz, seq, d)


def kernel(x, c, w_ada, b_ada, w_in, w_pool_group, pool_scale, w_branch_a, w_alpha_up, b_alpha, gla_norm_gain,
           w_branch_b, w_out, ln1_gain, ln1_bias, w_router, b_router, w_gate_up, b_gate_up, w_down, b_down,
           ln2_gain, ln2_bias):
    for l in range(DEPTH):
        x = _layer(x, c, w_ada[l], b_ada[l], w_in[l], w_pool_group[l], pool_scale[l], w_branch_a[l],
                   w_alpha_up[l], b_alpha[l], gla_norm_gain[l], w_branch_b[l], w_out[l], ln1_gain[l],
                   ln1_bias[l], w_router[l], b_router[l], w_gate_up[l], b_gate_up[l], w_down[l], b_down[l],
                   ln2_gain[l], ln2_bias[l])
    return x
```

```python
import functools

import jax
import jax.numpy as jnp
from jax import lax
from jax.experimental import pallas as pl
from jax.experimental.pallas import tpu as pltpu

D_MODEL = 1024
CHUNK = 64
SUB = 16
N_SUB = CHUNK // SUB
POOL_WINDOWS = (2, 4, 8, 16)
POOL_GROUP_WIDTH = D_MODEL // len(POOL_WINDOWS)
POOL_HALO = 16
GLA_HEADS = 4
GLA_KEY_DIM = D_MODEL // 2
GLA_HEAD_K = GLA_KEY_DIM // GLA_HEADS
GLA_HEAD_V = D_MODEL // GLA_HEADS
GLA_GATE_RANK = 16
GLA_TAU = 16.0
N_EXPERTS = 32
TOP_K = 4
SWIGLU_ALPHA = 1.702
SWIGLU_LIMIT = 7.0
EXPERT_BLOCK = 512
EXPERT_HALF = EXPERT_BLOCK // 2
LN_EPS = 1e-5
RMS_EPS = 1e-6
DEPTH = 1
DEEPNORM_ALPHA = (2.0 * DEPTH) ** 0.25

LANES = 128
MXU_COLS = 256
RANK_PAD = LANES
_W = (D_MODEL, GLA_KEY_DIM, GLA_KEY_DIM, D_MODEL, D_MODEL, RANK_PAD, D_MODEL, D_MODEL)
_OFF = tuple(sum(_W[:i]) for i in range(len(_W) + 1))
EXP_CAP = 60.0
ROUTE_ROWS = 16

SEQ_TILE = 512
DISPATCH_TILE = 256
COMBINE_TILE = 256
DMA_UNROLL = 8
VMEM_LIMIT = 56 * 1024 * 1024

F32 = jnp.float32
BF16 = jnp.bfloat16
HI = lax.Precision.HIGHEST


def _dot(a, b):
    return jnp.dot(a, b, preferred_element_type=F32)


def _dot_nt(a, b):
    return lax.dot_general(a, b, (((1,), (1,)), ((), ())), preferred_element_type=F32)


def _dot_tn(a, b):
    return lax.dot_general(a, b, (((0,), (0,)), ((), ())), preferred_element_type=F32)


def _sigmoid(v):
    return 0.5 * jnp.tanh(0.5 * v) + 0.5


ROW_SUB = D_MODEL // LANES


def _load_rows(ref, n_rows, lead=()):
    return jnp.concatenate(
        [ref[lead + (pl.ds(j, n_rows, stride=ROW_SUB), slice(None))] for j in range(ROW_SUB)], axis=1)


def _store_rows(ref, val):
    for j in range(ROW_SUB):
        ref[pl.ds(j, val.shape[0], stride=ROW_SUB), :] = val[:, j * LANES:(j + 1) * LANES]


def _ada_kernel(c_ref, w_ref, b_ref, o_ref):
    c = c_ref[...]
    s = c * jax.nn.sigmoid(c)
    o_ref[...] = jnp.dot(s, w_ref[...], precision=HI, preferred_element_type=F32) + b_ref[...]


def _ada(c, w_ada, b_ada):
    bsz, d = c.shape
    n = w_ada.shape[1]
    tn = 1024
    return pl.pallas_call(
        _ada_kernel,
        grid=(n // tn,),
        in_specs=[
            pl.BlockSpec((bsz, d), lambda j: (0, 0)),
            pl.BlockSpec((d, tn), lambda j: (0, j)),
            pl.BlockSpec((1, tn), lambda j: (0, j)),
        ],
        out_specs=pl.BlockSpec((bsz, tn), lambda j: (0, j)),
        out_shape=jax.ShapeDtypeStruct((bsz, n), F32),
        name="ada",
    )(c, w_ada, b_ada.reshape(1, n))


def _layer_norm(z, gain, bias):
    mu = jnp.mean(z, axis=-1, keepdims=True)
    zc = z - mu
    var = jnp.mean(zc * zc, axis=-1, keepdims=True)
    return zc * lax.rsqrt(var + LN_EPS) * gain + bias


def _mixer_kernel(x_ref, mod_ref, mod_prev_ref, w_in_ref, w_pool_ref, pool_scale_ref, w_a_ref, w_al_ref,
                  b_al_ref, gain_ref, w_b_ref, w_out_ref, ln_g_ref, ln_b_ref, w_r_ref, b_r_ref,
                  x1_ref, u2_ref, route_ref, counts_ref,
                  a_ext, s_ref, cnt_ref, o_ref, z_ref, *, tiles_per_seq, n_tiles):
    j = pl.program_id(0)
    s_idx = lax.rem(jnp.minimum(j, n_tiles - 1), tiles_per_seq)
    tile = x_ref.shape[1]

    @pl.when(s_idx == 0)
    def _():
        s_ref[...] = jnp.zeros_like(s_ref)
        a_ext[0:POOL_HALO, :] = jnp.zeros((POOL_HALO, D_MODEL), F32)

    @pl.when(j == 0)
    def _():
        cnt_ref[...] = jnp.zeros_like(cnt_ref)
        z_ref[...] = jnp.zeros_like(z_ref)


    mod_prev = mod_prev_ref[0]
    x1 = _layer_norm(z_ref[...], ln_g_ref[...], ln_b_ref[...])
    x1_ref[0] = x1
    u2 = x1 * (1.0 + mod_prev[4:5]) + mod_prev[3:4]
    _store_rows(u2_ref, u2)
    u2_hi = u2.astype(BF16)
    u2_lo = (u2 - u2_hi.astype(F32)).astype(BF16)

    mod = mod_ref[0]
    sh_m, sc_m, g_m = mod[0:1], mod[1:2], mod[2:3]
    x = x_ref[0]
    u = (x * (1.0 + sc_m) + sh_m).astype(BF16)

    def proj(i):
        return _dot(u, w_in_ref[:, _OFF[i]:_OFF[i + 1]])

    gate_cols = [_OFF[i] + jj * MXU_COLS for i in (4, 6, 7) for jj in range(D_MODEL // MXU_COLS)]
    gate_parts = []

    def issue_gate_pieces(n):
        for _ in range(n):
            if len(gate_parts) < len(gate_cols):
                c0 = gate_cols[len(gate_parts)]
                gate_parts.append(_dot(u, w_in_ref[:, c0:c0 + MXU_COLS]))

    a = proj(0)
    q = proj(1) * (GLA_HEAD_K ** -0.5)
    k_all = proj(2)
    v_all = proj(3)
    alpha_low = proj(5)

    part = _dot_nt(w_r_ref[...], u2_hi)
    logits = (part[:N_EXPERTS] + part[N_EXPERTS:] + _dot_nt(w_r_ref[0:N_EXPERTS, :], u2_lo)
              + b_r_ref[...])
    issue_gate_pieces(2)

    a_ext[POOL_HALO:POOL_HALO + tile, :] = a
    t_glob = s_idx * tile + lax.broadcasted_iota(jnp.int32, (tile, 1), 0)
    mapped = []
    for g, w in enumerate(POOL_WINDOWS):
        lo, hi = g * POOL_GROUP_WIDTH, (g + 1) * POOL_GROUP_WIDTH
        win = a_ext[:, lo:hi]
        k = 1
        while k < w:
            win = win + pltpu.roll(win, k, 0)
            k *= 2
        inv_cnt = 1.0 / jnp.minimum(t_glob + 1, w).astype(F32)
        pooled = win[POOL_HALO:, :] * inv_cnt - a[:, lo:hi]
        mapped.append(_dot(pooled.astype(BF16), w_pool_ref[g]))
    a_ext[0:POOL_HALO, :] = a[tile - POOL_HALO:tile, :]
    ya = jnp.concatenate(mapped, axis=1) * pool_scale_ref[...]
    ya = _dot(ya.astype(BF16), w_a_ref[...])

    erow = lax.broadcasted_iota(jnp.int32, (N_EXPERTS, tile), 0).astype(F32)
    work = logits
    sel = jnp.zeros((N_EXPERTS, tile), F32)
    vals, hits = [], []
    for _ in range(TOP_K):
        m = jnp.max(work, axis=0, keepdims=True)
        idx = jnp.min(jnp.where(work == m, erow, float(N_EXPERTS)), axis=0, keepdims=True)
        hit = erow == idx
        vals.append(m)
        hits.append((idx, hit))
        sel = jnp.where(hit, 1.0, sel)
        work = jnp.where(hit, -jnp.inf, work)
    exps = [jnp.exp(v - vals[0]) for v in vals]
    inv_den = 1.0 / (exps[0] + exps[1] + exps[2] + exps[3])
    issue_gate_pieces(1)

    z = _dot(alpha_low.astype(BF16), w_al_ref[...]) + b_al_ref[...]
    log_a = (jnp.minimum(z, 0.0) - jnp.log1p(jnp.exp(-jnp.abs(z)))) * (1.0 / GLA_TAU)

    ri = lax.broadcasted_iota(jnp.int32, (CHUNK, CHUNK), 0)
    ci = lax.broadcasted_iota(jnp.int32, (CHUNK, CHUNK), 1)
    causal = ci <= ri
    cum_mat = causal.astype(BF16)
    la_hi = log_a.astype(BF16)
    la_split = jnp.concatenate([la_hi, (log_a - la_hi.astype(F32)).astype(BF16)], axis=1)

    n_chunks = tile // CHUNK
    pairs = [(c, h) for c in range(n_chunks) for h in range(GLA_HEADS)]
    rows_of = lambda c: slice(c * CHUNK, (c + 1) * CHUNK)
    ks_of = lambda h: slice(h * GLA_HEAD_K, (h + 1) * GLA_HEAD_K)
    vs_of = lambda h: slice(h * GLA_HEAD_V, (h + 1) * GLA_HEAD_V)

    b_cum, ref_pts, b_ref_pt = [], [], []
    for c in range(n_chunks):
        cum = _dot(cum_mat, la_split[rows_of(c)])
        b_cum.append(cum[:, :GLA_KEY_DIM] + cum[:, GLA_KEY_DIM:])
        ref_pts.append([jnp.zeros((1, GLA_KEY_DIM), F32)]
                       + [b_cum[c][i * SUB - 1:i * SUB, :] for i in range(1, N_SUB)])
        b_ref_pt.append(jnp.concatenate([jnp.broadcast_to(p, (SUB, GLA_KEY_DIM)) for p in ref_pts[c]], axis=0))

    rt = lax.broadcasted_iota(jnp.int32, (tile, tile), 0)
    ct = lax.broadcasted_iota(jnp.int32, (tile, tile), 1)
    base = _dot(sel.astype(BF16), (rt < ct).astype(BF16)) + cnt_ref[:, 0:1]

    s_all = {}
    for n_pair, (c, h) in enumerate(pairs):
        if n_pair % 3 == 0:
            issue_gate_pieces(1)
        qh, kh, bh = q[rows_of(c), ks_of(h)], k_all[rows_of(c), ks_of(h)], b_cum[c][:, ks_of(h)]
        q_dec = (qh * jnp.exp(bh - b_ref_pt[c][:, ks_of(h)])).astype(BF16)
        k_dec = jnp.concatenate(
            [(kh * jnp.exp(jnp.minimum(ref_pts[c][i][:, ks_of(h)] - bh, EXP_CAP))).astype(BF16)
             for i in range(N_SUB)], axis=0)
        s_all[c, h] = _dot_nt(q_dec, k_dec)

    o_intra, kv, decay_last = {}, {}, {}
    for c, h in pairs:
        kh, bh = k_all[rows_of(c), ks_of(h)], b_cum[c][:, ks_of(h)]
        vh = v_all[rows_of(c), vs_of(h)].astype(BF16)
        scores = jnp.concatenate(
            [s_all[c, h][i * SUB:(i + 1) * SUB, i * CHUNK:(i + 1) * CHUNK] for i in range(N_SUB)], axis=0)
        o_intra[c, h] = _dot(jnp.where(causal, scores, 0.0).astype(BF16), vh)
        b_last = bh[CHUNK - 1:CHUNK, :]
        kv[c, h] = _dot_tn(vh, (kh * jnp.exp(b_last - bh)).astype(BF16))
        decay_last[c, h] = jnp.exp(b_last)

    orow = lax.broadcasted_iota(jnp.int32, (ROUTE_ROWS, tile), 0)
    route = jnp.zeros((ROUTE_ROWS, tile), F32)
    for kk in range(TOP_K):
        idx, hit = hits[kk]
        rank = jnp.sum(jnp.where(hit, base, 0.0), axis=0, keepdims=True)
        route = jnp.where(orow == kk, idx, route)
        route = jnp.where(orow == TOP_K + kk, rank, route)
        route = jnp.where(orow == 2 * TOP_K + kk, exps[kk] * inv_den, route)
    route_ref[...] = route
    cnt_ref[...] = cnt_ref[...] + (j > 0).astype(F32) * jnp.sum(sel, axis=1, keepdims=True)
    counts_ref[...] = cnt_ref[...]

    o_inter = {}
    state_t = [s_ref[h] for h in range(GLA_HEADS)]
    for c, h in pairs:
        q_in = (q[rows_of(c), ks_of(h)] * jnp.exp(b_cum[c][:, ks_of(h)])).astype(BF16)
        o_inter[c, h] = _dot_nt(q_in, state_t[h].astype(BF16))
        state_t[h] = state_t[h] * decay_last[c, h] + kv[c, h]
    for h in range(GLA_HEADS):
        s_ref[h] = state_t[h]

    for c, h in pairs:
        o = o_intra[c, h] + o_inter[c, h]
        o = o * lax.rsqrt(jnp.mean(o * o, axis=-1, keepdims=True) + RMS_EPS) * gain_ref[...]
        o_ref[rows_of(c), vs_of(h)] = o

    issue_gate_pieces(len(gate_cols))
    per_proj = D_MODEL // MXU_COLS
    r, gate_a, gate_b = (jnp.concatenate(gate_parts[n * per_proj:(n + 1) * per_proj], axis=1) for n in range(3))
    yb = _dot((o_ref[...] * (r * _sigmoid(r))).astype(BF16), w_b_ref[...])

    merged = _sigmoid(gate_a) * ya + _sigmoid(gate_b) * yb
    y = _dot(merged.astype(BF16), w_out_ref[...])
    z_ref[...] = DEEPNORM_ALPHA * x + g_m * y


def _mixer(x, mod, w_in_p, w_pool, pool_scale, w_a, w_al_p, b_alpha, gain, w_b, w_out, ln_g, ln_b,
           w_r_p, b_r_p):
    bsz, seq, d = x.shape
    tile = SEQ_TILE
    n_s = seq // tile
    n_tiles = bsz * n_s

    def const(shape):
        nd = len(shape)
        return pl.BlockSpec(shape, lambda j: (0,) * nd, pipeline_mode=pl.Buffered(1))

    cur = lambda j: jnp.minimum(j, n_tiles - 1)
    prev = lambda j: jnp.maximum(j - 1, 0)
    return pl.pallas_call(
        functools.partial(_mixer_kernel, tiles_per_seq=n_s, n_tiles=n_tiles),
        grid=(n_tiles + 1,),
        in_specs=[
            pl.BlockSpec((1, tile, d), lambda j: (cur(j) // n_s, cur(j) % n_s, 0)),
            pl.BlockSpec((1, 6, d), lambda j: (cur(j) // n_s, 0, 0)),
            pl.BlockSpec((1, 6, d), lambda j: (prev(j) // n_s, 0, 0)),
            const(w_in_p.shape), const(w_pool.shape), const(pool_scale.shape), const(w_a.shape),
            const(w_al_p.shape), const(b_alpha.shape), const(gain.shape), const(w_b.shape),
            const(w_out.shape), const(ln_g.shape), const(ln_b.shape), const(w_r_p.shape),
            const(b_r_p.shape),
        ],
        out_specs=[pl.BlockSpec((1, tile, d), lambda j: (prev(j) // n_s, prev(j) % n_s, 0)),
                   pl.BlockSpec((tile * ROW_SUB, LANES), lambda j: (prev(j), 0)),
                   pl.BlockSpec((ROUTE_ROWS, tile), lambda j: (0, prev(j))),
                   pl.BlockSpec((N_EXPERTS, LANES), lambda j: (0, 0))],
        out_shape=[
            jax.ShapeDtypeStruct((bsz, seq, d), F32),
            jax.ShapeDtypeStruct((bsz * seq * ROW_SUB, LANES), F32),
            jax.ShapeDtypeStruct((ROUTE_ROWS, bsz * seq), F32),
            jax.ShapeDtypeStruct((N_EXPERTS, LANES), F32),
        ],
        scratch_shapes=[
            pltpu.VMEM((POOL_HALO + tile, d), F32),
            pltpu.VMEM((GLA_HEADS, GLA_HEAD_V, GLA_HEAD_K), F32),
            pltpu.VMEM((N_EXPERTS, LANES), F32),
            pltpu.VMEM((tile, d), F32),
            pltpu.VMEM((tile, d), F32),
        ],
        compiler_params=pltpu.CompilerParams(
            dimension_semantics=("arbitrary",), vmem_limit_bytes=VMEM_LIMIT),
        name="mixer",
    )(x, mod, mod, w_in_p, w_pool, pool_scale, w_a, w_al_p, b_alpha, gain, w_b, w_out, ln_g, ln_b,
      w_r_p, b_r_p)


def _row_at(ref, row8):
    return ref.at[pl.ds(pl.multiple_of(row8, ROW_SUB), ROW_SUB)]


def _dispatch_kernel(fill_ref, dest_ref, u2_ref, rows_hbm, zbuf, sem_fill, sem_rows):
    i = pl.program_id(0)
    tile = u2_ref.shape[0] // ROW_SUB
    blk8 = EXPERT_BLOCK * ROW_SUB

    @pl.when(i == 0)
    def _():
        zbuf[...] = jnp.zeros_like(zbuf)
        n_windows = rows_hbm.shape[0] // blk8

        def fill(window):
            return pltpu.make_async_copy(zbuf, rows_hbm.at[pl.ds(pl.multiple_of(window * blk8, blk8), blk8)],
                                         sem_fill.at[0])

        for phase in ("start", "wait"):
            for e in range(N_EXPERTS):
                @pl.when(fill_ref[e] >= 0)
                def _():
                    getattr(fill(fill_ref[e]), phase)()

            def unused(window, carry):
                getattr(fill(window), phase)()
                return carry

            lax.fori_loop(fill_ref[N_EXPERTS], n_windows, unused, 0)

    def body(r, carry):
        for kk in range(TOP_K):
            pltpu.make_async_copy(_row_at(u2_ref, r * ROW_SUB), _row_at(rows_hbm, dest_ref[0, 0, r * TOP_K + kk]),
                                  sem_rows.at[0]).start(priority=kk % 2)
        return carry

    lax.fori_loop(0, tile, body, 0, unroll=DMA_UNROLL)
    for _ in range(TOP_K):
        pltpu.make_async_copy(u2_ref, rows_hbm.at[pl.ds(0, tile * ROW_SUB)], sem_rows.at[0]).wait()


def _dispatch(fill_start, dest8, u2_rows, n_rows):
    n_tok = u2_rows.shape[0] // ROW_SUB
    tile = DISPATCH_TILE
    n_t = n_tok // tile
    grid_spec = pltpu.PrefetchScalarGridSpec(
        num_scalar_prefetch=1,
        grid=(n_t,),
        in_specs=[
            pl.BlockSpec((1, 1, tile * TOP_K), lambda i, fs: (i, 0, 0), memory_space=pltpu.SMEM),
            pl.BlockSpec((tile * ROW_SUB, LANES), lambda i, fs: (i, 0)),
        ],
        out_specs=pl.BlockSpec(memory_space=pl.ANY),
        scratch_shapes=[
            pltpu.VMEM((EXPERT_BLOCK * ROW_SUB, LANES), F32),
            pltpu.SemaphoreType.DMA((1,)),
            pltpu.SemaphoreType.DMA((1,)),
        ],
    )
    return pl.pallas_call(
        _dispatch_kernel,
        grid_spec=grid_spec,
        out_shape=jax.ShapeDtypeStruct(((n_rows + EXPERT_BLOCK) * ROW_SUB, LANES), F32),
        compiler_params=pltpu.CompilerParams(dimension_semantics=("arbitrary",)),
        name="dispatch",
    )(fill_start, dest8.reshape(n_t, 1, tile * TOP_K), u2_rows)


def _moe_kernel(be_ref, nxt_ref, par_ref, full_ref, nb_ref, x_ref, bg_ref, bu_ref, bd_ref, wgu_hbm, wd_hbm, y_ref,
                wgu_buf, wd_buf, wg_s, wu_s, wd_s, sem_gu, sem_d):
    i = pl.program_id(0)
    new_expert = (i == 0) | (be_ref[i] != be_ref[jnp.maximum(i - 1, 0)])

    def fetch(e, s):
        return (pltpu.make_async_copy(wgu_hbm.at[e], wgu_buf.at[s], sem_gu.at[s]),
                pltpu.make_async_copy(wd_hbm.at[e], wd_buf.at[s], sem_d.at[s]))

    @pl.when(new_expert & (i < nb_ref[0]))
    def _():
        slot = par_ref[i]

        @pl.when(i == 0)
        def _():
            for cp in fetch(be_ref[0], slot):
                cp.start()

        @pl.when(nxt_ref[i] >= 0)
        def _():
            for cp in fetch(nxt_ref[i], 1 - slot):
                cp.start()

        for cp in fetch(be_ref[i], slot):
            cp.wait()
        src = lax.broadcasted_iota(jnp.int32, (MXU_COLS, MXU_COLS), 0)
        col = lax.broadcasted_iota(jnp.int32, (MXU_COLS, MXU_COLS), 1)
        half = MXU_COLS // 2
        want = jnp.where(col < half, 2 * col, 2 * (col - half) + 1)
        unzip = (src == want).astype(BF16)
        for g in range(wgu_buf.shape[2] // MXU_COLS):
            blk = wgu_buf[slot, :, g * MXU_COLS:(g + 1) * MXU_COLS].astype(BF16)
            sep = _dot(blk, unzip)
            wg_s[:, g * half:(g + 1) * half] = sep[:, :half].astype(BF16)
            wu_s[:, g * half:(g + 1) * half] = sep[:, half:].astype(BF16)
        wd_s[...] = wd_buf[slot].astype(BF16)

    def expert_mlp(n_rows):
        xb = _load_rows(x_ref, n_rows).astype(BF16)
        gate = jnp.minimum(_dot(xb, wg_s[...]) + bg_ref[0], SWIGLU_LIMIT)
        up = jnp.clip(_dot(xb, wu_s[...]) + bu_ref[0], -SWIGLU_LIMIT, SWIGLU_LIMIT)
        glu = gate * _sigmoid(gate * SWIGLU_ALPHA)
        _store_rows(y_ref, _dot(((up + 1.0) * glu).astype(BF16), wd_s[...]) + bd_ref[0])

    active = i < nb_ref[0]
    both_halves = full_ref[i] == 1

    @pl.when(active & both_halves)
    def _():
        expert_mlp(EXPERT_BLOCK)

    @pl.when(active & jnp.logical_not(both_halves))
    def _():
        expert_mlp(EXPERT_HALF)
        y_ref[EXPERT_HALF * ROW_SUB:, :] = jnp.zeros((EXPERT_HALF * ROW_SUB, LANES), F32)

    @pl.when(jnp.logical_not(active))
    def _():
        y_ref[...] = jnp.zeros_like(y_ref)


def _moe(block_expert, next_expert, slot_parity, both_halves, n_active, x_rows, w_gate_up, bg, bu, w_down, bd):
    n_blocks = block_expert.shape[0]
    _, d, f2 = w_gate_up.shape
    f = f2 // 2
    blk8 = EXPERT_BLOCK * ROW_SUB
    rows_in = pl.BlockSpec((blk8, LANES),
                           lambda i, be, nx, pr, fl, nb: (jnp.maximum(jnp.minimum(i, nb[0] - 1), 0), 0))
    rows_out = pl.BlockSpec((blk8, LANES), lambda i, be, nx, pr, fl, nb: (i, 0))
    per_expert = lambda shape: pl.BlockSpec((1,) + shape, lambda i, be, nx, pr, fl, nb: (be[i], 0, 0))
    hbm = pl.BlockSpec(memory_space=pl.ANY)
    grid_spec = pltpu.PrefetchScalarGridSpec(
        num_scalar_prefetch=5,
        grid=(n_blocks,),
        in_specs=[rows_in, per_expert((1, f)), per_expert((1, f)), per_expert((1, d)), hbm, hbm],
        out_specs=rows_out,
        scratch_shapes=[pltpu.VMEM((2, d, f2), F32), pltpu.VMEM((2, f, d), F32),
                        pltpu.VMEM((d, f), BF16), pltpu.VMEM((d, f), BF16), pltpu.VMEM((f, d), BF16),
                        pltpu.SemaphoreType.DMA((2,)), pltpu.SemaphoreType.DMA((2,))],
    )
    return pl.pallas_call(
        _moe_kernel,
        grid_spec=grid_spec,
        out_shape=jax.ShapeDtypeStruct((n_blocks * blk8, LANES), F32),
        compiler_params=pltpu.CompilerParams(
            dimension_semantics=("arbitrary",), vmem_limit_bytes=VMEM_LIMIT),
        name="moe",
    )(block_expert, next_expert, slot_parity, both_halves, n_active, x_rows, bg, bu, bd, w_gate_up, w_down)


def _combine_kernel(dest_cur_ref, dest_nxt_ref, x1_ref, mod_ref, route_ref, ln_g_ref, ln_b_ref, y_hbm,
                    o_ref, ybuf, sem):
    i = pl.program_id(0)
    n_steps = pl.num_programs(0)
    tile = x1_ref.shape[0]
    slot = lax.rem(i, 2)

    def gather(dest_ref, s):
        def body(r, carry):
            for kk in range(TOP_K):
                pltpu.make_async_copy(_row_at(y_hbm, dest_ref[0, 0, r * TOP_K + kk]),
                                      _row_at(ybuf.at[s, kk], r * ROW_SUB), sem.at[s]).start(priority=kk % 2)
            return carry
        lax.fori_loop(0, tile, body, 0, unroll=DMA_UNROLL)

    @pl.when(i == 0)
    def _():
        gather(dest_cur_ref, 0)

    @pl.when(i + 1 < n_steps)
    def _():
        gather(dest_nxt_ref, 1 - slot)

    for kk in range(TOP_K):
        pltpu.make_async_copy(y_hbm.at[pl.ds(0, tile * ROW_SUB)], ybuf.at[slot, kk], sem.at[slot]).wait()

    g_f = mod_ref[0][5:6]
    weight = route_ref[...]
    y = jnp.zeros(x1_ref.shape, F32)
    for kk in range(TOP_K):
        y = y + weight[:, kk:kk + 1] * _load_rows(ybuf, tile, (slot, kk))
    o_ref[...] = _layer_norm(DEEPNORM_ALPHA * x1_ref[...] + g_f * y, ln_g_ref[...], ln_b_ref[...])


def _combine(dest8, x1, mod, route, ln_g, ln_b, y_rows, seq):
    n_tok, d = x1.shape
    tile = COMBINE_TILE
    n_t = n_tok // tile
    per_seq = seq // tile
    dest3 = dest8.reshape(n_t, 1, tile * TOP_K)
    smem_blk = lambda fn: pl.BlockSpec((1, 1, tile * TOP_K), fn, memory_space=pltpu.SMEM)
    row = lambda width: pl.BlockSpec((tile, width), lambda i: (i, 0))
    vec = pl.BlockSpec((1, d), lambda i: (0, 0))
    return pl.pallas_call(
        _combine_kernel,
        grid=(n_t,),
        in_specs=[smem_blk(lambda i: (i, 0, 0)), smem_blk(lambda i: (jnp.minimum(i + 1, n_t - 1), 0, 0)),
                  row(d), pl.BlockSpec((1, 6, d), lambda i: (i // per_seq, 0, 0)), row(TOP_K), vec, vec,
                  pl.BlockSpec(memory_space=pl.ANY)],
        out_specs=row(d),
        out_shape=jax.ShapeDtypeStruct((n_tok, d), F32),
        scratch_shapes=[pltpu.VMEM((2, TOP_K, tile * ROW_SUB, LANES), F32), pltpu.SemaphoreType.DMA((2,))],
        compiler_params=pltpu.CompilerParams(
            dimension_semantics=("arbitrary",), vmem_limit_bytes=VMEM_LIMIT),
        name="combine",
    )(dest3, dest3, x1, mod, route, ln_g, ln_b, y_rows)


def _pad_in_proj(w_in):
    rank_end = 3 * D_MODEL + 2 * GLA_KEY_DIM + GLA_GATE_RANK
    pad = jnp.zeros((w_in.shape[0], RANK_PAD - GLA_GATE_RANK), BF16)
    return jnp.concatenate([w_in[:, :rank_end].astype(BF16), pad, w_in[:, rank_end:].astype(BF16)], axis=1)


def _layer(x, c, w_ada, b_ada, w_in, w_pool_group, pool_scale, w_branch_a, w_alpha_up, b_alpha,
           gla_norm_gain, w_branch_b, w_out, ln1_gain, ln1_bias, w_router, b_router,
           w_gate_up, b_gate_up, w_down, b_down, ln2_gain, ln2_bias):
    bsz, seq, d = x.shape
    n_tok = bsz * seq
    n_assign = n_tok * TOP_K
    row2 = lambda v: v.reshape(1, -1)

    mod = _ada(c, w_ada, b_ada).reshape(bsz, 6, d)

    w_in_p = _pad_in_proj(w_in)
    w_al_p = jnp.concatenate(
        [w_alpha_up, jnp.zeros((RANK_PAD - GLA_GATE_RANK, GLA_KEY_DIM), w_alpha_up.dtype)], axis=0).astype(BF16)
    w_r_t = w_router.T
    w_r_hi = w_r_t.astype(BF16)
    w_r_split = jnp.concatenate([w_r_hi, (w_r_t - w_r_hi.astype(F32)).astype(BF16)], axis=0)
    x1, u2_rows, route_t, counts = _mixer(
        x, mod, w_in_p, w_pool_group.astype(BF16), row2(pool_scale), w_branch_a.astype(BF16), w_al_p,
        row2(b_alpha), row2(gla_norm_gain), w_branch_b.astype(BF16), w_out.astype(BF16), row2(ln1_gain),
        row2(ln1_bias), w_r_split, b_router.reshape(N_EXPERTS, 1))

    top_idx = route_t[0:TOP_K].T.astype(jnp.int32)
    rank = route_t[TOP_K:2 * TOP_K].T.astype(jnp.int32)
    route = route_t[2 * TOP_K:3 * TOP_K].T
    counts = counts[:, 0].astype(jnp.int32)
    padded = (counts + EXPERT_BLOCK - 1) // EXPERT_BLOCK * EXPERT_BLOCK
    pad_end = jnp.cumsum(padded)
    pad_start = pad_end - padded
    n_rows = (n_assign + N_EXPERTS * (EXPERT_BLOCK - 1) + EXPERT_BLOCK - 1) // EXPERT_BLOCK * EXPERT_BLOCK
    n_blocks = n_rows // EXPERT_BLOCK
    n_active = (pad_end[-1] // EXPERT_BLOCK).astype(jnp.int32)
    start_of = jnp.sum(jnp.where(top_idx[..., None] == jnp.arange(N_EXPERTS, dtype=jnp.int32), pad_start, 0),
                       axis=-1)
    dest8 = ((start_of + rank) * ROW_SUB).reshape(-1)
    blk_row = jnp.arange(n_blocks, dtype=jnp.int32)[:, None] * EXPERT_BLOCK
    block_expert = jnp.minimum(jnp.sum(pad_end[None, :] <= blk_row, axis=1), N_EXPERTS - 1).astype(jnp.int32)

    last_window = jnp.where(padded > 0, pad_end // EXPERT_BLOCK - 1, -1)
    fill_windows = jnp.concatenate([last_window, n_active.reshape(1)]).astype(jnp.int32)
    x_rows = _dispatch(fill_windows, dest8, u2_rows, n_rows)
    f = w_down.shape[1]
    bg = b_gate_up[:, 0::2].reshape(N_EXPERTS, 1, f)
    bu = b_gate_up[:, 1::2].reshape(N_EXPERTS, 1, f)
    eid = jnp.arange(N_EXPERTS, dtype=jnp.int32)
    owns = counts > 0
    later = jnp.where((eid[None, :] > eid[:, None]) & owns[None, :], eid[None, :], N_EXPERTS)
    next_of = jnp.min(later, axis=1)
    next_of = jnp.where(next_of == N_EXPERTS, -1, next_of).astype(jnp.int32)
    parity_of = ((jnp.cumsum(owns.astype(jnp.int32)) - 1) % 2).astype(jnp.int32)
    is_block_expert = block_expert[:, None] == eid[None, :]
    per_block = lambda v: jnp.sum(jnp.where(is_block_expert, v[None, :], 0), axis=1).astype(jnp.int32)
    left = per_block(pad_start + counts) - blk_row[:, 0]
    both_halves = (left > EXPERT_HALF).astype(jnp.int32)
    y_rows = _moe(block_expert, per_block(next_of), per_block(parity_of), both_halves, n_active.reshape(1),
                  x_rows, w_gate_up, bg, bu, w_down, b_down.reshape(N_EXPERTS, 1, d))
    out = _combine(dest8, x1.reshape(n_tok, d), mod, route, row2(ln2_gain), row2(ln2_bias), y_rows, seq)
    return out.reshape(bsz, seq, d)


def kernel(x, c, w_ada, b_ada, w_in, w_pool_group, pool_scale, w_branch_a, w_alpha_up, b_alpha, gla_norm_gain,
           w_branch_b, w_out, ln1_gain, ln1_bias, w_router, b_router, w_gate_up, b_gate_up, w_down, b_down,
           ln2_gain, ln2_bias):
    for l in range(DEPTH):
        x = _layer(x, c, w_ada[l], b_ada[l], w_in[l], w_pool_group[l], pool_scale[l], w_branch_a[l],
                   w_alpha_up[l], b_alpha[l], gla_norm_gain[l], w_branch_b[l], w_out[l], ln1_gain[l],
                   ln1_bias[l], w_router[l], b_router[l], w_gate_up[l], b_gate_up[l], w_down[l], b_down[l],
                   ln2_gain[l], ln2_bias[l])
    return x
```

```python
import functools

import jax
import jax.numpy as jnp
from jax import lax
from jax.experimental import pallas as pl
from jax.experimental.pallas import tpu as pltpu

D_MODEL = 1024
CHUNK = 64
SUB = 16
N_SUB = CHUNK // SUB
POOL_WINDOWS = (2, 4, 8, 16)
POOL_GROUP_WIDTH = D_MODEL // len(POOL_WINDOWS)
POOL_HALO = 16
GLA_HEADS = 4
GLA_KEY_DIM = D_MODEL // 2
GLA_HEAD_K = GLA_KEY_DIM // GLA_HEADS
GLA_HEAD_V = D_MODEL // GLA_HEADS
GLA_GATE_RANK = 16
GLA_TAU = 16.0
N_EXPERTS = 32
TOP_K = 4
SWIGLU_ALPHA = 1.702
SWIGLU_LIMIT = 7.0
EXPERT_BLOCK = 512
EXPERT_HALF = EXPERT_BLOCK // 2
LN_EPS = 1e-5
RMS_EPS = 1e-6
DEPTH = 1
DEEPNORM_ALPHA = (2.0 * DEPTH) ** 0.25

LANES = 128
MXU_COLS = 256
RANK_PAD = LANES
_W = (D_MODEL, GLA_KEY_DIM, GLA_KEY_DIM, D_MODEL, D_MODEL, RANK_PAD, D_MODEL, D_MODEL)
_OFF = tuple(sum(_W[:i]) for i in range(len(_W) + 1))
EXP_CAP = 60.0
ROUTE_ROWS = 16

SEQ_TILE = 512
DISPATCH_TILE = 256
COMBINE_TILE = 256
DMA_UNROLL = 8
VMEM_LIMIT = 56 * 1024 * 1024

F32 = jnp.float32
BF16 = jnp.bfloat16
HI = lax.Precision.HIGHEST


def _dot(a, b):
    return jnp.dot(a, b, preferred_element_type=F32)


def _dot_nt(a, b):
    return lax.dot_general(a, b, (((1,), (1,)), ((), ())), preferred_element_type=F32)


def _dot_tn(a, b):
    return lax.dot_general(a, b, (((0,), (0,)), ((), ())), preferred_element_type=F32)


def _sigmoid(v):
    return 0.5 * jnp.tanh(0.5 * v) + 0.5


ROW_SUB = D_MODEL // LANES


def _load_rows(ref, n_rows, lead=()):
    return jnp.concatenate(
        [ref[lead + (pl.ds(j, n_rows, stride=ROW_SUB), slice(None))] for j in range(ROW_SUB)], axis=1)


def _store_rows(ref, val):
    for j in range(ROW_SUB):
        ref[pl.ds(j, val.shape[0], stride=ROW_SUB), :] = val[:, j * LANES:(j + 1) * LANES]


def _ada_kernel(c_ref, w_ref, b_ref, o_ref):
    c = c_ref[...]
    s = c * jax.nn.sigmoid(c)
    o_ref[...] = jnp.dot(s, w_ref[...], precision=HI, preferred_element_type=F32) + b_ref[...]


def _ada(c, w_ada, b_ada):
    bsz, d = c.shape
    n = w_ada.shape[1]
    tn = 1024
    return pl.pallas_call(
        _ada_kernel,
        grid=(n // tn,),
        in_specs=[
            pl.BlockSpec((bsz, d), lambda j: (0, 0)),
            pl.BlockSpec((d, tn), lambda j: (0, j)),
            pl.BlockSpec((1, tn), lambda j: (0, j)),
        ],
        out_specs=pl.BlockSpec((bsz, tn), lambda j: (0, j)),
        out_shape=jax.ShapeDtypeStruct((bsz, n), F32),
        name="ada",
    )(c, w_ada, b_ada.reshape(1, n))


def _layer_norm(z, gain, bias):
    mu = jnp.mean(z, axis=-1, keepdims=True)
    zc = z - mu
    var = jnp.mean(zc * zc, axis=-1, keepdims=True)
    return zc * lax.rsqrt(var + LN_EPS) * gain + bias


def _mixer_kernel(x_ref, mod_ref, mod_prev_ref, w_in_ref, w_rank_ref, w_gates_ref, w_pool_ref, pool_scale_ref,
                  w_a_ref, w_al_ref,
                  b_al_ref, gain_ref, w_b_ref, w_out_ref, ln_g_ref, ln_b_ref, w_r_ref, b_r_ref,
                  x1_ref, u2_ref, route_ref, counts_ref,
                  a_ext, s_ref, cnt_ref, o_ref, z_ref, *, tiles_per_seq, n_tiles):
    j = pl.program_id(0)
    s_idx = lax.rem(jnp.minimum(j, n_tiles - 1), tiles_per_seq)
    tile = x_ref.shape[1]

    @pl.when(s_idx == 0)
    def _():
        s_ref[...] = jnp.zeros_like(s_ref)
        a_ext[0:POOL_HALO, :] = jnp.zeros((POOL_HALO, D_MODEL), F32)

    @pl.when(j == 0)
    def _():
        cnt_ref[...] = jnp.zeros_like(cnt_ref)
        z_ref[...] = jnp.zeros_like(z_ref)


    mod_prev = mod_prev_ref[0]
    x1 = _layer_norm(z_ref[...], ln_g_ref[...], ln_b_ref[...])
    x1_ref[0] = x1
    u2 = x1 * (1.0 + mod_prev[4:5]) + mod_prev[3:4]
    _store_rows(u2_ref, u2)
    u2_hi = u2.astype(BF16)
    u2_lo = (u2 - u2_hi.astype(F32)).astype(BF16)

    mod = mod_ref[0]
    sh_m, sc_m, g_m = mod[0:1], mod[1:2], mod[2:3]
    x = x_ref[0]
    u = (x * (1.0 + sc_m) + sh_m).astype(BF16)

    def proj(i):
        return _dot(u, w_in_ref[:, _OFF[i]:_OFF[i + 1]])

    gate_cols = ([(w_in_ref, _OFF[4] + jj * MXU_COLS) for jj in range(D_MODEL // MXU_COLS)]
                 + [(w_gates_ref, jj * MXU_COLS) for jj in range(2 * D_MODEL // MXU_COLS)])
    fillers = [functools.partial(lambda w_ref, c0: _dot(u, w_ref[:, c0:c0 + MXU_COLS]), w_ref, c0)
               for w_ref, c0 in gate_cols]
    filled = []

    def issue_fillers(n):
        for _ in range(n):
            if len(filled) < len(fillers):
                filled.append(fillers[len(filled)]())

    a = proj(0)
    q = proj(1) * (GLA_HEAD_K ** -0.5)
    k_all = proj(2)
    v_all = proj(3)
    alpha_low = _dot(u, w_rank_ref[...])

    part = _dot_nt(w_r_ref[...], u2_hi)
    logits = (part[:N_EXPERTS] + part[N_EXPERTS:] + _dot_nt(w_r_ref[0:N_EXPERTS, :], u2_lo)
              + b_r_ref[...])
    issue_fillers(2)

    a_ext[POOL_HALO:POOL_HALO + tile, :] = a
    t_glob = s_idx * tile + lax.broadcasted_iota(jnp.int32, (tile, 1), 0)
    mapped = []
    for g, w in enumerate(POOL_WINDOWS):
        lo, hi = g * POOL_GROUP_WIDTH, (g + 1) * POOL_GROUP_WIDTH
        win = a_ext[:, lo:hi]
        k = 1
        while k < w:
            win = win + pltpu.roll(win, k, 0)
            k *= 2
        inv_cnt = 1.0 / jnp.minimum(t_glob + 1, w).astype(F32)
        pooled = win[POOL_HALO:, :] * inv_cnt - a[:, lo:hi]
        mapped.append(_dot(pooled.astype(BF16), w_pool_ref[g]))
    a_ext[0:POOL_HALO, :] = a[tile - POOL_HALO:tile, :]
    ya = _dot((jnp.concatenate(mapped, axis=1) * pool_scale_ref[...]).astype(BF16), w_a_ref[...])

    erow = lax.broadcasted_iota(jnp.int32, (N_EXPERTS, tile), 0).astype(F32)
    work = logits
    sel = jnp.zeros((N_EXPERTS, tile), F32)
    vals, hits = [], []
    for _ in range(TOP_K):
        m = jnp.max(work, axis=0, keepdims=True)
        idx = jnp.min(jnp.where(work == m, erow, float(N_EXPERTS)), axis=0, keepdims=True)
        hit = erow == idx
        vals.append(m)
        hits.append((idx, hit))
        sel = jnp.where(hit, 1.0, sel)
        work = jnp.where(hit, -jnp.inf, work)
    exps = [jnp.exp(v - vals[0]) for v in vals]
    inv_den = 1.0 / (exps[0] + exps[1] + exps[2] + exps[3])
    issue_fillers(1)

    z = _dot(alpha_low.astype(BF16), w_al_ref[...]) + b_al_ref[...]
    log_a = (jnp.minimum(z, 0.0) - jnp.log1p(jnp.exp(-jnp.abs(z)))) * (1.0 / GLA_TAU)

    ri = lax.broadcasted_iota(jnp.int32, (CHUNK, CHUNK), 0)
    ci = lax.broadcasted_iota(jnp.int32, (CHUNK, CHUNK), 1)
    causal = ci <= ri
    cum_mat = causal.astype(BF16)
    la_hi = log_a.astype(BF16)
    la_split = jnp.concatenate([la_hi, (log_a - la_hi.astype(F32)).astype(BF16)], axis=1)

    n_chunks = tile // CHUNK
    pairs = [(c, h) for c in range(n_chunks) for h in range(GLA_HEADS)]
    rows_of = lambda c: slice(c * CHUNK, (c + 1) * CHUNK)
    ks_of = lambda h: slice(h * GLA_HEAD_K, (h + 1) * GLA_HEAD_K)
    vs_of = lambda h: slice(h * GLA_HEAD_V, (h + 1) * GLA_HEAD_V)

    b_cum, ref_pts, b_ref_pt = [], [], []
    for c in range(n_chunks):
        cum = _dot(cum_mat, la_split[rows_of(c)])
        b_cum.append(cum[:, :GLA_KEY_DIM] + cum[:, GLA_KEY_DIM:])
        ref_pts.append([jnp.zeros((1, GLA_KEY_DIM), F32)]
                       + [b_cum[c][i * SUB - 1:i * SUB, :] for i in range(1, N_SUB)])
        b_ref_pt.append(jnp.concatenate([jnp.broadcast_to(p, (SUB, GLA_KEY_DIM)) for p in ref_pts[c]], axis=0))

    rt = lax.broadcasted_iota(jnp.int32, (tile, tile), 0)
    ct = lax.broadcasted_iota(jnp.int32, (tile, tile), 1)
    base = _dot(sel.astype(BF16), (rt < ct).astype(BF16)) + cnt_ref[:, 0:1]

    s_all = {}
    for n_pair, (c, h) in enumerate(pairs):
        if n_pair % 3 == 0:
            issue_fillers(1)
        qh, kh, bh = q[rows_of(c), ks_of(h)], k_all[rows_of(c), ks_of(h)], b_cum[c][:, ks_of(h)]
        q_dec = (qh * jnp.exp(bh - b_ref_pt[c][:, ks_of(h)])).astype(BF16)
        k_dec = jnp.concatenate(
            [(kh * jnp.exp(jnp.minimum(ref_pts[c][i][:, ks_of(h)] - bh, EXP_CAP))).astype(BF16)
             for i in range(N_SUB)], axis=0)
        s_all[c, h] = _dot_nt(q_dec, k_dec)

    o_intra, kv, decay_last = {}, {}, {}
    for c, h in pairs:
        kh, bh = k_all[rows_of(c), ks_of(h)], b_cum[c][:, ks_of(h)]
        vh = v_all[rows_of(c), vs_of(h)].astype(BF16)
        scores = jnp.concatenate(
            [s_all[c, h][i * SUB:(i + 1) * SUB, i * CHUNK:(i + 1) * CHUNK] for i in range(N_SUB)], axis=0)
        o_intra[c, h] = _dot(jnp.where(causal, scores, 0.0).astype(BF16), vh)
        b_last = bh[CHUNK - 1:CHUNK, :]
        kv[c, h] = _dot_tn(vh, (kh * jnp.exp(b_last - bh)).astype(BF16))
        decay_last[c, h] = jnp.exp(b_last)

    orow = lax.broadcasted_iota(jnp.int32, (ROUTE_ROWS, tile), 0)
    route = jnp.zeros((ROUTE_ROWS, tile), F32)
    for kk in range(TOP_K):
        idx, hit = hits[kk]
        rank = jnp.sum(jnp.where(hit, base, 0.0), axis=0, keepdims=True)
        route = jnp.where(orow == kk, idx, route)
        route = jnp.where(orow == TOP_K + kk, rank, route)
        route = jnp.where(orow == 2 * TOP_K + kk, exps[kk] * inv_den, route)
    route_ref[...] = route
    cnt_ref[...] = cnt_ref[...] + (j > 0).astype(F32) * jnp.sum(sel, axis=1, keepdims=True)
    counts_ref[...] = cnt_ref[...]

    o_inter = {}
    state_t = [s_ref[h] for h in range(GLA_HEADS)]
    for c, h in pairs:
        q_in = (q[rows_of(c), ks_of(h)] * jnp.exp(b_cum[c][:, ks_of(h)])).astype(BF16)
        o_inter[c, h] = _dot_nt(q_in, state_t[h].astype(BF16))
        state_t[h] = state_t[h] * decay_last[c, h] + kv[c, h]
    for h in range(GLA_HEADS):
        s_ref[h] = state_t[h]

    for c, h in pairs:
        o = o_intra[c, h] + o_inter[c, h]
        o = o * lax.rsqrt(jnp.mean(o * o, axis=-1, keepdims=True) + RMS_EPS) * gain_ref[...]
        o_ref[rows_of(c), vs_of(h)] = o

    issue_fillers(len(fillers))
    per_proj = D_MODEL // MXU_COLS
    r, gate_a, gate_b = (jnp.concatenate(filled[n * per_proj:(n + 1) * per_proj], axis=1) for n in range(3))
    yb = _dot((o_ref[...] * (r * _sigmoid(r))).astype(BF16), w_b_ref[...])

    merged = _sigmoid(gate_a) * ya + _sigmoid(gate_b) * yb
    y = _dot(merged.astype(BF16), w_out_ref[...])
    z_ref[...] = DEEPNORM_ALPHA * x + g_m * y


def _mixer(x, mod, w_in_p, w_rank_p, w_gates, w_pool, pool_scale, w_a, w_al_p, b_alpha, gain, w_b, w_out,
           ln_g, ln_b, w_r_p, b_r_p):
    bsz, seq, d = x.shape
    tile = SEQ_TILE
    n_s = seq // tile
    n_tiles = bsz * n_s

    def const(shape):
        nd = len(shape)
        return pl.BlockSpec(shape, lambda j: (0,) * nd, pipeline_mode=pl.Buffered(1))

    cur = lambda j: jnp.minimum(j, n_tiles - 1)
    prev = lambda j: jnp.maximum(j - 1, 0)
    return pl.pallas_call(
        functools.partial(_mixer_kernel, tiles_per_seq=n_s, n_tiles=n_tiles),
        grid=(n_tiles + 1,),
        in_specs=[
            pl.BlockSpec((1, tile, d), lambda j: (cur(j) // n_s, cur(j) % n_s, 0)),
            pl.BlockSpec((1, 6, d), lambda j: (cur(j) // n_s, 0, 0)),
            pl.BlockSpec((1, 6, d), lambda j: (prev(j) // n_s, 0, 0)),
            const(w_in_p.shape), const(w_rank_p.shape), const(w_gates.shape),
            const(w_pool.shape), const(pool_scale.shape), const(w_a.shape),
            const(w_al_p.shape), const(b_alpha.shape), const(gain.shape), const(w_b.shape),
            const(w_out.shape), const(ln_g.shape), const(ln_b.shape), const(w_r_p.shape),
            const(b_r_p.shape),
        ],
        out_specs=[pl.BlockSpec((1, tile, d), lambda j: (prev(j) // n_s, prev(j) % n_s, 0)),
                   pl.BlockSpec((tile * ROW_SUB, LANES), lambda j: (prev(j), 0)),
                   pl.BlockSpec((ROUTE_ROWS, tile), lambda j: (0, prev(j))),
                   pl.BlockSpec((N_EXPERTS, LANES), lambda j: (0, 0))],
        out_shape=[
            jax.ShapeDtypeStruct((bsz, seq, d), F32),
            jax.ShapeDtypeStruct((bsz * seq * ROW_SUB, LANES), F32),
            jax.ShapeDtypeStruct((ROUTE_ROWS, bsz * seq), F32),
            jax.ShapeDtypeStruct((N_EXPERTS, LANES), F32),
        ],
        scratch_shapes=[
            pltpu.VMEM((POOL_HALO + tile, d), F32),
            pltpu.VMEM((GLA_HEADS, GLA_HEAD_V, GLA_HEAD_K), F32),
            pltpu.VMEM((N_EXPERTS, LANES), F32),
            pltpu.VMEM((tile, d), F32),
            pltpu.VMEM((tile, d), F32),
        ],
        compiler_params=pltpu.CompilerParams(
            dimension_semantics=("arbitrary",), vmem_limit_bytes=VMEM_LIMIT),
        name="mixer",
    )(x, mod, mod, w_in_p, w_rank_p, w_gates, w_pool, pool_scale, w_a, w_al_p, b_alpha, gain, w_b, w_out,
      ln_g, ln_b, w_r_p, b_r_p)


def _row_at(ref, row8):
    return ref.at[pl.ds(pl.multiple_of(row8, ROW_SUB), ROW_SUB)]


def _dispatch_kernel(fill_ref, dest_ref, u2_ref, rows_hbm, zbuf, sem_fill, sem_rows):
    i = pl.program_id(0)
    tile = u2_ref.shape[0] // ROW_SUB
    blk8 = EXPERT_BLOCK * ROW_SUB

    @pl.when(i == 0)
    def _():
        zbuf[...] = jnp.zeros_like(zbuf)
        n_windows = rows_hbm.shape[0] // blk8

        def fill(window):
            return pltpu.make_async_copy(zbuf, rows_hbm.at[pl.ds(pl.multiple_of(window * blk8, blk8), blk8)],
                                         sem_fill.at[0])

        for phase in ("start", "wait"):
            for e in range(N_EXPERTS):
                @pl.when(fill_ref[e] >= 0)
                def _():
                    getattr(fill(fill_ref[e]), phase)()

            def unused(window, carry):
                getattr(fill(window), phase)()
                return carry

            lax.fori_loop(fill_ref[N_EXPERTS], n_windows, unused, 0)

    def body(r, carry):
        for kk in range(TOP_K):
            pltpu.make_async_copy(_row_at(u2_ref, r * ROW_SUB), _row_at(rows_hbm, dest_ref[0, 0, r * TOP_K + kk]),
                                  sem_rows.at[0]).start(priority=kk % 2)
        return carry

    lax.fori_loop(0, tile, body, 0, unroll=DMA_UNROLL)
    for _ in range(TOP_K):
        pltpu.make_async_copy(u2_ref, rows_hbm.at[pl.ds(0, tile * ROW_SUB)], sem_rows.at[0]).wait()


def _dispatch(fill_start, dest8, u2_rows, n_rows):
    n_tok = u2_rows.shape[0] // ROW_SUB
    tile = DISPATCH_TILE
    n_t = n_tok // tile
    grid_spec = pltpu.PrefetchScalarGridSpec(
        num_scalar_prefetch=1,
        grid=(n_t,),
        in_specs=[
            pl.BlockSpec((1, 1, tile * TOP_K), lambda i, fs: (i, 0, 0), memory_space=pltpu.SMEM),
            pl.BlockSpec((tile * ROW_SUB, LANES), lambda i, fs: (i, 0)),
        ],
        out_specs=pl.BlockSpec(memory_space=pl.ANY),
        scratch_shapes=[
            pltpu.VMEM((EXPERT_BLOCK * ROW_SUB, LANES), F32),
            pltpu.SemaphoreType.DMA((1,)),
            pltpu.SemaphoreType.DMA((1,)),
        ],
    )
    return pl.pallas_call(
        _dispatch_kernel,
        grid_spec=grid_spec,
        out_shape=jax.ShapeDtypeStruct(((n_rows + EXPERT_BLOCK) * ROW_SUB, LANES), F32),
        compiler_params=pltpu.CompilerParams(dimension_semantics=("arbitrary",)),
        name="dispatch",
    )(fill_start, dest8.reshape(n_t, 1, tile * TOP_K), u2_rows)


def _moe_kernel(be_ref, nxt_ref, par_ref, full_ref, nb_ref, x_ref, bg_ref, bu_ref, bd_ref, wgu_hbm, wd_hbm, y_ref,
                wgu_buf, wd_buf, wg_s, wu_s, wd_s, sem_gu, sem_d):
    i = pl.program_id(0)
    new_expert = (i == 0) | (be_ref[i] != be_ref[jnp.maximum(i - 1, 0)])

    def fetch(e, s):
        return (pltpu.make_async_copy(wgu_hbm.at[e], wgu_buf.at[s], sem_gu.at[s]),
                pltpu.make_async_copy(wd_hbm.at[e], wd_buf.at[s], sem_d.at[s]))

    @pl.when(new_expert & (i < nb_ref[0]))
    def _():
        slot = par_ref[i]

        @pl.when(i == 0)
        def _():
            for cp in fetch(be_ref[0], slot):
                cp.start()

        @pl.when(nxt_ref[i] >= 0)
        def _():
            for cp in fetch(nxt_ref[i], 1 - slot):
                cp.start()

        for cp in fetch(be_ref[i], slot):
            cp.wait()
        src = lax.broadcasted_iota(jnp.int32, (MXU_COLS, MXU_COLS), 0)
        col = lax.broadcasted_iota(jnp.int32, (MXU_COLS, MXU_COLS), 1)
        half = MXU_COLS // 2
        want = jnp.where(col < half, 2 * col, 2 * (col - half) + 1)
        unzip = (src == want).astype(BF16)
        for g in range(wgu_buf.shape[2] // MXU_COLS):
            blk = wgu_buf[slot, :, g * MXU_COLS:(g + 1) * MXU_COLS].astype(BF16)
            sep = _dot(blk, unzip)
            wg_s[:, g * half:(g + 1) * half] = sep[:, :half].astype(BF16)
            wu_s[:, g * half:(g + 1) * half] = sep[:, half:].astype(BF16)
        wd_s[...] = wd_buf[slot].astype(BF16)

    def expert_mlp(n_rows):
        xb = _load_rows(x_ref, n_rows).astype(BF16)
        gate = jnp.minimum(_dot(xb, wg_s[...]) + bg_ref[0], SWIGLU_LIMIT)
        up = jnp.clip(_dot(xb, wu_s[...]) + bu_ref[0], -SWIGLU_LIMIT, SWIGLU_LIMIT)
        glu = gate * _sigmoid(gate * SWIGLU_ALPHA)
        _store_rows(y_ref, _dot(((up + 1.0) * glu).astype(BF16), wd_s[...]) + bd_ref[0])

    active = i < nb_ref[0]
    both_halves = full_ref[i] == 1

    @pl.when(active & both_halves)
    def _():
        expert_mlp(EXPERT_BLOCK)

    @pl.when(active & jnp.logical_not(both_halves))
    def _():
        expert_mlp(EXPERT_HALF)
        y_ref[EXPERT_HALF * ROW_SUB:, :] = jnp.zeros((EXPERT_HALF * ROW_SUB, LANES), F32)

    @pl.when(jnp.logical_not(active))
    def _():
        y_ref[...] = jnp.zeros_like(y_ref)


def _moe(block_expert, next_expert, slot_parity, both_halves, n_active, x_rows, w_gate_up, bg, bu, w_down, bd):
    n_blocks = block_expert.shape[0]
    _, d, f2 = w_gate_up.shape
    f = f2 // 2
    blk8 = EXPERT_BLOCK * ROW_SUB
    rows_in = pl.BlockSpec((blk8, LANES),
                           lambda i, be, nx, pr, fl, nb: (jnp.maximum(jnp.minimum(i, nb[0] - 1), 0), 0))
    rows_out = pl.BlockSpec((blk8, LANES), lambda i, be, nx, pr, fl, nb: (i, 0))
    per_expert = lambda shape: pl.BlockSpec((1,) + shape, lambda i, be, nx, pr, fl, nb: (be[i], 0, 0))
    hbm = pl.BlockSpec(memory_space=pl.ANY)
    grid_spec = pltpu.PrefetchScalarGridSpec(
        num_scalar_prefetch=5,
        grid=(n_blocks,),
        in_specs=[rows_in, per_expert((1, f)), per_expert((1, f)), per_expert((1, d)), hbm, hbm],
        out_specs=rows_out,
        scratch_shapes=[pltpu.VMEM((2, d, f2), F32), pltpu.VMEM((2, f, d), F32),
                        pltpu.VMEM((d, f), BF16), pltpu.VMEM((d, f), BF16), pltpu.VMEM((f, d), BF16),
                        pltpu.SemaphoreType.DMA((2,)), pltpu.SemaphoreType.DMA((2,))],
    )
    return pl.pallas_call(
        _moe_kernel,
        grid_spec=grid_spec,
        out_shape=jax.ShapeDtypeStruct((n_blocks * blk8, LANES), F32),
        compiler_params=pltpu.CompilerParams(
            dimension_semantics=("arbitrary",), vmem_limit_bytes=VMEM_LIMIT),
        name="moe",
    )(block_expert, next_expert, slot_parity, both_halves, n_active, x_rows, bg, bu, bd, w_gate_up, w_down)


def _combine_kernel(dest_cur_ref, dest_nxt_ref, x1_ref, mod_ref, route_ref, ln_g_ref, ln_b_ref, y_hbm,
                    o_ref, ybuf, sem):
    i = pl.program_id(0)
    n_steps = pl.num_programs(0)
    tile = x1_ref.shape[0]
    slot = lax.rem(i, 2)

    def gather(dest_ref, s):
        def body(r, carry):
            for kk in range(TOP_K):
                pltpu.make_async_copy(_row_at(y_hbm, dest_ref[0, 0, r * TOP_K + kk]),
                                      _row_at(ybuf.at[s, kk], r * ROW_SUB), sem.at[s]).start(priority=kk % 2)
            return carry
        lax.fori_loop(0, tile, body, 0, unroll=DMA_UNROLL)

    @pl.when(i == 0)
    def _():
        gather(dest_cur_ref, 0)

    @pl.when(i + 1 < n_steps)
    def _():
        gather(dest_nxt_ref, 1 - slot)

    for kk in range(TOP_K):
        pltpu.make_async_copy(y_hbm.at[pl.ds(0, tile * ROW_SUB)], ybuf.at[slot, kk], sem.at[slot]).wait()

    g_f = mod_ref[0][5:6]
    weight = route_ref[...]
    y = jnp.zeros(x1_ref.shape, F32)
    for kk in range(TOP_K):
        y = y + weight[:, kk:kk + 1] * _load_rows(ybuf, tile, (slot, kk))
    o_ref[...] = _layer_norm(DEEPNORM_ALPHA * x1_ref[...] + g_f * y, ln_g_ref[...], ln_b_ref[...])


def _combine(dest8, x1, mod, route, ln_g, ln_b, y_rows, seq):
    n_tok, d = x1.shape
    tile = COMBINE_TILE
    n_t = n_tok // tile
    per_seq = seq // tile
    dest3 = dest8.reshape(n_t, 1, tile * TOP_K)
    smem_blk = lambda fn: pl.BlockSpec((1, 1, tile * TOP_K), fn, memory_space=pltpu.SMEM)
    row = lambda width: pl.BlockSpec((tile, width), lambda i: (i, 0))
    vec = pl.BlockSpec((1, d), lambda i: (0, 0))
    return pl.pallas_call(
        _combine_kernel,
        grid=(n_t,),
        in_specs=[smem_blk(lambda i: (i, 0, 0)), smem_blk(lambda i: (jnp.minimum(i + 1, n_t - 1), 0, 0)),
                  row(d), pl.BlockSpec((1, 6, d), lambda i: (i // per_seq, 0, 0)), row(TOP_K), vec, vec,
                  pl.BlockSpec(memory_space=pl.ANY)],
        out_specs=row(d),
        out_shape=jax.ShapeDtypeStruct((n_tok, d), F32),
        scratch_shapes=[pltpu.VMEM((2, TOP_K, tile * ROW_SUB, LANES), F32), pltpu.SemaphoreType.DMA((2,))],
        compiler_params=pltpu.CompilerParams(
            dimension_semantics=("arbitrary",), vmem_limit_bytes=VMEM_LIMIT),
        name="combine",
    )(dest3, dest3, x1, mod, route, ln_g, ln_b, y_rows)


def _split_in_proj(w_in):
    rank_lo = _OFF[5]
    rank_hi = rank_lo + GLA_GATE_RANK
    pad = jnp.zeros((w_in.shape[0], RANK_PAD - GLA_GATE_RANK), BF16)
    return (w_in[:, :rank_lo].astype(BF16),
            jnp.concatenate([w_in[:, rank_lo:rank_hi].astype(BF16), pad], axis=1),
            w_in[:, rank_hi:].astype(BF16))


def _layer(x, c, w_ada, b_ada, w_in, w_pool_group, pool_scale, w_branch_a, w_alpha_up, b_alpha,
           gla_norm_gain, w_branch_b, w_out, ln1_gain, ln1_bias, w_router, b_router,
           w_gate_up, b_gate_up, w_down, b_down, ln2_gain, ln2_bias):
    bsz, seq, d = x.shape
    n_tok = bsz * seq
    n_assign = n_tok * TOP_K
    row2 = lambda v: v.reshape(1, -1)

    mod = _ada(c, w_ada, b_ada).reshape(bsz, 6, d)

    w_in_p, w_rank_p, w_gates = _split_in_proj(w_in)
    w_al_p = jnp.concatenate(
        [w_alpha_up, jnp.zeros((RANK_PAD - GLA_GATE_RANK, GLA_KEY_DIM), w_alpha_up.dtype)], axis=0).astype(BF16)
    w_r_t = w_router.T
    w_r_hi = w_r_t.astype(BF16)
    w_r_split = jnp.concatenate([w_r_hi, (w_r_t - w_r_hi.astype(F32)).astype(BF16)], axis=0)
    x1, u2_rows, route_t, counts = _mixer(
        x, mod, w_in_p, w_rank_p, w_gates, w_pool_group.astype(BF16), row2(pool_scale),
        w_branch_a.astype(BF16), w_al_p,
        row2(b_alpha), row2(gla_norm_gain), w_branch_b.astype(BF16), w_out.astype(BF16), row2(ln1_gain),
        row2(ln1_bias), w_r_split, b_router.reshape(N_EXPERTS, 1))

    top_idx = route_t[0:TOP_K].T.astype(jnp.int32)
    rank = route_t[TOP_K:2 * TOP_K].T.astype(jnp.int32)
    route = route_t[2 * TOP_K:3 * TOP_K].T
    counts = counts[:, 0].astype(jnp.int32)
    padded = (counts + EXPERT_BLOCK - 1) // EXPERT_BLOCK * EXPERT_BLOCK
    pad_end = jnp.cumsum(padded)
    pad_start = pad_end - padded
    n_rows = (n_assign + N_EXPERTS * (EXPERT_BLOCK - 1) + EXPERT_BLOCK - 1) // EXPERT_BLOCK * EXPERT_BLOCK
    n_blocks = n_rows // EXPERT_BLOCK
    n_active = (pad_end[-1] // EXPERT_BLOCK).astype(jnp.int32)
    start_of = jnp.sum(jnp.where(top_idx[..., None] == jnp.arange(N_EXPERTS, dtype=jnp.int32), pad_start, 0),
                       axis=-1)
    dest8 = ((start_of + rank) * ROW_SUB).reshape(-1)
    blk_row = jnp.arange(n_blocks, dtype=jnp.int32)[:, None] * EXPERT_BLOCK
    block_expert = jnp.minimum(jnp.sum(pad_end[None, :] <= blk_row, axis=1), N_EXPERTS - 1).astype(jnp.int32)

    last_window = jnp.where(padded > 0, pad_end // EXPERT_BLOCK - 1, -1)
    fill_windows = jnp.concatenate([last_window, n_active.reshape(1)]).astype(jnp.int32)
    x_rows = _dispatch(fill_windows, dest8, u2_rows, n_rows)
    f = w_down.shape[1]
    bg = b_gate_up[:, 0::2].reshape(N_EXPERTS, 1, f)
    bu = b_gate_up[:, 1::2].reshape(N_EXPERTS, 1, f)
    eid = jnp.arange(N_EXPERTS, dtype=jnp.int32)
    owns = counts > 0
    later = jnp.where((eid[None, :] > eid[:, None]) & owns[None, :], eid[None, :], N_EXPERTS)
    next_of = jnp.min(later, axis=1)
    next_of = jnp.where(next_of == N_EXPERTS, -1, next_of).astype(jnp.int32)
    parity_of = ((jnp.cumsum(owns.astype(jnp.int32)) - 1) % 2).astype(jnp.int32)
    is_block_expert = block_expert[:, None] == eid[None, :]
    per_block = lambda v: jnp.sum(jnp.where(is_block_expert, v[None, :], 0), axis=1).astype(jnp.int32)
    left = per_block(pad_start + counts) - blk_row[:, 0]
    both_halves = (left > EXPERT_HALF).astype(jnp.int32)
    y_rows = _moe(block_expert, per_block(next_of), per_block(parity_of), both_halves, n_active.reshape(1),
                  x_rows, w_gate_up, bg, bu, w_down, b_down.reshape(N_EXPERTS, 1, d))
    out = _combine(dest8, x1.reshape(n_tok, d), mod, route, row2(ln2_gain), row2(ln2_bias), y_rows, seq)
    return out.reshape(bsz, seq, d)


def kernel(x, c, w_ada, b_ada, w_in, w_pool_group, pool_scale, w_branch_a, w_alpha_up, b_alpha, gla_norm_gain,
           w_branch_b, w_out, ln1_gain, ln1_bias, w_router, b_router, w_gate_up, b_gate_up, w_down, b_down,
           ln2_gain, ln2_bias):
    for l in range(DEPTH):
        x = _layer(x, c, w_ada[l], b_ada[l], w_in[l], w_pool_group[l], pool_scale[l], w_branch_a[l],
                   w_alpha_up[l], b_alpha[l], gla_norm_gain[l], w_branch_b[l], w_out[l], ln1_gain[l],
                   ln1_bias[l], w_router[l], b_router[l], w_gate_up[l], b_gate_up[l], w_down[l], b_down[l],
                   ln2_gain[l], ln2_bias[l])
    return x
```

```python
import functools

import jax
import jax.numpy as jnp
from jax import lax
from jax.experimental import pallas as pl
from jax.experimental.pallas import tpu as pltpu

D_MODEL = 1024
CHUNK = 64
SUB = 16
N_SUB = CHUNK // SUB
POOL_WINDOWS = (2, 4, 8, 16)
POOL_GROUP_WIDTH = D_MODEL // len(POOL_WINDOWS)
POOL_HALO = 16
GLA_HEADS = 4
GLA_KEY_DIM = D_MODEL // 2
GLA_HEAD_K = GLA_KEY_DIM // GLA_HEADS
GLA_HEAD_V = D_MODEL // GLA_HEADS
GLA_GATE_RANK = 16
GLA_TAU = 16.0
N_EXPERTS = 32
TOP_K = 4
SWIGLU_ALPHA = 1.702
SWIGLU_LIMIT = 7.0
EXPERT_BLOCK = 512
EXPERT_HALF = EXPERT_BLOCK // 2
LN_EPS = 1e-5
RMS_EPS = 1e-6
DEPTH = 1
DEEPNORM_ALPHA = (2.0 * DEPTH) ** 0.25

LANES = 128
MXU_COLS = 256
RANK_PAD = LANES
_W = (D_MODEL, GLA_KEY_DIM, GLA_KEY_DIM, D_MODEL, D_MODEL, RANK_PAD, D_MODEL, D_MODEL)
_OFF = tuple(sum(_W[:i]) for i in range(len(_W) + 1))
EXP_CAP = 60.0
SAFE_SUB_DECAY = 40.0
ROUTE_ROWS = 16

SEQ_TILE = 512
DISPATCH_TILE = 256
COMBINE_TILE = 256
DMA_UNROLL = 8
VMEM_LIMIT = 60 * 1024 * 1024

F32 = jnp.float32
BF16 = jnp.bfloat16
HI = lax.Precision.HIGHEST


def _dot(a, b):
    return jnp.dot(a, b, preferred_element_type=F32)


def _dot_nt(a, b):
    return lax.dot_general(a, b, (((1,), (1,)), ((), ())), preferred_element_type=F32)


def _dot_tn(a, b):
    return lax.dot_general(a, b, (((0,), (0,)), ((), ())), preferred_element_type=F32)


def _sigmoid(v):
    return 0.5 * jnp.tanh(0.5 * v) + 0.5


ROW_SUB = D_MODEL // LANES


def _load_rows(ref, n_rows, lead=()):
    return jnp.concatenate(
        [ref[lead + (pl.ds(j, n_rows, stride=ROW_SUB), slice(None))] for j in range(ROW_SUB)], axis=1)


def _store_rows(ref, val):
    for j in range(ROW_SUB):
        ref[pl.ds(j, val.shape[0], stride=ROW_SUB), :] = val[:, j * LANES:(j + 1) * LANES]


def _ada_kernel(c_ref, w_ref, b_ref, o_ref):
    c = c_ref[...]
    s = c * jax.nn.sigmoid(c)
    o_ref[...] = jnp.dot(s, w_ref[...], precision=HI, preferred_element_type=F32) + b_ref[...]


def _ada(c, w_ada, b_ada):
    bsz, d = c.shape
    n = w_ada.shape[1]
    tn = 1024
    return pl.pallas_call(
        _ada_kernel,
        grid=(n // tn,),
        in_specs=[
            pl.BlockSpec((bsz, d), lambda j: (0, 0)),
            pl.BlockSpec((d, tn), lambda j: (0, j)),
            pl.BlockSpec((1, tn), lambda j: (0, j)),
        ],
        out_specs=pl.BlockSpec((bsz, tn), lambda j: (0, j)),
        out_shape=jax.ShapeDtypeStruct((bsz, n), F32),
        name="ada",
    )(c, w_ada, b_ada.reshape(1, n))


def _layer_norm(z, gain, bias):
    mu = jnp.mean(z, axis=-1, keepdims=True)
    zc = z - mu
    var = jnp.mean(zc * zc, axis=-1, keepdims=True)
    return zc * lax.rsqrt(var + LN_EPS) * gain + bias


def _mixer_kernel(x_ref, mod_ref, mod_prev_ref, w_in_ref, w_rank_ref, w_gates_ref, w_pool_ref, pool_scale_ref,
                  w_a_ref, w_al_ref,
                  b_al_ref, gain_ref, w_b_ref, w_out_ref, ln_g_ref, ln_b_ref, w_r_ref, b_r_ref,
                  x1_ref, u2_ref, route_ref, counts_ref,
                  a_ext, s_ref, cnt_ref, o_ref, z_ref, sc_ref, *, tiles_per_seq, n_tiles):
    j = pl.program_id(0)
    s_idx = lax.rem(jnp.minimum(j, n_tiles - 1), tiles_per_seq)
    tile = x_ref.shape[1]

    @pl.when(s_idx == 0)
    def _():
        s_ref[...] = jnp.zeros_like(s_ref)
        a_ext[0:POOL_HALO, :] = jnp.zeros((POOL_HALO, D_MODEL), F32)

    @pl.when(j == 0)
    def _():
        cnt_ref[...] = jnp.zeros_like(cnt_ref)
        z_ref[...] = jnp.zeros_like(z_ref)


    mod_prev = mod_prev_ref[0]
    x1 = _layer_norm(z_ref[...], ln_g_ref[...], ln_b_ref[...])
    x1_ref[0] = x1
    u2 = x1 * (1.0 + mod_prev[4:5]) + mod_prev[3:4]
    _store_rows(u2_ref, u2)
    u2_hi = u2.astype(BF16)
    u2_lo = (u2 - u2_hi.astype(F32)).astype(BF16)

    mod = mod_ref[0]
    sh_m, sc_m, g_m = mod[0:1], mod[1:2], mod[2:3]
    x = x_ref[0]
    u = (x * (1.0 + sc_m) + sh_m).astype(BF16)

    def proj(i):
        return _dot(u, w_in_ref[:, _OFF[i]:_OFF[i + 1]])

    gate_cols = ([(w_in_ref, _OFF[4] + jj * MXU_COLS) for jj in range(D_MODEL // MXU_COLS)]
                 + [(w_gates_ref, jj * MXU_COLS) for jj in range(2 * D_MODEL // MXU_COLS)])
    fillers = [functools.partial(lambda w_ref, c0: _dot(u, w_ref[:, c0:c0 + MXU_COLS]), w_ref, c0)
               for w_ref, c0 in gate_cols]
    filled = []

    def issue_fillers(n):
        for _ in range(n):
            if len(filled) < len(fillers):
                filled.append(fillers[len(filled)]())

    a = proj(0)
    q = proj(1) * (GLA_HEAD_K ** -0.5)
    k_all = proj(2)
    v_all = proj(3)
    alpha_low = _dot(u, w_rank_ref[...])

    part = _dot_nt(w_r_ref[...], u2_hi)
    logits = (part[:N_EXPERTS] + part[N_EXPERTS:] + _dot_nt(w_r_ref[0:N_EXPERTS, :], u2_lo)
              + b_r_ref[...])
    issue_fillers(2)

    a_ext[POOL_HALO:POOL_HALO + tile, :] = a
    t_glob = s_idx * tile + lax.broadcasted_iota(jnp.int32, (tile, 1), 0)
    mapped = []
    for g, w in enumerate(POOL_WINDOWS):
        lo, hi = g * POOL_GROUP_WIDTH, (g + 1) * POOL_GROUP_WIDTH
        win = a_ext[:, lo:hi]
        k = 1
        while k < w:
            win = win + pltpu.roll(win, k, 0)
            k *= 2
        inv_cnt = 1.0 / jnp.minimum(t_glob + 1, w).astype(F32)
        pooled = win[POOL_HALO:, :] * inv_cnt - a[:, lo:hi]
        mapped.append(_dot(pooled.astype(BF16), w_pool_ref[g]))
    a_ext[0:POOL_HALO, :] = a[tile - POOL_HALO:tile, :]
    ya = _dot((jnp.concatenate(mapped, axis=1) * pool_scale_ref[...]).astype(BF16), w_a_ref[...])

    erow = lax.broadcasted_iota(jnp.int32, (N_EXPERTS, tile), 0).astype(F32)
    work = logits
    sel = jnp.zeros((N_EXPERTS, tile), F32)
    vals, hits = [], []
    for _ in range(TOP_K):
        m = jnp.max(work, axis=0, keepdims=True)
        idx = jnp.min(jnp.where(work == m, erow, float(N_EXPERTS)), axis=0, keepdims=True)
        hit = erow == idx
        vals.append(m)
        hits.append((idx, hit))
        sel = jnp.where(hit, 1.0, sel)
        work = jnp.where(hit, -jnp.inf, work)
    exps = [jnp.exp(v - vals[0]) for v in vals]
    inv_den = 1.0 / (exps[0] + exps[1] + exps[2] + exps[3])
    issue_fillers(1)

    z = _dot(alpha_low.astype(BF16), w_al_ref[...]) + b_al_ref[...]
    log_a = (jnp.minimum(z, 0.0) - jnp.log1p(jnp.exp(-jnp.abs(z)))) * (1.0 / GLA_TAU)

    ri = lax.broadcasted_iota(jnp.int32, (CHUNK, CHUNK), 0)
    ci = lax.broadcasted_iota(jnp.int32, (CHUNK, CHUNK), 1)
    causal = ci <= ri
    cum_mat = causal.astype(BF16)
    la_hi = log_a.astype(BF16)
    la_split = jnp.concatenate([la_hi, (log_a - la_hi.astype(F32)).astype(BF16)], axis=1)

    n_chunks = tile // CHUNK
    pairs = [(c, h) for c in range(n_chunks) for h in range(GLA_HEADS)]
    rows_of = lambda c: slice(c * CHUNK, (c + 1) * CHUNK)
    ks_of = lambda h: slice(h * GLA_HEAD_K, (h + 1) * GLA_HEAD_K)
    vs_of = lambda h: slice(h * GLA_HEAD_V, (h + 1) * GLA_HEAD_V)

    b_cum, ref_pts, b_ref_pt = [], [], []
    for c in range(n_chunks):
        cum = _dot(cum_mat, la_split[rows_of(c)])
        b_cum.append(cum[:, :GLA_KEY_DIM] + cum[:, GLA_KEY_DIM:])
        ref_pts.append([jnp.zeros((1, GLA_KEY_DIM), F32)]
                       + [b_cum[c][i * SUB - 1:i * SUB, :] for i in range(1, N_SUB)])
        b_ref_pt.append(jnp.concatenate([jnp.broadcast_to(p, (SUB, GLA_KEY_DIM)) for p in ref_pts[c]], axis=0))

    rt = lax.broadcasted_iota(jnp.int32, (tile, tile), 0)
    ct = lax.broadcasted_iota(jnp.int32, (tile, tile), 1)
    base = _dot(sel.astype(BF16), (rt < ct).astype(BF16)) + cnt_ref[:, 0:1]

    for n_pair, (c, h) in enumerate(pairs):
        if n_pair % 3 == 0:
            issue_fillers(1)
        qh, kh, bh = q[rows_of(c), ks_of(h)], k_all[rows_of(c), ks_of(h)], b_cum[c][:, ks_of(h)]
        q_dec = (qh * jnp.exp(bh - b_ref_pt[c][:, ks_of(h)])).astype(BF16)
        k_dec = jnp.concatenate(
            [(kh * jnp.exp(jnp.minimum(ref_pts[c][i][:, ks_of(h)] - bh, EXP_CAP))).astype(BF16)
             for i in range(N_SUB)], axis=0)
        s_all = _dot_nt(q_dec, k_dec)
        sc_ref[n_pair] = jnp.concatenate(
            [s_all[i * SUB:(i + 1) * SUB, i * CHUNK:(i + 1) * CHUNK] for i in range(N_SUB)], axis=0)

    sub_decay = b_ref_pt[0] - b_cum[0]
    for c in range(1, n_chunks):
        sub_decay = jnp.maximum(sub_decay, b_ref_pt[c] - b_cum[c])

    @pl.when(jnp.max(sub_decay) > SAFE_SUB_DECAY)
    def _():
        o_ref[:, 0:GLA_KEY_DIM] = q
        o_ref[:, GLA_KEY_DIM:2 * GLA_KEY_DIM] = jnp.concatenate(b_cum, axis=0)
        for n_pair, (c, h) in enumerate(pairs):
            kh, bh = k_all[rows_of(c), ks_of(h)], b_cum[c][:, ks_of(h)]

            def score_rows(g, carry):
                first = pl.multiple_of(g * 8, 8)
                q_8 = o_ref[pl.ds(c * CHUNK + first, 8), ks_of(h)]
                b_8 = o_ref[pl.ds(c * CHUNK + first, 8),
                            GLA_KEY_DIM + h * GLA_HEAD_K:GLA_KEY_DIM + (h + 1) * GLA_HEAD_K]
                rows = []
                for r in range(8):
                    k_i = (kh * jnp.exp(jnp.minimum(b_8[r:r + 1] - bh, 0.0))).astype(BF16)
                    q_i = jnp.broadcast_to(q_8[r:r + 1], (8, GLA_HEAD_K)).astype(BF16)
                    rows.append(_dot_nt(q_i, k_i)[0:1])
                sc_ref[n_pair, pl.ds(first, 8), :] = jnp.concatenate(rows, axis=0)
                return carry

            lax.fori_loop(0, CHUNK // 8, score_rows, 0)

    o_intra, kv, decay_last = {}, {}, {}
    for n_pair, (c, h) in enumerate(pairs):
        kh, bh = k_all[rows_of(c), ks_of(h)], b_cum[c][:, ks_of(h)]
        vh = v_all[rows_of(c), vs_of(h)].astype(BF16)
        o_intra[c, h] = _dot(jnp.where(causal, sc_ref[n_pair], 0.0).astype(BF16), vh)
        b_last = bh[CHUNK - 1:CHUNK, :]
        kv[c, h] = _dot_tn(vh, (kh * jnp.exp(b_last - bh)).astype(BF16))
        decay_last[c, h] = jnp.exp(b_last)

    orow = lax.broadcasted_iota(jnp.int32, (ROUTE_ROWS, tile), 0)
    route = jnp.zeros((ROUTE_ROWS, tile), F32)
    for kk in range(TOP_K):
        idx, hit = hits[kk]
        rank = jnp.sum(jnp.where(hit, base, 0.0), axis=0, keepdims=True)
        route = jnp.where(orow == kk, idx, route)
        route = jnp.where(orow == TOP_K + kk, rank, route)
        route = jnp.where(orow == 2 * TOP_K + kk, exps[kk] * inv_den, route)
    route_ref[...] = route
    cnt_ref[...] = cnt_ref[...] + (j > 0).astype(F32) * jnp.sum(sel, axis=1, keepdims=True)
    counts_ref[...] = cnt_ref[...]

    o_inter = {}
    state_t = [s_ref[h] for h in range(GLA_HEADS)]
    for c, h in pairs:
        q_in = (q[rows_of(c), ks_of(h)] * jnp.exp(b_cum[c][:, ks_of(h)])).astype(BF16)
        o_inter[c, h] = _dot_nt(q_in, state_t[h].astype(BF16))
        state_t[h] = state_t[h] * decay_last[c, h] + kv[c, h]
    for h in range(GLA_HEADS):
        s_ref[h] = state_t[h]

    for c, h in pairs:
        o = o_intra[c, h] + o_inter[c, h]
        o = o * lax.rsqrt(jnp.mean(o * o, axis=-1, keepdims=True) + RMS_EPS) * gain_ref[...]
        o_ref[rows_of(c), vs_of(h)] = o

    issue_fillers(len(fillers))
    per_proj = D_MODEL // MXU_COLS
    r, gate_a, gate_b = (jnp.concatenate(filled[n * per_proj:(n + 1) * per_proj], axis=1) for n in range(3))
    yb = _dot((o_ref[...] * (r * _sigmoid(r))).astype(BF16), w_b_ref[...])

    merged = _sigmoid(gate_a) * ya + _sigmoid(gate_b) * yb
    y = _dot(merged.astype(BF16), w_out_ref[...])
    z_ref[...] = DEEPNORM_ALPHA * x + g_m * y


def _mixer(x, mod, w_in_p, w_rank_p, w_gates, w_pool, pool_scale, w_a, w_al_p, b_alpha, gain, w_b, w_out,
           ln_g, ln_b, w_r_p, b_r_p):
    bsz, seq, d = x.shape
    tile = SEQ_TILE
    n_s = seq // tile
    n_tiles = bsz * n_s

    def const(shape):
        nd = len(shape)
        return pl.BlockSpec(shape, lambda j: (0,) * nd, pipeline_mode=pl.Buffered(1))

    cur = lambda j: jnp.minimum(j, n_tiles - 1)
    prev = lambda j: jnp.maximum(j - 1, 0)
    return pl.pallas_call(
        functools.partial(_mixer_kernel, tiles_per_seq=n_s, n_tiles=n_tiles),
        grid=(n_tiles + 1,),
        in_specs=[
            pl.BlockSpec((1, tile, d), lambda j: (cur(j) // n_s, cur(j) % n_s, 0)),
            pl.BlockSpec((1, 6, d), lambda j: (cur(j) // n_s, 0, 0)),
            pl.BlockSpec((1, 6, d), lambda j: (prev(j) // n_s, 0, 0)),
            const(w_in_p.shape), const(w_rank_p.shape), const(w_gates.shape),
            const(w_pool.shape), const(pool_scale.shape), const(w_a.shape),
            const(w_al_p.shape), const(b_alpha.shape), const(gain.shape), const(w_b.shape),
            const(w_out.shape), const(ln_g.shape), const(ln_b.shape), const(w_r_p.shape),
            const(b_r_p.shape),
        ],
        out_specs=[pl.BlockSpec((1, tile, d), lambda j: (prev(j) // n_s, prev(j) % n_s, 0)),
                   pl.BlockSpec((tile * ROW_SUB, LANES), lambda j: (prev(j), 0)),
                   pl.BlockSpec((ROUTE_ROWS, tile), lambda j: (0, prev(j))),
                   pl.BlockSpec((N_EXPERTS, LANES), lambda j: (0, 0))],
        out_shape=[
            jax.ShapeDtypeStruct((bsz, seq, d), F32),
            jax.ShapeDtypeStruct((bsz * seq * ROW_SUB, LANES), F32),
            jax.ShapeDtypeStruct((ROUTE_ROWS, bsz * seq), F32),
            jax.ShapeDtypeStruct((N_EXPERTS, LANES), F32),
        ],
        scratch_shapes=[
            pltpu.VMEM((POOL_HALO + tile, d), F32),
            pltpu.VMEM((GLA_HEADS, GLA_HEAD_V, GLA_HEAD_K), F32),
            pltpu.VMEM((N_EXPERTS, LANES), F32),
            pltpu.VMEM((tile, d), F32),
            pltpu.VMEM((tile, d), F32),
            pltpu.VMEM((tile // CHUNK * GLA_HEADS, CHUNK, CHUNK), F32),
        ],
        compiler_params=pltpu.CompilerParams(
            dimension_semantics=("arbitrary",), vmem_limit_bytes=VMEM_LIMIT),
        name="mixer",
    )(x, mod, mod, w_in_p, w_rank_p, w_gates, w_pool, pool_scale, w_a, w_al_p, b_alpha, gain, w_b, w_out,
      ln_g, ln_b, w_r_p, b_r_p)


def _row_at(ref, row8):
    return ref.at[pl.ds(pl.multiple_of(row8, ROW_SUB), ROW_SUB)]


def _dispatch_kernel(fill_ref, dest_ref, u2_ref, rows_hbm, zbuf, sem_fill, sem_rows):
    i = pl.program_id(0)
    tile = u2_ref.shape[0] // ROW_SUB
    blk8 = EXPERT_BLOCK * ROW_SUB

    @pl.when(i == 0)
    def _():
        zbuf[...] = jnp.zeros_like(zbuf)
        n_windows = rows_hbm.shape[0] // blk8

        def fill(window):
            return pltpu.make_async_copy(zbuf, rows_hbm.at[pl.ds(pl.multiple_of(window * blk8, blk8), blk8)],
                                         sem_fill.at[0])

        for phase in ("start", "wait"):
            for e in range(N_EXPERTS):
                @pl.when(fill_ref[e] >= 0)
                def _():
                    getattr(fill(fill_ref[e]), phase)()

            def unused(window, carry):
                getattr(fill(window), phase)()
                return carry

            lax.fori_loop(fill_ref[N_EXPERTS], n_windows, unused, 0)

    def body(r, carry):
        for kk in range(TOP_K):
            pltpu.make_async_copy(_row_at(u2_ref, r * ROW_SUB), _row_at(rows_hbm, dest_ref[0, 0, r * TOP_K + kk]),
                                  sem_rows.at[0]).start(priority=kk % 2)
        return carry

    lax.fori_loop(0, tile, body, 0, unroll=DMA_UNROLL)
    for _ in range(TOP_K):
        pltpu.make_async_copy(u2_ref, rows_hbm.at[pl.ds(0, tile * ROW_SUB)], sem_rows.at[0]).wait()


def _dispatch(fill_start, dest8, u2_rows, n_rows):
    n_tok = u2_rows.shape[0] // ROW_SUB
    tile = DISPATCH_TILE
    n_t = n_tok // tile
    grid_spec = pltpu.PrefetchScalarGridSpec(
        num_scalar_prefetch=1,
        grid=(n_t,),
        in_specs=[
            pl.BlockSpec((1, 1, tile * TOP_K), lambda i, fs: (i, 0, 0), memory_space=pltpu.SMEM),
            pl.BlockSpec((tile * ROW_SUB, LANES), lambda i, fs: (i, 0)),
        ],
        out_specs=pl.BlockSpec(memory_space=pl.ANY),
        scratch_shapes=[
            pltpu.VMEM((EXPERT_BLOCK * ROW_SUB, LANES), F32),
            pltpu.SemaphoreType.DMA((1,)),
            pltpu.SemaphoreType.DMA((1,)),
        ],
    )
    return pl.pallas_call(
        _dispatch_kernel,
        grid_spec=grid_spec,
        out_shape=jax.ShapeDtypeStruct(((n_rows + EXPERT_BLOCK) * ROW_SUB, LANES), F32),
        compiler_params=pltpu.CompilerParams(dimension_semantics=("arbitrary",)),
        name="dispatch",
    )(fill_start, dest8.reshape(n_t, 1, tile * TOP_K), u2_rows)


def _moe_kernel(be_ref, nxt_ref, par_ref, full_ref, nb_ref, x_ref, bg_ref, bu_ref, bd_ref, wgu_hbm, wd_hbm, y_ref,
                wgu_buf, wd_buf, wg_s, wu_s, wd_s, sem_gu, sem_d):
    i = pl.program_id(0)
    new_expert = (i == 0) | (be_ref[i] != be_ref[jnp.maximum(i - 1, 0)])

    def fetch(e, s):
        return (pltpu.make_async_copy(wgu_hbm.at[e], wgu_buf.at[s], sem_gu.at[s]),
                pltpu.make_async_copy(wd_hbm.at[e], wd_buf.at[s], sem_d.at[s]))

    @pl.when(new_expert & (i < nb_ref[0]))
    def _():
        slot = par_ref[i]

        @pl.when(i == 0)
        def _():
            for cp in fetch(be_ref[0], slot):
                cp.start()

        @pl.when(nxt_ref[i] >= 0)
        def _():
            for cp in fetch(nxt_ref[i], 1 - slot):
                cp.start()

        for cp in fetch(be_ref[i], slot):
            cp.wait()
        src = lax.broadcasted_iota(jnp.int32, (MXU_COLS, MXU_COLS), 0)
        col = lax.broadcasted_iota(jnp.int32, (MXU_COLS, MXU_COLS), 1)
        half = MXU_COLS // 2
        want = jnp.where(col < half, 2 * col, 2 * (col - half) + 1)
        unzip = (src == want).astype(BF16)
        for g in range(wgu_buf.shape[2] // MXU_COLS):
            blk = wgu_buf[slot, :, g * MXU_COLS:(g + 1) * MXU_COLS].astype(BF16)
            sep = _dot(blk, unzip)
            wg_s[:, g * half:(g + 1) * half] = sep[:, :half].astype(BF16)
            wu_s[:, g * half:(g + 1) * half] = sep[:, half:].astype(BF16)
        wd_s[...] = wd_buf[slot].astype(BF16)

    def expert_mlp(n_rows):
        xb = _load_rows(x_ref, n_rows).astype(BF16)
        gate = jnp.minimum(_dot(xb, wg_s[...]) + bg_ref[0], SWIGLU_LIMIT)
        up = jnp.clip(_dot(xb, wu_s[...]) + bu_ref[0], -SWIGLU_LIMIT, SWIGLU_LIMIT)
        glu = gate * _sigmoid(gate * SWIGLU_ALPHA)
        _store_rows(y_ref, _dot(((up + 1.0) * glu).astype(BF16), wd_s[...]) + bd_ref[0])

    active = i < nb_ref[0]
    both_halves = full_ref[i] == 1

    @pl.when(active & both_halves)
    def _():
        expert_mlp(EXPERT_BLOCK)

    @pl.when(active & jnp.logical_not(both_halves))
    def _():
        expert_mlp(EXPERT_HALF)
        y_ref[EXPERT_HALF * ROW_SUB:, :] = jnp.zeros((EXPERT_HALF * ROW_SUB, LANES), F32)

    @pl.when(jnp.logical_not(active))
    def _():
        y_ref[...] = jnp.zeros_like(y_ref)


def _moe(block_expert, next_expert, slot_parity, both_halves, n_active, x_rows, w_gate_up, bg, bu, w_down, bd):
    n_blocks = block_expert.shape[0]
    _, d, f2 = w_gate_up.shape
    f = f2 // 2
    blk8 = EXPERT_BLOCK * ROW_SUB
    rows_in = pl.BlockSpec((blk8, LANES),
                           lambda i, be, nx, pr, fl, nb: (jnp.maximum(jnp.minimum(i, nb[0] - 1), 0), 0))
    rows_out = pl.BlockSpec((blk8, LANES), lambda i, be, nx, pr, fl, nb: (i, 0))
    per_expert = lambda shape: pl.BlockSpec((1,) + shape, lambda i, be, nx, pr, fl, nb: (be[i], 0, 0))
    hbm = pl.BlockSpec(memory_space=pl.ANY)
    grid_spec = pltpu.PrefetchScalarGridSpec(
        num_scalar_prefetch=5,
        grid=(n_blocks,),
        in_specs=[rows_in, per_expert((1, f)), per_expert((1, f)), per_expert((1, d)), hbm, hbm],
        out_specs=rows_out,
        scratch_shapes=[pltpu.VMEM((2, d, f2), F32), pltpu.VMEM((2, f, d), F32),
                        pltpu.VMEM((d, f), BF16), pltpu.VMEM((d, f), BF16), pltpu.VMEM((f, d), BF16),
                        pltpu.SemaphoreType.DMA((2,)), pltpu.SemaphoreType.DMA((2,))],
    )
    return pl.pallas_call(
        _moe_kernel,
        grid_spec=grid_spec,
        out_shape=jax.ShapeDtypeStruct((n_blocks * blk8, LANES), F32),
        compiler_params=pltpu.CompilerParams(
            dimension_semantics=("arbitrary",), vmem_limit_bytes=VMEM_LIMIT),
        name="moe",
    )(block_expert, next_expert, slot_parity, both_halves, n_active, x_rows, bg, bu, bd, w_gate_up, w_down)


def _combine_kernel(dest_cur_ref, dest_nxt_ref, x1_ref, mod_ref, route_ref, ln_g_ref, ln_b_ref, y_hbm,
                    o_ref, ybuf, sem):
    i = pl.program_id(0)
    n_steps = pl.num_programs(0)
    tile = x1_ref.shape[0]
    slot = lax.rem(i, 2)

    def gather(dest_ref, s):
        def body(r, carry):
            for kk in range(TOP_K):
                pltpu.make_async_copy(_row_at(y_hbm, dest_ref[0, 0, r * TOP_K + kk]),
                                      _row_at(ybuf.at[s, kk], r * ROW_SUB), sem.at[s]).start(priority=kk % 2)
            return carry
        lax.fori_loop(0, tile, body, 0, unroll=DMA_UNROLL)

    @pl.when(i == 0)
    def _():
        gather(dest_cur_ref, 0)

    @pl.when(i + 1 < n_steps)
    def _():
        gather(dest_nxt_ref, 1 - slot)

    for kk in range(TOP_K):
        pltpu.make_async_copy(y_hbm.at[pl.ds(0, tile * ROW_SUB)], ybuf.at[slot, kk], sem.at[slot]).wait()

    g_f = mod_ref[0][5:6]
    weight = route_ref[...]
    y = jnp.zeros(x1_ref.shape, F32)
    for kk in range(TOP_K):
        y = y + weight[:, kk:kk + 1] * _load_rows(ybuf, tile, (slot, kk))
    o_ref[...] = _layer_norm(DEEPNORM_ALPHA * x1_ref[...] + g_f * y, ln_g_ref[...], ln_b_ref[...])


def _combine(dest8, x1, mod, route, ln_g, ln_b, y_rows, seq):
    n_tok, d = x1.shape
    tile = COMBINE_TILE
    n_t = n_tok // tile
    per_seq = seq // tile
    dest3 = dest8.reshape(n_t, 1, tile * TOP_K)
    smem_blk = lambda fn: pl.BlockSpec((1, 1, tile * TOP_K), fn, memory_space=pltpu.SMEM)
    row = lambda width: pl.BlockSpec((tile, width), lambda i: (i, 0))
    vec = pl.BlockSpec((1, d), lambda i: (0, 0))
    return pl.pallas_call(
        _combine_kernel,
        grid=(n_t,),
        in_specs=[smem_blk(lambda i: (i, 0, 0)), smem_blk(lambda i: (jnp.minimum(i + 1, n_t - 1), 0, 0)),
                  row(d), pl.BlockSpec((1, 6, d), lambda i: (i // per_seq, 0, 0)), row(TOP_K), vec, vec,
                  pl.BlockSpec(memory_space=pl.ANY)],
        out_specs=row(d),
        out_shape=jax.ShapeDtypeStruct((n_tok, d), F32),
        scratch_shapes=[pltpu.VMEM((2, TOP_K, tile * ROW_SUB, LANES), F32), pltpu.SemaphoreType.DMA((2,))],
        compiler_params=pltpu.CompilerParams(
            dimension_semantics=("arbitrary",), vmem_limit_bytes=VMEM_LIMIT),
        name="combine",
    )(dest3, dest3, x1, mod, route, ln_g, ln_b, y_rows)


def _split_in_proj(w_in):
    rank_lo = _OFF[5]
    rank_hi = rank_lo + GLA_GATE_RANK
    pad = jnp.zeros((w_in.shape[0], RANK_PAD - GLA_GATE_RANK), BF16)
    return (w_in[:, :rank_lo].astype(BF16),
            jnp.concatenate([w_in[:, rank_lo:rank_hi].astype(BF16), pad], axis=1),
            w_in[:, rank_hi:].astype(BF16))


def _layer(x, c, w_ada, b_ada, w_in, w_pool_group, pool_scale, w_branch_a, w_alpha_up, b_alpha,
           gla_norm_gain, w_branch_b, w_out, ln1_gain, ln1_bias, w_router, b_router,
           w_gate_up, b_gate_up, w_down, b_down, ln2_gain, ln2_bias):
    bsz, seq, d = x.shape
    n_tok = bsz * seq
    n_assign = n_tok * TOP_K
    row2 = lambda v: v.reshape(1, -1)

    mod = _ada(c, w_ada, b_ada).reshape(bsz, 6, d)

    w_in_p, w_rank_p, w_gates = _split_in_proj(w_in)
    w_al_p = jnp.concatenate(
        [w_alpha_up, jnp.zeros((RANK_PAD - GLA_GATE_RANK, GLA_KEY_DIM), w_alpha_up.dtype)], axis=0).astype(BF16)
    w_r_t = w_router.T
    w_r_hi = w_r_t.astype(BF16)
    w_r_split = jnp.concatenate([w_r_hi, (w_r_t - w_r_hi.astype(F32)).astype(BF16)], axis=0)
    x1, u2_rows, route_t, counts = _mixer(
        x, mod, w_in_p, w_rank_p, w_gates, w_pool_group.astype(BF16), row2(pool_scale),
        w_branch_a.astype(BF16), w_al_p,
        row2(b_alpha), row2(gla_norm_gain), w_branch_b.astype(BF16), w_out.astype(BF16), row2(ln1_gain),
        row2(ln1_bias), w_r_split, b_router.reshape(N_EXPERTS, 1))

    top_idx = route_t[0:TOP_K].T.astype(jnp.int32)
    rank = route_t[TOP_K:2 * TOP_K].T.astype(jnp.int32)
    route = route_t[2 * TOP_K:3 * TOP_K].T
    counts = counts[:, 0].astype(jnp.int32)
    padded = (counts + EXPERT_BLOCK - 1) // EXPERT_BLOCK * EXPERT_BLOCK
    pad_end = jnp.cumsum(padded)
    pad_start = pad_end - padded
    n_rows = (n_assign + N_EXPERTS * (EXPERT_BLOCK - 1) + EXPERT_BLOCK - 1) // EXPERT_BLOCK * EXPERT_BLOCK
    n_blocks = n_rows // EXPERT_BLOCK
    n_active = (pad_end[-1] // EXPERT_BLOCK).astype(jnp.int32)
    start_of = jnp.sum(jnp.where(top_idx[..., None] == jnp.arange(N_EXPERTS, dtype=jnp.int32), pad_start, 0),
                       axis=-1)
    dest8 = ((start_of + rank) * ROW_SUB).reshape(-1)
    blk_row = jnp.arange(n_blocks, dtype=jnp.int32)[:, None] * EXPERT_BLOCK
    block_expert = jnp.minimum(jnp.sum(pad_end[None, :] <= blk_row, axis=1), N_EXPERTS - 1).astype(jnp.int32)

    last_window = jnp.where(padded > 0, pad_end // EXPERT_BLOCK - 1, -1)
    fill_windows = jnp.concatenate([last_window, n_active.reshape(1)]).astype(jnp.int32)
    x_rows = _dispatch(fill_windows, dest8, u2_rows, n_rows)
    f = w_down.shape[1]
    bg = b_gate_up[:, 0::2].reshape(N_EXPERTS, 1, f)
    bu = b_gate_up[:, 1::2].reshape(N_EXPERTS, 1, f)
    eid = jnp.arange(N_EXPERTS, dtype=jnp.int32)
    owns = counts > 0
    later = jnp.where((eid[None, :] > eid[:, None]) & owns[None, :], eid[None, :], N_EXPERTS)
    next_of = jnp.min(later, axis=1)
    next_of = jnp.where(next_of == N_EXPERTS, -1, next_of).astype(jnp.int32)
    parity_of = ((jnp.cumsum(owns.astype(jnp.int32)) - 1) % 2).astype(jnp.int32)
    is_block_expert = block_expert[:, None] == eid[None, :]
    per_block = lambda v: jnp.sum(jnp.where(is_block_expert, v[None, :], 0), axis=1).astype(jnp.int32)
    left = per_block(pad_start + counts) - blk_row[:, 0]
    both_halves = (left > EXPERT_HALF).astype(jnp.int32)
    y_rows = _moe(block_expert, per_block(next_of), per_block(parity_of), both_halves, n_active.reshape(1),
                  x_rows, w_gate_up, bg, bu, w_down, b_down.reshape(N_EXPERTS, 1, d))
    out = _combine(dest8, x1.reshape(n_tok, d), mod, route, row2(ln2_gain), row2(ln2_bias), y_rows, seq)
    return out.reshape(bsz, seq, d)


def kernel(x, c, w_ada, b_ada, w_in, w_pool_group, pool_scale, w_branch_a, w_alpha_up, b_alpha, gla_norm_gain,
           w_branch_b, w_out, ln1_gain, ln1_bias, w_router, b_router, w_gate_up, b_gate_up, w_down, b_down,
           ln2_gain, ln2_bias):
    for l in range(DEPTH):
        x = _layer(x, c, w_ada[l], b_ada[l], w_in[l], w_pool_group[l], pool_scale[l], w_branch_a[l],
                   w_alpha_up[l], b_alpha[l], gla_norm_gain[l], w_branch_b[l], w_out[l], ln1_gain[l],
                   ln1_bias[l], w_router[l], b_router[l], w_gate_up[l], b_gate_up[l], w_down[l], b_down[l],
                   ln2_gain[l], ln2_bias[l])
    return x
```

```python
import functools

import jax
import jax.numpy as jnp
from jax import lax
from jax.experimental import pallas as pl
from jax.experimental.pallas import tpu as pltpu

D_MODEL = 1024
CHUNK = 64
SUB = 16
N_SUB = CHUNK // SUB
POOL_WINDOWS = (2, 4, 8, 16)
POOL_GROUP_WIDTH = D_MODEL // len(POOL_WINDOWS)
POOL_HALO = 16
GLA_HEADS = 4
GLA_KEY_DIM = D_MODEL // 2
GLA_HEAD_K = GLA_KEY_DIM // GLA_HEADS
GLA_HEAD_V = D_MODEL // GLA_HEADS
GLA_GATE_RANK = 16
GLA_TAU = 16.0
N_EXPERTS = 32
TOP_K = 4
SWIGLU_ALPHA = 1.702
SWIGLU_LIMIT = 7.0
EXPERT_BLOCK = 512
EXPERT_HALF = EXPERT_BLOCK // 2
LN_EPS = 1e-5
RMS_EPS = 1e-6
DEPTH = 1
DEEPNORM_ALPHA = (2.0 * DEPTH) ** 0.25

LANES = 128
MXU_COLS = 256
RANK_PAD = LANES
_W = (D_MODEL, GLA_KEY_DIM, GLA_KEY_DIM, D_MODEL, D_MODEL, RANK_PAD, D_MODEL, D_MODEL)
_OFF = tuple(sum(_W[:i]) for i in range(len(_W) + 1))
EXP_CAP = 60.0
SAFE_SUB_DECAY = 40.0
ROUTE_ROWS = 16

SEQ_TILE = 512
DISPATCH_TILE = 256
COMBINE_TILE = 256
DMA_UNROLL = 8
VMEM_LIMIT = 60 * 1024 * 1024

F32 = jnp.float32
BF16 = jnp.bfloat16
HI = lax.Precision.HIGHEST


def _dot(a, b):
    return jnp.dot(a, b, preferred_element_type=F32)


def _dot_nt(a, b):
    return lax.dot_general(a, b, (((1,), (1,)), ((), ())), preferred_element_type=F32)


def _dot_tn(a, b):
    return lax.dot_general(a, b, (((0,), (0,)), ((), ())), preferred_element_type=F32)


def _sigmoid(v):
    return 0.5 * jnp.tanh(0.5 * v) + 0.5


ROW_SUB = D_MODEL // LANES


def _load_rows(ref, n_rows, lead=()):
    return jnp.concatenate(
        [ref[lead + (pl.ds(j, n_rows, stride=ROW_SUB), slice(None))] for j in range(ROW_SUB)], axis=1)


def _store_rows(ref, val):
    for j in range(ROW_SUB):
        ref[pl.ds(j, val.shape[0], stride=ROW_SUB), :] = val[:, j * LANES:(j + 1) * LANES]


def _ada_kernel(c_ref, w_ref, b_ref, o_ref):
    c = c_ref[...]
    s = c * jax.nn.sigmoid(c)
    o_ref[...] = jnp.dot(s, w_ref[...], precision=HI, preferred_element_type=F32) + b_ref[...]


def _ada(c, w_ada, b_ada):
    bsz, d = c.shape
    n = w_ada.shape[1]
    tn = 1024
    return pl.pallas_call(
        _ada_kernel,
        grid=(n // tn,),
        in_specs=[
            pl.BlockSpec((bsz, d), lambda j: (0, 0)),
            pl.BlockSpec((d, tn), lambda j: (0, j)),
            pl.BlockSpec((1, tn), lambda j: (0, j)),
        ],
        out_specs=pl.BlockSpec((bsz, tn), lambda j: (0, j)),
        out_shape=jax.ShapeDtypeStruct((bsz, n), F32),
        name="ada",
    )(c, w_ada, b_ada.reshape(1, n))


def _layer_norm(z, gain, bias):
    mu = jnp.mean(z, axis=-1, keepdims=True)
    zc = z - mu
    var = jnp.mean(zc * zc, axis=-1, keepdims=True)
    return zc * lax.rsqrt(var + LN_EPS) * gain + bias


def _mixer_kernel(x_ref, mod_ref, mod_prev_ref, w_in_ref, w_rank_ref, w_gates_ref, w_pool_ref, pool_scale_ref,
                  w_a_ref, w_al_ref,
                  b_al_ref, gain_ref, w_b_ref, w_out_ref, ln_g_ref, ln_b_ref, w_r_ref, b_r_ref,
                  x1_ref, u2_ref, route_ref, counts_ref,
                  a_ext, s_ref, cnt_ref, o_ref, z_ref, sc_ref, *, tiles_per_seq, n_tiles):
    j = pl.program_id(0)
    s_idx = lax.rem(jnp.minimum(j, n_tiles - 1), tiles_per_seq)
    tile = x_ref.shape[1]

    @pl.when(s_idx == 0)
    def _():
        s_ref[...] = jnp.zeros_like(s_ref)
        a_ext[0:POOL_HALO, :] = jnp.zeros((POOL_HALO, D_MODEL), F32)

    @pl.when(j == 0)
    def _():
        cnt_ref[...] = jnp.zeros_like(cnt_ref)
        z_ref[...] = jnp.zeros_like(z_ref)


    mod_prev = mod_prev_ref[0]
    x1 = _layer_norm(z_ref[...], ln_g_ref[...], ln_b_ref[...])
    x1_ref[0] = x1
    u2 = x1 * (1.0 + mod_prev[4:5]) + mod_prev[3:4]
    _store_rows(u2_ref, u2)
    u2_hi = u2.astype(BF16)
    u2_lo = (u2 - u2_hi.astype(F32)).astype(BF16)

    mod = mod_ref[0]
    sh_m, sc_m, g_m = mod[0:1], mod[1:2], mod[2:3]
    x = x_ref[0]
    u = (x * (1.0 + sc_m) + sh_m).astype(BF16)

    def proj(i):
        return _dot(u, w_in_ref[:, _OFF[i]:_OFF[i + 1]])

    gate_cols = ([(w_in_ref, _OFF[4] + jj * MXU_COLS) for jj in range(D_MODEL // MXU_COLS)]
                 + [(w_gates_ref, jj * MXU_COLS) for jj in range(2 * D_MODEL // MXU_COLS)])
    fillers = [functools.partial(lambda w_ref, c0: _dot(u, w_ref[:, c0:c0 + MXU_COLS]), w_ref, c0)
               for w_ref, c0 in gate_cols]
    filled = []

    def issue_fillers(n):
        for _ in range(n):
            if len(filled) < len(fillers):
                filled.append(fillers[len(filled)]())

    a = proj(0)
    q = proj(1) * (GLA_HEAD_K ** -0.5)
    k_all = proj(2)
    v_all = proj(3)
    alpha_low = _dot(u, w_rank_ref[...])

    part = _dot_nt(w_r_ref[...], u2_hi)
    logits = (part[:N_EXPERTS] + part[N_EXPERTS:] + _dot_nt(w_r_ref[0:N_EXPERTS, :], u2_lo)
              + b_r_ref[...])
    issue_fillers(2)

    a_ext[POOL_HALO:POOL_HALO + tile, :] = a
    t_glob = s_idx * tile + lax.broadcasted_iota(jnp.int32, (tile, 1), 0)
    mapped = []
    for g, w in enumerate(POOL_WINDOWS):
        lo, hi = g * POOL_GROUP_WIDTH, (g + 1) * POOL_GROUP_WIDTH
        win = a_ext[:, lo:hi]
        k = 1
        while k < w:
            win = win + pltpu.roll(win, k, 0)
            k *= 2
        inv_cnt = 1.0 / jnp.minimum(t_glob + 1, w).astype(F32)
        pooled = win[POOL_HALO:, :] * inv_cnt - a[:, lo:hi]
        mapped.append(_dot(pooled.astype(BF16), w_pool_ref[g]))
    a_ext[0:POOL_HALO, :] = a[tile - POOL_HALO:tile, :]
    ya = _dot((jnp.concatenate(mapped, axis=1) * pool_scale_ref[...]).astype(BF16), w_a_ref[...])

    erow = lax.broadcasted_iota(jnp.int32, (N_EXPERTS, tile), 0).astype(F32)
    work = logits
    sel = jnp.zeros((N_EXPERTS, tile), F32)
    vals, hits = [], []
    for _ in range(TOP_K):
        m = jnp.max(work, axis=0, keepdims=True)
        idx = jnp.min(jnp.where(work == m, erow, float(N_EXPERTS)), axis=0, keepdims=True)
        hit = erow == idx
        vals.append(m)
        hits.append((idx, hit))
        sel = jnp.where(hit, 1.0, sel)
        work = jnp.where(hit, -jnp.inf, work)
    exps = [jnp.exp(v - vals[0]) for v in vals]
    inv_den = 1.0 / (exps[0] + exps[1] + exps[2] + exps[3])
    issue_fillers(1)

    z = _dot(alpha_low.astype(BF16), w_al_ref[...]) + b_al_ref[...]
    log_a = (jnp.minimum(z, 0.0) - jnp.log1p(jnp.exp(-jnp.abs(z)))) * (1.0 / GLA_TAU)

    ri = lax.broadcasted_iota(jnp.int32, (CHUNK, CHUNK), 0)
    ci = lax.broadcasted_iota(jnp.int32, (CHUNK, CHUNK), 1)
    causal = ci <= ri
    cum_mat = causal.astype(BF16)
    la_hi = log_a.astype(BF16)
    la_split = jnp.concatenate([la_hi, (log_a - la_hi.astype(F32)).astype(BF16)], axis=1)

    n_chunks = tile // CHUNK
    pairs = [(c, h) for c in range(n_chunks) for h in range(GLA_HEADS)]
    rows_of = lambda c: slice(c * CHUNK, (c + 1) * CHUNK)
    ks_of = lambda h: slice(h * GLA_HEAD_K, (h + 1) * GLA_HEAD_K)
    vs_of = lambda h: slice(h * GLA_HEAD_V, (h + 1) * GLA_HEAD_V)

    b_cum, ref_pts, b_ref_pt = [], [], []
    for c in range(n_chunks):
        cum = _dot(cum_mat, la_split[rows_of(c)])
        b_cum.append(cum[:, :GLA_KEY_DIM] + cum[:, GLA_KEY_DIM:])
        ref_pts.append([jnp.zeros((1, GLA_KEY_DIM), F32)]
                       + [b_cum[c][i * SUB - 1:i * SUB, :] for i in range(1, N_SUB)])
        b_ref_pt.append(jnp.concatenate([jnp.broadcast_to(p, (SUB, GLA_KEY_DIM)) for p in ref_pts[c]], axis=0))

    rt = lax.broadcasted_iota(jnp.int32, (tile, tile), 0)
    ct = lax.broadcasted_iota(jnp.int32, (tile, tile), 1)
    base = _dot(sel.astype(BF16), (rt < ct).astype(BF16)) + cnt_ref[:, 0:1]

    for n_pair, (c, h) in enumerate(pairs):
        if n_pair % 3 == 0:
            issue_fillers(1)
        qh, kh, bh = q[rows_of(c), ks_of(h)], k_all[rows_of(c), ks_of(h)], b_cum[c][:, ks_of(h)]
        q_dec = (qh * jnp.exp(bh - b_ref_pt[c][:, ks_of(h)])).astype(BF16)
        k_dec = jnp.concatenate(
            [(kh * jnp.exp(jnp.minimum(ref_pts[c][i][:, ks_of(h)] - bh, EXP_CAP))).astype(BF16)
             for i in range(N_SUB)], axis=0)
        s_all = _dot_nt(q_dec, k_dec)
        sc_ref[n_pair] = jnp.concatenate(
            [s_all[i * SUB:(i + 1) * SUB, i * CHUNK:(i + 1) * CHUNK] for i in range(N_SUB)], axis=0)

    sub_decay = b_ref_pt[0] - b_cum[0]
    for c in range(1, n_chunks):
        sub_decay = jnp.maximum(sub_decay, b_ref_pt[c] - b_cum[c])

    @pl.when(jnp.max(sub_decay) > SAFE_SUB_DECAY)
    def _():
        o_ref[:, 0:GLA_KEY_DIM] = q
        o_ref[:, GLA_KEY_DIM:2 * GLA_KEY_DIM] = jnp.concatenate(b_cum, axis=0)
        for n_pair, (c, h) in enumerate(pairs):
            kh, bh = k_all[rows_of(c), ks_of(h)], b_cum[c][:, ks_of(h)]

            def score_rows(g, carry):
                first = pl.multiple_of(g * 8, 8)
                q_8 = o_ref[pl.ds(c * CHUNK + first, 8), ks_of(h)]
                b_8 = o_ref[pl.ds(c * CHUNK + first, 8),
                            GLA_KEY_DIM + h * GLA_HEAD_K:GLA_KEY_DIM + (h + 1) * GLA_HEAD_K]
                rows = []
                for r in range(8):
                    k_i = (kh * jnp.exp(jnp.minimum(b_8[r:r + 1] - bh, 0.0))).astype(BF16)
                    q_i = jnp.broadcast_to(q_8[r:r + 1], (8, GLA_HEAD_K)).astype(BF16)
                    rows.append(_dot_nt(q_i, k_i)[0:1])
                sc_ref[n_pair, pl.ds(first, 8), :] = jnp.concatenate(rows, axis=0)
                return carry

            lax.fori_loop(0, CHUNK // 8, score_rows, 0)

    o_intra, kv, decay_last = {}, {}, {}
    for n_pair, (c, h) in enumerate(pairs):
        kh, bh = k_all[rows_of(c), ks_of(h)], b_cum[c][:, ks_of(h)]
        vh = v_all[rows_of(c), vs_of(h)].astype(BF16)
        o_intra[c, h] = _dot(jnp.where(causal, sc_ref[n_pair], 0.0).astype(BF16), vh)
        b_last = bh[CHUNK - 1:CHUNK, :]
        kv[c, h] = _dot_tn(vh, (kh * jnp.exp(b_last - bh)).astype(BF16))
        decay_last[c, h] = jnp.exp(b_last)

    orow = lax.broadcasted_iota(jnp.int32, (ROUTE_ROWS, tile), 0)
    route = jnp.zeros((ROUTE_ROWS, tile), F32)
    for kk in range(TOP_K):
        idx, hit = hits[kk]
        rank = jnp.sum(jnp.where(hit, base, 0.0), axis=0, keepdims=True)
        route = jnp.where(orow == kk, idx, route)
        route = jnp.where(orow == TOP_K + kk, rank, route)
        route = jnp.where(orow == 2 * TOP_K + kk, exps[kk] * inv_den, route)
    route_ref[...] = route
    cnt_ref[...] = cnt_ref[...] + (j > 0).astype(F32) * jnp.sum(sel, axis=1, keepdims=True)
    counts_ref[...] = cnt_ref[...]

    o_inter = {}
    state_t = [s_ref[h] for h in range(GLA_HEADS)]
    for c, h in pairs:
        q_in = (q[rows_of(c), ks_of(h)] * jnp.exp(b_cum[c][:, ks_of(h)])).astype(BF16)
        o_inter[c, h] = _dot_nt(q_in, state_t[h].astype(BF16))
        state_t[h] = state_t[h] * decay_last[c, h] + kv[c, h]
    for h in range(GLA_HEADS):
        s_ref[h] = state_t[h]

    for c, h in pairs:
        o = o_intra[c, h] + o_inter[c, h]
        o = o * lax.rsqrt(jnp.mean(o * o, axis=-1, keepdims=True) + RMS_EPS) * gain_ref[...]
        o_ref[rows_of(c), vs_of(h)] = o

    issue_fillers(len(fillers))
    per_proj = D_MODEL // MXU_COLS
    r, gate_a, gate_b = (jnp.concatenate(filled[n * per_proj:(n + 1) * per_proj], axis=1) for n in range(3))
    yb = _dot((o_ref[...] * (r * _sigmoid(r))).astype(BF16), w_b_ref[...])

    merged = _sigmoid(gate_a) * ya + _sigmoid(gate_b) * yb
    y = _dot(merged.astype(BF16), w_out_ref[...])
    z_ref[...] = DEEPNORM_ALPHA * x + g_m * y


def _mixer(x, mod, w_in_p, w_rank_p, w_gates, w_pool, pool_scale, w_a, w_al_p, b_alpha, gain, w_b, w_out,
           ln_g, ln_b, w_r_p, b_r_p):
    bsz, seq, d = x.shape
    tile = SEQ_TILE
    n_s = seq // tile
    n_tiles = bsz * n_s

    def const(shape):
        nd = len(shape)
        return pl.BlockSpec(shape, lambda j: (0,) * nd, pipeline_mode=pl.Buffered(1))

    cur = lambda j: jnp.minimum(j, n_tiles - 1)
    prev = lambda j: jnp.maximum(j - 1, 0)
    return pl.pallas_call(
        functools.partial(_mixer_kernel, tiles_per_seq=n_s, n_tiles=n_tiles),
        grid=(n_tiles + 1,),
        in_specs=[
            pl.BlockSpec((1, tile, d), lambda j: (cur(j) // n_s, cur(j) % n_s, 0)),
            pl.BlockSpec((1, 6, d), lambda j: (cur(j) // n_s, 0, 0)),
            pl.BlockSpec((1, 6, d), lambda j: (prev(j) // n_s, 0, 0)),
            const(w_in_p.shape), const(w_rank_p.shape), const(w_gates.shape),
            const(w_pool.shape), const(pool_scale.shape), const(w_a.shape),
            const(w_al_p.shape), const(b_alpha.shape), const(gain.shape), const(w_b.shape),
            const(w_out.shape), const(ln_g.shape), const(ln_b.shape), const(w_r_p.shape),
            const(b_r_p.shape),
        ],
        out_specs=[pl.BlockSpec((1, tile, d), lambda j: (prev(j) // n_s, prev(j) % n_s, 0)),
                   pl.BlockSpec((tile * ROW_SUB, LANES), lambda j: (prev(j), 0)),
                   pl.BlockSpec((ROUTE_ROWS, tile), lambda j: (0, prev(j))),
                   pl.BlockSpec((N_EXPERTS, LANES), lambda j: (0, 0))],
        out_shape=[
            jax.ShapeDtypeStruct((bsz, seq, d), F32),
            jax.ShapeDtypeStruct((bsz * seq * ROW_SUB, LANES), F32),
            jax.ShapeDtypeStruct((ROUTE_ROWS, bsz * seq), F32),
            jax.ShapeDtypeStruct((N_EXPERTS, LANES), F32),
        ],
        scratch_shapes=[
            pltpu.VMEM((POOL_HALO + tile, d), F32),
            pltpu.VMEM((GLA_HEADS, GLA_HEAD_V, GLA_HEAD_K), F32),
            pltpu.VMEM((N_EXPERTS, LANES), F32),
            pltpu.VMEM((tile, d), F32),
            pltpu.VMEM((tile, d), F32),
            pltpu.VMEM((tile // CHUNK * GLA_HEADS, CHUNK, CHUNK), F32),
        ],
        compiler_params=pltpu.CompilerParams(
            dimension_semantics=("arbitrary",), vmem_limit_bytes=VMEM_LIMIT),
        name="mixer",
    )(x, mod, mod, w_in_p, w_rank_p, w_gates, w_pool, pool_scale, w_a, w_al_p, b_alpha, gain, w_b, w_out,
      ln_g, ln_b, w_r_p, b_r_p)


def _row_at(ref, row8):
    return ref.at[pl.ds(pl.multiple_of(row8, ROW_SUB), ROW_SUB)]


def _dispatch_kernel(fill_ref, dest_ref, u2_ref, u2_hbm, rows_hbm, zbuf, sem_fill, sem_rows):
    i = pl.program_id(0)
    tile = u2_ref.shape[0] // ROW_SUB
    blk8 = EXPERT_BLOCK * ROW_SUB

    @pl.when(i == 0)
    def _():
        zbuf[...] = jnp.zeros_like(zbuf)
        n_windows = rows_hbm.shape[0] // blk8

        def fill(window):
            return pltpu.make_async_copy(zbuf, rows_hbm.at[pl.ds(pl.multiple_of(window * blk8, blk8), blk8)],
                                         sem_fill.at[0])

        for phase in ("start", "wait"):
            for e in range(N_EXPERTS):
                @pl.when(fill_ref[e] >= 0)
                def _():
                    getattr(fill(fill_ref[e]), phase)()

            def unused(window, carry):
                getattr(fill(window), phase)()
                return carry

            lax.fori_loop(fill_ref[N_EXPERTS], n_windows, unused, 0)

    def body(r, carry):
        for kk in range(TOP_K):
            src = (_row_at(u2_ref, r * ROW_SUB) if kk < TOP_K // 2
                   else _row_at(u2_hbm, (i * tile + r) * ROW_SUB))
            pltpu.make_async_copy(src, _row_at(rows_hbm, dest_ref[0, 0, r * TOP_K + kk]),
                                  sem_rows.at[0]).start(priority=kk % 2)
        return carry

    lax.fori_loop(0, tile, body, 0, unroll=DMA_UNROLL)
    for _ in range(TOP_K):
        pltpu.make_async_copy(u2_ref, rows_hbm.at[pl.ds(0, tile * ROW_SUB)], sem_rows.at[0]).wait()


def _dispatch(fill_start, dest8, u2_rows, n_rows):
    n_tok = u2_rows.shape[0] // ROW_SUB
    tile = DISPATCH_TILE
    n_t = n_tok // tile
    grid_spec = pltpu.PrefetchScalarGridSpec(
        num_scalar_prefetch=1,
        grid=(n_t,),
        in_specs=[
            pl.BlockSpec((1, 1, tile * TOP_K), lambda i, fs: (i, 0, 0), memory_space=pltpu.SMEM),
            pl.BlockSpec((tile * ROW_SUB, LANES), lambda i, fs: (i, 0)),
            pl.BlockSpec(memory_space=pl.ANY),
        ],
        out_specs=pl.BlockSpec(memory_space=pl.ANY),
        scratch_shapes=[
            pltpu.VMEM((EXPERT_BLOCK * ROW_SUB, LANES), F32),
            pltpu.SemaphoreType.DMA((1,)),
            pltpu.SemaphoreType.DMA((1,)),
        ],
    )
    return pl.pallas_call(
        _dispatch_kernel,
        grid_spec=grid_spec,
        out_shape=jax.ShapeDtypeStruct(((n_rows + EXPERT_BLOCK) * ROW_SUB, LANES), F32),
        compiler_params=pltpu.CompilerParams(dimension_semantics=("arbitrary",)),
        name="dispatch",
    )(fill_start, dest8.reshape(n_t, 1, tile * TOP_K), u2_rows, u2_rows)


def _moe_kernel(be_ref, nxt_ref, par_ref, full_ref, nb_ref, x_ref, bg_ref, bu_ref, bd_ref, wgu_hbm, wd_hbm, y_ref,
                wgu_buf, wd_buf, wg_s, wu_s, wd_s, sem_gu, sem_d):
    i = pl.program_id(0)
    new_expert = (i == 0) | (be_ref[i] != be_ref[jnp.maximum(i - 1, 0)])

    def fetch(e, s):
        return (pltpu.make_async_copy(wgu_hbm.at[e], wgu_buf.at[s], sem_gu.at[s]),
                pltpu.make_async_copy(wd_hbm.at[e], wd_buf.at[s], sem_d.at[s]))

    @pl.when(new_expert & (i < nb_ref[0]))
    def _():
        slot = par_ref[i]

        @pl.when(i == 0)
        def _():
            for cp in fetch(be_ref[0], slot):
                cp.start()

        @pl.when(nxt_ref[i] >= 0)
        def _():
            for cp in fetch(nxt_ref[i], 1 - slot):
                cp.start()

        for cp in fetch(be_ref[i], slot):
            cp.wait()
        src = lax.broadcasted_iota(jnp.int32, (MXU_COLS, MXU_COLS), 0)
        col = lax.broadcasted_iota(jnp.int32, (MXU_COLS, MXU_COLS), 1)
        half = MXU_COLS // 2
        want = jnp.where(col < half, 2 * col, 2 * (col - half) + 1)
        unzip = (src == want).astype(BF16)
        for g in range(wgu_buf.shape[2] // MXU_COLS):
            blk = wgu_buf[slot, :, g * MXU_COLS:(g + 1) * MXU_COLS].astype(BF16)
            sep = _dot(blk, unzip)
            wg_s[:, g * half:(g + 1) * half] = sep[:, :half].astype(BF16)
            wu_s[:, g * half:(g + 1) * half] = sep[:, half:].astype(BF16)
        wd_s[...] = wd_buf[slot].astype(BF16)

    def expert_mlp(n_rows):
        xb = _load_rows(x_ref, n_rows).astype(BF16)
        gate = jnp.minimum(_dot(xb, wg_s[...]) + bg_ref[0], SWIGLU_LIMIT)
        up = jnp.clip(_dot(xb, wu_s[...]) + bu_ref[0], -SWIGLU_LIMIT, SWIGLU_LIMIT)
        glu = gate * _sigmoid(gate * SWIGLU_ALPHA)
        _store_rows(y_ref, _dot(((up + 1.0) * glu).astype(BF16), wd_s[...]) + bd_ref[0])

    active = i < nb_ref[0]
    both_halves = full_ref[i] == 1

    @pl.when(active & both_halves)
    def _():
        expert_mlp(EXPERT_BLOCK)

    @pl.when(active & jnp.logical_not(both_halves))
    def _():
        expert_mlp(EXPERT_HALF)
        y_ref[EXPERT_HALF * ROW_SUB:, :] = jnp.zeros((EXPERT_HALF * ROW_SUB, LANES), F32)

    @pl.when(jnp.logical_not(active))
    def _():
        y_ref[...] = jnp.zeros_like(y_ref)


def _moe(block_expert, next_expert, slot_parity, both_halves, n_active, x_rows, w_gate_up, bg, bu, w_down, bd):
    n_blocks = block_expert.shape[0]
    _, d, f2 = w_gate_up.shape
    f = f2 // 2
    blk8 = EXPERT_BLOCK * ROW_SUB
    rows_in = pl.BlockSpec((blk8, LANES),
                           lambda i, be, nx, pr, fl, nb: (jnp.maximum(jnp.minimum(i, nb[0] - 1), 0), 0))
    rows_out = pl.BlockSpec((blk8, LANES), lambda i, be, nx, pr, fl, nb: (i, 0))
    per_expert = lambda shape: pl.BlockSpec((1,) + shape, lambda i, be, nx, pr, fl, nb: (be[i], 0, 0))
    hbm = pl.BlockSpec(memory_space=pl.ANY)
    grid_spec = pltpu.PrefetchScalarGridSpec(
        num_scalar_prefetch=5,
        grid=(n_blocks,),
        in_specs=[rows_in, per_expert((1, f)), per_expert((1, f)), per_expert((1, d)), hbm, hbm],
        out_specs=rows_out,
        scratch_shapes=[pltpu.VMEM((2, d, f2), F32), pltpu.VMEM((2, f, d), F32),
                        pltpu.VMEM((d, f), BF16), pltpu.VMEM((d, f), BF16), pltpu.VMEM((f, d), BF16),
                        pltpu.SemaphoreType.DMA((2,)), pltpu.SemaphoreType.DMA((2,))],
    )
    return pl.pallas_call(
        _moe_kernel,
        grid_spec=grid_spec,
        out_shape=jax.ShapeDtypeStruct((n_blocks * blk8, LANES), F32),
        compiler_params=pltpu.CompilerParams(
            dimension_semantics=("arbitrary",), vmem_limit_bytes=VMEM_LIMIT),
        name="moe",
    )(block_expert, next_expert, slot_parity, both_halves, n_active, x_rows, bg, bu, bd, w_gate_up, w_down)


def _combine_kernel(dest_cur_ref, dest_nxt_ref, x1_ref, mod_ref, route_ref, ln_g_ref, ln_b_ref, y_hbm,
                    o_ref, ybuf, sem):
    i = pl.program_id(0)
    n_steps = pl.num_programs(0)
    tile = x1_ref.shape[0]
    slot = lax.rem(i, 2)

    def gather(dest_ref, s):
        def body(r, carry):
            for kk in range(TOP_K):
                pltpu.make_async_copy(_row_at(y_hbm, dest_ref[0, 0, r * TOP_K + kk]),
                                      _row_at(ybuf.at[s, kk], r * ROW_SUB), sem.at[s]).start(priority=kk % 2)
            return carry
        lax.fori_loop(0, tile, body, 0, unroll=DMA_UNROLL)

    @pl.when(i == 0)
    def _():
        gather(dest_cur_ref, 0)

    @pl.when(i + 1 < n_steps)
    def _():
        gather(dest_nxt_ref, 1 - slot)

    for kk in range(TOP_K):
        pltpu.make_async_copy(y_hbm.at[pl.ds(0, tile * ROW_SUB)], ybuf.at[slot, kk], sem.at[slot]).wait()

    g_f = mod_ref[0][5:6]
    weight = route_ref[...]
    y = jnp.zeros(x1_ref.shape, F32)
    for kk in range(TOP_K):
        y = y + weight[:, kk:kk + 1] * _load_rows(ybuf, tile, (slot, kk))
    o_ref[...] = _layer_norm(DEEPNORM_ALPHA * x1_ref[...] + g_f * y, ln_g_ref[...], ln_b_ref[...])


def _combine(dest8, x1, mod, route, ln_g, ln_b, y_rows, seq):
    n_tok, d = x1.shape
    tile = COMBINE_TILE
    n_t = n_tok // tile
    per_seq = seq // tile
    dest3 = dest8.reshape(n_t, 1, tile * TOP_K)
    smem_blk = lambda fn: pl.BlockSpec((1, 1, tile * TOP_K), fn, memory_space=pltpu.SMEM)
    row = lambda width: pl.BlockSpec((tile, width), lambda i: (i, 0))
    vec = pl.BlockSpec((1, d), lambda i: (0, 0))
    return pl.pallas_call(
        _combine_kernel,
        grid=(n_t,),
        in_specs=[smem_blk(lambda i: (i, 0, 0)), smem_blk(lambda i: (jnp.minimum(i + 1, n_t - 1), 0, 0)),
                  row(d), pl.BlockSpec((1, 6, d), lambda i: (i // per_seq, 0, 0)), row(TOP_K), vec, vec,
                  pl.BlockSpec(memory_space=pl.ANY)],
        out_specs=row(d),
        out_shape=jax.ShapeDtypeStruct((n_tok, d), F32),
        scratch_shapes=[pltpu.VMEM((2, TOP_K, tile * ROW_SUB, LANES), F32), pltpu.SemaphoreType.DMA((2,))],
        compiler_params=pltpu.CompilerParams(
            dimension_semantics=("arbitrary",), vmem_limit_bytes=VMEM_LIMIT),
        name="combine",
    )(dest3, dest3, x1, mod, route, ln_g, ln_b, y_rows)


def _split_in_proj(w_in):
    rank_lo = _OFF[5]
    rank_hi = rank_lo + GLA_GATE_RANK
    pad = jnp.zeros((w_in.shape[0], RANK_PAD - GLA_GATE_RANK), BF16)
    return (w_in[:, :rank_lo].astype(BF16),
            jnp.concatenate([w_in[:, rank_lo:rank_hi].astype(BF16), pad], axis=1),
            w_in[:, rank_hi:].astype(BF16))


def _layer(x, c, w_ada, b_ada, w_in, w_pool_group, pool_scale, w_branch_a, w_alpha_up, b_alpha,
           gla_norm_gain, w_branch_b, w_out, ln1_gain, ln1_bias, w_router, b_router,
           w_gate_up, b_gate_up, w_down, b_down, ln2_gain, ln2_bias):
    bsz, seq, d = x.shape
    n_tok = bsz * seq
    n_assign = n_tok * TOP_K
    row2 = lambda v: v.reshape(1, -1)

    mod = _ada(c, w_ada, b_ada).reshape(bsz, 6, d)

    w_in_p, w_rank_p, w_gates = _split_in_proj(w_in)
    w_al_p = jnp.concatenate(
        [w_alpha_up, jnp.zeros((RANK_PAD - GLA_GATE_RANK, GLA_KEY_DIM), w_alpha_up.dtype)], axis=0).astype(BF16)
    w_r_t = w_router.T
    w_r_hi = w_r_t.astype(BF16)
    w_r_split = jnp.concatenate([w_r_hi, (w_r_t - w_r_hi.astype(F32)).astype(BF16)], axis=0)
    x1, u2_rows, route_t, counts = _mixer(
        x, mod, w_in_p, w_rank_p, w_gates, w_pool_group.astype(BF16), row2(pool_scale),
        w_branch_a.astype(BF16), w_al_p,
        row2(b_alpha), row2(gla_norm_gain), w_branch_b.astype(BF16), w_out.astype(BF16), row2(ln1_gain),
        row2(ln1_bias), w_r_split, b_router.reshape(N_EXPERTS, 1))

    top_idx = route_t[0:TOP_K].T.astype(jnp.int32)
    rank = route_t[TOP_K:2 * TOP_K].T.astype(jnp.int32)
    route = route_t[2 * TOP_K:3 * TOP_K].T
    counts = counts[:, 0].astype(jnp.int32)
    padded = (counts + EXPERT_BLOCK - 1) // EXPERT_BLOCK * EXPERT_BLOCK
    pad_end = jnp.cumsum(padded)
    pad_start = pad_end - padded
    n_rows = (n_assign + N_EXPERTS * (EXPERT_BLOCK - 1) + EXPERT_BLOCK - 1) // EXPERT_BLOCK * EXPERT_BLOCK
    n_blocks = n_rows // EXPERT_BLOCK
    n_active = (pad_end[-1] // EXPERT_BLOCK).astype(jnp.int32)
    start_of = jnp.sum(jnp.where(top_idx[..., None] == jnp.arange(N_EXPERTS, dtype=jnp.int32), pad_start, 0),
                       axis=-1)
    dest8 = ((start_of + rank) * ROW_SUB).reshape(-1)
    blk_row = jnp.arange(n_blocks, dtype=jnp.int32)[:, None] * EXPERT_BLOCK
    block_expert = jnp.minimum(jnp.sum(pad_end[None, :] <= blk_row, axis=1), N_EXPERTS - 1).astype(jnp.int32)

    last_window = jnp.where(padded > 0, pad_end // EXPERT_BLOCK - 1, -1)
    fill_windows = jnp.concatenate([last_window, n_active.reshape(1)]).astype(jnp.int32)
    x_rows = _dispatch(fill_windows, dest8, u2_rows, n_rows)
    f = w_down.shape[1]
    bg = b_gate_up[:, 0::2].reshape(N_EXPERTS, 1, f)
    bu = b_gate_up[:, 1::2].reshape(N_EXPERTS, 1, f)
    eid = jnp.arange(N_EXPERTS, dtype=jnp.int32)
    owns = counts > 0
    later = jnp.where((eid[None, :] > eid[:, None]) & owns[None, :], eid[None, :], N_EXPERTS)
    next_of = jnp.min(later, axis=1)
    next_of = jnp.where(next_of == N_EXPERTS, -1, next_of).astype(jnp.int32)
    parity_of = ((jnp.cumsum(owns.astype(jnp.int32)) - 1) % 2).astype(jnp.int32)
    is_block_expert = block_expert[:, None] == eid[None, :]
    per_block = lambda v: jnp.sum(jnp.where(is_block_expert, v[None, :], 0), axis=1).astype(jnp.int32)
    left = per_block(pad_start + counts) - blk_row[:, 0]
    both_halves = (left > EXPERT_HALF).astype(jnp.int32)
    y_rows = _moe(block_expert, per_block(next_of), per_block(parity_of), both_halves, n_active.reshape(1),
                  x_rows, w_gate_up, bg, bu, w_down, b_down.reshape(N_EXPERTS, 1, d))
    out = _combine(dest8, x1.reshape(n_tok, d), mod, route, row2(ln2_gain), row2(ln2_bias), y_rows, seq)
    return out.reshape(bsz, seq, d)


def kernel(x, c, w_ada, b_ada, w_in, w_pool_group, pool_scale, w_branch_a, w_alpha_up, b_alpha, gla_norm_gain,
           w_branch_b, w_out, ln1_gain, ln1_bias, w_router, b_router, w_gate_up, b_gate_up, w_down, b_down,
           ln2_gain, ln2_bias):
    for l in range(DEPTH):
        x = _layer(x, c, w_ada[l], b_ada[l], w_in[l], w_pool_group[l], pool_scale[l], w_branch_a[l],
                   w_alpha_up[l], b_alpha[l], gla_norm_gain[l], w_branch_b[l], w_out[l], ln1_gain[l],
                   ln1_bias[l], w_router[l], b_router[l], w_gate_up[l], b_gate_up[l], w_down[l], b_down[l],
                   ln2_gain[l], ln2_bias[l])
    return x
```

```python
import functools

import jax
import jax.numpy as jnp
from jax import lax
from jax.experimental import pallas as pl
from jax.experimental.pallas import tpu as pltpu

D_MODEL = 1024
CHUNK = 64
SUB = 16
N_SUB = CHUNK // SUB
POOL_WINDOWS = (2, 4, 8, 16)
POOL_GROUP_WIDTH = D_MODEL // len(POOL_WINDOWS)
POOL_HALO = 16
GLA_HEADS = 4
GLA_KEY_DIM = D_MODEL // 2
GLA_HEAD_K = GLA_KEY_DIM // GLA_HEADS
GLA_HEAD_V = D_MODEL // GLA_HEADS
GLA_GATE_RANK = 16
GLA_TAU = 16.0
N_EXPERTS = 32
TOP_K = 4
SWIGLU_ALPHA = 1.702
SWIGLU_LIMIT = 7.0
EXPERT_BLOCK = 512
EXPERT_HALF = EXPERT_BLOCK // 2
LN_EPS = 1e-5
RMS_EPS = 1e-6
DEPTH = 1
DEEPNORM_ALPHA = (2.0 * DEPTH) ** 0.25

LANES = 128
MXU_COLS = 256
RANK_PAD = LANES
_W = (D_MODEL, GLA_KEY_DIM, GLA_KEY_DIM, D_MODEL, D_MODEL, RANK_PAD, D_MODEL, D_MODEL)
_OFF = tuple(sum(_W[:i]) for i in range(len(_W) + 1))
EXP_CAP = 60.0
SAFE_SUB_DECAY = 40.0
ROUTE_ROWS = 16

SEQ_TILE = 512
DISPATCH_TILE = 256
COMBINE_TILE = 256
DMA_UNROLL = 8
VMEM_LIMIT = 60 * 1024 * 1024

F32 = jnp.float32
BF16 = jnp.bfloat16
HI = lax.Precision.HIGHEST


def _dot(a, b):
    return jnp.dot(a, b, preferred_element_type=F32)


def _dot_nt(a, b):
    return lax.dot_general(a, b, (((1,), (1,)), ((), ())), preferred_element_type=F32)


def _dot_tn(a, b):
    return lax.dot_general(a, b, (((0,), (0,)), ((), ())), preferred_element_type=F32)


def _sigmoid(v):
    return 0.5 * jnp.tanh(0.5 * v) + 0.5


ROW_SUB = D_MODEL // LANES


def _load_rows(ref, n_rows, lead=()):
    return jnp.concatenate(
        [ref[lead + (pl.ds(j, n_rows, stride=ROW_SUB), slice(None))] for j in range(ROW_SUB)], axis=1)


def _store_rows(ref, val):
    for j in range(ROW_SUB):
        ref[pl.ds(j, val.shape[0], stride=ROW_SUB), :] = val[:, j * LANES:(j + 1) * LANES]


def _ada_kernel(c_ref, w_ref, b_ref, o_ref):
    c = c_ref[...]
    s = c * jax.nn.sigmoid(c)
    o_ref[...] = jnp.dot(s, w_ref[...], precision=HI, preferred_element_type=F32) + b_ref[...]


def _ada(c, w_ada, b_ada):
    bsz, d = c.shape
    n = w_ada.shape[1]
    tn = 1024
    return pl.pallas_call(
        _ada_kernel,
        grid=(n // tn,),
        in_specs=[
            pl.BlockSpec((bsz, d), lambda j: (0, 0)),
            pl.BlockSpec((d, tn), lambda j: (0, j)),
            pl.BlockSpec((1, tn), lambda j: (0, j)),
        ],
        out_specs=pl.BlockSpec((bsz, tn), lambda j: (0, j)),
        out_shape=jax.ShapeDtypeStruct((bsz, n), F32),
        name="ada",
    )(c, w_ada, b_ada.reshape(1, n))


def _layer_norm(z, gain, bias):
    mu = jnp.mean(z, axis=-1, keepdims=True)
    zc = z - mu
    var = jnp.mean(zc * zc, axis=-1, keepdims=True)
    return zc * lax.rsqrt(var + LN_EPS) * gain + bias


def _mixer_kernel(x_ref, mod_ref, mod_prev_ref, w_in_ref, w_rank_ref, w_gates_ref, w_pool_ref, pool_scale_ref,
                  w_a_ref, w_al_ref,
                  b_al_ref, gain_ref, w_b_ref, w_out_ref, ln_g_ref, ln_b_ref, w_r_ref, b_r_ref,
                  x1_ref, u2_ref, route_ref, counts_ref,
                  a_ext, s_ref, cnt_ref, o_ref, z_ref, sc_ref, *, tiles_per_seq, n_tiles):
    j = pl.program_id(0)
    s_idx = lax.rem(jnp.minimum(j, n_tiles - 1), tiles_per_seq)
    tile = x_ref.shape[1]

    @pl.when(s_idx == 0)
    def _():
        s_ref[...] = jnp.zeros_like(s_ref)
        a_ext[0:POOL_HALO, :] = jnp.zeros((POOL_HALO, D_MODEL), F32)

    @pl.when(j == 0)
    def _():
        cnt_ref[...] = jnp.zeros_like(cnt_ref)
        z_ref[...] = jnp.zeros_like(z_ref)


    mod_prev = mod_prev_ref[0]
    x1 = _layer_norm(z_ref[...], ln_g_ref[...], ln_b_ref[...])
    x1_ref[0] = x1
    u2 = x1 * (1.0 + mod_prev[4:5]) + mod_prev[3:4]
    _store_rows(u2_ref, u2)
    u2_hi = u2.astype(BF16)
    u2_lo = (u2 - u2_hi.astype(F32)).astype(BF16)

    mod = mod_ref[0]
    sh_m, sc_m, g_m = mod[0:1], mod[1:2], mod[2:3]
    x = x_ref[0]
    u = (x * (1.0 + sc_m) + sh_m).astype(BF16)

    def proj(i):
        return _dot(u, w_in_ref[:, _OFF[i]:_OFF[i + 1]])

    gate_cols = ([(w_in_ref, _OFF[4] + jj * MXU_COLS) for jj in range(D_MODEL // MXU_COLS)]
                 + [(w_gates_ref, jj * MXU_COLS) for jj in range(2 * D_MODEL // MXU_COLS)])
    fillers = [functools.partial(lambda w_ref, c0: _dot(u, w_ref[:, c0:c0 + MXU_COLS]), w_ref, c0)
               for w_ref, c0 in gate_cols]
    filled = []

    def issue_fillers(n):
        for _ in range(n):
            if len(filled) < len(fillers):
                filled.append(fillers[len(filled)]())

    a = proj(0)
    q = proj(1) * (GLA_HEAD_K ** -0.5)
    k_all = proj(2)
    v_all = proj(3)
    alpha_low = _dot(u, w_rank_ref[...])

    part = _dot_nt(w_r_ref[...], u2_hi)
    logits = (part[:N_EXPERTS] + part[N_EXPERTS:] + _dot_nt(w_r_ref[0:N_EXPERTS, :], u2_lo)
              + b_r_ref[...])
    issue_fillers(2)

    a_ext[POOL_HALO:POOL_HALO + tile, :] = a
    t_glob = s_idx * tile + lax.broadcasted_iota(jnp.int32, (tile, 1), 0)
    mapped = []
    for g, w in enumerate(POOL_WINDOWS):
        lo, hi = g * POOL_GROUP_WIDTH, (g + 1) * POOL_GROUP_WIDTH
        win = a_ext[:, lo:hi]
        k = 1
        while k < w:
            win = win + pltpu.roll(win, k, 0)
            k *= 2
        inv_cnt = 1.0 / jnp.minimum(t_glob + 1, w).astype(F32)
        pooled = win[POOL_HALO:, :] * inv_cnt - a[:, lo:hi]
        mapped.append(_dot(pooled.astype(BF16), w_pool_ref[g]))
    a_ext[0:POOL_HALO, :] = a[tile - POOL_HALO:tile, :]
    ya = _dot((jnp.concatenate(mapped, axis=1) * pool_scale_ref[...]).astype(BF16), w_a_ref[...])

    erow = lax.broadcasted_iota(jnp.int32, (N_EXPERTS, tile), 0).astype(F32)
    work = logits
    sel = jnp.zeros((N_EXPERTS, tile), F32)
    vals, hits = [], []
    for _ in range(TOP_K):
        m = jnp.max(work, axis=0, keepdims=True)
        idx = jnp.min(jnp.where(work == m, erow, float(N_EXPERTS)), axis=0, keepdims=True)
        hit = erow == idx
        vals.append(m)
        hits.append((idx, hit))
        sel = jnp.where(hit, 1.0, sel)
        work = jnp.where(hit, -jnp.inf, work)
    exps = [jnp.exp(v - vals[0]) for v in vals]
    inv_den = 1.0 / (exps[0] + exps[1] + exps[2] + exps[3])
    issue_fillers(1)

    z = _dot(alpha_low.astype(BF16), w_al_ref[...]) + b_al_ref[...]
    log_a = (jnp.minimum(z, 0.0) - jnp.log1p(jnp.exp(-jnp.abs(z)))) * (1.0 / GLA_TAU)

    ri = lax.broadcasted_iota(jnp.int32, (CHUNK, CHUNK), 0)
    ci = lax.broadcasted_iota(jnp.int32, (CHUNK, CHUNK), 1)
    causal = ci <= ri
    cum_mat = causal.astype(BF16)
    la_hi = log_a.astype(BF16)
    la_split = jnp.concatenate([la_hi, (log_a - la_hi.astype(F32)).astype(BF16)], axis=1)

    n_chunks = tile // CHUNK
    pairs = [(c, h) for c in range(n_chunks) for h in range(GLA_HEADS)]
    rows_of = lambda c: slice(c * CHUNK, (c + 1) * CHUNK)
    ks_of = lambda h: slice(h * GLA_HEAD_K, (h + 1) * GLA_HEAD_K)
    vs_of = lambda h: slice(h * GLA_HEAD_V, (h + 1) * GLA_HEAD_V)

    b_cum, ref_pts, b_ref_pt = [], [], []
    for c in range(n_chunks):
        cum = _dot(cum_mat, la_split[rows_of(c)])
        b_cum.append(cum[:, :GLA_KEY_DIM] + cum[:, GLA_KEY_DIM:])
        ref_pts.append([jnp.zeros((1, GLA_KEY_DIM), F32)]
                       + [b_cum[c][i * SUB - 1:i * SUB, :] for i in range(1, N_SUB)])
        b_ref_pt.append(jnp.concatenate([jnp.broadcast_to(p, (SUB, GLA_KEY_DIM)) for p in ref_pts[c]], axis=0))

    rt = lax.broadcasted_iota(jnp.int32, (tile, tile), 0)
    ct = lax.broadcasted_iota(jnp.int32, (tile, tile), 1)
    base = _dot(sel.astype(BF16), (rt < ct).astype(BF16)) + cnt_ref[:, 0:1]

    for n_pair, (c, h) in enumerate(pairs):
        if n_pair % 3 == 0:
            issue_fillers(1)
        qh, kh, bh = q[rows_of(c), ks_of(h)], k_all[rows_of(c), ks_of(h)], b_cum[c][:, ks_of(h)]
        q_dec = (qh * jnp.exp(bh - b_ref_pt[c][:, ks_of(h)])).astype(BF16)
        k_dec = jnp.concatenate(
            [(kh * jnp.exp(jnp.minimum(ref_pts[c][i][:, ks_of(h)] - bh, EXP_CAP))).astype(BF16)
             for i in range(N_SUB)], axis=0)
        s_all = _dot_nt(q_dec, k_dec)
        sc_ref[n_pair] = jnp.concatenate(
            [s_all[i * SUB:(i + 1) * SUB, i * CHUNK:(i + 1) * CHUNK] for i in range(N_SUB)], axis=0)

    sub_decay = b_ref_pt[0] - b_cum[0]
    for c in range(1, n_chunks):
        sub_decay = jnp.maximum(sub_decay, b_ref_pt[c] - b_cum[c])

    @pl.when(jnp.max(sub_decay) > SAFE_SUB_DECAY)
    def _():
        o_ref[:, 0:GLA_KEY_DIM] = q
        o_ref[:, GLA_KEY_DIM:2 * GLA_KEY_DIM] = jnp.concatenate(b_cum, axis=0)
        for n_pair, (c, h) in enumerate(pairs):
            kh, bh = k_all[rows_of(c), ks_of(h)], b_cum[c][:, ks_of(h)]

            def score_rows(g, carry):
                first = pl.multiple_of(g * 8, 8)
                q_8 = o_ref[pl.ds(c * CHUNK + first, 8), ks_of(h)]
                b_8 = o_ref[pl.ds(c * CHUNK + first, 8),
                            GLA_KEY_DIM + h * GLA_HEAD_K:GLA_KEY_DIM + (h + 1) * GLA_HEAD_K]
                rows = []
                for r in range(8):
                    k_i = (kh * jnp.exp(jnp.minimum(b_8[r:r + 1] - bh, 0.0))).astype(BF16)
                    q_i = jnp.broadcast_to(q_8[r:r + 1], (8, GLA_HEAD_K)).astype(BF16)
                    rows.append(_dot_nt(q_i, k_i)[0:1])
                sc_ref[n_pair, pl.ds(first, 8), :] = jnp.concatenate(rows, axis=0)
                return carry

            lax.fori_loop(0, CHUNK // 8, score_rows, 0)

    o_intra, kv, decay_last = {}, {}, {}
    for n_pair, (c, h) in enumerate(pairs):
        kh, bh = k_all[rows_of(c), ks_of(h)], b_cum[c][:, ks_of(h)]
        vh = v_all[rows_of(c), vs_of(h)].astype(BF16)
        o_intra[c, h] = _dot(jnp.where(causal, sc_ref[n_pair], 0.0).astype(BF16), vh)
        b_last = bh[CHUNK - 1:CHUNK, :]
        kv[c, h] = _dot_tn(vh, (kh * jnp.exp(b_last - bh)).astype(BF16))
        decay_last[c, h] = jnp.exp(b_last)

    orow = lax.broadcasted_iota(jnp.int32, (ROUTE_ROWS, tile), 0)
    route = jnp.zeros((ROUTE_ROWS, tile), F32)
    for kk in range(TOP_K):
        idx, hit = hits[kk]
        rank = jnp.sum(jnp.where(hit, base, 0.0), axis=0, keepdims=True)
        route = jnp.where(orow == kk, idx, route)
        route = jnp.where(orow == TOP_K + kk, rank, route)
        route = jnp.where(orow == 2 * TOP_K + kk, exps[kk] * inv_den, route)
    route_ref[...] = route
    cnt_ref[...] = cnt_ref[...] + (j > 0).astype(F32) * jnp.sum(sel, axis=1, keepdims=True)
    counts_ref[...] = cnt_ref[...]

    o_inter = {}
    state_t = [s_ref[h] for h in range(GLA_HEADS)]
    for c, h in pairs:
        q_in = (q[rows_of(c), ks_of(h)] * jnp.exp(b_cum[c][:, ks_of(h)])).astype(BF16)
        o_inter[c, h] = _dot_nt(q_in, state_t[h].astype(BF16))
        state_t[h] = state_t[h] * decay_last[c, h] + kv[c, h]
    for h in range(GLA_HEADS):
        s_ref[h] = state_t[h]

    for c, h in pairs:
        o = o_intra[c, h] + o_inter[c, h]
        o = o * lax.rsqrt(jnp.mean(o * o, axis=-1, keepdims=True) + RMS_EPS) * gain_ref[...]
        o_ref[rows_of(c), vs_of(h)] = o

    issue_fillers(len(fillers))
    per_proj = D_MODEL // MXU_COLS
    r, gate_a, gate_b = (jnp.concatenate(filled[n * per_proj:(n + 1) * per_proj], axis=1) for n in range(3))
    yb = _dot((o_ref[...] * (r * _sigmoid(r))).astype(BF16), w_b_ref[...])

    merged = _sigmoid(gate_a) * ya + _sigmoid(gate_b) * yb
    y = _dot(merged.astype(BF16), w_out_ref[...])
    z_ref[...] = DEEPNORM_ALPHA * x + g_m * y


def _mixer(x, mod, w_in_p, w_rank_p, w_gates, w_pool, pool_scale, w_a, w_al_p, b_alpha, gain, w_b, w_out,
           ln_g, ln_b, w_r_p, b_r_p):
    bsz, seq, d = x.shape
    tile = SEQ_TILE
    n_s = seq // tile
    n_tiles = bsz * n_s

    def const(shape):
        nd = len(shape)
        return pl.BlockSpec(shape, lambda j: (0,) * nd, pipeline_mode=pl.Buffered(1))

    cur = lambda j: jnp.minimum(j, n_tiles - 1)
    prev = lambda j: jnp.maximum(j - 1, 0)
    return pl.pallas_call(
        functools.partial(_mixer_kernel, tiles_per_seq=n_s, n_tiles=n_tiles),
        grid=(n_tiles + 1,),
        in_specs=[
            pl.BlockSpec((1, tile, d), lambda j: (cur(j) // n_s, cur(j) % n_s, 0)),
            pl.BlockSpec((1, 6, d), lambda j: (cur(j) // n_s, 0, 0)),
            pl.BlockSpec((1, 6, d), lambda j: (prev(j) // n_s, 0, 0)),
            const(w_in_p.shape), const(w_rank_p.shape), const(w_gates.shape),
            const(w_pool.shape), const(pool_scale.shape), const(w_a.shape),
            const(w_al_p.shape), const(b_alpha.shape), const(gain.shape), const(w_b.shape),
            const(w_out.shape), const(ln_g.shape), const(ln_b.shape), const(w_r_p.shape),
            const(b_r_p.shape),
        ],
        out_specs=[pl.BlockSpec((1, tile, d), lambda j: (prev(j) // n_s, prev(j) % n_s, 0)),
                   pl.BlockSpec((tile * ROW_SUB, LANES), lambda j: (prev(j), 0)),
                   pl.BlockSpec((ROUTE_ROWS, tile), lambda j: (0, prev(j))),
                   pl.BlockSpec((N_EXPERTS, LANES), lambda j: (0, 0))],
        out_shape=[
            jax.ShapeDtypeStruct((bsz, seq, d), F32),
            jax.ShapeDtypeStruct((bsz * seq * ROW_SUB, LANES), F32),
            jax.ShapeDtypeStruct((ROUTE_ROWS, bsz * seq), F32),
            jax.ShapeDtypeStruct((N_EXPERTS, LANES), F32),
        ],
        scratch_shapes=[
            pltpu.VMEM((POOL_HALO + tile, d), F32),
            pltpu.VMEM((GLA_HEADS, GLA_HEAD_V, GLA_HEAD_K), F32),
            pltpu.VMEM((N_EXPERTS, LANES), F32),
            pltpu.VMEM((tile, d), F32),
            pltpu.VMEM((tile, d), F32),
            pltpu.VMEM((tile // CHUNK * GLA_HEADS, CHUNK, CHUNK), F32),
        ],
        compiler_params=pltpu.CompilerParams(
            dimension_semantics=("arbitrary",), vmem_limit_bytes=VMEM_LIMIT),
        name="mixer",
    )(x, mod, mod, w_in_p, w_rank_p, w_gates, w_pool, pool_scale, w_a, w_al_p, b_alpha, gain, w_b, w_out,
      ln_g, ln_b, w_r_p, b_r_p)


def _row_at(ref, row8):
    return ref.at[pl.ds(pl.multiple_of(row8, ROW_SUB), ROW_SUB)]


def _per_tile(dest8, tile):
    n_t = dest8.shape[1] // tile
    return dest8.reshape(TOP_K, n_t, tile).transpose(1, 0, 2).reshape(n_t, 1, TOP_K * tile)


def _dispatch_kernel(fill_ref, dest_ref, u2_ref, rows_hbm, zbuf, sem_fill, sem_rows):
    i = pl.program_id(0)
    tile = u2_ref.shape[0] // ROW_SUB
    blk8 = EXPERT_BLOCK * ROW_SUB

    @pl.when(i == 0)
    def _():
        zbuf[...] = jnp.zeros_like(zbuf)
        n_windows = rows_hbm.shape[0] // blk8

        def fill(window):
            return pltpu.make_async_copy(zbuf, rows_hbm.at[pl.ds(pl.multiple_of(window * blk8, blk8), blk8)],
                                         sem_fill.at[0])

        for phase in ("start", "wait"):
            for e in range(N_EXPERTS):
                @pl.when(fill_ref[e] >= 0)
                def _():
                    getattr(fill(fill_ref[e]), phase)()

            def unused(window, carry):
                getattr(fill(window), phase)()
                return carry

            lax.fori_loop(fill_ref[N_EXPERTS], n_windows, unused, 0)

    def body(r, carry):
        for kk in range(TOP_K):
            pltpu.make_async_copy(_row_at(u2_ref, r * ROW_SUB), _row_at(rows_hbm, dest_ref[0, 0, kk * tile + r]),
                                  sem_rows.at[0]).start(priority=kk % 2)
        return carry

    lax.fori_loop(0, tile, body, 0, unroll=DMA_UNROLL)
    for _ in range(TOP_K):
        pltpu.make_async_copy(u2_ref, rows_hbm.at[pl.ds(0, tile * ROW_SUB)], sem_rows.at[0]).wait()


def _dispatch(fill_start, dest8, u2_rows, n_rows):
    n_tok = u2_rows.shape[0] // ROW_SUB
    tile = DISPATCH_TILE
    n_t = n_tok // tile
    grid_spec = pltpu.PrefetchScalarGridSpec(
        num_scalar_prefetch=1,
        grid=(n_t,),
        in_specs=[
            pl.BlockSpec((1, 1, tile * TOP_K), lambda i, fs: (i, 0, 0), memory_space=pltpu.SMEM),
            pl.BlockSpec((tile * ROW_SUB, LANES), lambda i, fs: (i, 0)),
        ],
        out_specs=pl.BlockSpec(memory_space=pl.ANY),
        scratch_shapes=[
            pltpu.VMEM((EXPERT_BLOCK * ROW_SUB, LANES), F32),
            pltpu.SemaphoreType.DMA((1,)),
            pltpu.SemaphoreType.DMA((1,)),
        ],
    )
    return pl.pallas_call(
        _dispatch_kernel,
        grid_spec=grid_spec,
        out_shape=jax.ShapeDtypeStruct(((n_rows + EXPERT_BLOCK) * ROW_SUB, LANES), F32),
        compiler_params=pltpu.CompilerParams(dimension_semantics=("arbitrary",)),
        name="dispatch",
    )(fill_start, _per_tile(dest8, tile), u2_rows)


def _moe_kernel(be_ref, nxt_ref, par_ref, full_ref, nb_ref, x_ref, bg_ref, bu_ref, bd_ref, wgu_hbm, wd_hbm, y_ref,
                wgu_buf, wd_buf, wg_s, wu_s, wd_s, sem_gu, sem_d):
    i = pl.program_id(0)
    new_expert = (i == 0) | (be_ref[i] != be_ref[jnp.maximum(i - 1, 0)])

    def fetch(e, s):
        return (pltpu.make_async_copy(wgu_hbm.at[e], wgu_buf.at[s], sem_gu.at[s]),
                pltpu.make_async_copy(wd_hbm.at[e], wd_buf.at[s], sem_d.at[s]))

    @pl.when(new_expert & (i < nb_ref[0]))
    def _():
        slot = par_ref[i]

        @pl.when(i == 0)
        def _():
            for cp in fetch(be_ref[0], slot):
                cp.start()

        @pl.when(nxt_ref[i] >= 0)
        def _():
            for cp in fetch(nxt_ref[i], 1 - slot):
                cp.start()

        for cp in fetch(be_ref[i], slot):
            cp.wait()
        src = lax.broadcasted_iota(jnp.int32, (MXU_COLS, MXU_COLS), 0)
        col = lax.broadcasted_iota(jnp.int32, (MXU_COLS, MXU_COLS), 1)
        half = MXU_COLS // 2
        want = jnp.where(col < half, 2 * col, 2 * (col - half) + 1)
        unzip = (src == want).astype(BF16)
        for g in range(wgu_buf.shape[2] // MXU_COLS):
            blk = wgu_buf[slot, :, g * MXU_COLS:(g + 1) * MXU_COLS].astype(BF16)
            sep = _dot(blk, unzip)
            wg_s[:, g * half:(g + 1) * half] = sep[:, :half].astype(BF16)
            wu_s[:, g * half:(g + 1) * half] = sep[:, half:].astype(BF16)
        wd_s[...] = wd_buf[slot].astype(BF16)

    def expert_mlp(n_rows):
        xb = _load_rows(x_ref, n_rows).astype(BF16)
        gate = jnp.minimum(_dot(xb, wg_s[...]) + bg_ref[0], SWIGLU_LIMIT)
        up = jnp.clip(_dot(xb, wu_s[...]) + bu_ref[0], -SWIGLU_LIMIT, SWIGLU_LIMIT)
        glu = gate * _sigmoid(gate * SWIGLU_ALPHA)
        _store_rows(y_ref, _dot(((up + 1.0) * glu).astype(BF16), wd_s[...]) + bd_ref[0])

    active = i < nb_ref[0]
    both_halves = full_ref[i] == 1

    @pl.when(active & both_halves)
    def _():
        expert_mlp(EXPERT_BLOCK)

    @pl.when(active & jnp.logical_not(both_halves))
    def _():
        expert_mlp(EXPERT_HALF)
        y_ref[EXPERT_HALF * ROW_SUB:, :] = jnp.zeros((EXPERT_HALF * ROW_SUB, LANES), F32)

    @pl.when(jnp.logical_not(active))
    def _():
        y_ref[...] = jnp.zeros_like(y_ref)


def _moe(block_expert, next_expert, slot_parity, both_halves, n_active, x_rows, w_gate_up, bg, bu, w_down, bd):
    n_blocks = block_expert.shape[0]
    _, d, f2 = w_gate_up.shape
    f = f2 // 2
    blk8 = EXPERT_BLOCK * ROW_SUB
    rows_in = pl.BlockSpec((blk8, LANES),
                           lambda i, be, nx, pr, fl, nb: (jnp.maximum(jnp.minimum(i, nb[0] - 1), 0), 0))
    rows_out = pl.BlockSpec((blk8, LANES), lambda i, be, nx, pr, fl, nb: (i, 0))
    per_expert = lambda shape: pl.BlockSpec((1,) + shape, lambda i, be, nx, pr, fl, nb: (be[i], 0, 0))
    hbm = pl.BlockSpec(memory_space=pl.ANY)
    grid_spec = pltpu.PrefetchScalarGridSpec(
        num_scalar_prefetch=5,
        grid=(n_blocks,),
        in_specs=[rows_in, per_expert((1, f)), per_expert((1, f)), per_expert((1, d)), hbm, hbm],
        out_specs=rows_out,
        scratch_shapes=[pltpu.VMEM((2, d, f2), F32), pltpu.VMEM((2, f, d), F32),
                        pltpu.VMEM((d, f), BF16), pltpu.VMEM((d, f), BF16), pltpu.VMEM((f, d), BF16),
                        pltpu.SemaphoreType.DMA((2,)), pltpu.SemaphoreType.DMA((2,))],
    )
    return pl.pallas_call(
        _moe_kernel,
        grid_spec=grid_spec,
        out_shape=jax.ShapeDtypeStruct((n_blocks * blk8, LANES), F32),
        compiler_params=pltpu.CompilerParams(
            dimension_semantics=("arbitrary",), vmem_limit_bytes=VMEM_LIMIT),
        name="moe",
    )(block_expert, next_expert, slot_parity, both_halves, n_active, x_rows, bg, bu, bd, w_gate_up, w_down)


def _combine_kernel(dest_cur_ref, dest_nxt_ref, x1_ref, mod_ref, route_ref, ln_g_ref, ln_b_ref, y_hbm,
                    o_ref, ybuf, sem):
    i = pl.program_id(0)
    n_steps = pl.num_programs(0)
    tile = x1_ref.shape[0]
    slot = lax.rem(i, 2)

    def gather(dest_ref, s):
        def body(r, carry):
            for kk in range(TOP_K):
                pltpu.make_async_copy(_row_at(y_hbm, dest_ref[0, 0, kk * tile + r]),
                                      _row_at(ybuf.at[s, kk], r * ROW_SUB), sem.at[s]).start(priority=kk % 2)
            return carry
        lax.fori_loop(0, tile, body, 0, unroll=DMA_UNROLL)

    @pl.when(i == 0)
    def _():
        gather(dest_cur_ref, 0)

    @pl.when(i + 1 < n_steps)
    def _():
        gather(dest_nxt_ref, 1 - slot)

    for kk in range(TOP_K):
        pltpu.make_async_copy(y_hbm.at[pl.ds(0, tile * ROW_SUB)], ybuf.at[slot, kk], sem.at[slot]).wait()

    g_f = mod_ref[0][5:6]
    weight = route_ref[...]
    y = jnp.zeros(x1_ref.shape, F32)
    for kk in range(TOP_K):
        y = y + weight[:, kk:kk + 1] * _load_rows(ybuf, tile, (slot, kk))
    o_ref[...] = _layer_norm(DEEPNORM_ALPHA * x1_ref[...] + g_f * y, ln_g_ref[...], ln_b_ref[...])


def _combine(dest8, x1, mod, route, ln_g, ln_b, y_rows, seq):
    n_tok, d = x1.shape
    tile = COMBINE_TILE
    n_t = n_tok // tile
    per_seq = seq // tile
    dest3 = _per_tile(dest8, tile)
    smem_blk = lambda fn: pl.BlockSpec((1, 1, tile * TOP_K), fn, memory_space=pltpu.SMEM)
    row = lambda width: pl.BlockSpec((tile, width), lambda i: (i, 0))
    vec = pl.BlockSpec((1, d), lambda i: (0, 0))
    return pl.pallas_call(
        _combine_kernel,
        grid=(n_t,),
        in_specs=[smem_blk(lambda i: (i, 0, 0)), smem_blk(lambda i: (jnp.minimum(i + 1, n_t - 1), 0, 0)),
                  row(d), pl.BlockSpec((1, 6, d), lambda i: (i // per_seq, 0, 0)), row(TOP_K), vec, vec,
                  pl.BlockSpec(memory_space=pl.ANY)],
        out_specs=row(d),
        out_shape=jax.ShapeDtypeStruct((n_tok, d), F32),
        scratch_shapes=[pltpu.VMEM((2, TOP_K, tile * ROW_SUB, LANES), F32), pltpu.SemaphoreType.DMA((2,))],
        compiler_params=pltpu.CompilerParams(
            dimension_semantics=("arbitrary",), vmem_limit_bytes=VMEM_LIMIT),
        name="combine",
    )(dest3, dest3, x1, mod, route, ln_g, ln_b, y_rows)


def _split_in_proj(w_in):
    rank_lo = _OFF[5]
    rank_hi = rank_lo + GLA_GATE_RANK
    pad = jnp.zeros((w_in.shape[0], RANK_PAD - GLA_GATE_RANK), BF16)
    return (w_in[:, :rank_lo].astype(BF16),
            jnp.concatenate([w_in[:, rank_lo:rank_hi].astype(BF16), pad], axis=1),
            w_in[:, rank_hi:].astype(BF16))


def _layer(x, c, w_ada, b_ada, w_in, w_pool_group, pool_scale, w_branch_a, w_alpha_up, b_alpha,
           gla_norm_gain, w_branch_b, w_out, ln1_gain, ln1_bias, w_router, b_router,
           w_gate_up, b_gate_up, w_down, b_down, ln2_gain, ln2_bias):
    bsz, seq, d = x.shape
    n_tok = bsz * seq
    n_assign = n_tok * TOP_K
    row2 = lambda v: v.reshape(1, -1)

    mod = _ada(c, w_ada, b_ada).reshape(bsz, 6, d)

    w_in_p, w_rank_p, w_gates = _split_in_proj(w_in)
    w_al_p = jnp.concatenate(
        [w_alpha_up, jnp.zeros((RANK_PAD - GLA_GATE_RANK, GLA_KEY_DIM), w_alpha_up.dtype)], axis=0).astype(BF16)
    w_r_t = w_router.T
    w_r_hi = w_r_t.astype(BF16)
    w_r_split = jnp.concatenate([w_r_hi, (w_r_t - w_r_hi.astype(F32)).astype(BF16)], axis=0)
    x1, u2_rows, route_t, counts = _mixer(
        x, mod, w_in_p, w_rank_p, w_gates, w_pool_group.astype(BF16), row2(pool_scale),
        w_branch_a.astype(BF16), w_al_p,
        row2(b_alpha), row2(gla_norm_gain), w_branch_b.astype(BF16), w_out.astype(BF16), row2(ln1_gain),
        row2(ln1_bias), w_r_split, b_router.reshape(N_EXPERTS, 1))

    top_idx = route_t[0:TOP_K].astype(jnp.int32)
    rank = route_t[TOP_K:2 * TOP_K].astype(jnp.int32)
    route = route_t[2 * TOP_K:3 * TOP_K].T
    counts = counts[:, 0].astype(jnp.int32)
    padded = (counts + EXPERT_BLOCK - 1) // EXPERT_BLOCK * EXPERT_BLOCK
    pad_end = jnp.cumsum(padded)
    pad_start = pad_end - padded
    n_rows = (n_assign + N_EXPERTS * (EXPERT_BLOCK - 1) + EXPERT_BLOCK - 1) // EXPERT_BLOCK * EXPERT_BLOCK
    n_blocks = n_rows // EXPERT_BLOCK
    n_active = (pad_end[-1] // EXPERT_BLOCK).astype(jnp.int32)
    start_of = jnp.sum(jnp.where(top_idx[..., None] == jnp.arange(N_EXPERTS, dtype=jnp.int32), pad_start, 0),
                       axis=-1)
    dest8 = (start_of + rank) * ROW_SUB
    blk_row = jnp.arange(n_blocks, dtype=jnp.int32)[:, None] * EXPERT_BLOCK
    block_expert = jnp.minimum(jnp.sum(pad_end[None, :] <= blk_row, axis=1), N_EXPERTS - 1).astype(jnp.int32)

    last_window = jnp.where(padded > 0, pad_end // EXPERT_BLOCK - 1, -1)
    fill_windows = jnp.concatenate([last_window, n_active.reshape(1)]).astype(jnp.int32)
    x_rows = _dispatch(fill_windows, dest8, u2_rows, n_rows)
    f = w_down.shape[1]
    bg = b_gate_up[:, 0::2].reshape(N_EXPERTS, 1, f)
    bu = b_gate_up[:, 1::2].reshape(N_EXPERTS, 1, f)
    eid = jnp.arange(N_EXPERTS, dtype=jnp.int32)
    owns = counts > 0
    later = jnp.where((eid[None, :] > eid[:, None]) & owns[None, :], eid[None, :], N_EXPERTS)
    next_of = jnp.min(later, axis=1)
    next_of = jnp.where(next_of == N_EXPERTS, -1, next_of).astype(jnp.int32)
    parity_of = ((jnp.cumsum(owns.astype(jnp.int32)) - 1) % 2).astype(jnp.int32)
    is_block_expert = block_expert[:, None] == eid[None, :]
    per_block = lambda v: jnp.sum(jnp.where(is_block_expert, v[None, :], 0), axis=1).astype(jnp.int32)
    left = per_block(pad_start + counts) - blk_row[:, 0]
    both_halves = (left > EXPERT_HALF).astype(jnp.int32)
    y_rows = _moe(block_expert, per_block(next_of), per_block(parity_of), both_halves, n_active.reshape(1),
                  x_rows, w_gate_up, bg, bu, w_down, b_down.reshape(N_EXPERTS, 1, d))
    out = _combine(dest8, x1.reshape(n_tok, d), mod, route, row2(ln2_gain), row2(ln2_bias), y_rows, seq)
    return out.reshape(bsz, seq, d)


def kernel(x, c, w_ada, b_ada, w_in, w_pool_group, pool_scale, w_branch_a, w_alpha_up, b_alpha, gla_norm_gain,
           w_branch_b, w_out, ln1_gain, ln1_bias, w_router, b_router, w_gate_up, b_gate_up, w_down, b_down,
           ln2_gain, ln2_bias):
    for l in range(DEPTH):
        x = _layer(x, c, w_ada[l], b_ada[l], w_in[l], w_pool_group[l], pool_scale[l], w_branch_a[l],
                   w_alpha_up[l], b_alpha[l], gla_norm_gain[l], w_branch_b[l], w_out[l], ln1_gain[l],
                   ln1_bias[l], w_router[l], b_router[l], w_gate_up[l], b_gate_up[l], w_down[l], b_down[l],
                   ln2_gain[l], ln2_bias[l])
    return x
```

```python
import functools

import jax
import jax.numpy as jnp
from jax import lax
from jax.experimental import pallas as pl
from jax.experimental.pallas import tpu as pltpu

D_MODEL = 1024
CHUNK = 64
SUB = 16
N_SUB = CHUNK // SUB
POOL_WINDOWS = (2, 4, 8, 16)
POOL_GROUP_WIDTH = D_MODEL // len(POOL_WINDOWS)
POOL_HALO = 16
GLA_HEADS = 4
GLA_KEY_DIM = D_MODEL // 2
GLA_HEAD_K = GLA_KEY_DIM // GLA_HEADS
GLA_HEAD_V = D_MODEL // GLA_HEADS
GLA_GATE_RANK = 16
GLA_TAU = 16.0
N_EXPERTS = 32
TOP_K = 4
SWIGLU_ALPHA = 1.702
SWIGLU_LIMIT = 7.0
EXPERT_BLOCK = 512
EXPERT_HALF = EXPERT_BLOCK // 2
LN_EPS = 1e-5
RMS_EPS = 1e-6
DEPTH = 1
DEEPNORM_ALPHA = (2.0 * DEPTH) ** 0.25

LANES = 128
MXU_COLS = 256
RANK_PAD = LANES
_W = (D_MODEL, GLA_KEY_DIM, GLA_KEY_DIM, D_MODEL, D_MODEL, RANK_PAD, D_MODEL, D_MODEL)
_OFF = tuple(sum(_W[:i]) for i in range(len(_W) + 1))
EXP_CAP = 60.0
SAFE_SUB_DECAY = 40.0
ROUTE_ROWS = 16

SEQ_TILE = 512
DISPATCH_TILE = 1024
COMBINE_TILE = 512
DMA_UNROLL = 8
VMEM_LIMIT = 60 * 1024 * 1024

F32 = jnp.float32
BF16 = jnp.bfloat16
HI = lax.Precision.HIGHEST


def _dot(a, b):
    return jnp.dot(a, b, preferred_element_type=F32)


def _dot_nt(a, b):
    return lax.dot_general(a, b, (((1,), (1,)), ((), ())), preferred_element_type=F32)


def _dot_tn(a, b):
    return lax.dot_general(a, b, (((0,), (0,)), ((), ())), preferred_element_type=F32)


def _sigmoid(v):
    return 0.5 * jnp.tanh(0.5 * v) + 0.5


ROW_SUB = D_MODEL // LANES


def _load_rows(ref, n_rows, lead=()):
    return jnp.concatenate(
        [ref[lead + (pl.ds(j, n_rows, stride=ROW_SUB), slice(None))] for j in range(ROW_SUB)], axis=1)


def _store_rows(ref, val):
    for j in range(ROW_SUB):
        ref[pl.ds(j, val.shape[0], stride=ROW_SUB), :] = val[:, j * LANES:(j + 1) * LANES]


def _ada_kernel(c_ref, w_ref, b_ref, o_ref):
    c = c_ref[...]
    s = c * jax.nn.sigmoid(c)
    o_ref[...] = jnp.dot(s, w_ref[...], precision=HI, preferred_element_type=F32) + b_ref[...]


def _ada(c, w_ada, b_ada):
    bsz, d = c.shape
    n = w_ada.shape[1]
    tn = 1024
    return pl.pallas_call(
        _ada_kernel,
        grid=(n // tn,),
        in_specs=[
            pl.BlockSpec((bsz, d), lambda j: (0, 0)),
            pl.BlockSpec((d, tn), lambda j: (0, j)),
            pl.BlockSpec((1, tn), lambda j: (0, j)),
        ],
        out_specs=pl.BlockSpec((bsz, tn), lambda j: (0, j)),
        out_shape=jax.ShapeDtypeStruct((bsz, n), F32),
        name="ada",
    )(c, w_ada, b_ada.reshape(1, n))


def _layer_norm(z, gain, bias):
    mu = jnp.mean(z, axis=-1, keepdims=True)
    zc = z - mu
    var = jnp.mean(zc * zc, axis=-1, keepdims=True)
    return zc * lax.rsqrt(var + LN_EPS) * gain + bias


def _mixer_kernel(x_ref, mod_ref, mod_prev_ref, w_in_ref, w_rank_ref, w_gates_ref, w_pool_ref, pool_scale_ref,
                  w_a_ref, w_al_ref,
                  b_al_ref, gain_ref, w_b_ref, w_out_ref, ln_g_ref, ln_b_ref, w_r_ref, b_r_ref,
                  x1_ref, u2_ref, route_ref, counts_ref,
                  a_ext, s_ref, cnt_ref, o_ref, z_ref, sc_ref, *, tiles_per_seq, n_tiles):
    j = pl.program_id(0)
    s_idx = lax.rem(jnp.minimum(j, n_tiles - 1), tiles_per_seq)
    tile = x_ref.shape[1]

    @pl.when(s_idx == 0)
    def _():
        s_ref[...] = jnp.zeros_like(s_ref)
        a_ext[0:POOL_HALO, :] = jnp.zeros((POOL_HALO, D_MODEL), F32)

    @pl.when(j == 0)
    def _():
        cnt_ref[...] = jnp.zeros_like(cnt_ref)
        z_ref[...] = jnp.zeros_like(z_ref)


    mod_prev = mod_prev_ref[0]
    x1 = _layer_norm(z_ref[...], ln_g_ref[...], ln_b_ref[...])
    x1_ref[0] = x1
    u2 = x1 * (1.0 + mod_prev[4:5]) + mod_prev[3:4]
    _store_rows(u2_ref, u2)
    u2_hi = u2.astype(BF16)
    u2_lo = (u2 - u2_hi.astype(F32)).astype(BF16)

    mod = mod_ref[0]
    sh_m, sc_m, g_m = mod[0:1], mod[1:2], mod[2:3]
    x = x_ref[0]
    u = (x * (1.0 + sc_m) + sh_m).astype(BF16)

    def proj(i):
        return _dot(u, w_in_ref[:, _OFF[i]:_OFF[i + 1]])

    gate_cols = ([(w_in_ref, _OFF[4] + jj * MXU_COLS) for jj in range(D_MODEL // MXU_COLS)]
                 + [(w_gates_ref, jj * MXU_COLS) for jj in range(2 * D_MODEL // MXU_COLS)])
    fillers = [functools.partial(lambda w_ref, c0: _dot(u, w_ref[:, c0:c0 + MXU_COLS]), w_ref, c0)
               for w_ref, c0 in gate_cols]
    filled = []

    def issue_fillers(n):
        for _ in range(n):
            if len(filled) < len(fillers):
                filled.append(fillers[len(filled)]())

    a = proj(0)
    q = proj(1) * (GLA_HEAD_K ** -0.5)
    k_all = proj(2)
    v_all = proj(3)
    alpha_low = _dot(u, w_rank_ref[...])

    part = _dot_nt(w_r_ref[...], u2_hi)
    logits = (part[:N_EXPERTS] + part[N_EXPERTS:] + _dot_nt(w_r_ref[0:N_EXPERTS, :], u2_lo)
              + b_r_ref[...])
    issue_fillers(2)

    a_ext[POOL_HALO:POOL_HALO + tile, :] = a
    t_glob = s_idx * tile + lax.broadcasted_iota(jnp.int32, (tile, 1), 0)
    mapped = []
    for g, w in enumerate(POOL_WINDOWS):
        lo, hi = g * POOL_GROUP_WIDTH, (g + 1) * POOL_GROUP_WIDTH
        win = a_ext[:, lo:hi]
        k = 1
        while k < w:
            win = win + pltpu.roll(win, k, 0)
            k *= 2
        inv_cnt = 1.0 / jnp.minimum(t_glob + 1, w).astype(F32)
        pooled = win[POOL_HALO:, :] * inv_cnt - a[:, lo:hi]
        mapped.append(_dot(pooled.astype(BF16), w_pool_ref[g]))
    a_ext[0:POOL_HALO, :] = a[tile - POOL_HALO:tile, :]
    ya = _dot((jnp.concatenate(mapped, axis=1) * pool_scale_ref[...]).astype(BF16), w_a_ref[...])

    erow = lax.broadcasted_iota(jnp.int32, (N_EXPERTS, tile), 0).astype(F32)
    work = logits
    sel = jnp.zeros((N_EXPERTS, tile), F32)
    vals, hits = [], []
    for _ in range(TOP_K):
        m = jnp.max(work, axis=0, keepdims=True)
        idx = jnp.min(jnp.where(work == m, erow, float(N_EXPERTS)), axis=0, keepdims=True)
        hit = erow == idx
        vals.append(m)
        hits.append((idx, hit))
        sel = jnp.where(hit, 1.0, sel)
        work = jnp.where(hit, -jnp.inf, work)
    exps = [jnp.exp(v - vals[0]) for v in vals]
    inv_den = 1.0 / (exps[0] + exps[1] + exps[2] + exps[3])
    issue_fillers(1)

    z = _dot(alpha_low.astype(BF16), w_al_ref[...]) + b_al_ref[...]
    log_a = (jnp.minimum(z, 0.0) - jnp.log1p(jnp.exp(-jnp.abs(z)))) * (1.0 / GLA_TAU)

    ri = lax.broadcasted_iota(jnp.int32, (CHUNK, CHUNK), 0)
    ci = lax.broadcasted_iota(jnp.int32, (CHUNK, CHUNK), 1)
    causal = ci <= ri
    cum_mat = causal.astype(BF16)
    la_hi = log_a.astype(BF16)
    la_split = jnp.concatenate([la_hi, (log_a - la_hi.astype(F32)).astype(BF16)], axis=1)

    n_chunks = tile // CHUNK
    pairs = [(c, h) for c in range(n_chunks) for h in range(GLA_HEADS)]
    rows_of = lambda c: slice(c * CHUNK, (c + 1) * CHUNK)
    ks_of = lambda h: slice(h * GLA_HEAD_K, (h + 1) * GLA_HEAD_K)
    vs_of = lambda h: slice(h * GLA_HEAD_V, (h + 1) * GLA_HEAD_V)

    b_cum, ref_pts, b_ref_pt = [], [], []
    for c in range(n_chunks):
        cum = _dot(cum_mat, la_split[rows_of(c)])
        b_cum.append(cum[:, :GLA_KEY_DIM] + cum[:, GLA_KEY_DIM:])
        ref_pts.append([jnp.zeros((1, GLA_KEY_DIM), F32)]
                       + [b_cum[c][i * SUB - 1:i * SUB, :] for i in range(1, N_SUB)])
        b_ref_pt.append(jnp.concatenate([jnp.broadcast_to(p, (SUB, GLA_KEY_DIM)) for p in ref_pts[c]], axis=0))

    rt = lax.broadcasted_iota(jnp.int32, (tile, tile), 0)
    ct = lax.broadcasted_iota(jnp.int32, (tile, tile), 1)
    base = _dot(sel.astype(BF16), (rt < ct).astype(BF16)) + cnt_ref[:, 0:1]

    for n_pair, (c, h) in enumerate(pairs):
        if n_pair % 3 == 0:
            issue_fillers(1)
        qh, kh, bh = q[rows_of(c), ks_of(h)], k_all[rows_of(c), ks_of(h)], b_cum[c][:, ks_of(h)]
        q_dec = (qh * jnp.exp(bh - b_ref_pt[c][:, ks_of(h)])).astype(BF16)
        k_dec = jnp.concatenate(
            [(kh * jnp.exp(jnp.minimum(ref_pts[c][i][:, ks_of(h)] - bh, EXP_CAP))).astype(BF16)
             for i in range(N_SUB)], axis=0)
        s_all = _dot_nt(q_dec, k_dec)
        sc_ref[n_pair] = jnp.concatenate(
            [s_all[i * SUB:(i + 1) * SUB, i * CHUNK:(i + 1) * CHUNK] for i in range(N_SUB)], axis=0)

    sub_decay = b_ref_pt[0] - b_cum[0]
    for c in range(1, n_chunks):
        sub_decay = jnp.maximum(sub_decay, b_ref_pt[c] - b_cum[c])

    @pl.when(jnp.max(sub_decay) > SAFE_SUB_DECAY)
    def _():
        o_ref[:, 0:GLA_KEY_DIM] = q
        o_ref[:, GLA_KEY_DIM:2 * GLA_KEY_DIM] = jnp.concatenate(b_cum, axis=0)
        for n_pair, (c, h) in enumerate(pairs):
            kh, bh = k_all[rows_of(c), ks_of(h)], b_cum[c][:, ks_of(h)]

            def score_rows(g, carry):
                first = pl.multiple_of(g * 8, 8)
                q_8 = o_ref[pl.ds(c * CHUNK + first, 8), ks_of(h)]
                b_8 = o_ref[pl.ds(c * CHUNK + first, 8),
                            GLA_KEY_DIM + h * GLA_HEAD_K:GLA_KEY_DIM + (h + 1) * GLA_HEAD_K]
                rows = []
                for r in range(8):
                    k_i = (kh * jnp.exp(jnp.minimum(b_8[r:r + 1] - bh, 0.0))).astype(BF16)
                    q_i = jnp.broadcast_to(q_8[r:r + 1], (8, GLA_HEAD_K)).astype(BF16)
                    rows.append(_dot_nt(q_i, k_i)[0:1])
                sc_ref[n_pair, pl.ds(first, 8), :] = jnp.concatenate(rows, axis=0)
                return carry

            lax.fori_loop(0, CHUNK // 8, score_rows, 0)

    o_intra, kv, decay_last = {}, {}, {}
    for n_pair, (c, h) in enumerate(pairs):
        kh, bh = k_all[rows_of(c), ks_of(h)], b_cum[c][:, ks_of(h)]
        vh = v_all[rows_of(c), vs_of(h)].astype(BF16)
        o_intra[c, h] = _dot(jnp.where(causal, sc_ref[n_pair], 0.0).astype(BF16), vh)
        b_last = bh[CHUNK - 1:CHUNK, :]
        kv[c, h] = _dot_tn(vh, (kh * jnp.exp(b_last - bh)).astype(BF16))
        decay_last[c, h] = jnp.exp(b_last)

    orow = lax.broadcasted_iota(jnp.int32, (ROUTE_ROWS, tile), 0)
    route = jnp.zeros((ROUTE_ROWS, tile), F32)
    for kk in range(TOP_K):
        idx, hit = hits[kk]
        rank = jnp.sum(jnp.where(hit, base, 0.0), axis=0, keepdims=True)
        route = jnp.where(orow == kk, idx, route)
        route = jnp.where(orow == TOP_K + kk, rank, route)
        route = jnp.where(orow == 2 * TOP_K + kk, exps[kk] * inv_den, route)
    route_ref[...] = route
    cnt_ref[...] = cnt_ref[...] + (j > 0).astype(F32) * jnp.sum(sel, axis=1, keepdims=True)
    counts_ref[...] = cnt_ref[...]

    o_inter = {}
    state_t = [s_ref[h] for h in range(GLA_HEADS)]
    for c, h in pairs:
        q_in = (q[rows_of(c), ks_of(h)] * jnp.exp(b_cum[c][:, ks_of(h)])).astype(BF16)
        o_inter[c, h] = _dot_nt(q_in, state_t[h].astype(BF16))
        state_t[h] = state_t[h] * decay_last[c, h] + kv[c, h]
    for h in range(GLA_HEADS):
        s_ref[h] = state_t[h]

    for c, h in pairs:
        o = o_intra[c, h] + o_inter[c, h]
        o = o * lax.rsqrt(jnp.mean(o * o, axis=-1, keepdims=True) + RMS_EPS) * gain_ref[...]
        o_ref[rows_of(c), vs_of(h)] = o

    issue_fillers(len(fillers))
    per_proj = D_MODEL // MXU_COLS
    r, gate_a, gate_b = (jnp.concatenate(filled[n * per_proj:(n + 1) * per_proj], axis=1) for n in range(3))
    yb = _dot((o_ref[...] * (r * _sigmoid(r))).astype(BF16), w_b_ref[...])

    merged = _sigmoid(gate_a) * ya + _sigmoid(gate_b) * yb
    y = _dot(merged.astype(BF16), w_out_ref[...])
    z_ref[...] = DEEPNORM_ALPHA * x + g_m * y


def _mixer(x, mod, w_in_p, w_rank_p, w_gates, w_pool, pool_scale, w_a, w_al_p, b_alpha, gain, w_b, w_out,
           ln_g, ln_b, w_r_p, b_r_p):
    bsz, seq, d = x.shape
    tile = SEQ_TILE
    n_s = seq // tile
    n_tiles = bsz * n_s

    def const(shape):
        nd = len(shape)
        return pl.BlockSpec(shape, lambda j: (0,) * nd, pipeline_mode=pl.Buffered(1))

    cur = lambda j: jnp.minimum(j, n_tiles - 1)
    prev = lambda j: jnp.maximum(j - 1, 0)
    return pl.pallas_call(
        functools.partial(_mixer_kernel, tiles_per_seq=n_s, n_tiles=n_tiles),
        grid=(n_tiles + 1,),
        in_specs=[
            pl.BlockSpec((1, tile, d), lambda j: (cur(j) // n_s, cur(j) % n_s, 0)),
            pl.BlockSpec((1, 6, d), lambda j: (cur(j) // n_s, 0, 0)),
            pl.BlockSpec((1, 6, d), lambda j: (prev(j) // n_s, 0, 0)),
            const(w_in_p.shape), const(w_rank_p.shape), const(w_gates.shape),
            const(w_pool.shape), const(pool_scale.shape), const(w_a.shape),
            const(w_al_p.shape), const(b_alpha.shape), const(gain.shape), const(w_b.shape),
            const(w_out.shape), const(ln_g.shape), const(ln_b.shape), const(w_r_p.shape),
            const(b_r_p.shape),
        ],
        out_specs=[pl.BlockSpec((1, tile, d), lambda j: (prev(j) // n_s, prev(j) % n_s, 0)),
                   pl.BlockSpec((tile * ROW_SUB, LANES), lambda j: (prev(j), 0)),
                   pl.BlockSpec((ROUTE_ROWS, tile), lambda j: (0, prev(j))),
                   pl.BlockSpec((N_EXPERTS, LANES), lambda j: (0, 0))],
        out_shape=[
            jax.ShapeDtypeStruct((bsz, seq, d), F32),
            jax.ShapeDtypeStruct((bsz * seq * ROW_SUB, LANES), F32),
            jax.ShapeDtypeStruct((ROUTE_ROWS, bsz * seq), F32),
            jax.ShapeDtypeStruct((N_EXPERTS, LANES), F32),
        ],
        scratch_shapes=[
            pltpu.VMEM((POOL_HALO + tile, d), F32),
            pltpu.VMEM((GLA_HEADS, GLA_HEAD_V, GLA_HEAD_K), F32),
            pltpu.VMEM((N_EXPERTS, LANES), F32),
            pltpu.VMEM((tile, d), F32),
            pltpu.VMEM((tile, d), F32),
            pltpu.VMEM((tile // CHUNK * GLA_HEADS, CHUNK, CHUNK), F32),
        ],
        compiler_params=pltpu.CompilerParams(
            dimension_semantics=("arbitrary",), vmem_limit_bytes=VMEM_LIMIT),
        name="mixer",
    )(x, mod, mod, w_in_p, w_rank_p, w_gates, w_pool, pool_scale, w_a, w_al_p, b_alpha, gain, w_b, w_out,
      ln_g, ln_b, w_r_p, b_r_p)


def _row_at(ref, row8):
    return ref.at[pl.ds(pl.multiple_of(row8, ROW_SUB), ROW_SUB)]


def _per_tile(dest8, tile):
    n_t = dest8.shape[1] // tile
    return dest8.reshape(TOP_K, n_t, tile).transpose(1, 0, 2).reshape(n_t, 1, TOP_K * tile)


def _dispatch_kernel(fill_ref, dest_ref, u2_ref, rows_hbm, zbuf, sem_fill, sem_rows):
    i = pl.program_id(0)
    tile = u2_ref.shape[0] // ROW_SUB
    blk8 = EXPERT_BLOCK * ROW_SUB

    @pl.when(i == 0)
    def _():
        zbuf[...] = jnp.zeros_like(zbuf)
        n_windows = rows_hbm.shape[0] // blk8

        def fill(window):
            return pltpu.make_async_copy(zbuf, rows_hbm.at[pl.ds(pl.multiple_of(window * blk8, blk8), blk8)],
                                         sem_fill.at[0])

        for phase in ("start", "wait"):
            for e in range(N_EXPERTS):
                @pl.when(fill_ref[e] >= 0)
                def _():
                    getattr(fill(fill_ref[e]), phase)()

            def unused(window, carry):
                getattr(fill(window), phase)()
                return carry

            lax.fori_loop(fill_ref[N_EXPERTS], n_windows, unused, 0)

    def body(r, carry):
        for kk in range(TOP_K):
            pltpu.make_async_copy(_row_at(u2_ref, r * ROW_SUB), _row_at(rows_hbm, dest_ref[0, 0, kk * tile + r]),
                                  sem_rows.at[0]).start(priority=kk % 2)
        return carry

    lax.fori_loop(0, tile, body, 0, unroll=DMA_UNROLL)
    for _ in range(TOP_K):
        pltpu.make_async_copy(u2_ref, rows_hbm.at[pl.ds(0, tile * ROW_SUB)], sem_rows.at[0]).wait()


def _dispatch(fill_start, dest8, u2_rows, n_rows):
    n_tok = u2_rows.shape[0] // ROW_SUB
    tile = DISPATCH_TILE
    n_t = n_tok // tile
    grid_spec = pltpu.PrefetchScalarGridSpec(
        num_scalar_prefetch=1,
        grid=(n_t,),
        in_specs=[
            pl.BlockSpec((1, 1, tile * TOP_K), lambda i, fs: (i, 0, 0), memory_space=pltpu.SMEM),
            pl.BlockSpec((tile * ROW_SUB, LANES), lambda i, fs: (i, 0)),
        ],
        out_specs=pl.BlockSpec(memory_space=pl.ANY),
        scratch_shapes=[
            pltpu.VMEM((EXPERT_BLOCK * ROW_SUB, LANES), F32),
            pltpu.SemaphoreType.DMA((1,)),
            pltpu.SemaphoreType.DMA((1,)),
        ],
    )
    return pl.pallas_call(
        _dispatch_kernel,
        grid_spec=grid_spec,
        out_shape=jax.ShapeDtypeStruct(((n_rows + EXPERT_BLOCK) * ROW_SUB, LANES), F32),
        compiler_params=pltpu.CompilerParams(dimension_semantics=("arbitrary",)),
        name="dispatch",
    )(fill_start, _per_tile(dest8, tile), u2_rows)


def _moe_kernel(be_ref, nxt_ref, par_ref, full_ref, nb_ref, x_ref, bg_ref, bu_ref, bd_ref, wgu_hbm, wd_hbm, y_ref,
                wgu_buf, wd_buf, wg_s, wu_s, wd_s, sem_gu, sem_d):
    i = pl.program_id(0)
    new_expert = (i == 0) | (be_ref[i] != be_ref[jnp.maximum(i - 1, 0)])

    def fetch(e, s):
        return (pltpu.make_async_copy(wgu_hbm.at[e], wgu_buf.at[s], sem_gu.at[s]),
                pltpu.make_async_copy(wd_hbm.at[e], wd_buf.at[s], sem_d.at[s]))

    @pl.when(new_expert & (i < nb_ref[0]))
    def _():
        slot = par_ref[i]

        @pl.when(i == 0)
        def _():
            for cp in fetch(be_ref[0], slot):
                cp.start()

        @pl.when(nxt_ref[i] >= 0)
        def _():
            for cp in fetch(nxt_ref[i], 1 - slot):
                cp.start()

        for cp in fetch(be_ref[i], slot):
            cp.wait()
        src = lax.broadcasted_iota(jnp.int32, (MXU_COLS, MXU_COLS), 0)
        col = lax.broadcasted_iota(jnp.int32, (MXU_COLS, MXU_COLS), 1)
        half = MXU_COLS // 2
        want = jnp.where(col < half, 2 * col, 2 * (col - half) + 1)
        unzip = (src == want).astype(BF16)
        for g in range(wgu_buf.shape[2] // MXU_COLS):
            blk = wgu_buf[slot, :, g * MXU_COLS:(g + 1) * MXU_COLS].astype(BF16)
            sep = _dot(blk, unzip)
            wg_s[:, g * half:(g + 1) * half] = sep[:, :half].astype(BF16)
            wu_s[:, g * half:(g + 1) * half] = sep[:, half:].astype(BF16)
        wd_s[...] = wd_buf[slot].astype(BF16)

    def expert_mlp(n_rows):
        xb = _load_rows(x_ref, n_rows).astype(BF16)
        gate = jnp.minimum(_dot(xb, wg_s[...]) + bg_ref[0], SWIGLU_LIMIT)
        up = jnp.clip(_dot(xb, wu_s[...]) + bu_ref[0], -SWIGLU_LIMIT, SWIGLU_LIMIT)
        glu = gate * _sigmoid(gate * SWIGLU_ALPHA)
        _store_rows(y_ref, _dot(((up + 1.0) * glu).astype(BF16), wd_s[...]) + bd_ref[0])

    active = i < nb_ref[0]
    both_halves = full_ref[i] == 1

    @pl.when(active & both_halves)
    def _():
        expert_mlp(EXPERT_BLOCK)

    @pl.when(active & jnp.logical_not(both_halves))
    def _():
        expert_mlp(EXPERT_HALF)
        y_ref[EXPERT_HALF * ROW_SUB:, :] = jnp.zeros((EXPERT_HALF * ROW_SUB, LANES), F32)

    @pl.when(jnp.logical_not(active))
    def _():
        y_ref[...] = jnp.zeros_like(y_ref)


def _moe(block_expert, next_expert, slot_parity, both_halves, n_active, x_rows, w_gate_up, bg, bu, w_down, bd):
    n_blocks = block_expert.shape[0]
    _, d, f2 = w_gate_up.shape
    f = f2 // 2
    blk8 = EXPERT_BLOCK * ROW_SUB
    rows_in = pl.BlockSpec((blk8, LANES),
                           lambda i, be, nx, pr, fl, nb: (jnp.maximum(jnp.minimum(i, nb[0] - 1), 0), 0))
    rows_out = pl.BlockSpec((blk8, LANES), lambda i, be, nx, pr, fl, nb: (i, 0))
    per_expert = lambda shape: pl.BlockSpec((1,) + shape, lambda i, be, nx, pr, fl, nb: (be[i], 0, 0))
    hbm = pl.BlockSpec(memory_space=pl.ANY)
    grid_spec = pltpu.PrefetchScalarGridSpec(
        num_scalar_prefetch=5,
        grid=(n_blocks,),
        in_specs=[rows_in, per_expert((1, f)), per_expert((1, f)), per_expert((1, d)), hbm, hbm],
        out_specs=rows_out,
        scratch_shapes=[pltpu.VMEM((2, d, f2), F32), pltpu.VMEM((2, f, d), F32),
                        pltpu.VMEM((d, f), BF16), pltpu.VMEM((d, f), BF16), pltpu.VMEM((f, d), BF16),
                        pltpu.SemaphoreType.DMA((2,)), pltpu.SemaphoreType.DMA((2,))],
    )
    return pl.pallas_call(
        _moe_kernel,
        grid_spec=grid_spec,
        out_shape=jax.ShapeDtypeStruct((n_blocks * blk8, LANES), F32),
        compiler_params=pltpu.CompilerParams(
            dimension_semantics=("arbitrary",), vmem_limit_bytes=VMEM_LIMIT),
        name="moe",
    )(block_expert, next_expert, slot_parity, both_halves, n_active, x_rows, bg, bu, bd, w_gate_up, w_down)


def _combine_kernel(dest_cur_ref, dest_nxt_ref, x1_ref, mod_ref, route_ref, ln_g_ref, ln_b_ref, y_hbm,
                    o_ref, ybuf, sem):
    i = pl.program_id(0)
    n_steps = pl.num_programs(0)
    tile = x1_ref.shape[0]
    slot = lax.rem(i, 2)

    def gather(dest_ref, s):
        def body(r, carry):
            for kk in range(TOP_K):
                pltpu.make_async_copy(_row_at(y_hbm, dest_ref[0, 0, kk * tile + r]),
                                      _row_at(ybuf.at[s, kk], r * ROW_SUB), sem.at[s]).start(priority=kk % 2)
            return carry
        lax.fori_loop(0, tile, body, 0, unroll=DMA_UNROLL)

    @pl.when(i == 0)
    def _():
        gather(dest_cur_ref, 0)

    @pl.when(i + 1 < n_steps)
    def _():
        gather(dest_nxt_ref, 1 - slot)

    for kk in range(TOP_K):
        pltpu.make_async_copy(y_hbm.at[pl.ds(0, tile * ROW_SUB)], ybuf.at[slot, kk], sem.at[slot]).wait()

    g_f = mod_ref[0][5:6]
    weight = route_ref[...]
    y = jnp.zeros(x1_ref.shape, F32)
    for kk in range(TOP_K):
        y = y + weight[:, kk:kk + 1] * _load_rows(ybuf, tile, (slot, kk))
    o_ref[...] = _layer_norm(DEEPNORM_ALPHA * x1_ref[...] + g_f * y, ln_g_ref[...], ln_b_ref[...])


def _combine(dest8, x1, mod, route, ln_g, ln_b, y_rows, seq):
    n_tok, d = x1.shape
    tile = COMBINE_TILE
    n_t = n_tok // tile
    per_seq = seq // tile
    dest3 = _per_tile(dest8, tile)
    smem_blk = lambda fn: pl.BlockSpec((1, 1, tile * TOP_K), fn, memory_space=pltpu.SMEM)
    row = lambda width: pl.BlockSpec((tile, width), lambda i: (i, 0))
    vec = pl.BlockSpec((1, d), lambda i: (0, 0))
    return pl.pallas_call(
        _combine_kernel,
        grid=(n_t,),
        in_specs=[smem_blk(lambda i: (i, 0, 0)), smem_blk(lambda i: (jnp.minimum(i + 1, n_t - 1), 0, 0)),
                  row(d), pl.BlockSpec((1, 6, d), lambda i: (i // per_seq, 0, 0)), row(TOP_K), vec, vec,
                  pl.BlockSpec(memory_space=pl.ANY)],
        out_specs=row(d),
        out_shape=jax.ShapeDtypeStruct((n_tok, d), F32),
        scratch_shapes=[pltpu.VMEM((2, TOP_K, tile * ROW_SUB, LANES), F32), pltpu.SemaphoreType.DMA((2,))],
        compiler_params=pltpu.CompilerParams(
            dimension_semantics=("arbitrary",), vmem_limit_bytes=VMEM_LIMIT),
        name="combine",
    )(dest3, dest3, x1, mod, route, ln_g, ln_b, y_rows)


def _split_in_proj(w_in):
    rank_lo = _OFF[5]
    rank_hi = rank_lo + GLA_GATE_RANK
    pad = jnp.zeros((w_in.shape[0], RANK_PAD - GLA_GATE_RANK), BF16)
    return (w_in[:, :rank_lo].astype(BF16),
            jnp.concatenate([w_in[:, rank_lo:rank_hi].astype(BF16), pad], axis=1),
            w_in[:, rank_hi:].astype(BF16))


def _layer(x, c, w_ada, b_ada, w_in, w_pool_group, pool_scale, w_branch_a, w_alpha_up, b_alpha,
           gla_norm_gain, w_branch_b, w_out, ln1_gain, ln1_bias, w_router, b_router,
           w_gate_up, b_gate_up, w_down, b_down, ln2_gain, ln2_bias):
    bsz, seq, d = x.shape
    n_tok = bsz * seq
    n_assign = n_tok * TOP_K
    row2 = lambda v: v.reshape(1, -1)

    mod = _ada(c, w_ada, b_ada).reshape(bsz, 6, d)

    w_in_p, w_rank_p, w_gates = _split_in_proj(w_in)
    w_al_p = jnp.concatenate(
        [w_alpha_up, jnp.zeros((RANK_PAD - GLA_GATE_RANK, GLA_KEY_DIM), w_alpha_up.dtype)], axis=0).astype(BF16)
    w_r_t = w_router.T
    w_r_hi = w_r_t.astype(BF16)
    w_r_split = jnp.concatenate([w_r_hi, (w_r_t - w_r_hi.astype(F32)).astype(BF16)], axis=0)
    x1, u2_rows, route_t, counts = _mixer(
        x, mod, w_in_p, w_rank_p, w_gates, w_pool_group.astype(BF16), row2(pool_scale),
        w_branch_a.astype(BF16), w_al_p,
        row2(b_alpha), row2(gla_norm_gain), w_branch_b.astype(BF16), w_out.astype(BF16), row2(ln1_gain),
        row2(ln1_bias), w_r_split, b_router.reshape(N_EXPERTS, 1))

    top_idx = route_t[0:TOP_K].astype(jnp.int32)
    rank = route_t[TOP_K:2 * TOP_K].astype(jnp.int32)
    route = route_t[2 * TOP_K:3 * TOP_K].T
    counts = counts[:, 0].astype(jnp.int32)
    padded = (counts + EXPERT_BLOCK - 1) // EXPERT_BLOCK * EXPERT_BLOCK
    pad_end = jnp.cumsum(padded)
    pad_start = pad_end - padded
    n_rows = (n_assign + N_EXPERTS * (EXPERT_BLOCK - 1) + EXPERT_BLOCK - 1) // EXPERT_BLOCK * EXPERT_BLOCK
    n_blocks = n_rows // EXPERT_BLOCK
    n_active = (pad_end[-1] // EXPERT_BLOCK).astype(jnp.int32)
    start_of = jnp.sum(jnp.where(top_idx[..., None] == jnp.arange(N_EXPERTS, dtype=jnp.int32), pad_start, 0),
                       axis=-1)
    dest8 = (start_of + rank) * ROW_SUB
    blk_row = jnp.arange(n_blocks, dtype=jnp.int32)[:, None] * EXPERT_BLOCK
    block_expert = jnp.minimum(jnp.sum(pad_end[None, :] <= blk_row, axis=1), N_EXPERTS - 1).astype(jnp.int32)

    last_window = jnp.where(padded > 0, pad_end // EXPERT_BLOCK - 1, -1)
    fill_windows = jnp.concatenate([last_window, n_active.reshape(1)]).astype(jnp.int32)
    x_rows = _dispatch(fill_windows, dest8, u2_rows, n_rows)
    f = w_down.shape[1]
    bg = b_gate_up[:, 0::2].reshape(N_EXPERTS, 1, f)
    bu = b_gate_up[:, 1::2].reshape(N_EXPERTS, 1, f)
    eid = jnp.arange(N_EXPERTS, dtype=jnp.int32)
    owns = counts > 0
    later = jnp.where((eid[None, :] > eid[:, None]) & owns[None, :], eid[None, :], N_EXPERTS)
    next_of = jnp.min(later, axis=1)
    next_of = jnp.where(next_of == N_EXPERTS, -1, next_of).astype(jnp.int32)
    parity_of = ((jnp.cumsum(owns.astype(jnp.int32)) - 1) % 2).astype(jnp.int32)
    is_block_expert = block_expert[:, None] == eid[None, :]
    per_block = lambda v: jnp.sum(jnp.where(is_block_expert, v[None, :], 0), axis=1).astype(jnp.int32)
    left = per_block(pad_start + counts) - blk_row[:, 0]
    both_halves = (left > EXPERT_HALF).astype(jnp.int32)
    y_rows = _moe(block_expert, per_block(next_of), per_block(parity_of), both_halves, n_active.reshape(1),
                  x_rows, w_gate_up, bg, bu, w_down, b_down.reshape(N_EXPERTS, 1, d))
    out = _combine(dest8, x1.reshape(n_tok, d), mod, route, row2(ln2_gain), row2(ln2_bias), y_rows, seq)
    return out.reshape(bsz, seq, d)


def kernel(x, c, w_ada, b_ada, w_in, w_pool_group, pool_scale, w_branch_a, w_alpha_up, b_alpha, gla_norm_gain,
           w_branch_b, w_out, ln1_gain, ln1_bias, w_router, b_router, w_gate_up, b_gate_up, w_down, b_down,
           ln2_gain, ln2_bias):
    for l in range(DEPTH):
        x = _layer(x, c, w_ada[l], b_ada[l], w_in[l], w_pool_group[l], pool_scale[l], w_branch_a[l],
                   w_alpha_up[l], b_alpha[l], gla_norm_gain[l], w_branch_b[l], w_out[l], ln1_gain[l],
                   ln1_bias[l], w_router[l], b_router[l], w_gate_up[l], b_gate_up[l], w_down[l], b_down[l],
                   ln2_gain[l], ln2_bias[l])
    return x
```

```python
import functools

import jax
import jax.numpy as jnp
from jax import lax
from jax.experimental import pallas as pl
from jax.experimental.pallas import tpu as pltpu

D_MODEL = 1024
CHUNK = 64
SUB = 16
N_SUB = CHUNK // SUB
POOL_WINDOWS = (2, 4, 8, 16)
POOL_GROUP_WIDTH = D_MODEL // len(POOL_WINDOWS)
POOL_HALO = 16
GLA_HEADS = 4
GLA_KEY_DIM = D_MODEL // 2
GLA_HEAD_K = GLA_KEY_DIM // GLA_HEADS
GLA_HEAD_V = D_MODEL // GLA_HEADS
GLA_GATE_RANK = 16
GLA_TAU = 16.0
N_EXPERTS = 32
TOP_K = 4
SWIGLU_ALPHA = 1.702
SWIGLU_LIMIT = 7.0
EXPERT_BLOCK = 512
EXPERT_HALF = EXPERT_BLOCK // 2
LN_EPS = 1e-5
RMS_EPS = 1e-6
DEPTH = 1
DEEPNORM_ALPHA = (2.0 * DEPTH) ** 0.25

LANES = 128
MXU_COLS = 256
RANK_PAD = LANES
_W = (D_MODEL, GLA_KEY_DIM, GLA_KEY_DIM, D_MODEL, D_MODEL, RANK_PAD, D_MODEL, D_MODEL)
_OFF = tuple(sum(_W[:i]) for i in range(len(_W) + 1))
EXP_CAP = 60.0
SAFE_SUB_DECAY = 40.0
ROUTE_ROWS = 16

SEQ_TILE = 512
DISPATCH_TILE = 2048
COMBINE_TILE = 256
DMA_UNROLL = 8
VMEM_LIMIT = 60 * 1024 * 1024

F32 = jnp.float32
BF16 = jnp.bfloat16
HI = lax.Precision.HIGHEST


def _dot(a, b):
    return jnp.dot(a, b, preferred_element_type=F32)


def _dot_nt(a, b):
    return lax.dot_general(a, b, (((1,), (1,)), ((), ())), preferred_element_type=F32)


def _dot_tn(a, b):
    return lax.dot_general(a, b, (((0,), (0,)), ((), ())), preferred_element_type=F32)


def _sigmoid(v):
    return 0.5 * jnp.tanh(0.5 * v) + 0.5


ROW_SUB = D_MODEL // LANES


def _load_rows(ref, n_rows, lead=()):
    return jnp.concatenate(
        [ref[lead + (pl.ds(j, n_rows, stride=ROW_SUB), slice(None))] for j in range(ROW_SUB)], axis=1)


def _store_rows(ref, val):
    for j in range(ROW_SUB):
        ref[pl.ds(j, val.shape[0], stride=ROW_SUB), :] = val[:, j * LANES:(j + 1) * LANES]


def _ada_kernel(c_ref, w_ref, b_ref, o_ref):
    c = c_ref[...]
    s = c * jax.nn.sigmoid(c)
    o_ref[...] = jnp.dot(s, w_ref[...], precision=HI, preferred_element_type=F32) + b_ref[...]


def _ada(c, w_ada, b_ada):
    bsz, d = c.shape
    n = w_ada.shape[1]
    tn = 1024
    return pl.pallas_call(
        _ada_kernel,
        grid=(n // tn,),
        in_specs=[
            pl.BlockSpec((bsz, d), lambda j: (0, 0)),
            pl.BlockSpec((d, tn), lambda j: (0, j)),
            pl.BlockSpec((1, tn), lambda j: (0, j)),
        ],
        out_specs=pl.BlockSpec((bsz, tn), lambda j: (0, j)),
        out_shape=jax.ShapeDtypeStruct((bsz, n), F32),
        name="ada",
    )(c, w_ada, b_ada.reshape(1, n))


def _layer_norm(z, gain, bias):
    mu = jnp.mean(z, axis=-1, keepdims=True)
    zc = z - mu
    var = jnp.mean(zc * zc, axis=-1, keepdims=True)
    return zc * lax.rsqrt(var + LN_EPS) * gain + bias


def _mixer_kernel(x_ref, mod_ref, mod_prev_ref, w_in_ref, w_rank_ref, w_gates_ref, w_pool_ref, pool_scale_ref,
                  w_a_ref, w_al_ref,
                  b_al_ref, gain_ref, w_b_ref, w_out_ref, ln_g_ref, ln_b_ref, w_r_ref, b_r_ref,
                  x1_ref, u2_ref, route_ref, counts_ref,
                  a_ext, s_ref, cnt_ref, o_ref, z_ref, sc_ref, *, tiles_per_seq, n_tiles):
    j = pl.program_id(0)
    s_idx = lax.rem(jnp.minimum(j, n_tiles - 1), tiles_per_seq)
    tile = x_ref.shape[1]

    @pl.when(s_idx == 0)
    def _():
        s_ref[...] = jnp.zeros_like(s_ref)
        a_ext[0:POOL_HALO, :] = jnp.zeros((POOL_HALO, D_MODEL), F32)

    @pl.when(j == 0)
    def _():
        cnt_ref[...] = jnp.zeros_like(cnt_ref)
        z_ref[...] = jnp.zeros_like(z_ref)


    mod_prev = mod_prev_ref[0]
    x1 = _layer_norm(z_ref[...], ln_g_ref[...], ln_b_ref[...])
    x1_ref[0] = x1
    u2 = x1 * (1.0 + mod_prev[4:5]) + mod_prev[3:4]
    _store_rows(u2_ref, u2)
    u2_hi = u2.astype(BF16)
    u2_lo = (u2 - u2_hi.astype(F32)).astype(BF16)

    mod = mod_ref[0]
    sh_m, sc_m, g_m = mod[0:1], mod[1:2], mod[2:3]
    x = x_ref[0]
    u = (x * (1.0 + sc_m) + sh_m).astype(BF16)

    def proj(i):
        return _dot(u, w_in_ref[:, _OFF[i]:_OFF[i + 1]])

    gate_cols = ([(w_in_ref, _OFF[4] + jj * MXU_COLS) for jj in range(D_MODEL // MXU_COLS)]
                 + [(w_gates_ref, jj * MXU_COLS) for jj in range(2 * D_MODEL // MXU_COLS)])
    fillers = [functools.partial(lambda w_ref, c0: _dot(u, w_ref[:, c0:c0 + MXU_COLS]), w_ref, c0)
               for w_ref, c0 in gate_cols]
    filled = []

    def issue_fillers(n):
        for _ in range(n):
            if len(filled) < len(fillers):
                filled.append(fillers[len(filled)]())

    a = proj(0)
    q = proj(1) * (GLA_HEAD_K ** -0.5)
    k_all = proj(2)
    v_all = proj(3)
    alpha_low = _dot(u, w_rank_ref[...])

    part = _dot_nt(w_r_ref[...], u2_hi)
    logits = (part[:N_EXPERTS] + part[N_EXPERTS:] + _dot_nt(w_r_ref[0:N_EXPERTS, :], u2_lo)
              + b_r_ref[...])
    issue_fillers(2)

    a_ext[POOL_HALO:POOL_HALO + tile, :] = a
    t_glob = s_idx * tile + lax.broadcasted_iota(jnp.int32, (tile, 1), 0)
    mapped = []
    for g, w in enumerate(POOL_WINDOWS):
        lo, hi = g * POOL_GROUP_WIDTH, (g + 1) * POOL_GROUP_WIDTH
        win = a_ext[:, lo:hi]
        k = 1
        while k < w:
            win = win + pltpu.roll(win, k, 0)
            k *= 2
        inv_cnt = 1.0 / jnp.minimum(t_glob + 1, w).astype(F32)
        pooled = win[POOL_HALO:, :] * inv_cnt - a[:, lo:hi]
        mapped.append(_dot(pooled.astype(BF16), w_pool_ref[g]))
    a_ext[0:POOL_HALO, :] = a[tile - POOL_HALO:tile, :]
    ya = _dot((jnp.concatenate(mapped, axis=1) * pool_scale_ref[...]).astype(BF16), w_a_ref[...])

    erow = lax.broadcasted_iota(jnp.int32, (N_EXPERTS, tile), 0).astype(F32)
    work = logits
    sel = jnp.zeros((N_EXPERTS, tile), F32)
    vals, hits = [], []
    for _ in range(TOP_K):
        m = jnp.max(work, axis=0, keepdims=True)
        idx = jnp.min(jnp.where(work == m, erow, float(N_EXPERTS)), axis=0, keepdims=True)
        hit = erow == idx
        vals.append(m)
        hits.append((idx, hit))
        sel = jnp.where(hit, 1.0, sel)
        work = jnp.where(hit, -jnp.inf, work)
    exps = [jnp.exp(v - vals[0]) for v in vals]
    inv_den = 1.0 / (exps[0] + exps[1] + exps[2] + exps[3])
    issue_fillers(1)

    z = _dot(alpha_low.astype(BF16), w_al_ref[...]) + b_al_ref[...]
    log_a = (jnp.minimum(z, 0.0) - jnp.log1p(jnp.exp(-jnp.abs(z)))) * (1.0 / GLA_TAU)

    ri = lax.broadcasted_iota(jnp.int32, (CHUNK, CHUNK), 0)
    ci = lax.broadcasted_iota(jnp.int32, (CHUNK, CHUNK), 1)
    causal = ci <= ri
    cum_mat = causal.astype(BF16)
    la_hi = log_a.astype(BF16)
    la_split = jnp.concatenate([la_hi, (log_a - la_hi.astype(F32)).astype(BF16)], axis=1)

    n_chunks = tile // CHUNK
    pairs = [(c, h) for c in range(n_chunks) for h in range(GLA_HEADS)]
    rows_of = lambda c: slice(c * CHUNK, (c + 1) * CHUNK)
    ks_of = lambda h: slice(h * GLA_HEAD_K, (h + 1) * GLA_HEAD_K)
    vs_of = lambda h: slice(h * GLA_HEAD_V, (h + 1) * GLA_HEAD_V)

    b_cum, ref_pts, b_ref_pt = [], [], []
    for c in range(n_chunks):
        cum = _dot(cum_mat, la_split[rows_of(c)])
        b_cum.append(cum[:, :GLA_KEY_DIM] + cum[:, GLA_KEY_DIM:])
        ref_pts.append([jnp.zeros((1, GLA_KEY_DIM), F32)]
                       + [b_cum[c][i * SUB - 1:i * SUB, :] for i in range(1, N_SUB)])
        b_ref_pt.append(jnp.concatenate([jnp.broadcast_to(p, (SUB, GLA_KEY_DIM)) for p in ref_pts[c]], axis=0))

    rt = lax.broadcasted_iota(jnp.int32, (tile, tile), 0)
    ct = lax.broadcasted_iota(jnp.int32, (tile, tile), 1)
    base = _dot(sel.astype(BF16), (rt < ct).astype(BF16)) + cnt_ref[:, 0:1]

    for n_pair, (c, h) in enumerate(pairs):
        if n_pair % 3 == 0:
            issue_fillers(1)
        qh, kh, bh = q[rows_of(c), ks_of(h)], k_all[rows_of(c), ks_of(h)], b_cum[c][:, ks_of(h)]
        q_dec = (qh * jnp.exp(bh - b_ref_pt[c][:, ks_of(h)])).astype(BF16)
        k_dec = jnp.concatenate(
            [(kh * jnp.exp(jnp.minimum(ref_pts[c][i][:, ks_of(h)] - bh, EXP_CAP))).astype(BF16)
             for i in range(N_SUB)], axis=0)
        s_all = _dot_nt(q_dec, k_dec)
        sc_ref[n_pair] = jnp.concatenate(
            [s_all[i * SUB:(i + 1) * SUB, i * CHUNK:(i + 1) * CHUNK] for i in range(N_SUB)], axis=0)

    sub_decay = b_ref_pt[0] - b_cum[0]
    for c in range(1, n_chunks):
        sub_decay = jnp.maximum(sub_decay, b_ref_pt[c] - b_cum[c])

    @pl.when(jnp.max(sub_decay) > SAFE_SUB_DECAY)
    def _():
        o_ref[:, 0:GLA_KEY_DIM] = q
        o_ref[:, GLA_KEY_DIM:2 * GLA_KEY_DIM] = jnp.concatenate(b_cum, axis=0)
        for n_pair, (c, h) in enumerate(pairs):
            kh, bh = k_all[rows_of(c), ks_of(h)], b_cum[c][:, ks_of(h)]

            def score_rows(g, carry):
                first = pl.multiple_of(g * 8, 8)
                q_8 = o_ref[pl.ds(c * CHUNK + first, 8), ks_of(h)]
                b_8 = o_ref[pl.ds(c * CHUNK + first, 8),
                            GLA_KEY_DIM + h * GLA_HEAD_K:GLA_KEY_DIM + (h + 1) * GLA_HEAD_K]
                rows = []
                for r in range(8):
                    k_i = (kh * jnp.exp(jnp.minimum(b_8[r:r + 1] - bh, 0.0))).astype(BF16)
                    q_i = jnp.broadcast_to(q_8[r:r + 1], (8, GLA_HEAD_K)).astype(BF16)
                    rows.append(_dot_nt(q_i, k_i)[0:1])
                sc_ref[n_pair, pl.ds(first, 8), :] = jnp.concatenate(rows, axis=0)
                return carry

            lax.fori_loop(0, CHUNK // 8, score_rows, 0)

    o_intra, kv, decay_last = {}, {}, {}
    for n_pair, (c, h) in enumerate(pairs):
        kh, bh = k_all[rows_of(c), ks_of(h)], b_cum[c][:, ks_of(h)]
        vh = v_all[rows_of(c), vs_of(h)].astype(BF16)
        o_intra[c, h] = _dot(jnp.where(causal, sc_ref[n_pair], 0.0).astype(BF16), vh)
        b_last = bh[CHUNK - 1:CHUNK, :]
        kv[c, h] = _dot_tn(vh, (kh * jnp.exp(b_last - bh)).astype(BF16))
        decay_last[c, h] = jnp.exp(b_last)

    orow = lax.broadcasted_iota(jnp.int32, (ROUTE_ROWS, tile), 0)
    route = jnp.zeros((ROUTE_ROWS, tile), F32)
    for kk in range(TOP_K):
        idx, hit = hits[kk]
        rank = jnp.sum(jnp.where(hit, base, 0.0), axis=0, keepdims=True)
        route = jnp.where(orow == kk, idx, route)
        route = jnp.where(orow == TOP_K + kk, rank, route)
        route = jnp.where(orow == 2 * TOP_K + kk, exps[kk] * inv_den, route)
    route_ref[...] = route
    cnt_ref[...] = cnt_ref[...] + (j > 0).astype(F32) * jnp.sum(sel, axis=1, keepdims=True)
    counts_ref[...] = cnt_ref[...]

    o_inter = {}
    state_t = [s_ref[h] for h in range(GLA_HEADS)]
    for c, h in pairs:
        q_in = (q[rows_of(c), ks_of(h)] * jnp.exp(b_cum[c][:, ks_of(h)])).astype(BF16)
        o_inter[c, h] = _dot_nt(q_in, state_t[h].astype(BF16))
        state_t[h] = state_t[h] * decay_last[c, h] + kv[c, h]
    for h in range(GLA_HEADS):
        s_ref[h] = state_t[h]

    for c, h in pairs:
        o = o_intra[c, h] + o_inter[c, h]
        o = o * lax.rsqrt(jnp.mean(o * o, axis=-1, keepdims=True) + RMS_EPS) * gain_ref[...]
        o_ref[rows_of(c), vs_of(h)] = o

    issue_fillers(len(fillers))
    per_proj = D_MODEL // MXU_COLS
    r, gate_a, gate_b = (jnp.concatenate(filled[n * per_proj:(n + 1) * per_proj], axis=1) for n in range(3))
    yb = _dot((o_ref[...] * (r * _sigmoid(r))).astype(BF16), w_b_ref[...])

    merged = _sigmoid(gate_a) * ya + _sigmoid(gate_b) * yb
    y = _dot(merged.astype(BF16), w_out_ref[...])
    z_ref[...] = DEEPNORM_ALPHA * x + g_m * y


def _mixer(x, mod, w_in_p, w_rank_p, w_gates, w_pool, pool_scale, w_a, w_al_p, b_alpha, gain, w_b, w_out,
           ln_g, ln_b, w_r_p, b_r_p):
    bsz, seq, d = x.shape
    tile = SEQ_TILE
    n_s = seq // tile
    n_tiles = bsz * n_s

    def const(shape):
        nd = len(shape)
        return pl.BlockSpec(shape, lambda j: (0,) * nd, pipeline_mode=pl.Buffered(1))

    cur = lambda j: jnp.minimum(j, n_tiles - 1)
    prev = lambda j: jnp.maximum(j - 1, 0)
    return pl.pallas_call(
        functools.partial(_mixer_kernel, tiles_per_seq=n_s, n_tiles=n_tiles),
        grid=(n_tiles + 1,),
        in_specs=[
            pl.BlockSpec((1, tile, d), lambda j: (cur(j) // n_s, cur(j) % n_s, 0)),
            pl.BlockSpec((1, 6, d), lambda j: (cur(j) // n_s, 0, 0)),
            pl.BlockSpec((1, 6, d), lambda j: (prev(j) // n_s, 0, 0)),
            const(w_in_p.shape), const(w_rank_p.shape), const(w_gates.shape),
            const(w_pool.shape), const(pool_scale.shape), const(w_a.shape),
            const(w_al_p.shape), const(b_alpha.shape), const(gain.shape), const(w_b.shape),
            const(w_out.shape), const(ln_g.shape), const(ln_b.shape), const(w_r_p.shape),
            const(b_r_p.shape),
        ],
        out_specs=[pl.BlockSpec((1, tile, d), lambda j: (prev(j) // n_s, prev(j) % n_s, 0)),
                   pl.BlockSpec((tile * ROW_SUB, LANES), lambda j: (prev(j), 0)),
                   pl.BlockSpec((ROUTE_ROWS, tile), lambda j: (0, prev(j))),
                   pl.BlockSpec((N_EXPERTS, LANES), lambda j: (0, 0))],
        out_shape=[
            jax.ShapeDtypeStruct((bsz, seq, d), F32),
            jax.ShapeDtypeStruct((bsz * seq * ROW_SUB, LANES), F32),
            jax.ShapeDtypeStruct((ROUTE_ROWS, bsz * seq), F32),
            jax.ShapeDtypeStruct((N_EXPERTS, LANES), F32),
        ],
        scratch_shapes=[
            pltpu.VMEM((POOL_HALO + tile, d), F32),
            pltpu.VMEM((GLA_HEADS, GLA_HEAD_V, GLA_HEAD_K), F32),
            pltpu.VMEM((N_EXPERTS, LANES), F32),
            pltpu.VMEM((tile, d), F32),
            pltpu.VMEM((tile, d), F32),
            pltpu.VMEM((tile // CHUNK * GLA_HEADS, CHUNK, CHUNK), F32),
        ],
        compiler_params=pltpu.CompilerParams(
            dimension_semantics=("arbitrary",), vmem_limit_bytes=VMEM_LIMIT),
        name="mixer",
    )(x, mod, mod, w_in_p, w_rank_p, w_gates, w_pool, pool_scale, w_a, w_al_p, b_alpha, gain, w_b, w_out,
      ln_g, ln_b, w_r_p, b_r_p)


def _row_at(ref, row8):
    return ref.at[pl.ds(pl.multiple_of(row8, ROW_SUB), ROW_SUB)]


def _per_tile(dest8, tile):
    n_t = dest8.shape[1] // tile
    return dest8.reshape(TOP_K, n_t, tile).transpose(1, 0, 2).reshape(n_t, 1, TOP_K * tile)


def _dispatch_kernel(fill_ref, dest_ref, u2_ref, rows_hbm, zbuf, sem_fill, sem_rows):
    i = pl.program_id(0)
    tile = u2_ref.shape[0] // ROW_SUB
    blk8 = EXPERT_BLOCK * ROW_SUB

    @pl.when(i == 0)
    def _():
        zbuf[...] = jnp.zeros_like(zbuf)
        n_windows = rows_hbm.shape[0] // blk8

        def fill(window):
            return pltpu.make_async_copy(zbuf, rows_hbm.at[pl.ds(pl.multiple_of(window * blk8, blk8), blk8)],
                                         sem_fill.at[0])

        for phase in ("start", "wait"):
            for e in range(N_EXPERTS):
                @pl.when(fill_ref[e] >= 0)
                def _():
                    getattr(fill(fill_ref[e]), phase)()

            def unused(window, carry):
                getattr(fill(window), phase)()
                return carry

            lax.fori_loop(fill_ref[N_EXPERTS], n_windows, unused, 0)

    def body(r, carry):
        for kk in range(TOP_K):
            pltpu.make_async_copy(_row_at(u2_ref, r * ROW_SUB), _row_at(rows_hbm, dest_ref[0, 0, kk * tile + r]),
                                  sem_rows.at[0]).start(priority=kk % 2)
        return carry

    lax.fori_loop(0, tile, body, 0, unroll=DMA_UNROLL)
    for _ in range(TOP_K):
        pltpu.make_async_copy(u2_ref, rows_hbm.at[pl.ds(0, tile * ROW_SUB)], sem_rows.at[0]).wait()


def _dispatch(fill_start, dest8, u2_rows, n_rows):
    n_tok = u2_rows.shape[0] // ROW_SUB
    tile = DISPATCH_TILE
    n_t = n_tok // tile
    grid_spec = pltpu.PrefetchScalarGridSpec(
        num_scalar_prefetch=1,
        grid=(n_t,),
        in_specs=[
            pl.BlockSpec((1, 1, tile * TOP_K), lambda i, fs: (i, 0, 0), memory_space=pltpu.SMEM),
            pl.BlockSpec((tile * ROW_SUB, LANES), lambda i, fs: (i, 0)),
        ],
        out_specs=pl.BlockSpec(memory_space=pl.ANY),
        scratch_shapes=[
            pltpu.VMEM((EXPERT_BLOCK * ROW_SUB, LANES), F32),
            pltpu.SemaphoreType.DMA((1,)),
            pltpu.SemaphoreType.DMA((1,)),
        ],
    )
    return pl.pallas_call(
        _dispatch_kernel,
        grid_spec=grid_spec,
        out_shape=jax.ShapeDtypeStruct(((n_rows + EXPERT_BLOCK) * ROW_SUB, LANES), F32),
        compiler_params=pltpu.CompilerParams(dimension_semantics=("arbitrary",)),
        name="dispatch",
    )(fill_start, _per_tile(dest8, tile), u2_rows)


def _moe_kernel(be_ref, nxt_ref, par_ref, full_ref, nb_ref, x_ref, bg_ref, bu_ref, bd_ref, wgu_hbm, wd_hbm, y_ref,
                wgu_buf, wd_buf, wg_s, wu_s, wd_s, sem_gu, sem_d):
    i = pl.program_id(0)
    new_expert = (i == 0) | (be_ref[i] != be_ref[jnp.maximum(i - 1, 0)])

    def fetch(e, s):
        return (pltpu.make_async_copy(wgu_hbm.at[e], wgu_buf.at[s], sem_gu.at[s]),
                pltpu.make_async_copy(wd_hbm.at[e], wd_buf.at[s], sem_d.at[s]))

    @pl.when(new_expert & (i < nb_ref[0]))
    def _():
        slot = par_ref[i]

        @pl.when(i == 0)
        def _():
            for cp in fetch(be_ref[0], slot):
                cp.start()

        @pl.when(nxt_ref[i] >= 0)
        def _():
            for cp in fetch(nxt_ref[i], 1 - slot):
                cp.start()

        for cp in fetch(be_ref[i], slot):
            cp.wait()
        src = lax.broadcasted_iota(jnp.int32, (MXU_COLS, MXU_COLS), 0)
        col = lax.broadcasted_iota(jnp.int32, (MXU_COLS, MXU_COLS), 1)
        half = MXU_COLS // 2
        want = jnp.where(col < half, 2 * col, 2 * (col - half) + 1)
        unzip = (src == want).astype(BF16)
        for g in range(wgu_buf.shape[2] // MXU_COLS):
            blk = wgu_buf[slot, :, g * MXU_COLS:(g + 1) * MXU_COLS].astype(BF16)
            sep = _dot(blk, unzip)
            wg_s[:, g * half:(g + 1) * half] = sep[:, :half].astype(BF16)
            wu_s[:, g * half:(g + 1) * half] = sep[:, half:].astype(BF16)
        wd_s[...] = wd_buf[slot].astype(BF16)

    def expert_mlp(n_rows):
        xb = _load_rows(x_ref, n_rows).astype(BF16)
        gate = jnp.minimum(_dot(xb, wg_s[...]) + bg_ref[0], SWIGLU_LIMIT)
        up = jnp.clip(_dot(xb, wu_s[...]) + bu_ref[0], -SWIGLU_LIMIT, SWIGLU_LIMIT)
        glu = gate * _sigmoid(gate * SWIGLU_ALPHA)
        _store_rows(y_ref, _dot(((up + 1.0) * glu).astype(BF16), wd_s[...]) + bd_ref[0])

    active = i < nb_ref[0]
    both_halves = full_ref[i] == 1

    @pl.when(active & both_halves)
    def _():
        expert_mlp(EXPERT_BLOCK)

    @pl.when(active & jnp.logical_not(both_halves))
    def _():
        expert_mlp(EXPERT_HALF)
        y_ref[EXPERT_HALF * ROW_SUB:, :] = jnp.zeros((EXPERT_HALF * ROW_SUB, LANES), F32)

    @pl.when(jnp.logical_not(active))
    def _():
        y_ref[...] = jnp.zeros_like(y_ref)


def _moe(block_expert, next_expert, slot_parity, both_halves, n_active, x_rows, w_gate_up, bg, bu, w_down, bd):
    n_blocks = block_expert.shape[0]
    _, d, f2 = w_gate_up.shape
    f = f2 // 2
    blk8 = EXPERT_BLOCK * ROW_SUB
    rows_in = pl.BlockSpec((blk8, LANES),
                           lambda i, be, nx, pr, fl, nb: (jnp.maximum(jnp.minimum(i, nb[0] - 1), 0), 0))
    rows_out = pl.BlockSpec((blk8, LANES), lambda i, be, nx, pr, fl, nb: (i, 0))
    per_expert = lambda shape: pl.BlockSpec((1,) + shape, lambda i, be, nx, pr, fl, nb: (be[i], 0, 0))
    hbm = pl.BlockSpec(memory_space=pl.ANY)
    grid_spec = pltpu.PrefetchScalarGridSpec(
        num_scalar_prefetch=5,
        grid=(n_blocks,),
        in_specs=[rows_in, per_expert((1, f)), per_expert((1, f)), per_expert((1, d)), hbm, hbm],
        out_specs=rows_out,
        scratch_shapes=[pltpu.VMEM((2, d, f2), F32), pltpu.VMEM((2, f, d), F32),
                        pltpu.VMEM((d, f), BF16), pltpu.VMEM((d, f), BF16), pltpu.VMEM((f, d), BF16),
                        pltpu.SemaphoreType.DMA((2,)), pltpu.SemaphoreType.DMA((2,))],
    )
    return pl.pallas_call(
        _moe_kernel,
        grid_spec=grid_spec,
        out_shape=jax.ShapeDtypeStruct((n_blocks * blk8, LANES), F32),
        compiler_params=pltpu.CompilerParams(
            dimension_semantics=("arbitrary",), vmem_limit_bytes=VMEM_LIMIT),
        name="moe",
    )(block_expert, next_expert, slot_parity, both_halves, n_active, x_rows, bg, bu, bd, w_gate_up, w_down)


def _combine_kernel(dest_cur_ref, dest_nxt_ref, x1_ref, mod_ref, route_ref, ln_g_ref, ln_b_ref, y_hbm,
                    o_ref, ybuf, sem):
    i = pl.program_id(0)
    n_steps = pl.num_programs(0)
    tile = x1_ref.shape[0]
    slot = lax.rem(i, 2)

    def gather(dest_ref, s):
        def body(r, carry):
            for kk in range(TOP_K):
                pltpu.make_async_copy(_row_at(y_hbm, dest_ref[0, 0, kk * tile + r]),
                                      _row_at(ybuf.at[s, kk], r * ROW_SUB), sem.at[s]).start(priority=kk % 2)
            return carry
        lax.fori_loop(0, tile, body, 0, unroll=DMA_UNROLL)

    @pl.when(i == 0)
    def _():
        gather(dest_cur_ref, 0)

    @pl.when(i + 1 < n_steps)
    def _():
        gather(dest_nxt_ref, 1 - slot)

    for kk in range(TOP_K):
        pltpu.make_async_copy(y_hbm.at[pl.ds(0, tile * ROW_SUB)], ybuf.at[slot, kk], sem.at[slot]).wait()

    g_f = mod_ref[0][5:6]
    weight = route_ref[...]
    y = jnp.zeros(x1_ref.shape, F32)
    for kk in range(TOP_K):
        y = y + weight[:, kk:kk + 1] * _load_rows(ybuf, tile, (slot, kk))
    o_ref[...] = _layer_norm(DEEPNORM_ALPHA * x1_ref[...] + g_f * y, ln_g_ref[...], ln_b_ref[...])


def _combine(dest8, x1, mod, route, ln_g, ln_b, y_rows, seq):
    n_tok, d = x1.shape
    tile = COMBINE_TILE
    n_t = n_tok // tile
    per_seq = seq // tile
    dest3 = _per_tile(dest8, tile)
    smem_blk = lambda fn: pl.BlockSpec((1, 1, tile * TOP_K), fn, memory_space=pltpu.SMEM)
    row = lambda width: pl.BlockSpec((tile, width), lambda i: (i, 0))
    vec = pl.BlockSpec((1, d), lambda i: (0, 0))
    return pl.pallas_call(
        _combine_kernel,
        grid=(n_t,),
        in_specs=[smem_blk(lambda i: (i, 0, 0)), smem_blk(lambda i: (jnp.minimum(i + 1, n_t - 1), 0, 0)),
                  row(d), pl.BlockSpec((1, 6, d), lambda i: (i // per_seq, 0, 0)), row(TOP_K), vec, vec,
                  pl.BlockSpec(memory_space=pl.ANY)],
        out_specs=row(d),
        out_shape=jax.ShapeDtypeStruct((n_tok, d), F32),
        scratch_shapes=[pltpu.VMEM((2, TOP_K, tile * ROW_SUB, LANES), F32), pltpu.SemaphoreType.DMA((2,))],
        compiler_params=pltpu.CompilerParams(
            dimension_semantics=("arbitrary",), vmem_limit_bytes=VMEM_LIMIT),
        name="combine",
    )(dest3, dest3, x1, mod, route, ln_g, ln_b, y_rows)


def _split_in_proj(w_in):
    rank_lo = _OFF[5]
    rank_hi = rank_lo + GLA_GATE_RANK
    pad = jnp.zeros((w_in.shape[0], RANK_PAD - GLA_GATE_RANK), BF16)
    return (w_in[:, :rank_lo].astype(BF16),
            jnp.concatenate([w_in[:, rank_lo:rank_hi].astype(BF16), pad], axis=1),
            w_in[:, rank_hi:].astype(BF16))


def _layer(x, c, w_ada, b_ada, w_in, w_pool_group, pool_scale, w_branch_a, w_alpha_up, b_alpha,
           gla_norm_gain, w_branch_b, w_out, ln1_gain, ln1_bias, w_router, b_router,
           w_gate_up, b_gate_up, w_down, b_down, ln2_gain, ln2_bias):
    bsz, seq, d = x.shape
    n_tok = bsz * seq
    n_assign = n_tok * TOP_K
    row2 = lambda v: v.reshape(1, -1)

    mod = _ada(c, w_ada, b_ada).reshape(bsz, 6, d)

    w_in_p, w_rank_p, w_gates = _split_in_proj(w_in)
    w_al_p = jnp.concatenate(
        [w_alpha_up, jnp.zeros((RANK_PAD - GLA_GATE_RANK, GLA_KEY_DIM), w_alpha_up.dtype)], axis=0).astype(BF16)
    w_r_t = w_router.T
    w_r_hi = w_r_t.astype(BF16)
    w_r_split = jnp.concatenate([w_r_hi, (w_r_t - w_r_hi.astype(F32)).astype(BF16)], axis=0)
    x1, u2_rows, route_t, counts = _mixer(
        x, mod, w_in_p, w_rank_p, w_gates, w_pool_group.astype(BF16), row2(pool_scale),
        w_branch_a.astype(BF16), w_al_p,
        row2(b_alpha), row2(gla_norm_gain), w_branch_b.astype(BF16), w_out.astype(BF16), row2(ln1_gain),
        row2(ln1_bias), w_r_split, b_router.reshape(N_EXPERTS, 1))

    top_idx = route_t[0:TOP_K].astype(jnp.int32)
    rank = route_t[TOP_K:2 * TOP_K].astype(jnp.int32)
    route = route_t[2 * TOP_K:3 * TOP_K].T
    counts = counts[:, 0].astype(jnp.int32)
    padded = (counts + EXPERT_BLOCK - 1) // EXPERT_BLOCK * EXPERT_BLOCK
    pad_end = jnp.cumsum(padded)
    pad_start = pad_end - padded
    n_rows = (n_assign + N_EXPERTS * (EXPERT_BLOCK - 1) + EXPERT_BLOCK - 1) // EXPERT_BLOCK * EXPERT_BLOCK
    n_blocks = n_rows // EXPERT_BLOCK
    n_active = (pad_end[-1] // EXPERT_BLOCK).astype(jnp.int32)
    start_of = jnp.sum(jnp.where(top_idx[..., None] == jnp.arange(N_EXPERTS, dtype=jnp.int32), pad_start, 0),
                       axis=-1)
    dest8 = (start_of + rank) * ROW_SUB
    blk_row = jnp.arange(n_blocks, dtype=jnp.int32)[:, None] * EXPERT_BLOCK
    block_expert = jnp.minimum(jnp.sum(pad_end[None, :] <= blk_row, axis=1), N_EXPERTS - 1).astype(jnp.int32)

    last_window = jnp.where(padded > 0, pad_end // EXPERT_BLOCK - 1, -1)
    fill_windows = jnp.concatenate([last_window, n_active.reshape(1)]).astype(jnp.int32)
    x_rows = _dispatch(fill_windows, dest8, u2_rows, n_rows)
    f = w_down.shape[1]
    bg = b_gate_up[:, 0::2].reshape(N_EXPERTS, 1, f)
    bu = b_gate_up[:, 1::2].reshape(N_EXPERTS, 1, f)
    eid = jnp.arange(N_EXPERTS, dtype=jnp.int32)
    owns = counts > 0
    later = jnp.where((eid[None, :] > eid[:, None]) & owns[None, :], eid[None, :], N_EXPERTS)
    next_of = jnp.min(later, axis=1)
    next_of = jnp.where(next_of == N_EXPERTS, -1, next_of).astype(jnp.int32)
    parity_of = ((jnp.cumsum(owns.astype(jnp.int32)) - 1) % 2).astype(jnp.int32)
    is_block_expert = block_expert[:, None] == eid[None, :]
    per_block = lambda v: jnp.sum(jnp.where(is_block_expert, v[None, :], 0), axis=1).astype(jnp.int32)
    left = per_block(pad_start + counts) - blk_row[:, 0]
    both_halves = (left > EXPERT_HALF).astype(jnp.int32)
    y_rows = _moe(block_expert, per_block(next_of), per_block(parity_of), both_halves, n_active.reshape(1),
                  x_rows, w_gate_up, bg, bu, w_down, b_down.reshape(N_EXPERTS, 1, d))
    out = _combine(dest8, x1.reshape(n_tok, d), mod, route, row2(ln2_gain), row2(ln2_bias), y_rows, seq)
    return out.reshape(bsz, seq, d)


def kernel(x, c, w_ada, b_ada, w_in, w_pool_group, pool_scale, w_branch_a, w_alpha_up, b_alpha, gla_norm_gain,
           w_branch_b, w_out, ln1_gain, ln1_bias, w_router, b_router, w_gate_up, b_gate_up, w_down, b_down,
           ln2_gain, ln2_bias):
    for l in range(DEPTH):
        x = _layer(x, c, w_ada[l], b_ada[l], w_in[l], w_pool_group[l], pool_scale[l], w_branch_a[l],
                   w_alpha_up[l], b_alpha[l], gla_norm_gain[l], w_branch_b[l], w_out[l], ln1_gain[l],
                   ln1_bias[l], w_router[l], b_router[l], w_gate_up[l], b_gate_up[l], w_down[l], b_down[l],
                   ln2_gain[l], ln2_bias[l])
    return x
```

```python
import functools

import jax
import jax.numpy as jnp
from jax import lax
from jax.experimental import pallas as pl
from jax.experimental.pallas import tpu as pltpu

D_MODEL = 1024
CHUNK = 64
SUB = 16
N_SUB = CHUNK // SUB
POOL_WINDOWS = (2, 4, 8, 16)
POOL_GROUP_WIDTH = D_MODEL // len(POOL_WINDOWS)
POOL_HALO = 16
GLA_HEADS = 4
GLA_KEY_DIM = D_MODEL // 2
GLA_HEAD_K = GLA_KEY_DIM // GLA_HEADS
GLA_HEAD_V = D_MODEL // GLA_HEADS
GLA_GATE_RANK = 16
GLA_TAU = 16.0
N_EXPERTS = 32
TOP_K = 4
SWIGLU_ALPHA = 1.702
SWIGLU_LIMIT = 7.0
EXPERT_BLOCK = 512
EXPERT_HALF = EXPERT_BLOCK // 2
LN_EPS = 1e-5
RMS_EPS = 1e-6
DEPTH = 1
DEEPNORM_ALPHA = (2.0 * DEPTH) ** 0.25

LANES = 128
MXU_COLS = 256
RANK_PAD = LANES
_W = (D_MODEL, GLA_KEY_DIM, GLA_KEY_DIM, D_MODEL, D_MODEL, RANK_PAD, D_MODEL, D_MODEL)
_OFF = tuple(sum(_W[:i]) for i in range(len(_W) + 1))
EXP_CAP = 60.0
SAFE_SUB_DECAY = 40.0
ROUTE_ROWS = 16

SEQ_TILE = 512
DISPATCH_TILE = 2048
COMBINE_TILE = 256
DMA_UNROLL = 8
VMEM_LIMIT = 60 * 1024 * 1024

F32 = jnp.float32
BF16 = jnp.bfloat16
HI = lax.Precision.HIGHEST


def _dot(a, b):
    return jnp.dot(a, b, preferred_element_type=F32)


def _dot_nt(a, b):
    return lax.dot_general(a, b, (((1,), (1,)), ((), ())), preferred_element_type=F32)


def _dot_tn(a, b):
    return lax.dot_general(a, b, (((0,), (0,)), ((), ())), preferred_element_type=F32)


def _sigmoid(v):
    return 0.5 * jnp.tanh(0.5 * v) + 0.5


ROW_SUB = D_MODEL // LANES


def _load_rows(ref, n_rows, lead=()):
    return jnp.concatenate(
        [ref[lead + (pl.ds(j, n_rows, stride=ROW_SUB), slice(None))] for j in range(ROW_SUB)], axis=1)


def _store_rows(ref, val):
    for j in range(ROW_SUB):
        ref[pl.ds(j, val.shape[0], stride=ROW_SUB), :] = val[:, j * LANES:(j + 1) * LANES]


def _ada_kernel(c_ref, w_ref, b_ref, o_ref):
    c = c_ref[...]
    s = c * jax.nn.sigmoid(c)
    o_ref[...] = jnp.dot(s, w_ref[...], precision=HI, preferred_element_type=F32) + b_ref[...]


def _ada(c, w_ada, b_ada):
    bsz, d = c.shape
    n = w_ada.shape[1]
    tn = 1024
    return pl.pallas_call(
        _ada_kernel,
        grid=(n // tn,),
        in_specs=[
            pl.BlockSpec((bsz, d), lambda j: (0, 0)),
            pl.BlockSpec((d, tn), lambda j: (0, j)),
            pl.BlockSpec((1, tn), lambda j: (0, j)),
        ],
        out_specs=pl.BlockSpec((bsz, tn), lambda j: (0, j)),
        out_shape=jax.ShapeDtypeStruct((bsz, n), F32),
        name="ada",
    )(c, w_ada, b_ada.reshape(1, n))


def _layer_norm(z, gain, bias):
    mu = jnp.mean(z, axis=-1, keepdims=True)
    zc = z - mu
    var = jnp.mean(zc * zc, axis=-1, keepdims=True)
    return zc * lax.rsqrt(var + LN_EPS) * gain + bias


def _mixer_kernel(x_ref, mod_ref, mod_prev_ref, w_in_ref, w_rank_ref, w_gates_ref, w_pool_ref, pool_scale_ref,
                  w_a_ref, w_al_ref,
                  b_al_ref, gain_ref, w_b_ref, w_out_ref, ln_g_ref, ln_b_ref, w_r_ref, b_r_ref,
                  x1_ref, u2_ref, route_ref, counts_ref,
                  a_ext, s_ref, cnt_ref, o_ref, z_ref, sc_ref, *, tiles_per_seq, n_tiles):
    j = pl.program_id(0)
    s_idx = lax.rem(jnp.minimum(j, n_tiles - 1), tiles_per_seq)
    tile = x_ref.shape[1]

    @pl.when(s_idx == 0)
    def _():
        s_ref[...] = jnp.zeros_like(s_ref)
        a_ext[0:POOL_HALO, :] = jnp.zeros((POOL_HALO, D_MODEL), F32)

    @pl.when(j == 0)
    def _():
        cnt_ref[...] = jnp.zeros_like(cnt_ref)
        z_ref[...] = jnp.zeros_like(z_ref)


    mod_prev = mod_prev_ref[0]
    x1 = _layer_norm(z_ref[...], ln_g_ref[...], ln_b_ref[...])
    x1_ref[0] = x1
    u2 = x1 * (1.0 + mod_prev[4:5]) + mod_prev[3:4]
    _store_rows(u2_ref, u2)
    u2_hi = u2.astype(BF16)
    u2_lo = (u2 - u2_hi.astype(F32)).astype(BF16)

    mod = mod_ref[0]
    sh_m, sc_m, g_m = mod[0:1], mod[1:2], mod[2:3]
    x = x_ref[0]
    u = (x * (1.0 + sc_m) + sh_m).astype(BF16)

    def proj(i):
        return _dot(u, w_in_ref[:, _OFF[i]:_OFF[i + 1]])

    gate_cols = ([(w_in_ref, _OFF[4] + jj * MXU_COLS) for jj in range(D_MODEL // MXU_COLS)]
                 + [(w_gates_ref, jj * MXU_COLS) for jj in range(2 * D_MODEL // MXU_COLS)])
    fillers = [functools.partial(lambda w_ref, c0: _dot(u, w_ref[:, c0:c0 + MXU_COLS]), w_ref, c0)
               for w_ref, c0 in gate_cols]
    filled = []

    def issue_fillers(n):
        for _ in range(n):
            if len(filled) < len(fillers):
                filled.append(fillers[len(filled)]())

    a = proj(0)
    q = proj(1) * (GLA_HEAD_K ** -0.5)
    k_all = proj(2)
    v_all = proj(3)
    alpha_low = _dot(u, w_rank_ref[...])

    part = _dot_nt(w_r_ref[...], u2_hi)
    logits = (part[:N_EXPERTS] + part[N_EXPERTS:] + _dot_nt(w_r_ref[0:N_EXPERTS, :], u2_lo)
              + b_r_ref[...])
    issue_fillers(2)

    a_ext[POOL_HALO:POOL_HALO + tile, :] = a
    t_glob = s_idx * tile + lax.broadcasted_iota(jnp.int32, (tile, 1), 0)
    mapped = []
    for g, w in enumerate(POOL_WINDOWS):
        lo, hi = g * POOL_GROUP_WIDTH, (g + 1) * POOL_GROUP_WIDTH
        win = a_ext[:, lo:hi]
        k = 1
        while k < w:
            win = win + pltpu.roll(win, k, 0)
            k *= 2
        inv_cnt = 1.0 / jnp.minimum(t_glob + 1, w).astype(F32)
        pooled = win[POOL_HALO:, :] * inv_cnt - a[:, lo:hi]
        mapped.append(_dot(pooled.astype(BF16), w_pool_ref[g]))
    a_ext[0:POOL_HALO, :] = a[tile - POOL_HALO:tile, :]
    ya = _dot((jnp.concatenate(mapped, axis=1) * pool_scale_ref[...]).astype(BF16), w_a_ref[...])

    erow = lax.broadcasted_iota(jnp.int32, (N_EXPERTS, tile), 0).astype(F32)
    work = logits
    sel = jnp.zeros((N_EXPERTS, tile), F32)
    vals, hits = [], []
    for _ in range(TOP_K):
        m = jnp.max(work, axis=0, keepdims=True)
        idx = jnp.min(jnp.where(work == m, erow, float(N_EXPERTS)), axis=0, keepdims=True)
        hit = erow == idx
        vals.append(m)
        hits.append((idx, hit))
        sel = jnp.where(hit, 1.0, sel)
        work = jnp.where(hit, -jnp.inf, work)
    exps = [jnp.exp(v - vals[0]) for v in vals]
    inv_den = 1.0 / (exps[0] + exps[1] + exps[2] + exps[3])
    issue_fillers(1)

    z = _dot(alpha_low.astype(BF16), w_al_ref[...]) + b_al_ref[...]
    log_a = (jnp.minimum(z, 0.0) - jnp.log1p(jnp.exp(-jnp.abs(z)))) * (1.0 / GLA_TAU)

    ri = lax.broadcasted_iota(jnp.int32, (CHUNK, CHUNK), 0)
    ci = lax.broadcasted_iota(jnp.int32, (CHUNK, CHUNK), 1)
    causal = ci <= ri
    cum_mat = causal.astype(BF16)
    la_hi = log_a.astype(BF16)
    la_split = jnp.concatenate([la_hi, (log_a - la_hi.astype(F32)).astype(BF16)], axis=1)

    n_chunks = tile // CHUNK
    pairs = [(c, h) for c in range(n_chunks) for h in range(GLA_HEADS)]
    rows_of = lambda c: slice(c * CHUNK, (c + 1) * CHUNK)
    ks_of = lambda h: slice(h * GLA_HEAD_K, (h + 1) * GLA_HEAD_K)
    vs_of = lambda h: slice(h * GLA_HEAD_V, (h + 1) * GLA_HEAD_V)

    b_cum, ref_pts, b_ref_pt = [], [], []
    for c in range(n_chunks):
        cum = _dot(cum_mat, la_split[rows_of(c)])
        b_cum.append(cum[:, :GLA_KEY_DIM] + cum[:, GLA_KEY_DIM:])
        ref_pts.append([jnp.zeros((1, GLA_KEY_DIM), F32)]
                       + [b_cum[c][i * SUB - 1:i * SUB, :] for i in range(1, N_SUB)])
        b_ref_pt.append(jnp.concatenate([jnp.broadcast_to(p, (SUB, GLA_KEY_DIM)) for p in ref_pts[c]], axis=0))

    rt = lax.broadcasted_iota(jnp.int32, (tile, tile), 0)
    ct = lax.broadcasted_iota(jnp.int32, (tile, tile), 1)
    base = _dot(sel.astype(BF16), (rt < ct).astype(BF16)) + cnt_ref[:, 0:1]

    for n_pair, (c, h) in enumerate(pairs):
        if n_pair % 3 == 0:
            issue_fillers(1)
        qh, kh, bh = q[rows_of(c), ks_of(h)], k_all[rows_of(c), ks_of(h)], b_cum[c][:, ks_of(h)]
        q_dec = (qh * jnp.exp(bh - b_ref_pt[c][:, ks_of(h)])).astype(BF16)
        k_dec = jnp.concatenate(
            [(kh * jnp.exp(jnp.minimum(ref_pts[c][i][:, ks_of(h)] - bh, EXP_CAP))).astype(BF16)
             for i in range(N_SUB)], axis=0)
        s_all = _dot_nt(q_dec, k_dec)
        sc_ref[n_pair] = jnp.concatenate(
            [s_all[i * SUB:(i + 1) * SUB, i * CHUNK:(i + 1) * CHUNK] for i in range(N_SUB)], axis=0)

    sub_decay = b_ref_pt[0] - b_cum[0]
    for c in range(1, n_chunks):
        sub_decay = jnp.maximum(sub_decay, b_ref_pt[c] - b_cum[c])

    @pl.when(jnp.max(sub_decay) > SAFE_SUB_DECAY)
    def _():
        o_ref[:, 0:GLA_KEY_DIM] = q
        o_ref[:, GLA_KEY_DIM:2 * GLA_KEY_DIM] = jnp.concatenate(b_cum, axis=0)
        for n_pair, (c, h) in enumerate(pairs):
            kh, bh = k_all[rows_of(c), ks_of(h)], b_cum[c][:, ks_of(h)]

            def score_rows(g, carry):
                first = pl.multiple_of(g * 8, 8)
                q_8 = o_ref[pl.ds(c * CHUNK + first, 8), ks_of(h)]
                b_8 = o_ref[pl.ds(c * CHUNK + first, 8),
                            GLA_KEY_DIM + h * GLA_HEAD_K:GLA_KEY_DIM + (h + 1) * GLA_HEAD_K]
                rows = []
                for r in range(8):
                    k_i = (kh * jnp.exp(jnp.minimum(b_8[r:r + 1] - bh, 0.0))).astype(BF16)
                    q_i = jnp.broadcast_to(q_8[r:r + 1], (8, GLA_HEAD_K)).astype(BF16)
                    rows.append(_dot_nt(q_i, k_i)[0:1])
                sc_ref[n_pair, pl.ds(first, 8), :] = jnp.concatenate(rows, axis=0)
                return carry

            lax.fori_loop(0, CHUNK // 8, score_rows, 0)

    o_intra, kv, decay_last = {}, {}, {}
    for n_pair, (c, h) in enumerate(pairs):
        kh, bh = k_all[rows_of(c), ks_of(h)], b_cum[c][:, ks_of(h)]
        vh = v_all[rows_of(c), vs_of(h)].astype(BF16)
        o_intra[c, h] = _dot(jnp.where(causal, sc_ref[n_pair], 0.0).astype(BF16), vh)
        b_last = bh[CHUNK - 1:CHUNK, :]
        kv[c, h] = _dot_tn(vh, (kh * jnp.exp(b_last - bh)).astype(BF16))
        decay_last[c, h] = jnp.exp(b_last)

    orow = lax.broadcasted_iota(jnp.int32, (ROUTE_ROWS, tile), 0)
    route = jnp.zeros((ROUTE_ROWS, tile), F32)
    for kk in range(TOP_K):
        idx, hit = hits[kk]
        rank = jnp.sum(jnp.where(hit, base, 0.0), axis=0, keepdims=True)
        route = jnp.where(orow == kk, idx, route)
        route = jnp.where(orow == TOP_K + kk, rank, route)
        route = jnp.where(orow == 2 * TOP_K + kk, exps[kk] * inv_den, route)
    route_ref[...] = route
    cnt_ref[...] = cnt_ref[...] + (j > 0).astype(F32) * jnp.sum(sel, axis=1, keepdims=True)
    counts_ref[...] = cnt_ref[...]

    o_inter = {}
    state_t = [s_ref[h] for h in range(GLA_HEADS)]
    for c, h in pairs:
        q_in = (q[rows_of(c), ks_of(h)] * jnp.exp(b_cum[c][:, ks_of(h)])).astype(BF16)
        o_inter[c, h] = _dot_nt(q_in, state_t[h].astype(BF16))
        state_t[h] = state_t[h] * decay_last[c, h] + kv[c, h]
    for h in range(GLA_HEADS):
        s_ref[h] = state_t[h]

    for c, h in pairs:
        o = o_intra[c, h] + o_inter[c, h]
        o = o * lax.rsqrt(jnp.mean(o * o, axis=-1, keepdims=True) + RMS_EPS) * gain_ref[...]
        o_ref[rows_of(c), vs_of(h)] = o

    issue_fillers(len(fillers))
    per_proj = D_MODEL // MXU_COLS
    r, gate_a, gate_b = (jnp.concatenate(filled[n * per_proj:(n + 1) * per_proj], axis=1) for n in range(3))
    yb = _dot((o_ref[...] * (r * _sigmoid(r))).astype(BF16), w_b_ref[...])

    merged = _sigmoid(gate_a) * ya + _sigmoid(gate_b) * yb
    y = _dot(merged.astype(BF16), w_out_ref[...])
    z_ref[...] = DEEPNORM_ALPHA * x + g_m * y


def _mixer(x, mod, w_in_p, w_rank_p, w_gates, w_pool, pool_scale, w_a, w_al_p, b_alpha, gain, w_b, w_out,
           ln_g, ln_b, w_r_p, b_r_p):
    bsz, seq, d = x.shape
    tile = SEQ_TILE
    n_s = seq // tile
    n_tiles = bsz * n_s

    def const(shape):
        nd = len(shape)
        return pl.BlockSpec(shape, lambda j: (0,) * nd, pipeline_mode=pl.Buffered(1))

    cur = lambda j: jnp.minimum(j, n_tiles - 1)
    prev = lambda j: jnp.maximum(j - 1, 0)
    return pl.pallas_call(
        functools.partial(_mixer_kernel, tiles_per_seq=n_s, n_tiles=n_tiles),
        grid=(n_tiles + 1,),
        in_specs=[
            pl.BlockSpec((1, tile, d), lambda j: (cur(j) // n_s, cur(j) % n_s, 0)),
            pl.BlockSpec((1, 6, d), lambda j: (cur(j) // n_s, 0, 0)),
            pl.BlockSpec((1, 6, d), lambda j: (prev(j) // n_s, 0, 0)),
            const(w_in_p.shape), const(w_rank_p.shape), const(w_gates.shape),
            const(w_pool.shape), const(pool_scale.shape), const(w_a.shape),
            const(w_al_p.shape), const(b_alpha.shape), const(gain.shape), const(w_b.shape),
            const(w_out.shape), const(ln_g.shape), const(ln_b.shape), const(w_r_p.shape),
            const(b_r_p.shape),
        ],
        out_specs=[pl.BlockSpec((1, tile, d), lambda j: (prev(j) // n_s, prev(j) % n_s, 0)),
                   pl.BlockSpec((tile * ROW_SUB, LANES), lambda j: (prev(j), 0)),
                   pl.BlockSpec((ROUTE_ROWS, tile), lambda j: (0, prev(j))),
                   pl.BlockSpec((N_EXPERTS, LANES), lambda j: (0, 0))],
        out_shape=[
            jax.ShapeDtypeStruct((bsz, seq, d), F32),
            jax.ShapeDtypeStruct((bsz * seq * ROW_SUB, LANES), F32),
            jax.ShapeDtypeStruct((ROUTE_ROWS, bsz * seq), F32),
            jax.ShapeDtypeStruct((N_EXPERTS, LANES), F32),
        ],
        scratch_shapes=[
            pltpu.VMEM((POOL_HALO + tile, d), F32),
            pltpu.VMEM((GLA_HEADS, GLA_HEAD_V, GLA_HEAD_K), F32),
            pltpu.VMEM((N_EXPERTS, LANES), F32),
            pltpu.VMEM((tile, d), F32),
            pltpu.VMEM((tile, d), F32),
            pltpu.VMEM((tile // CHUNK * GLA_HEADS, CHUNK, CHUNK), F32),
        ],
        compiler_params=pltpu.CompilerParams(
            dimension_semantics=("arbitrary",), vmem_limit_bytes=VMEM_LIMIT),
        name="mixer",
    )(x, mod, mod, w_in_p, w_rank_p, w_gates, w_pool, pool_scale, w_a, w_al_p, b_alpha, gain, w_b, w_out,
      ln_g, ln_b, w_r_p, b_r_p)


def _row_at(ref, row8):
    return ref.at[pl.ds(pl.multiple_of(row8, ROW_SUB), ROW_SUB)]


def _per_tile(dest8, tile):
    n_t = dest8.shape[1] // tile
    return dest8.reshape(TOP_K, n_t, tile).transpose(1, 0, 2).reshape(n_t, 1, TOP_K * tile)


def _dispatch_kernel(fill_ref, dest_ref, u2_ref, rows_hbm, zbuf, sem_fill, sem_rows):
    i = pl.program_id(0)
    tile = u2_ref.shape[0] // ROW_SUB
    blk8 = EXPERT_BLOCK * ROW_SUB

    @pl.when(i == 0)
    def _():
        zbuf[...] = jnp.zeros_like(zbuf)
        n_windows = rows_hbm.shape[0] // blk8

        def fill(window):
            return pltpu.make_async_copy(zbuf, rows_hbm.at[pl.ds(pl.multiple_of(window * blk8, blk8), blk8)],
                                         sem_fill.at[0])

        for phase in ("start", "wait"):
            for e in range(N_EXPERTS):
                @pl.when(fill_ref[e] >= 0)
                def _():
                    getattr(fill(fill_ref[e]), phase)()

            def unused(window, carry):
                getattr(fill(window), phase)()
                return carry

            lax.fori_loop(fill_ref[N_EXPERTS], n_windows, unused, 0)

    def body(r, carry):
        for kk in range(TOP_K):
            pltpu.make_async_copy(_row_at(u2_ref, r * ROW_SUB), _row_at(rows_hbm, dest_ref[0, 0, kk * tile + r]),
                                  sem_rows.at[0]).start(priority=kk % 2)
        return carry

    lax.fori_loop(0, tile, body, 0, unroll=DMA_UNROLL)
    for _ in range(TOP_K):
        pltpu.make_async_copy(u2_ref, rows_hbm.at[pl.ds(0, tile * ROW_SUB)], sem_rows.at[0]).wait()


def _dispatch(fill_start, dest8, u2_rows, n_rows):
    n_tok = u2_rows.shape[0] // ROW_SUB
    tile = DISPATCH_TILE
    n_t = n_tok // tile
    grid_spec = pltpu.PrefetchScalarGridSpec(
        num_scalar_prefetch=1,
        grid=(n_t,),
        in_specs=[
            pl.BlockSpec((1, 1, tile * TOP_K), lambda i, fs: (i, 0, 0), memory_space=pltpu.SMEM),
            pl.BlockSpec((tile * ROW_SUB, LANES), lambda i, fs: (i, 0)),
        ],
        out_specs=pl.BlockSpec(memory_space=pl.ANY),
        scratch_shapes=[
            pltpu.VMEM((EXPERT_BLOCK * ROW_SUB, LANES), F32),
            pltpu.SemaphoreType.DMA((1,)),
            pltpu.SemaphoreType.DMA((1,)),
        ],
    )
    return pl.pallas_call(
        _dispatch_kernel,
        grid_spec=grid_spec,
        out_shape=jax.ShapeDtypeStruct(((n_rows + EXPERT_BLOCK) * ROW_SUB, LANES), F32),
        compiler_params=pltpu.CompilerParams(dimension_semantics=("arbitrary",)),
        name="dispatch",
    )(fill_start, _per_tile(dest8, tile), u2_rows)


def _moe_kernel(be_ref, nxt_ref, par_ref, full_ref, nb_ref, x_ref, bg_ref, bu_ref, bd_ref, wgu_hbm, wd_hbm, y_ref,
                wgu_buf, wd_buf, wg_s, wu_s, wd_s, sem_gu, sem_d):
    i = pl.program_id(0)
    new_expert = (i == 0) | (be_ref[i] != be_ref[jnp.maximum(i - 1, 0)])

    def fetch(e, s):
        return (pltpu.make_async_copy(wgu_hbm.at[e], wgu_buf.at[s], sem_gu.at[s]),
                pltpu.make_async_copy(wd_hbm.at[e], wd_buf.at[s], sem_d.at[s]))

    @pl.when(new_expert & (i < nb_ref[0]))
    def _():
        slot = par_ref[i]

        @pl.when(i == 0)
        def _():
            for cp in fetch(be_ref[0], slot):
                cp.start()

        @pl.when(nxt_ref[i] >= 0)
        def _():
            for cp in fetch(nxt_ref[i], 1 - slot):
                cp.start()

        for cp in fetch(be_ref[i], slot):
            cp.wait()
        src = lax.broadcasted_iota(jnp.int32, (MXU_COLS, MXU_COLS), 0)
        col = lax.broadcasted_iota(jnp.int32, (MXU_COLS, MXU_COLS), 1)
        half = MXU_COLS // 2
        want = jnp.where(col < half, 2 * col, 2 * (col - half) + 1)
        unzip = (src == want).astype(BF16)
        for g in range(wgu_buf.shape[2] // MXU_COLS):
            blk = wgu_buf[slot, :, g * MXU_COLS:(g + 1) * MXU_COLS].astype(BF16)
            sep = _dot(blk, unzip)
            wg_s[:, g * half:(g + 1) * half] = sep[:, :half].astype(BF16)
            wu_s[:, g * half:(g + 1) * half] = sep[:, half:].astype(BF16)
        wd_s[...] = wd_buf[slot].astype(BF16)

    def expert_mlp(n_rows):
        xb = _load_rows(x_ref, n_rows).astype(BF16)
        gate = jnp.minimum(_dot(xb, wg_s[...]) + bg_ref[0], SWIGLU_LIMIT)
        up = jnp.clip(_dot(xb, wu_s[...]) + bu_ref[0], -SWIGLU_LIMIT, SWIGLU_LIMIT)
        glu = gate * _sigmoid(gate * SWIGLU_ALPHA)
        _store_rows(y_ref, _dot(((up + 1.0) * glu).astype(BF16), wd_s[...]) + bd_ref[0])

    active = i < nb_ref[0]
    both_halves = full_ref[i] == 1

    @pl.when(active & both_halves)
    def _():
        expert_mlp(EXPERT_BLOCK)

    @pl.when(active & jnp.logical_not(both_halves))
    def _():
        expert_mlp(EXPERT_HALF)
        y_ref[EXPERT_HALF * ROW_SUB:, :] = jnp.zeros((EXPERT_HALF * ROW_SUB, LANES), F32)

    @pl.when(jnp.logical_not(active))
    def _():
        y_ref[...] = jnp.zeros_like(y_ref)


def _moe(block_expert, next_expert, slot_parity, both_halves, n_active, x_rows, w_gate_up, bg, bu, w_down, bd):
    n_blocks = block_expert.shape[0]
    _, d, f2 = w_gate_up.shape
    f = f2 // 2
    blk8 = EXPERT_BLOCK * ROW_SUB
    rows_in = pl.BlockSpec((blk8, LANES),
                           lambda i, be, nx, pr, fl, nb: (jnp.maximum(jnp.minimum(i, nb[0] - 1), 0), 0))
    rows_out = pl.BlockSpec((blk8, LANES), lambda i, be, nx, pr, fl, nb: (i, 0))
    per_expert = lambda shape: pl.BlockSpec((1,) + shape, lambda i, be, nx, pr, fl, nb: (be[i], 0, 0))
    hbm = pl.BlockSpec(memory_space=pl.ANY)
    grid_spec = pltpu.PrefetchScalarGridSpec(
        num_scalar_prefetch=5,
        grid=(n_blocks,),
        in_specs=[rows_in, per_expert((1, f)), per_expert((1, f)), per_expert((1, d)), hbm, hbm],
        out_specs=rows_out,
        scratch_shapes=[pltpu.VMEM((2, d, f2), F32), pltpu.VMEM((2, f, d), F32),
                        pltpu.VMEM((d, f), BF16), pltpu.VMEM((d, f), BF16), pltpu.VMEM((f, d), BF16),
                        pltpu.SemaphoreType.DMA((2,)), pltpu.SemaphoreType.DMA((2,))],
    )
    return pl.pallas_call(
        _moe_kernel,
        grid_spec=grid_spec,
        out_shape=jax.ShapeDtypeStruct((n_blocks * blk8, LANES), F32),
        compiler_params=pltpu.CompilerParams(
            dimension_semantics=("arbitrary",), vmem_limit_bytes=VMEM_LIMIT),
        name="moe",
    )(block_expert, next_expert, slot_parity, both_halves, n_active, x_rows, bg, bu, bd, w_gate_up, w_down)


def _combine_kernel(dest_cur_ref, dest_nxt_ref, x1_ref, mod_ref, route_ref, ln_g_ref, ln_b_ref, y_hbm,
                    o_ref, ybuf, sem):
    i = pl.program_id(0)
    n_steps = pl.num_programs(0)
    tile = x1_ref.shape[0]
    slot = lax.rem(i, 2)

    def gather(dest_ref, s):
        def body(r, carry):
            for kk in range(TOP_K):
                pltpu.make_async_copy(_row_at(y_hbm, dest_ref[0, 0, kk * tile + r]),
                                      _row_at(ybuf.at[s, kk], r * ROW_SUB), sem.at[s]).start(priority=kk % 2)
            return carry
        lax.fori_loop(0, tile, body, 0, unroll=DMA_UNROLL)

    @pl.when(i == 0)
    def _():
        gather(dest_cur_ref, 0)

    def wait_slot(s):
        for kk in range(TOP_K):
            pltpu.make_async_copy(y_hbm.at[pl.ds(0, tile * ROW_SUB)], ybuf.at[s, kk], sem.at[s]).wait()

    for r in range(tile):
        for kk in range(TOP_K):
            pltpu.make_async_copy(_row_at(y_hbm, dest_nxt_ref[0, 0, kk * tile + r]),
                                  _row_at(ybuf.at[1 - slot, kk], r * ROW_SUB),
                                  sem.at[1 - slot]).start(priority=kk % 2)
    wait_slot(slot)

    g_f = mod_ref[0][5:6]
    weight = route_ref[...]
    y = jnp.zeros(x1_ref.shape, F32)
    for kk in range(TOP_K):
        y = y + weight[:, kk:kk + 1] * _load_rows(ybuf, tile, (slot, kk))
    o_ref[...] = _layer_norm(DEEPNORM_ALPHA * x1_ref[...] + g_f * y, ln_g_ref[...], ln_b_ref[...])

    @pl.when(i == n_steps - 1)
    def _():
        wait_slot(1 - slot)


def _combine(dest8, x1, mod, route, ln_g, ln_b, y_rows, seq):
    n_tok, d = x1.shape
    tile = COMBINE_TILE
    n_t = n_tok // tile
    per_seq = seq // tile
    dest3 = _per_tile(dest8, tile)
    smem_blk = lambda fn: pl.BlockSpec((1, 1, tile * TOP_K), fn, memory_space=pltpu.SMEM)
    row = lambda width: pl.BlockSpec((tile, width), lambda i: (i, 0))
    vec = pl.BlockSpec((1, d), lambda i: (0, 0))
    return pl.pallas_call(
        _combine_kernel,
        grid=(n_t,),
        in_specs=[smem_blk(lambda i: (i, 0, 0)), smem_blk(lambda i: (jnp.minimum(i + 1, n_t - 1), 0, 0)),
                  row(d), pl.BlockSpec((1, 6, d), lambda i: (i // per_seq, 0, 0)), row(TOP_K), vec, vec,
                  pl.BlockSpec(memory_space=pl.ANY)],
        out_specs=row(d),
        out_shape=jax.ShapeDtypeStruct((n_tok, d), F32),
        scratch_shapes=[pltpu.VMEM((2, TOP_K, tile * ROW_SUB, LANES), F32), pltpu.SemaphoreType.DMA((2,))],
        compiler_params=pltpu.CompilerParams(
            dimension_semantics=("arbitrary",), vmem_limit_bytes=VMEM_LIMIT),
        name="combine",
    )(dest3, dest3, x1, mod, route, ln_g, ln_b, y_rows)


def _split_in_proj(w_in):
    rank_lo = _OFF[5]
    rank_hi = rank_lo + GLA_GATE_RANK
    pad = jnp.zeros((w_in.shape[0], RANK_PAD - GLA_GATE_RANK), BF16)
    return (w_in[:, :rank_lo].astype(BF16),
            jnp.concatenate([w_in[:, rank_lo:rank_hi].astype(BF16), pad], axis=1),
            w_in[:, rank_hi:].astype(BF16))


def _layer(x, c, w_ada, b_ada, w_in, w_pool_group, pool_scale, w_branch_a, w_alpha_up, b_alpha,
           gla_norm_gain, w_branch_b, w_out, ln1_gain, ln1_bias, w_router, b_router,
           w_gate_up, b_gate_up, w_down, b_down, ln2_gain, ln2_bias):
    bsz, seq, d = x.shape
    n_tok = bsz * seq
    n_assign = n_tok * TOP_K
    row2 = lambda v: v.reshape(1, -1)

    mod = _ada(c, w_ada, b_ada).reshape(bsz, 6, d)

    w_in_p, w_rank_p, w_gates = _split_in_proj(w_in)
    w_al_p = jnp.concatenate(
        [w_alpha_up, jnp.zeros((RANK_PAD - GLA_GATE_RANK, GLA_KEY_DIM), w_alpha_up.dtype)], axis=0).astype(BF16)
    w_r_t = w_router.T
    w_r_hi = w_r_t.astype(BF16)
    w_r_split = jnp.concatenate([w_r_hi, (w_r_t - w_r_hi.astype(F32)).astype(BF16)], axis=0)
    x1, u2_rows, route_t, counts = _mixer(
        x, mod, w_in_p, w_rank_p, w_gates, w_pool_group.astype(BF16), row2(pool_scale),
        w_branch_a.astype(BF16), w_al_p,
        row2(b_alpha), row2(gla_norm_gain), w_branch_b.astype(BF16), w_out.astype(BF16), row2(ln1_gain),
        row2(ln1_bias), w_r_split, b_router.reshape(N_EXPERTS, 1))

    top_idx = route_t[0:TOP_K].astype(jnp.int32)
    rank = route_t[TOP_K:2 * TOP_K].astype(jnp.int32)
    route = route_t[2 * TOP_K:3 * TOP_K].T
    counts = counts[:, 0].astype(jnp.int32)
    padded = (counts + EXPERT_BLOCK - 1) // EXPERT_BLOCK * EXPERT_BLOCK
    pad_end = jnp.cumsum(padded)
    pad_start = pad_end - padded
    n_rows = (n_assign + N_EXPERTS * (EXPERT_BLOCK - 1) + EXPERT_BLOCK - 1) // EXPERT_BLOCK * EXPERT_BLOCK
    n_blocks = n_rows // EXPERT_BLOCK
    n_active = (pad_end[-1] // EXPERT_BLOCK).astype(jnp.int32)
    start_of = jnp.sum(jnp.where(top_idx[..., None] == jnp.arange(N_EXPERTS, dtype=jnp.int32), pad_start, 0),
                       axis=-1)
    dest8 = (start_of + rank) * ROW_SUB
    blk_row = jnp.arange(n_blocks, dtype=jnp.int32)[:, None] * EXPERT_BLOCK
    block_expert = jnp.minimum(jnp.sum(pad_end[None, :] <= blk_row, axis=1), N_EXPERTS - 1).astype(jnp.int32)

    last_window = jnp.where(padded > 0, pad_end // EXPERT_BLOCK - 1, -1)
    fill_windows = jnp.concatenate([last_window, n_active.reshape(1)]).astype(jnp.int32)
    x_rows = _dispatch(fill_windows, dest8, u2_rows, n_rows)
    f = w_down.shape[1]
    bg = b_gate_up[:, 0::2].reshape(N_EXPERTS, 1, f)
    bu = b_gate_up[:, 1::2].reshape(N_EXPERTS, 1, f)
    eid = jnp.arange(N_EXPERTS, dtype=jnp.int32)
    owns = counts > 0
    later = jnp.where((eid[None, :] > eid[:, None]) & owns[None, :], eid[None, :], N_EXPERTS)
    next_of = jnp.min(later, axis=1)
    next_of = jnp.where(next_of == N_EXPERTS, -1, next_of).astype(jnp.int32)
    parity_of = ((jnp.cumsum(owns.astype(jnp.int32)) - 1) % 2).astype(jnp.int32)
    is_block_expert = block_expert[:, None] == eid[None, :]
    per_block = lambda v: jnp.sum(jnp.where(is_block_expert, v[None, :], 0), axis=1).astype(jnp.int32)
    left = per_block(pad_start + counts) - blk_row[:, 0]
    both_halves = (left > EXPERT_HALF).astype(jnp.int32)
    y_rows = _moe(block_expert, per_block(next_of), per_block(parity_of), both_halves, n_active.reshape(1),
                  x_rows, w_gate_up, bg, bu, w_down, b_down.reshape(N_EXPERTS, 1, d))
    out = _combine(dest8, x1.reshape(n_tok, d), mod, route, row2(ln2_gain), row2(ln2_bias), y_rows, seq)
    return out.reshape(bsz, seq, d)


def kernel(x, c, w_ada, b_ada, w_in, w_pool_group, pool_scale, w_branch_a, w_alpha_up, b_alpha, gla_norm_gain,
           w_branch_b, w_out, ln1_gain, ln1_bias, w_router, b_router, w_gate_up, b_gate_up, w_down, b_down,
           ln2_gain, ln2_bias):
    for l in range(DEPTH):
        x = _layer(x, c, w_ada[l], b_ada[l], w_in[l], w_pool_group[l], pool_scale[l], w_branch_a[l],
                   w_alpha_up[l], b_alpha[l], gla_norm_gain[l], w_branch_b[l], w_out[l], ln1_gain[l],
                   ln1_bias[l], w_router[l], b_router[l], w_gate_up[l], b_gate_up[l], w_down[l], b_down[l],
                   ln2_gain[l], ln2_bias[l])
    return x
```

```python
import functools

import jax
import jax.numpy as jnp
from jax import lax
from jax.experimental import pallas as pl
from jax.experimental.pallas import tpu as pltpu

D_MODEL = 1024
CHUNK = 64
SUB = 16
N_SUB = CHUNK // SUB
POOL_WINDOWS = (2, 4, 8, 16)
POOL_GROUP_WIDTH = D_MODEL // len(POOL_WINDOWS)
POOL_HALO = 16
GLA_HEADS = 4
GLA_KEY_DIM = D_MODEL // 2
GLA_HEAD_K = GLA_KEY_DIM // GLA_HEADS
GLA_HEAD_V = D_MODEL // GLA_HEADS
GLA_GATE_RANK = 16
GLA_TAU = 16.0
N_EXPERTS = 32
TOP_K = 4
SWIGLU_ALPHA = 1.702
SWIGLU_LIMIT = 7.0
EXPERT_BLOCK = 512
EXPERT_HALF = EXPERT_BLOCK // 2
LN_EPS = 1e-5
RMS_EPS = 1e-6
DEPTH = 1
DEEPNORM_ALPHA = (2.0 * DEPTH) ** 0.25

LANES = 128
MXU_COLS = 256
RANK_PAD = LANES
_W = (D_MODEL, GLA_KEY_DIM, GLA_KEY_DIM, D_MODEL, D_MODEL, RANK_PAD, D_MODEL, D_MODEL)
_OFF = tuple(sum(_W[:i]) for i in range(len(_W) + 1))
EXP_CAP = 60.0
SAFE_SUB_DECAY = 40.0
ROUTE_ROWS = 16

SEQ_TILE = 512
DISPATCH_TILE = 2048
COMBINE_TILE = 256
DMA_UNROLL = 8
VMEM_LIMIT = 60 * 1024 * 1024

F32 = jnp.float32
BF16 = jnp.bfloat16
HI = lax.Precision.HIGHEST


def _dot(a, b):
    return jnp.dot(a, b, preferred_element_type=F32)


def _dot_nt(a, b):
    return lax.dot_general(a, b, (((1,), (1,)), ((), ())), preferred_element_type=F32)


def _dot_tn(a, b):
    return lax.dot_general(a, b, (((0,), (0,)), ((), ())), preferred_element_type=F32)


def _sigmoid(v):
    return 0.5 * jnp.tanh(0.5 * v) + 0.5


ROW_SUB = D_MODEL // LANES


def _load_rows(ref, n_rows, lead=()):
    return jnp.concatenate(
        [ref[lead + (pl.ds(j, n_rows, stride=ROW_SUB), slice(None))] for j in range(ROW_SUB)], axis=1)


def _store_rows(ref, val):
    for j in range(ROW_SUB):
        ref[pl.ds(j, val.shape[0], stride=ROW_SUB), :] = val[:, j * LANES:(j + 1) * LANES]


def _ada_kernel(c_ref, w_ref, b_ref, o_ref):
    c = c_ref[...]
    s = c * jax.nn.sigmoid(c)
    o_ref[...] = jnp.dot(s, w_ref[...], precision=HI, preferred_element_type=F32) + b_ref[...]


def _ada(c, w_ada, b_ada):
    bsz, d = c.shape
    n = w_ada.shape[1]
    tn = 1024
    return pl.pallas_call(
        _ada_kernel,
        grid=(n // tn,),
        in_specs=[
            pl.BlockSpec((bsz, d), lambda j: (0, 0)),
            pl.BlockSpec((d, tn), lambda j: (0, j)),
            pl.BlockSpec((1, tn), lambda j: (0, j)),
        ],
        out_specs=pl.BlockSpec((bsz, tn), lambda j: (0, j)),
        out_shape=jax.ShapeDtypeStruct((bsz, n), F32),
        name="ada",
    )(c, w_ada, b_ada.reshape(1, n))


def _layer_norm(z, gain, bias):
    mu = jnp.mean(z, axis=-1, keepdims=True)
    zc = z - mu
    var = jnp.mean(zc * zc, axis=-1, keepdims=True)
    return zc * lax.rsqrt(var + LN_EPS) * gain + bias


def _mixer_kernel(x_ref, mod_ref, mod_prev_ref, w_in_ref, w_rank_ref, w_gates_ref, w_pool_ref, pool_scale_ref,
                  w_a_ref, w_al_ref,
                  b_al_ref, gain_ref, w_b_ref, w_out_ref, ln_g_ref, ln_b_ref, w_r_ref, b_r_ref,
                  x1_ref, u2_ref, route_ref, counts_ref,
                  a_ext, s_ref, cnt_ref, o_ref, z_ref, sc_ref, *, tiles_per_seq, n_tiles):
    j = pl.program_id(0)
    s_idx = lax.rem(jnp.minimum(j, n_tiles - 1), tiles_per_seq)
    tile = x_ref.shape[1]

    @pl.when(s_idx == 0)
    def _():
        s_ref[...] = jnp.zeros_like(s_ref)
        a_ext[0:POOL_HALO, :] = jnp.zeros((POOL_HALO, D_MODEL), F32)

    @pl.when(j == 0)
    def _():
        cnt_ref[...] = jnp.zeros_like(cnt_ref)
        z_ref[...] = jnp.zeros_like(z_ref)


    mod_prev = mod_prev_ref[0]
    x1 = _layer_norm(z_ref[...], ln_g_ref[...], ln_b_ref[...])
    x1_ref[0] = x1
    u2 = x1 * (1.0 + mod_prev[4:5]) + mod_prev[3:4]
    _store_rows(u2_ref, u2)
    u2_hi = u2.astype(BF16)
    u2_lo = (u2 - u2_hi.astype(F32)).astype(BF16)

    mod = mod_ref[0]
    sh_m, sc_m, g_m = mod[0:1], mod[1:2], mod[2:3]
    x = x_ref[0]
    u = (x * (1.0 + sc_m) + sh_m).astype(BF16)

    def proj(i):
        return _dot(u, w_in_ref[:, _OFF[i]:_OFF[i + 1]])

    gate_cols = ([(w_in_ref, _OFF[4] + jj * MXU_COLS) for jj in range(D_MODEL // MXU_COLS)]
                 + [(w_gates_ref, jj * MXU_COLS) for jj in range(2 * D_MODEL // MXU_COLS)])
    fillers = [functools.partial(lambda w_ref, c0: _dot(u, w_ref[:, c0:c0 + MXU_COLS]), w_ref, c0)
               for w_ref, c0 in gate_cols]
    filled = []

    def issue_fillers(n):
        for _ in range(n):
            if len(filled) < len(fillers):
                filled.append(fillers[len(filled)]())

    a = proj(0)
    q = proj(1) * (GLA_HEAD_K ** -0.5)
    k_all = proj(2)
    v_all = proj(3)
    alpha_low = _dot(u, w_rank_ref[...])

    part = _dot_nt(w_r_ref[...], u2_hi)
    logits = (part[:N_EXPERTS] + part[N_EXPERTS:] + _dot_nt(w_r_ref[0:N_EXPERTS, :], u2_lo)
              + b_r_ref[...])
    issue_fillers(2)

    a_ext[POOL_HALO:POOL_HALO + tile, :] = a
    t_glob = s_idx * tile + lax.broadcasted_iota(jnp.int32, (tile, 1), 0)
    mapped = []
    for g, w in enumerate(POOL_WINDOWS):
        lo, hi = g * POOL_GROUP_WIDTH, (g + 1) * POOL_GROUP_WIDTH
        win = a_ext[:, lo:hi]
        k = 1
        while k < w:
            win = win + pltpu.roll(win, k, 0)
            k *= 2
        inv_cnt = 1.0 / jnp.minimum(t_glob + 1, w).astype(F32)
        pooled = win[POOL_HALO:, :] * inv_cnt - a[:, lo:hi]
        mapped.append(_dot(pooled.astype(BF16), w_pool_ref[g]))
    a_ext[0:POOL_HALO, :] = a[tile - POOL_HALO:tile, :]
    ya = _dot((jnp.concatenate(mapped, axis=1) * pool_scale_ref[...]).astype(BF16), w_a_ref[...])

    erow = lax.broadcasted_iota(jnp.int32, (N_EXPERTS, tile), 0).astype(F32)
    work = logits
    sel = jnp.zeros((N_EXPERTS, tile), F32)
    vals, hits = [], []
    for _ in range(TOP_K):
        m = jnp.max(work, axis=0, keepdims=True)
        idx = jnp.min(jnp.where(work == m, erow, float(N_EXPERTS)), axis=0, keepdims=True)
        hit = erow == idx
        vals.append(m)
        hits.append((idx, hit))
        sel = jnp.where(hit, 1.0, sel)
        work = jnp.where(hit, -jnp.inf, work)
    exps = [jnp.exp(v - vals[0]) for v in vals]
    inv_den = 1.0 / (exps[0] + exps[1] + exps[2] + exps[3])
    issue_fillers(1)

    z = _dot(alpha_low.astype(BF16), w_al_ref[...]) + b_al_ref[...]
    log_a = (jnp.minimum(z, 0.0) - jnp.log1p(jnp.exp(-jnp.abs(z)))) * (1.0 / GLA_TAU)

    ri = lax.broadcasted_iota(jnp.int32, (CHUNK, CHUNK), 0)
    ci = lax.broadcasted_iota(jnp.int32, (CHUNK, CHUNK), 1)
    causal = ci <= ri
    cum_mat = causal.astype(BF16)
    la_hi = log_a.astype(BF16)
    la_split = jnp.concatenate([la_hi, (log_a - la_hi.astype(F32)).astype(BF16)], axis=1)

    n_chunks = tile // CHUNK
    pairs = [(c, h) for c in range(n_chunks) for h in range(GLA_HEADS)]
    rows_of = lambda c: slice(c * CHUNK, (c + 1) * CHUNK)
    ks_of = lambda h: slice(h * GLA_HEAD_K, (h + 1) * GLA_HEAD_K)
    vs_of = lambda h: slice(h * GLA_HEAD_V, (h + 1) * GLA_HEAD_V)

    b_cum, ref_pts, b_ref_pt = [], [], []
    for c in range(n_chunks):
        cum = _dot(cum_mat, la_split[rows_of(c)])
        b_cum.append(cum[:, :GLA_KEY_DIM] + cum[:, GLA_KEY_DIM:])
        ref_pts.append([jnp.zeros((1, GLA_KEY_DIM), F32)]
                       + [b_cum[c][i * SUB - 1:i * SUB, :] for i in range(1, N_SUB)])
        b_ref_pt.append(jnp.concatenate([jnp.broadcast_to(p, (SUB, GLA_KEY_DIM)) for p in ref_pts[c]], axis=0))

    rt = lax.broadcasted_iota(jnp.int32, (tile, tile), 0)
    ct = lax.broadcasted_iota(jnp.int32, (tile, tile), 1)
    base = _dot(sel.astype(BF16), (rt < ct).astype(BF16)) + cnt_ref[:, 0:1]

    for n_pair, (c, h) in enumerate(pairs):
        if n_pair % 3 == 0:
            issue_fillers(1)
        qh, kh, bh = q[rows_of(c), ks_of(h)], k_all[rows_of(c), ks_of(h)], b_cum[c][:, ks_of(h)]
        q_dec = (qh * jnp.exp(bh - b_ref_pt[c][:, ks_of(h)])).astype(BF16)
        k_dec = jnp.concatenate(
            [(kh * jnp.exp(jnp.minimum(ref_pts[c][i][:, ks_of(h)] - bh, EXP_CAP))).astype(BF16)
             for i in range(N_SUB)], axis=0)
        s_all = _dot_nt(q_dec, k_dec)
        sc_ref[n_pair] = jnp.concatenate(
            [s_all[i * SUB:(i + 1) * SUB, i * CHUNK:(i + 1) * CHUNK] for i in range(N_SUB)], axis=0)

    sub_decay = b_ref_pt[0] - b_cum[0]
    for c in range(1, n_chunks):
        sub_decay = jnp.maximum(sub_decay, b_ref_pt[c] - b_cum[c])

    @pl.when(jnp.max(sub_decay) > SAFE_SUB_DECAY)
    def _():
        o_ref[:, 0:GLA_KEY_DIM] = q
        o_ref[:, GLA_KEY_DIM:2 * GLA_KEY_DIM] = jnp.concatenate(b_cum, axis=0)
        for n_pair, (c, h) in enumerate(pairs):
            kh, bh = k_all[rows_of(c), ks_of(h)], b_cum[c][:, ks_of(h)]

            def score_rows(g, carry):
                first = pl.multiple_of(g * 8, 8)
                q_8 = o_ref[pl.ds(c * CHUNK + first, 8), ks_of(h)]
                b_8 = o_ref[pl.ds(c * CHUNK + first, 8),
                            GLA_KEY_DIM + h * GLA_HEAD_K:GLA_KEY_DIM + (h + 1) * GLA_HEAD_K]
                rows = []
                for r in range(8):
                    k_i = (kh * jnp.exp(jnp.minimum(b_8[r:r + 1] - bh, 0.0))).astype(BF16)
                    q_i = jnp.broadcast_to(q_8[r:r + 1], (8, GLA_HEAD_K)).astype(BF16)
                    rows.append(_dot_nt(q_i, k_i)[0:1])
                sc_ref[n_pair, pl.ds(first, 8), :] = jnp.concatenate(rows, axis=0)
                return carry

            lax.fori_loop(0, CHUNK // 8, score_rows, 0)

    o_intra, kv, decay_last = {}, {}, {}
    for n_pair, (c, h) in enumerate(pairs):
        kh, bh = k_all[rows_of(c), ks_of(h)], b_cum[c][:, ks_of(h)]
        vh = v_all[rows_of(c), vs_of(h)].astype(BF16)
        o_intra[c, h] = _dot(jnp.where(causal, sc_ref[n_pair], 0.0).astype(BF16), vh)
        b_last = bh[CHUNK - 1:CHUNK, :]
        kv[c, h] = _dot_tn(vh, (kh * jnp.exp(b_last - bh)).astype(BF16))
        decay_last[c, h] = jnp.exp(b_last)

    orow = lax.broadcasted_iota(jnp.int32, (ROUTE_ROWS, tile), 0)
    route = jnp.zeros((ROUTE_ROWS, tile), F32)
    for kk in range(TOP_K):
        idx, hit = hits[kk]
        rank = jnp.sum(jnp.where(hit, base, 0.0), axis=0, keepdims=True)
        route = jnp.where(orow == kk, idx, route)
        route = jnp.where(orow == TOP_K + kk, rank, route)
        route = jnp.where(orow == 2 * TOP_K + kk, exps[kk] * inv_den, route)
    route_ref[...] = route
    cnt_ref[...] = cnt_ref[...] + (j > 0).astype(F32) * jnp.sum(sel, axis=1, keepdims=True)
    counts_ref[...] = cnt_ref[...]

    o_inter = {}
    state_t = [s_ref[h] for h in range(GLA_HEADS)]
    for c, h in pairs:
        q_in = (q[rows_of(c), ks_of(h)] * jnp.exp(b_cum[c][:, ks_of(h)])).astype(BF16)
        o_inter[c, h] = _dot_nt(q_in, state_t[h].astype(BF16))
        state_t[h] = state_t[h] * decay_last[c, h] + kv[c, h]
    for h in range(GLA_HEADS):
        s_ref[h] = state_t[h]

    for c, h in pairs:
        o = o_intra[c, h] + o_inter[c, h]
        o = o * lax.rsqrt(jnp.mean(o * o, axis=-1, keepdims=True) + RMS_EPS) * gain_ref[...]
        o_ref[rows_of(c), vs_of(h)] = o

    issue_fillers(len(fillers))
    per_proj = D_MODEL // MXU_COLS
    r, gate_a, gate_b = (jnp.concatenate(filled[n * per_proj:(n + 1) * per_proj], axis=1) for n in range(3))
    yb = _dot((o_ref[...] * (r * _sigmoid(r))).astype(BF16), w_b_ref[...])

    merged = _sigmoid(gate_a) * ya + _sigmoid(gate_b) * yb
    y = _dot(merged.astype(BF16), w_out_ref[...])
    z_ref[...] = DEEPNORM_ALPHA * x + g_m * y


def _mixer(x, mod, w_in_p, w_rank_p, w_gates, w_pool, pool_scale, w_a, w_al_p, b_alpha, gain, w_b, w_out,
           ln_g, ln_b, w_r_p, b_r_p):
    bsz, seq, d = x.shape
    tile = SEQ_TILE
    n_s = seq // tile
    n_tiles = bsz * n_s

    def const(shape):
        nd = len(shape)
        return pl.BlockSpec(shape, lambda j: (0,) * nd, pipeline_mode=pl.Buffered(1))

    cur = lambda j: jnp.minimum(j, n_tiles - 1)
    prev = lambda j: jnp.maximum(j - 1, 0)
    return pl.pallas_call(
        functools.partial(_mixer_kernel, tiles_per_seq=n_s, n_tiles=n_tiles),
        grid=(n_tiles + 1,),
        in_specs=[
            pl.BlockSpec((1, tile, d), lambda j: (cur(j) // n_s, cur(j) % n_s, 0)),
            pl.BlockSpec((1, 6, d), lambda j: (cur(j) // n_s, 0, 0)),
            pl.BlockSpec((1, 6, d), lambda j: (prev(j) // n_s, 0, 0)),
            const(w_in_p.shape), const(w_rank_p.shape), const(w_gates.shape),
            const(w_pool.shape), const(pool_scale.shape), const(w_a.shape),
            const(w_al_p.shape), const(b_alpha.shape), const(gain.shape), const(w_b.shape),
            const(w_out.shape), const(ln_g.shape), const(ln_b.shape), const(w_r_p.shape),
            const(b_r_p.shape),
        ],
        out_specs=[pl.BlockSpec((1, tile, d), lambda j: (prev(j) // n_s, prev(j) % n_s, 0)),
                   pl.BlockSpec((tile * ROW_SUB, LANES), lambda j: (prev(j), 0)),
                   pl.BlockSpec((ROUTE_ROWS, tile), lambda j: (0, prev(j))),
                   pl.BlockSpec((N_EXPERTS, LANES), lambda j: (0, 0))],
        out_shape=[
            jax.ShapeDtypeStruct((bsz, seq, d), F32),
            jax.ShapeDtypeStruct((bsz * seq * ROW_SUB, LANES), F32),
            jax.ShapeDtypeStruct((ROUTE_ROWS, bsz * seq), F32),
            jax.ShapeDtypeStruct((N_EXPERTS, LANES), F32),
        ],
        scratch_shapes=[
            pltpu.VMEM((POOL_HALO + tile, d), F32),
            pltpu.VMEM((GLA_HEADS, GLA_HEAD_V, GLA_HEAD_K), F32),
            pltpu.VMEM((N_EXPERTS, LANES), F32),
            pltpu.VMEM((tile, d), F32),
            pltpu.VMEM((tile, d), F32),
            pltpu.VMEM((tile // CHUNK * GLA_HEADS, CHUNK, CHUNK), F32),
        ],
        compiler_params=pltpu.CompilerParams(
            dimension_semantics=("arbitrary",), vmem_limit_bytes=VMEM_LIMIT),
        name="mixer",
    )(x, mod, mod, w_in_p, w_rank_p, w_gates, w_pool, pool_scale, w_a, w_al_p, b_alpha, gain, w_b, w_out,
      ln_g, ln_b, w_r_p, b_r_p)


def _row_at(ref, row8):
    return ref.at[pl.ds(pl.multiple_of(row8, ROW_SUB), ROW_SUB)]


def _per_tile(dest8, tile):
    n_t = dest8.shape[1] // tile
    return dest8.reshape(TOP_K, n_t, tile).transpose(1, 0, 2).reshape(n_t, 1, TOP_K * tile)


def _dispatch_kernel(fill_ref, dest_ref, u2_ref, rows_hbm, zbuf, sem_fill, sem_rows):
    i = pl.program_id(0)
    tile = u2_ref.shape[0] // ROW_SUB
    blk8 = EXPERT_BLOCK * ROW_SUB

    def zero_fills(phase):
        def piece(first_row, n_rows):
            dst = rows_hbm.at[pl.ds(pl.multiple_of(first_row * ROW_SUB, ROW_SUB), n_rows * ROW_SUB)]
            getattr(pltpu.make_async_copy(zbuf.at[pl.ds(0, n_rows * ROW_SUB)], dst, sem_fill.at[0]), phase)()

        for e in range(N_EXPERTS):
            first_row, gap = fill_ref[e], fill_ref[N_EXPERTS + e]
            n_rows = EXPERT_BLOCK // 2
            while n_rows >= 1:
                @pl.when(jnp.bitwise_and(gap, n_rows) != 0)
                def _(first_row=first_row, n_rows=n_rows):
                    piece(first_row, n_rows)
                first_row = first_row + jnp.bitwise_and(gap, n_rows)
                n_rows //= 2

        def unused(window, carry):
            piece(window * EXPERT_BLOCK, EXPERT_BLOCK)
            return carry

        lax.fori_loop(fill_ref[2 * N_EXPERTS], rows_hbm.shape[0] // blk8, unused, 0)

    @pl.when(i == 0)
    def _():
        zbuf[...] = jnp.zeros_like(zbuf)
        zero_fills("start")

    def body(r, carry):
        for kk in range(TOP_K):
            pltpu.make_async_copy(_row_at(u2_ref, r * ROW_SUB), _row_at(rows_hbm, dest_ref[0, 0, kk * tile + r]),
                                  sem_rows.at[0]).start(priority=kk % 2)
        return carry

    lax.fori_loop(0, tile, body, 0, unroll=DMA_UNROLL)
    for _ in range(TOP_K):
        pltpu.make_async_copy(u2_ref, rows_hbm.at[pl.ds(0, tile * ROW_SUB)], sem_rows.at[0]).wait()

    @pl.when(i == pl.num_programs(0) - 1)
    def _():
        zero_fills("wait")


def _dispatch(fill_start, dest8, u2_rows, n_rows):
    n_tok = u2_rows.shape[0] // ROW_SUB
    tile = DISPATCH_TILE
    n_t = n_tok // tile
    grid_spec = pltpu.PrefetchScalarGridSpec(
        num_scalar_prefetch=1,
        grid=(n_t,),
        in_specs=[
            pl.BlockSpec((1, 1, tile * TOP_K), lambda i, fs: (i, 0, 0), memory_space=pltpu.SMEM),
            pl.BlockSpec((tile * ROW_SUB, LANES), lambda i, fs: (i, 0)),
        ],
        out_specs=pl.BlockSpec(memory_space=pl.ANY),
        scratch_shapes=[
            pltpu.VMEM((EXPERT_BLOCK * ROW_SUB, LANES), F32),
            pltpu.SemaphoreType.DMA((1,)),
            pltpu.SemaphoreType.DMA((1,)),
        ],
    )
    return pl.pallas_call(
        _dispatch_kernel,
        grid_spec=grid_spec,
        out_shape=jax.ShapeDtypeStruct(((n_rows + EXPERT_BLOCK) * ROW_SUB, LANES), F32),
        compiler_params=pltpu.CompilerParams(dimension_semantics=("arbitrary",)),
        name="dispatch",
    )(fill_start, _per_tile(dest8, tile), u2_rows)


def _moe_kernel(be_ref, nxt_ref, par_ref, full_ref, nb_ref, x_ref, bg_ref, bu_ref, bd_ref, wgu_hbm, wd_hbm, y_ref,
                wgu_buf, wd_buf, wg_s, wu_s, wd_s, sem_gu, sem_d):
    i = pl.program_id(0)
    new_expert = (i == 0) | (be_ref[i] != be_ref[jnp.maximum(i - 1, 0)])

    def fetch(e, s):
        return (pltpu.make_async_copy(wgu_hbm.at[e], wgu_buf.at[s], sem_gu.at[s]),
                pltpu.make_async_copy(wd_hbm.at[e], wd_buf.at[s], sem_d.at[s]))

    @pl.when(new_expert & (i < nb_ref[0]))
    def _():
        slot = par_ref[i]

        @pl.when(i == 0)
        def _():
            for cp in fetch(be_ref[0], slot):
                cp.start()

        @pl.when(nxt_ref[i] >= 0)
        def _():
            for cp in fetch(nxt_ref[i], 1 - slot):
                cp.start()

        for cp in fetch(be_ref[i], slot):
            cp.wait()
        src = lax.broadcasted_iota(jnp.int32, (MXU_COLS, MXU_COLS), 0)
        col = lax.broadcasted_iota(jnp.int32, (MXU_COLS, MXU_COLS), 1)
        half = MXU_COLS // 2
        want = jnp.where(col < half, 2 * col, 2 * (col - half) + 1)
        unzip = (src == want).astype(BF16)
        for g in range(wgu_buf.shape[2] // MXU_COLS):
            blk = wgu_buf[slot, :, g * MXU_COLS:(g + 1) * MXU_COLS].astype(BF16)
            sep = _dot(blk, unzip)
            wg_s[:, g * half:(g + 1) * half] = sep[:, :half].astype(BF16)
            wu_s[:, g * half:(g + 1) * half] = sep[:, half:].astype(BF16)
        wd_s[...] = wd_buf[slot].astype(BF16)

    def expert_mlp(n_rows):
        xb = _load_rows(x_ref, n_rows).astype(BF16)
        gate = jnp.minimum(_dot(xb, wg_s[...]) + bg_ref[0], SWIGLU_LIMIT)
        up = jnp.clip(_dot(xb, wu_s[...]) + bu_ref[0], -SWIGLU_LIMIT, SWIGLU_LIMIT)
        glu = gate * _sigmoid(gate * SWIGLU_ALPHA)
        _store_rows(y_ref, _dot(((up + 1.0) * glu).astype(BF16), wd_s[...]) + bd_ref[0])

    active = i < nb_ref[0]
    both_halves = full_ref[i] == 1

    @pl.when(active & both_halves)
    def _():
        expert_mlp(EXPERT_BLOCK)

    @pl.when(active & jnp.logical_not(both_halves))
    def _():
        expert_mlp(EXPERT_HALF)
        y_ref[EXPERT_HALF * ROW_SUB:, :] = jnp.zeros((EXPERT_HALF * ROW_SUB, LANES), F32)

    @pl.when(jnp.logical_not(active))
    def _():
        y_ref[...] = jnp.zeros_like(y_ref)


def _moe(block_expert, next_expert, slot_parity, both_halves, n_active, x_rows, w_gate_up, bg, bu, w_down, bd):
    n_blocks = block_expert.shape[0]
    _, d, f2 = w_gate_up.shape
    f = f2 // 2
    blk8 = EXPERT_BLOCK * ROW_SUB
    rows_in = pl.BlockSpec((blk8, LANES),
                           lambda i, be, nx, pr, fl, nb: (jnp.maximum(jnp.minimum(i, nb[0] - 1), 0), 0))
    rows_out = pl.BlockSpec((blk8, LANES), lambda i, be, nx, pr, fl, nb: (i, 0))
    per_expert = lambda shape: pl.BlockSpec((1,) + shape, lambda i, be, nx, pr, fl, nb: (be[i], 0, 0))
    hbm = pl.BlockSpec(memory_space=pl.ANY)
    grid_spec = pltpu.PrefetchScalarGridSpec(
        num_scalar_prefetch=5,
        grid=(n_blocks,),
        in_specs=[rows_in, per_expert((1, f)), per_expert((1, f)), per_expert((1, d)), hbm, hbm],
        out_specs=rows_out,
        scratch_shapes=[pltpu.VMEM((2, d, f2), F32), pltpu.VMEM((2, f, d), F32),
                        pltpu.VMEM((d, f), BF16), pltpu.VMEM((d, f), BF16), pltpu.VMEM((f, d), BF16),
                        pltpu.SemaphoreType.DMA((2,)), pltpu.SemaphoreType.DMA((2,))],
    )
    return pl.pallas_call(
        _moe_kernel,
        grid_spec=grid_spec,
        out_shape=jax.ShapeDtypeStruct((n_blocks * blk8, LANES), F32),
        compiler_params=pltpu.CompilerParams(
            dimension_semantics=("arbitrary",), vmem_limit_bytes=VMEM_LIMIT),
        name="moe",
    )(block_expert, next_expert, slot_parity, both_halves, n_active, x_rows, bg, bu, bd, w_gate_up, w_down)


def _combine_kernel(dest_cur_ref, dest_nxt_ref, x1_ref, mod_ref, route_ref, ln_g_ref, ln_b_ref, y_hbm,
                    o_ref, ybuf, sem):
    i = pl.program_id(0)
    n_steps = pl.num_programs(0)
    tile = x1_ref.shape[0]
    slot = lax.rem(i, 2)

    def gather(dest_ref, s):
        def body(r, carry):
            for kk in range(TOP_K):
                pltpu.make_async_copy(_row_at(y_hbm, dest_ref[0, 0, kk * tile + r]),
                                      _row_at(ybuf.at[s, kk], r * ROW_SUB), sem.at[s]).start(priority=kk % 2)
            return carry
        lax.fori_loop(0, tile, body, 0, unroll=DMA_UNROLL)

    @pl.when(i == 0)
    def _():
        gather(dest_cur_ref, 0)

    def wait_slot(s):
        for kk in range(TOP_K):
            pltpu.make_async_copy(y_hbm.at[pl.ds(0, tile * ROW_SUB)], ybuf.at[s, kk], sem.at[s]).wait()

    for r in range(tile):
        for kk in range(TOP_K):
            pltpu.make_async_copy(_row_at(y_hbm, dest_nxt_ref[0, 0, kk * tile + r]),
                                  _row_at(ybuf.at[1 - slot, kk], r * ROW_SUB),
                                  sem.at[1 - slot]).start(priority=kk % 2)
    wait_slot(slot)

    g_f = mod_ref[0][5:6]
    weight = route_ref[...]
    y = jnp.zeros(x1_ref.shape, F32)
    for kk in range(TOP_K):
        y = y + weight[:, kk:kk + 1] * _load_rows(ybuf, tile, (slot, kk))
    o_ref[...] = _layer_norm(DEEPNORM_ALPHA * x1_ref[...] + g_f * y, ln_g_ref[...], ln_b_ref[...])

    @pl.when(i == n_steps - 1)
    def _():
        wait_slot(1 - slot)


def _combine(dest8, x1, mod, route, ln_g, ln_b, y_rows, seq):
    n_tok, d = x1.shape
    tile = COMBINE_TILE
    n_t = n_tok // tile
    per_seq = seq // tile
    dest3 = _per_tile(dest8, tile)
    smem_blk = lambda fn: pl.BlockSpec((1, 1, tile * TOP_K), fn, memory_space=pltpu.SMEM)
    row = lambda width: pl.BlockSpec((tile, width), lambda i: (i, 0))
    vec = pl.BlockSpec((1, d), lambda i: (0, 0))
    return pl.pallas_call(
        _combine_kernel,
        grid=(n_t,),
        in_specs=[smem_blk(lambda i: (i, 0, 0)), smem_blk(lambda i: (jnp.minimum(i + 1, n_t - 1), 0, 0)),
                  row(d), pl.BlockSpec((1, 6, d), lambda i: (i // per_seq, 0, 0)), row(TOP_K), vec, vec,
                  pl.BlockSpec(memory_space=pl.ANY)],
        out_specs=row(d),
        out_shape=jax.ShapeDtypeStruct((n_tok, d), F32),
        scratch_shapes=[pltpu.VMEM((2, TOP_K, tile * ROW_SUB, LANES), F32), pltpu.SemaphoreType.DMA((2,))],
        compiler_params=pltpu.CompilerParams(
            dimension_semantics=("arbitrary",), vmem_limit_bytes=VMEM_LIMIT),
        name="combine",
    )(dest3, dest3, x1, mod, route, ln_g, ln_b, y_rows)


def _split_in_proj(w_in):
    rank_lo = _OFF[5]
    rank_hi = rank_lo + GLA_GATE_RANK
    pad = jnp.zeros((w_in.shape[0], RANK_PAD - GLA_GATE_RANK), BF16)
    return (w_in[:, :rank_lo].astype(BF16),
            jnp.concatenate([w_in[:, rank_lo:rank_hi].astype(BF16), pad], axis=1),
            w_in[:, rank_hi:].astype(BF16))


def _layer(x, c, w_ada, b_ada, w_in, w_pool_group, pool_scale, w_branch_a, w_alpha_up, b_alpha,
           gla_norm_gain, w_branch_b, w_out, ln1_gain, ln1_bias, w_router, b_router,
           w_gate_up, b_gate_up, w_down, b_down, ln2_gain, ln2_bias):
    bsz, seq, d = x.shape
    n_tok = bsz * seq
    n_assign = n_tok * TOP_K
    row2 = lambda v: v.reshape(1, -1)

    mod = _ada(c, w_ada, b_ada).reshape(bsz, 6, d)

    w_in_p, w_rank_p, w_gates = _split_in_proj(w_in)
    w_al_p = jnp.concatenate(
        [w_alpha_up, jnp.zeros((RANK_PAD - GLA_GATE_RANK, GLA_KEY_DIM), w_alpha_up.dtype)], axis=0).astype(BF16)
    w_r_t = w_router.T
    w_r_hi = w_r_t.astype(BF16)
    w_r_split = jnp.concatenate([w_r_hi, (w_r_t - w_r_hi.astype(F32)).astype(BF16)], axis=0)
    x1, u2_rows, route_t, counts = _mixer(
        x, mod, w_in_p, w_rank_p, w_gates, w_pool_group.astype(BF16), row2(pool_scale),
        w_branch_a.astype(BF16), w_al_p,
        row2(b_alpha), row2(gla_norm_gain), w_branch_b.astype(BF16), w_out.astype(BF16), row2(ln1_gain),
        row2(ln1_bias), w_r_split, b_router.reshape(N_EXPERTS, 1))

    top_idx = route_t[0:TOP_K].astype(jnp.int32)
    rank = route_t[TOP_K:2 * TOP_K].astype(jnp.int32)
    route = route_t[2 * TOP_K:3 * TOP_K].T
    counts = counts[:, 0].astype(jnp.int32)
    padded = (counts + EXPERT_BLOCK - 1) // EXPERT_BLOCK * EXPERT_BLOCK
    pad_end = jnp.cumsum(padded)
    pad_start = pad_end - padded
    n_rows = (n_assign + N_EXPERTS * (EXPERT_BLOCK - 1) + EXPERT_BLOCK - 1) // EXPERT_BLOCK * EXPERT_BLOCK
    n_blocks = n_rows // EXPERT_BLOCK
    n_active = (pad_end[-1] // EXPERT_BLOCK).astype(jnp.int32)
    start_of = jnp.sum(jnp.where(top_idx[..., None] == jnp.arange(N_EXPERTS, dtype=jnp.int32), pad_start, 0),
                       axis=-1)
    dest8 = (start_of + rank) * ROW_SUB
    blk_row = jnp.arange(n_blocks, dtype=jnp.int32)[:, None] * EXPERT_BLOCK
    block_expert = jnp.minimum(jnp.sum(pad_end[None, :] <= blk_row, axis=1), N_EXPERTS - 1).astype(jnp.int32)

    fill_table = jnp.concatenate([pad_start + counts, padded - counts, n_active.reshape(1)]).astype(jnp.int32)
    x_rows = _dispatch(fill_table, dest8, u2_rows, n_rows)
    f = w_down.shape[1]
    bg = b_gate_up[:, 0::2].reshape(N_EXPERTS, 1, f)
    bu = b_gate_up[:, 1::2].reshape(N_EXPERTS, 1, f)
    eid = jnp.arange(N_EXPERTS, dtype=jnp.int32)
    owns = counts > 0
    later = jnp.where((eid[None, :] > eid[:, None]) & owns[None, :], eid[None, :], N_EXPERTS)
    next_of = jnp.min(later, axis=1)
    next_of = jnp.where(next_of == N_EXPERTS, -1, next_of).astype(jnp.int32)
    parity_of = ((jnp.cumsum(owns.astype(jnp.int32)) - 1) % 2).astype(jnp.int32)
    is_block_expert = block_expert[:, None] == eid[None, :]
    per_block = lambda v: jnp.sum(jnp.where(is_block_expert, v[None, :], 0), axis=1).astype(jnp.int32)
    left = per_block(pad_start + counts) - blk_row[:, 0]
    both_halves = (left > EXPERT_HALF).astype(jnp.int32)
    y_rows = _moe(block_expert, per_block(next_of), per_block(parity_of), both_halves, n_active.reshape(1),
                  x_rows, w_gate_up, bg, bu, w_down, b_down.reshape(N_EXPERTS, 1, d))
    out = _combine(dest8, x1.reshape(n_tok, d), mod, route, row2(ln2_gain), row2(ln2_bias), y_rows, seq)
    return out.reshape(bsz, seq, d)


def kernel(x, c, w_ada, b_ada, w_in, w_pool_group, pool_scale, w_branch_a, w_alpha_up, b_alpha, gla_norm_gain,
           w_branch_b, w_out, ln1_gain, ln1_bias, w_router, b_router, w_gate_up, b_gate_up, w_down, b_down,
           ln2_gain, ln2_bias):
    for l in range(DEPTH):
        x = _layer(x, c, w_ada[l], b_ada[l], w_in[l], w_pool_group[l], pool_scale[l], w_branch_a[l],
                   w_alpha_up[l], b_alpha[l], gla_norm_gain[l], w_branch_b[l], w_out[l], ln1_gain[l],
                   ln1_bias[l], w_router[l], b_router[l], w_gate_up[l], b_gate_up[l], w_down[l], b_down[l],
                   ln2_gain[l], ln2_bias[l])
    return x
```

```python
import functools

import jax
import jax.numpy as jnp
from jax import lax
from jax.experimental import pallas as pl
from jax.experimental.pallas import tpu as pltpu

D_MODEL = 1024
CHUNK = 64
SUB = 16
N_SUB = CHUNK // SUB
POOL_WINDOWS = (2, 4, 8, 16)
POOL_GROUP_WIDTH = D_MODEL // len(POOL_WINDOWS)
POOL_HALO = 16
GLA_HEADS = 4
GLA_KEY_DIM = D_MODEL // 2
GLA_HEAD_K = GLA_KEY_DIM // GLA_HEADS
GLA_HEAD_V = D_MODEL // GLA_HEADS
GLA_GATE_RANK = 16
GLA_TAU = 16.0
N_EXPERTS = 32
TOP_K = 4
SWIGLU_ALPHA = 1.702
SWIGLU_LIMIT = 7.0
EXPERT_BLOCK = 512
EXPERT_HALF = EXPERT_BLOCK // 2
LN_EPS = 1e-5
RMS_EPS = 1e-6
DEPTH = 1
DEEPNORM_ALPHA = (2.0 * DEPTH) ** 0.25

LANES = 128
MXU_COLS = 256
RANK_PAD = LANES
_W = (D_MODEL, GLA_KEY_DIM, GLA_KEY_DIM, D_MODEL, D_MODEL, RANK_PAD, D_MODEL, D_MODEL)
_OFF = tuple(sum(_W[:i]) for i in range(len(_W) + 1))
EXP_CAP = 60.0
SAFE_SUB_DECAY = 40.0
ROUTE_ROWS = 16

SEQ_TILE = 512
DISPATCH_TILE = 2048
COMBINE_TILE = 256
COMBINE_PIECES = 8
DMA_UNROLL = 8
VMEM_LIMIT = 60 * 1024 * 1024

F32 = jnp.float32
BF16 = jnp.bfloat16
HI = lax.Precision.HIGHEST


def _dot(a, b):
    return jnp.dot(a, b, preferred_element_type=F32)


def _dot_nt(a, b):
    return lax.dot_general(a, b, (((1,), (1,)), ((), ())), preferred_element_type=F32)


def _dot_tn(a, b):
    return lax.dot_general(a, b, (((0,), (0,)), ((), ())), preferred_element_type=F32)


def _sigmoid(v):
    return 0.5 * jnp.tanh(0.5 * v) + 0.5


ROW_SUB = D_MODEL // LANES


def _load_rows(ref, n_rows, lead=(), first=0):
    return jnp.concatenate(
        [ref[lead + (pl.ds(first * ROW_SUB + j, n_rows, stride=ROW_SUB), slice(None))] for j in range(ROW_SUB)],
        axis=1)


def _store_rows(ref, val):
    for j in range(ROW_SUB):
        ref[pl.ds(j, val.shape[0], stride=ROW_SUB), :] = val[:, j * LANES:(j + 1) * LANES]


def _ada_kernel(c_ref, w_ref, b_ref, o_ref):
    c = c_ref[...]
    s = c * jax.nn.sigmoid(c)
    o_ref[...] = jnp.dot(s, w_ref[...], precision=HI, preferred_element_type=F32) + b_ref[...]


def _ada(c, w_ada, b_ada):
    bsz, d = c.shape
    n = w_ada.shape[1]
    tn = 1024
    return pl.pallas_call(
        _ada_kernel,
        grid=(n // tn,),
        in_specs=[
            pl.BlockSpec((bsz, d), lambda j: (0, 0)),
            pl.BlockSpec((d, tn), lambda j: (0, j)),
            pl.BlockSpec((1, tn), lambda j: (0, j)),
        ],
        out_specs=pl.BlockSpec((bsz, tn), lambda j: (0, j)),
        out_shape=jax.ShapeDtypeStruct((bsz, n), F32),
        name="ada",
    )(c, w_ada, b_ada.reshape(1, n))


def _layer_norm(z, gain, bias):
    mu = jnp.mean(z, axis=-1, keepdims=True)
    zc = z - mu
    var = jnp.mean(zc * zc, axis=-1, keepdims=True)
    return zc * lax.rsqrt(var + LN_EPS) * gain + bias


def _mixer_kernel(x_ref, mod_ref, mod_prev_ref, w_in_ref, w_rank_ref, w_gates_ref, w_pool_ref, pool_scale_ref,
                  w_a_ref, w_al_ref,
                  b_al_ref, gain_ref, w_b_ref, w_out_ref, ln_g_ref, ln_b_ref, w_r_ref, b_r_ref,
                  x1_ref, u2_ref, route_ref, counts_ref,
                  a_ext, s_ref, cnt_ref, o_ref, z_ref, sc_ref, *, tiles_per_seq, n_tiles):
    j = pl.program_id(0)
    s_idx = lax.rem(jnp.minimum(j, n_tiles - 1), tiles_per_seq)
    tile = x_ref.shape[1]

    @pl.when(s_idx == 0)
    def _():
        s_ref[...] = jnp.zeros_like(s_ref)
        a_ext[0:POOL_HALO, :] = jnp.zeros((POOL_HALO, D_MODEL), F32)

    @pl.when(j == 0)
    def _():
        cnt_ref[...] = jnp.zeros_like(cnt_ref)
        z_ref[...] = jnp.zeros_like(z_ref)


    mod_prev = mod_prev_ref[0]
    x1 = _layer_norm(z_ref[...], ln_g_ref[...], ln_b_ref[...])
    x1_ref[0] = x1
    u2 = x1 * (1.0 + mod_prev[4:5]) + mod_prev[3:4]
    _store_rows(u2_ref, u2)
    u2_hi = u2.astype(BF16)
    u2_lo = (u2 - u2_hi.astype(F32)).astype(BF16)

    mod = mod_ref[0]
    sh_m, sc_m, g_m = mod[0:1], mod[1:2], mod[2:3]
    x = x_ref[0]
    u = (x * (1.0 + sc_m) + sh_m).astype(BF16)

    def proj(i):
        return _dot(u, w_in_ref[:, _OFF[i]:_OFF[i + 1]])

    gate_cols = ([(w_in_ref, _OFF[4] + jj * MXU_COLS) for jj in range(D_MODEL // MXU_COLS)]
                 + [(w_gates_ref, jj * MXU_COLS) for jj in range(2 * D_MODEL // MXU_COLS)])
    fillers = [functools.partial(lambda w_ref, c0: _dot(u, w_ref[:, c0:c0 + MXU_COLS]), w_ref, c0)
               for w_ref, c0 in gate_cols]
    filled = []

    def issue_fillers(n):
        for _ in range(n):
            if len(filled) < len(fillers):
                filled.append(fillers[len(filled)]())

    a = proj(0)
    q = proj(1) * (GLA_HEAD_K ** -0.5)
    k_all = proj(2)
    v_all = proj(3)
    alpha_low = _dot(u, w_rank_ref[...])

    part = _dot_nt(w_r_ref[...], u2_hi)
    logits = (part[:N_EXPERTS] + part[N_EXPERTS:] + _dot_nt(w_r_ref[0:N_EXPERTS, :], u2_lo)
              + b_r_ref[...])
    issue_fillers(2)

    a_ext[POOL_HALO:POOL_HALO + tile, :] = a
    t_glob = s_idx * tile + lax.broadcasted_iota(jnp.int32, (tile, 1), 0)
    mapped = []
    for g, w in enumerate(POOL_WINDOWS):
        lo, hi = g * POOL_GROUP_WIDTH, (g + 1) * POOL_GROUP_WIDTH
        win = a_ext[:, lo:hi]
        k = 1
        while k < w:
            win = win + pltpu.roll(win, k, 0)
            k *= 2
        inv_cnt = 1.0 / jnp.minimum(t_glob + 1, w).astype(F32)
        pooled = win[POOL_HALO:, :] * inv_cnt - a[:, lo:hi]
        mapped.append(_dot(pooled.astype(BF16), w_pool_ref[g]))
    a_ext[0:POOL_HALO, :] = a[tile - POOL_HALO:tile, :]
    ya = _dot((jnp.concatenate(mapped, axis=1) * pool_scale_ref[...]).astype(BF16), w_a_ref[...])

    erow = lax.broadcasted_iota(jnp.int32, (N_EXPERTS, tile), 0).astype(F32)
    work = logits
    sel = jnp.zeros((N_EXPERTS, tile), F32)
    vals, hits = [], []
    for _ in range(TOP_K):
        m = jnp.max(work, axis=0, keepdims=True)
        idx = jnp.min(jnp.where(work == m, erow, float(N_EXPERTS)), axis=0, keepdims=True)
        hit = erow == idx
        vals.append(m)
        hits.append((idx, hit))
        sel = jnp.where(hit, 1.0, sel)
        work = jnp.where(hit, -jnp.inf, work)
    exps = [jnp.exp(v - vals[0]) for v in vals]
    inv_den = 1.0 / (exps[0] + exps[1] + exps[2] + exps[3])
    issue_fillers(1)

    z = _dot(alpha_low.astype(BF16), w_al_ref[...]) + b_al_ref[...]
    log_a = (jnp.minimum(z, 0.0) - jnp.log1p(jnp.exp(-jnp.abs(z)))) * (1.0 / GLA_TAU)

    ri = lax.broadcasted_iota(jnp.int32, (CHUNK, CHUNK), 0)
    ci = lax.broadcasted_iota(jnp.int32, (CHUNK, CHUNK), 1)
    causal = ci <= ri
    cum_mat = causal.astype(BF16)
    la_hi = log_a.astype(BF16)
    la_split = jnp.concatenate([la_hi, (log_a - la_hi.astype(F32)).astype(BF16)], axis=1)

    n_chunks = tile // CHUNK
    pairs = [(c, h) for c in range(n_chunks) for h in range(GLA_HEADS)]
    rows_of = lambda c: slice(c * CHUNK, (c + 1) * CHUNK)
    ks_of = lambda h: slice(h * GLA_HEAD_K, (h + 1) * GLA_HEAD_K)
    vs_of = lambda h: slice(h * GLA_HEAD_V, (h + 1) * GLA_HEAD_V)

    b_cum, ref_pts, b_ref_pt = [], [], []
    for c in range(n_chunks):
        cum = _dot(cum_mat, la_split[rows_of(c)])
        b_cum.append(cum[:, :GLA_KEY_DIM] + cum[:, GLA_KEY_DIM:])
        ref_pts.append([jnp.zeros((1, GLA_KEY_DIM), F32)]
                       + [b_cum[c][i * SUB - 1:i * SUB, :] for i in range(1, N_SUB)])
        b_ref_pt.append(jnp.concatenate([jnp.broadcast_to(p, (SUB, GLA_KEY_DIM)) for p in ref_pts[c]], axis=0))

    rt = lax.broadcasted_iota(jnp.int32, (tile, tile), 0)
    ct = lax.broadcasted_iota(jnp.int32, (tile, tile), 1)
    base = _dot(sel.astype(BF16), (rt < ct).astype(BF16)) + cnt_ref[:, 0:1]

    for n_pair, (c, h) in enumerate(pairs):
        if n_pair % 3 == 0:
            issue_fillers(1)
        qh, kh, bh = q[rows_of(c), ks_of(h)], k_all[rows_of(c), ks_of(h)], b_cum[c][:, ks_of(h)]
        q_dec = (qh * jnp.exp(bh - b_ref_pt[c][:, ks_of(h)])).astype(BF16)
        k_dec = jnp.concatenate(
            [(kh * jnp.exp(jnp.minimum(ref_pts[c][i][:, ks_of(h)] - bh, EXP_CAP))).astype(BF16)
             for i in range(N_SUB)], axis=0)
        s_all = _dot_nt(q_dec, k_dec)
        sc_ref[n_pair] = jnp.concatenate(
            [s_all[i * SUB:(i + 1) * SUB, i * CHUNK:(i + 1) * CHUNK] for i in range(N_SUB)], axis=0)

    sub_decay = b_ref_pt[0] - b_cum[0]
    for c in range(1, n_chunks):
        sub_decay = jnp.maximum(sub_decay, b_ref_pt[c] - b_cum[c])

    @pl.when(jnp.max(sub_decay) > SAFE_SUB_DECAY)
    def _():
        o_ref[:, 0:GLA_KEY_DIM] = q
        o_ref[:, GLA_KEY_DIM:2 * GLA_KEY_DIM] = jnp.concatenate(b_cum, axis=0)
        for n_pair, (c, h) in enumerate(pairs):
            kh, bh = k_all[rows_of(c), ks_of(h)], b_cum[c][:, ks_of(h)]

            def score_rows(g, carry):
                first = pl.multiple_of(g * 8, 8)
                q_8 = o_ref[pl.ds(c * CHUNK + first, 8), ks_of(h)]
                b_8 = o_ref[pl.ds(c * CHUNK + first, 8),
                            GLA_KEY_DIM + h * GLA_HEAD_K:GLA_KEY_DIM + (h + 1) * GLA_HEAD_K]
                rows = []
                for r in range(8):
                    k_i = (kh * jnp.exp(jnp.minimum(b_8[r:r + 1] - bh, 0.0))).astype(BF16)
                    q_i = jnp.broadcast_to(q_8[r:r + 1], (8, GLA_HEAD_K)).astype(BF16)
                    rows.append(_dot_nt(q_i, k_i)[0:1])
                sc_ref[n_pair, pl.ds(first, 8), :] = jnp.concatenate(rows, axis=0)
                return carry

            lax.fori_loop(0, CHUNK // 8, score_rows, 0)

    o_intra, kv, decay_last = {}, {}, {}
    for n_pair, (c, h) in enumerate(pairs):
        kh, bh = k_all[rows_of(c), ks_of(h)], b_cum[c][:, ks_of(h)]
        vh = v_all[rows_of(c), vs_of(h)].astype(BF16)
        o_intra[c, h] = _dot(jnp.where(causal, sc_ref[n_pair], 0.0).astype(BF16), vh)
        b_last = bh[CHUNK - 1:CHUNK, :]
        kv[c, h] = _dot_tn(vh, (kh * jnp.exp(b_last - bh)).astype(BF16))
        decay_last[c, h] = jnp.exp(b_last)

    orow = lax.broadcasted_iota(jnp.int32, (ROUTE_ROWS, tile), 0)
    route = jnp.zeros((ROUTE_ROWS, tile), F32)
    for kk in range(TOP_K):
        idx, hit = hits[kk]
        rank = jnp.sum(jnp.where(hit, base, 0.0), axis=0, keepdims=True)
        route = jnp.where(orow == kk, idx, route)
        route = jnp.where(orow == TOP_K + kk, rank, route)
        route = jnp.where(orow == 2 * TOP_K + kk, exps[kk] * inv_den, route)
    route_ref[...] = route
    cnt_ref[...] = cnt_ref[...] + (j > 0).astype(F32) * jnp.sum(sel, axis=1, keepdims=True)
    counts_ref[...] = cnt_ref[...]

    o_inter = {}
    state_t = [s_ref[h] for h in range(GLA_HEADS)]
    for c, h in pairs:
        q_in = (q[rows_of(c), ks_of(h)] * jnp.exp(b_cum[c][:, ks_of(h)])).astype(BF16)
        o_inter[c, h] = _dot_nt(q_in, state_t[h].astype(BF16))
        state_t[h] = state_t[h] * decay_last[c, h] + kv[c, h]
    for h in range(GLA_HEADS):
        s_ref[h] = state_t[h]

    for c, h in pairs:
        o = o_intra[c, h] + o_inter[c, h]
        o = o * lax.rsqrt(jnp.mean(o * o, axis=-1, keepdims=True) + RMS_EPS) * gain_ref[...]
        o_ref[rows_of(c), vs_of(h)] = o

    issue_fillers(len(fillers))
    per_proj = D_MODEL // MXU_COLS
    r, gate_a, gate_b = (jnp.concatenate(filled[n * per_proj:(n + 1) * per_proj], axis=1) for n in range(3))
    yb = _dot((o_ref[...] * (r * _sigmoid(r))).astype(BF16), w_b_ref[...])

    merged = _sigmoid(gate_a) * ya + _sigmoid(gate_b) * yb
    y = _dot(merged.astype(BF16), w_out_ref[...])
    z_ref[...] = DEEPNORM_ALPHA * x + g_m * y


def _mixer(x, mod, w_in_p, w_rank_p, w_gates, w_pool, pool_scale, w_a, w_al_p, b_alpha, gain, w_b, w_out,
           ln_g, ln_b, w_r_p, b_r_p):
    bsz, seq, d = x.shape
    tile = SEQ_TILE
    n_s = seq // tile
    n_tiles = bsz * n_s

    def const(shape):
        nd = len(shape)
        return pl.BlockSpec(shape, lambda j: (0,) * nd, pipeline_mode=pl.Buffered(1))

    cur = lambda j: jnp.minimum(j, n_tiles - 1)
    prev = lambda j: jnp.maximum(j - 1, 0)
    return pl.pallas_call(
        functools.partial(_mixer_kernel, tiles_per_seq=n_s, n_tiles=n_tiles),
        grid=(n_tiles + 1,),
        in_specs=[
            pl.BlockSpec((1, tile, d), lambda j: (cur(j) // n_s, cur(j) % n_s, 0)),
            pl.BlockSpec((1, 6, d), lambda j: (cur(j) // n_s, 0, 0)),
            pl.BlockSpec((1, 6, d), lambda j: (prev(j) // n_s, 0, 0)),
            const(w_in_p.shape), const(w_rank_p.shape), const(w_gates.shape),
            const(w_pool.shape), const(pool_scale.shape), const(w_a.shape),
            const(w_al_p.shape), const(b_alpha.shape), const(gain.shape), const(w_b.shape),
            const(w_out.shape), const(ln_g.shape), const(ln_b.shape), const(w_r_p.shape),
            const(b_r_p.shape),
        ],
        out_specs=[pl.BlockSpec((1, tile, d), lambda j: (prev(j) // n_s, prev(j) % n_s, 0)),
                   pl.BlockSpec((tile * ROW_SUB, LANES), lambda j: (prev(j), 0)),
                   pl.BlockSpec((ROUTE_ROWS, tile), lambda j: (0, prev(j))),
                   pl.BlockSpec((N_EXPERTS, LANES), lambda j: (0, 0))],
        out_shape=[
            jax.ShapeDtypeStruct((bsz, seq, d), F32),
            jax.ShapeDtypeStruct((bsz * seq * ROW_SUB, LANES), F32),
            jax.ShapeDtypeStruct((ROUTE_ROWS, bsz * seq), F32),
            jax.ShapeDtypeStruct((N_EXPERTS, LANES), F32),
        ],
        scratch_shapes=[
            pltpu.VMEM((POOL_HALO + tile, d), F32),
            pltpu.VMEM((GLA_HEADS, GLA_HEAD_V, GLA_HEAD_K), F32),
            pltpu.VMEM((N_EXPERTS, LANES), F32),
            pltpu.VMEM((tile, d), F32),
            pltpu.VMEM((tile, d), F32),
            pltpu.VMEM((tile // CHUNK * GLA_HEADS, CHUNK, CHUNK), F32),
        ],
        compiler_params=pltpu.CompilerParams(
            dimension_semantics=("arbitrary",), vmem_limit_bytes=VMEM_LIMIT),
        name="mixer",
    )(x, mod, mod, w_in_p, w_rank_p, w_gates, w_pool, pool_scale, w_a, w_al_p, b_alpha, gain, w_b, w_out,
      ln_g, ln_b, w_r_p, b_r_p)


def _row_at(ref, row8):
    return ref.at[pl.ds(pl.multiple_of(row8, ROW_SUB), ROW_SUB)]


def _per_tile(dest8, tile):
    n_t = dest8.shape[1] // tile
    return dest8.reshape(TOP_K, n_t, tile).transpose(1, 0, 2).reshape(n_t, 1, TOP_K * tile)


def _dispatch_kernel(fill_ref, dest_ref, u2_ref, rows_hbm, zbuf, sem_fill, sem_rows):
    i = pl.program_id(0)
    tile = u2_ref.shape[0] // ROW_SUB
    blk8 = EXPERT_BLOCK * ROW_SUB

    def zero_fills(phase):
        def piece(first_row, n_rows):
            dst = rows_hbm.at[pl.ds(pl.multiple_of(first_row * ROW_SUB, ROW_SUB), n_rows * ROW_SUB)]
            getattr(pltpu.make_async_copy(zbuf.at[pl.ds(0, n_rows * ROW_SUB)], dst, sem_fill.at[0]), phase)()

        for e in range(N_EXPERTS):
            first_row, gap = fill_ref[e], fill_ref[N_EXPERTS + e]
            n_rows = EXPERT_BLOCK // 2
            while n_rows >= 1:
                @pl.when(jnp.bitwise_and(gap, n_rows) != 0)
                def _(first_row=first_row, n_rows=n_rows):
                    piece(first_row, n_rows)
                first_row = first_row + jnp.bitwise_and(gap, n_rows)
                n_rows //= 2

        def unused(window, carry):
            piece(window * EXPERT_BLOCK, EXPERT_BLOCK)
            return carry

        lax.fori_loop(fill_ref[2 * N_EXPERTS], rows_hbm.shape[0] // blk8, unused, 0)

    @pl.when(i == 0)
    def _():
        zbuf[...] = jnp.zeros_like(zbuf)
        zero_fills("start")

    def body(r, carry):
        for kk in range(TOP_K):
            pltpu.make_async_copy(_row_at(u2_ref, r * ROW_SUB), _row_at(rows_hbm, dest_ref[0, 0, kk * tile + r]),
                                  sem_rows.at[0]).start(priority=kk % 2)
        return carry

    lax.fori_loop(0, tile, body, 0, unroll=DMA_UNROLL)
    for _ in range(TOP_K):
        pltpu.make_async_copy(u2_ref, rows_hbm.at[pl.ds(0, tile * ROW_SUB)], sem_rows.at[0]).wait()

    @pl.when(i == pl.num_programs(0) - 1)
    def _():
        zero_fills("wait")


def _dispatch(fill_start, dest8, u2_rows, n_rows):
    n_tok = u2_rows.shape[0] // ROW_SUB
    tile = DISPATCH_TILE
    n_t = n_tok // tile
    grid_spec = pltpu.PrefetchScalarGridSpec(
        num_scalar_prefetch=1,
        grid=(n_t,),
        in_specs=[
            pl.BlockSpec((1, 1, tile * TOP_K), lambda i, fs: (i, 0, 0), memory_space=pltpu.SMEM),
            pl.BlockSpec((tile * ROW_SUB, LANES), lambda i, fs: (i, 0)),
        ],
        out_specs=pl.BlockSpec(memory_space=pl.ANY),
        scratch_shapes=[
            pltpu.VMEM((EXPERT_BLOCK * ROW_SUB, LANES), F32),
            pltpu.SemaphoreType.DMA((1,)),
            pltpu.SemaphoreType.DMA((1,)),
        ],
    )
    return pl.pallas_call(
        _dispatch_kernel,
        grid_spec=grid_spec,
        out_shape=jax.ShapeDtypeStruct(((n_rows + EXPERT_BLOCK) * ROW_SUB, LANES), F32),
        compiler_params=pltpu.CompilerParams(dimension_semantics=("arbitrary",)),
        name="dispatch",
    )(fill_start, _per_tile(dest8, tile), u2_rows)


def _moe_kernel(be_ref, nxt_ref, par_ref, full_ref, nb_ref, x_ref, bg_ref, bu_ref, bd_ref, wgu_hbm, wd_hbm, y_ref,
                wgu_buf, wd_buf, wg_s, wu_s, wd_s, sem_gu, sem_d):
    i = pl.program_id(0)
    new_expert = (i == 0) | (be_ref[i] != be_ref[jnp.maximum(i - 1, 0)])

    def fetch(e, s):
        return (pltpu.make_async_copy(wgu_hbm.at[e], wgu_buf.at[s], sem_gu.at[s]),
                pltpu.make_async_copy(wd_hbm.at[e], wd_buf.at[s], sem_d.at[s]))

    @pl.when(new_expert & (i < nb_ref[0]))
    def _():
        slot = par_ref[i]

        @pl.when(i == 0)
        def _():
            for cp in fetch(be_ref[0], slot):
                cp.start()

        @pl.when(nxt_ref[i] >= 0)
        def _():
            for cp in fetch(nxt_ref[i], 1 - slot):
                cp.start()

        for cp in fetch(be_ref[i], slot):
            cp.wait()
        src = lax.broadcasted_iota(jnp.int32, (MXU_COLS, MXU_COLS), 0)
        col = lax.broadcasted_iota(jnp.int32, (MXU_COLS, MXU_COLS), 1)
        half = MXU_COLS // 2
        want = jnp.where(col < half, 2 * col, 2 * (col - half) + 1)
        unzip = (src == want).astype(BF16)
        for g in range(wgu_buf.shape[2] // MXU_COLS):
            blk = wgu_buf[slot, :, g * MXU_COLS:(g + 1) * MXU_COLS].astype(BF16)
            sep = _dot(blk, unzip)
            wg_s[:, g * half:(g + 1) * half] = sep[:, :half].astype(BF16)
            wu_s[:, g * half:(g + 1) * half] = sep[:, half:].astype(BF16)
        wd_s[...] = wd_buf[slot].astype(BF16)

    def expert_mlp(n_rows):
        xb = _load_rows(x_ref, n_rows).astype(BF16)
        gate = jnp.minimum(_dot(xb, wg_s[...]) + bg_ref[0], SWIGLU_LIMIT)
        up = jnp.clip(_dot(xb, wu_s[...]) + bu_ref[0], -SWIGLU_LIMIT, SWIGLU_LIMIT)
        glu = gate * _sigmoid(gate * SWIGLU_ALPHA)
        _store_rows(y_ref, _dot(((up + 1.0) * glu).astype(BF16), wd_s[...]) + bd_ref[0])

    active = i < nb_ref[0]
    both_halves = full_ref[i] == 1

    @pl.when(active & both_halves)
    def _():
        expert_mlp(EXPERT_BLOCK)

    @pl.when(active & jnp.logical_not(both_halves))
    def _():
        expert_mlp(EXPERT_HALF)
        y_ref[EXPERT_HALF * ROW_SUB:, :] = jnp.zeros((EXPERT_HALF * ROW_SUB, LANES), F32)

    @pl.when(jnp.logical_not(active))
    def _():
        y_ref[...] = jnp.zeros_like(y_ref)


def _moe(block_expert, next_expert, slot_parity, both_halves, n_active, x_rows, w_gate_up, bg, bu, w_down, bd):
    n_blocks = block_expert.shape[0]
    _, d, f2 = w_gate_up.shape
    f = f2 // 2
    blk8 = EXPERT_BLOCK * ROW_SUB
    rows_in = pl.BlockSpec((blk8, LANES),
                           lambda i, be, nx, pr, fl, nb: (jnp.maximum(jnp.minimum(i, nb[0] - 1), 0), 0))
    rows_out = pl.BlockSpec((blk8, LANES), lambda i, be, nx, pr, fl, nb: (i, 0))
    per_expert = lambda shape: pl.BlockSpec((1,) + shape, lambda i, be, nx, pr, fl, nb: (be[i], 0, 0))
    hbm = pl.BlockSpec(memory_space=pl.ANY)
    grid_spec = pltpu.PrefetchScalarGridSpec(
        num_scalar_prefetch=5,
        grid=(n_blocks,),
        in_specs=[rows_in, per_expert((1, f)), per_expert((1, f)), per_expert((1, d)), hbm, hbm],
        out_specs=rows_out,
        scratch_shapes=[pltpu.VMEM((2, d, f2), F32), pltpu.VMEM((2, f, d), F32),
                        pltpu.VMEM((d, f), BF16), pltpu.VMEM((d, f), BF16), pltpu.VMEM((f, d), BF16),
                        pltpu.SemaphoreType.DMA((2,)), pltpu.SemaphoreType.DMA((2,))],
    )
    return pl.pallas_call(
        _moe_kernel,
        grid_spec=grid_spec,
        out_shape=jax.ShapeDtypeStruct((n_blocks * blk8, LANES), F32),
        compiler_params=pltpu.CompilerParams(
            dimension_semantics=("arbitrary",), vmem_limit_bytes=VMEM_LIMIT),
        name="moe",
    )(block_expert, next_expert, slot_parity, both_halves, n_active, x_rows, bg, bu, bd, w_gate_up, w_down)


def _combine_kernel(dest_cur_ref, dest_nxt_ref, x1_ref, mod_ref, route_ref, ln_g_ref, ln_b_ref, y_hbm,
                    o_ref, ybuf, sem):
    i = pl.program_id(0)
    n_steps = pl.num_programs(0)
    tile = x1_ref.shape[0]
    slot = lax.rem(i, 2)

    def gather(dest_ref, s):
        def body(r, carry):
            for kk in range(TOP_K):
                pltpu.make_async_copy(_row_at(y_hbm, dest_ref[0, 0, kk * tile + r]),
                                      _row_at(ybuf.at[s, kk], r * ROW_SUB), sem.at[s]).start(priority=kk % 2)
            return carry
        lax.fori_loop(0, tile, body, 0, unroll=DMA_UNROLL)

    @pl.when(i == 0)
    def _():
        gather(dest_cur_ref, 0)

    def wait_slot(s):
        for kk in range(TOP_K):
            pltpu.make_async_copy(y_hbm.at[pl.ds(0, tile * ROW_SUB)], ybuf.at[s, kk], sem.at[s]).wait()

    wait_slot(slot)
    g_f = mod_ref[0][5:6]
    weight = route_ref[...]
    piece = tile // COMBINE_PIECES
    for p in range(COMBINE_PIECES):
        rows = slice(p * piece, (p + 1) * piece)
        y = jnp.zeros((piece, D_MODEL), F32)
        for kk in range(TOP_K):
            y = y + weight[rows, kk:kk + 1] * _load_rows(ybuf, piece, (slot, kk), first=p * piece)
        o_ref[rows, :] = _layer_norm(DEEPNORM_ALPHA * x1_ref[rows, :] + g_f * y, ln_g_ref[...], ln_b_ref[...])
        for r in range(p * piece, (p + 1) * piece):
            for kk in range(TOP_K):
                pltpu.make_async_copy(_row_at(y_hbm, dest_nxt_ref[0, 0, kk * tile + r]),
                                      _row_at(ybuf.at[1 - slot, kk], r * ROW_SUB),
                                      sem.at[1 - slot]).start(priority=kk % 2)

    @pl.when(i == n_steps - 1)
    def _():
        wait_slot(1 - slot)


def _combine(dest8, x1, mod, route, ln_g, ln_b, y_rows, seq):
    n_tok, d = x1.shape
    tile = COMBINE_TILE
    n_t = n_tok // tile
    per_seq = seq // tile
    dest3 = _per_tile(dest8, tile)
    smem_blk = lambda fn: pl.BlockSpec((1, 1, tile * TOP_K), fn, memory_space=pltpu.SMEM)
    row = lambda width: pl.BlockSpec((tile, width), lambda i: (i, 0))
    vec = pl.BlockSpec((1, d), lambda i: (0, 0))
    return pl.pallas_call(
        _combine_kernel,
        grid=(n_t,),
        in_specs=[smem_blk(lambda i: (i, 0, 0)), smem_blk(lambda i: (jnp.minimum(i + 1, n_t - 1), 0, 0)),
                  row(d), pl.BlockSpec((1, 6, d), lambda i: (i // per_seq, 0, 0)), row(TOP_K), vec, vec,
                  pl.BlockSpec(memory_space=pl.ANY)],
        out_specs=row(d),
        out_shape=jax.ShapeDtypeStruct((n_tok, d), F32),
        scratch_shapes=[pltpu.VMEM((2, TOP_K, tile * ROW_SUB, LANES), F32), pltpu.SemaphoreType.DMA((2,))],
        compiler_params=pltpu.CompilerParams(
            dimension_semantics=("arbitrary",), vmem_limit_bytes=VMEM_LIMIT),
        name="combine",
    )(dest3, dest3, x1, mod, route, ln_g, ln_b, y_rows)


def _split_in_proj(w_in):
    rank_lo = _OFF[5]
    rank_hi = rank_lo + GLA_GATE_RANK
    pad = jnp.zeros((w_in.shape[0], RANK_PAD - GLA_GATE_RANK), BF16)
    return (w_in[:, :rank_lo].astype(BF16),
            jnp.concatenate([w_in[:, rank_lo:rank_hi].astype(BF16), pad], axis=1),
            w_in[:, rank_hi:].astype(BF16))


def _layer(x, c, w_ada, b_ada, w_in, w_pool_group, pool_scale, w_branch_a, w_alpha_up, b_alpha,
           gla_norm_gain, w_branch_b, w_out, ln1_gain, ln1_bias, w_router, b_router,
           w_gate_up, b_gate_up, w_down, b_down, ln2_gain, ln2_bias):
    bsz, seq, d = x.shape
    n_tok = bsz * seq
    n_assign = n_tok * TOP_K
    row2 = lambda v: v.reshape(1, -1)

    mod = _ada(c, w_ada, b_ada).reshape(bsz, 6, d)

    w_in_p, w_rank_p, w_gates = _split_in_proj(w_in)
    w_al_p = jnp.concatenate(
        [w_alpha_up, jnp.zeros((RANK_PAD - GLA_GATE_RANK, GLA_KEY_DIM), w_alpha_up.dtype)], axis=0).astype(BF16)
    w_r_t = w_router.T
    w_r_hi = w_r_t.astype(BF16)
    w_r_split = jnp.concatenate([w_r_hi, (w_r_t - w_r_hi.astype(F32)).astype(BF16)], axis=0)
    x1, u2_rows, route_t, counts = _mixer(
        x, mod, w_in_p, w_rank_p, w_gates, w_pool_group.astype(BF16), row2(pool_scale),
        w_branch_a.astype(BF16), w_al_p,
        row2(b_alpha), row2(gla_norm_gain), w_branch_b.astype(BF16), w_out.astype(BF16), row2(ln1_gain),
        row2(ln1_bias), w_r_split, b_router.reshape(N_EXPERTS, 1))

    top_idx = route_t[0:TOP_K].astype(jnp.int32)
    rank = route_t[TOP_K:2 * TOP_K].astype(jnp.int32)
    route = route_t[2 * TOP_K:3 * TOP_K].T
    counts = counts[:, 0].astype(jnp.int32)
    padded = (counts + EXPERT_BLOCK - 1) // EXPERT_BLOCK * EXPERT_BLOCK
    pad_end = jnp.cumsum(padded)
    pad_start = pad_end - padded
    n_rows = (n_assign + N_EXPERTS * (EXPERT_BLOCK - 1) + EXPERT_BLOCK - 1) // EXPERT_BLOCK * EXPERT_BLOCK
    n_blocks = n_rows // EXPERT_BLOCK
    n_active = (pad_end[-1] // EXPERT_BLOCK).astype(jnp.int32)
    start_of = jnp.sum(jnp.where(top_idx[..., None] == jnp.arange(N_EXPERTS, dtype=jnp.int32), pad_start, 0),
                       axis=-1)
    dest8 = (start_of + rank) * ROW_SUB
    blk_row = jnp.arange(n_blocks, dtype=jnp.int32)[:, None] * EXPERT_BLOCK
    block_expert = jnp.minimum(jnp.sum(pad_end[None, :] <= blk_row, axis=1), N_EXPERTS - 1).astype(jnp.int32)

    fill_table = jnp.concatenate([pad_start + counts, padded - counts, n_active.reshape(1)]).astype(jnp.int32)
    x_rows = _dispatch(fill_table, dest8, u2_rows, n_rows)
    f = w_down.shape[1]
    bg = b_gate_up[:, 0::2].reshape(N_EXPERTS, 1, f)
    bu = b_gate_up[:, 1::2].reshape(N_EXPERTS, 1, f)
    eid = jnp.arange(N_EXPERTS, dtype=jnp.int32)
    owns = counts > 0
    later = jnp.where((eid[None, :] > eid[:, None]) & owns[None, :], eid[None, :], N_EXPERTS)
    next_of = jnp.min(later, axis=1)
    next_of = jnp.where(next_of == N_EXPERTS, -1, next_of).astype(jnp.int32)
    parity_of = ((jnp.cumsum(owns.astype(jnp.int32)) - 1) % 2).astype(jnp.int32)
    is_block_expert = block_expert[:, None] == eid[None, :]
    per_block = lambda v: jnp.sum(jnp.where(is_block_expert, v[None, :], 0), axis=1).astype(jnp.int32)
    left = per_block(pad_start + counts) - blk_row[:, 0]
    both_halves = (left > EXPERT_HALF).astype(jnp.int32)
    y_rows = _moe(block_expert, per_block(next_of), per_block(parity_of), both_halves, n_active.reshape(1),
                  x_rows, w_gate_up, bg, bu, w_down, b_down.reshape(N_EXPERTS, 1, d))
    out = _combine(dest8, x1.reshape(n_tok, d), mod, route, row2(ln2_gain), row2(ln2_bias), y_rows, seq)
    return out.reshape(bsz, seq, d)


def kernel(x, c, w_ada, b_ada, w_in, w_pool_group, pool_scale, w_branch_a, w_alpha_up, b_alpha, gla_norm_gain,
           w_branch_b, w_out, ln1_gain, ln1_bias, w_router, b_router, w_gate_up, b_gate_up, w_down, b_down,
           ln2_gain, ln2_bias):
    for l in range(DEPTH):
        x = _layer(x, c, w_ada[l], b_ada[l], w_in[l], w_pool_group[l], pool_scale[l], w_branch_a[l],
                   w_alpha_up[l], b_alpha[l], gla_norm_gain[l], w_branch_b[l], w_out[l], ln1_gain[l],
                   ln1_bias[l], w_router[l], b_router[l], w_gate_up[l], b_gate_up[l], w_down[l], b_down[l],
                   ln2_gain[l], ln2_bias[l])
    return x
```

```python
import functools

import jax
import jax.numpy as jnp
from jax import lax
from jax.experimental import pallas as pl
from jax.experimental.pallas import tpu as pltpu

D_MODEL = 1024
CHUNK = 64
SUB = 16
N_SUB = CHUNK // SUB
POOL_WINDOWS = (2, 4, 8, 16)
POOL_GROUP_WIDTH = D_MODEL // len(POOL_WINDOWS)
POOL_HALO = 16
GLA_HEADS = 4
GLA_KEY_DIM = D_MODEL // 2
GLA_HEAD_K = GLA_KEY_DIM // GLA_HEADS
GLA_HEAD_V = D_MODEL // GLA_HEADS
GLA_GATE_RANK = 16
GLA_TAU = 16.0
N_EXPERTS = 32
TOP_K = 4
SWIGLU_ALPHA = 1.702
SWIGLU_LIMIT = 7.0
EXPERT_BLOCK = 512
EXPERT_HALF = EXPERT_BLOCK // 2
LN_EPS = 1e-5
RMS_EPS = 1e-6
DEPTH = 1
DEEPNORM_ALPHA = (2.0 * DEPTH) ** 0.25

LANES = 128
MXU_COLS = 256
RANK_PAD = LANES
_W = (D_MODEL, GLA_KEY_DIM, GLA_KEY_DIM, D_MODEL, D_MODEL, RANK_PAD, D_MODEL, D_MODEL)
_OFF = tuple(sum(_W[:i]) for i in range(len(_W) + 1))
EXP_CAP = 60.0
SAFE_SUB_DECAY = 40.0
ROUTE_ROWS = 16

SEQ_TILE = 512
DISPATCH_TILE = 2048
COMBINE_TILE = 256
DMA_UNROLL = 8
VMEM_LIMIT = 60 * 1024 * 1024

F32 = jnp.float32
BF16 = jnp.bfloat16
HI = lax.Precision.HIGHEST


def _dot(a, b):
    return jnp.dot(a, b, preferred_element_type=F32)


def _dot_nt(a, b):
    return lax.dot_general(a, b, (((1,), (1,)), ((), ())), preferred_element_type=F32)


def _dot_tn(a, b):
    return lax.dot_general(a, b, (((0,), (0,)), ((), ())), preferred_element_type=F32)


def _sigmoid(v):
    return 0.5 * jnp.tanh(0.5 * v) + 0.5


ROW_SUB = D_MODEL // LANES


def _load_rows(ref, n_rows, lead=()):
    return jnp.concatenate(
        [ref[lead + (pl.ds(j, n_rows, stride=ROW_SUB), slice(None))] for j in range(ROW_SUB)], axis=1)


def _store_rows(ref, val):
    for j in range(ROW_SUB):
        ref[pl.ds(j, val.shape[0], stride=ROW_SUB), :] = val[:, j * LANES:(j + 1) * LANES]


def _ada_kernel(c_ref, w_ref, b_ref, o_ref):
    c = c_ref[...]
    s = c * jax.nn.sigmoid(c)
    o_ref[...] = jnp.dot(s, w_ref[...], precision=HI, preferred_element_type=F32) + b_ref[...]


def _ada(c, w_ada, b_ada):
    bsz, d = c.shape
    n = w_ada.shape[1]
    tn = 1024
    return pl.pallas_call(
        _ada_kernel,
        grid=(n // tn,),
        in_specs=[
            pl.BlockSpec((bsz, d), lambda j: (0, 0)),
            pl.BlockSpec((d, tn), lambda j: (0, j)),
            pl.BlockSpec((1, tn), lambda j: (0, j)),
        ],
        out_specs=pl.BlockSpec((bsz, tn), lambda j: (0, j)),
        out_shape=jax.ShapeDtypeStruct((bsz, n), F32),
        name="ada",
    )(c, w_ada, b_ada.reshape(1, n))


def _layer_norm(z, gain, bias):
    mu = jnp.mean(z, axis=-1, keepdims=True)
    zc = z - mu
    var = jnp.mean(zc * zc, axis=-1, keepdims=True)
    return zc * lax.rsqrt(var + LN_EPS) * gain + bias


def _mixer_kernel(x_ref, mod_ref, mod_prev_ref, w_in_ref, w_rank_ref, w_gates_ref, w_pool_ref, pool_scale_ref,
                  w_a_ref, w_al_ref,
                  b_al_ref, gain_ref, w_b_ref, w_out_ref, ln_g_ref, ln_b_ref, w_r_ref, b_r_ref,
                  x1_ref, u2_ref, route_ref, counts_ref,
                  a_ext, s_ref, cnt_ref, o_ref, z_ref, sc_ref, *, tiles_per_seq, n_tiles):
    j = pl.program_id(0)
    s_idx = lax.rem(jnp.minimum(j, n_tiles - 1), tiles_per_seq)
    tile = x_ref.shape[1]

    @pl.when(s_idx == 0)
    def _():
        s_ref[...] = jnp.zeros_like(s_ref)
        a_ext[0:POOL_HALO, :] = jnp.zeros((POOL_HALO, D_MODEL), F32)

    @pl.when(j == 0)
    def _():
        cnt_ref[...] = jnp.zeros_like(cnt_ref)
        z_ref[...] = jnp.zeros_like(z_ref)


    mod_prev = mod_prev_ref[0]
    x1 = _layer_norm(z_ref[...], ln_g_ref[...], ln_b_ref[...])
    x1_ref[0] = x1
    u2 = x1 * (1.0 + mod_prev[4:5]) + mod_prev[3:4]
    _store_rows(u2_ref, u2)
    u2_hi = u2.astype(BF16)
    u2_lo = (u2 - u2_hi.astype(F32)).astype(BF16)

    mod = mod_ref[0]
    sh_m, sc_m, g_m = mod[0:1], mod[1:2], mod[2:3]
    x = x_ref[0]
    u = (x * (1.0 + sc_m) + sh_m).astype(BF16)

    def proj(i):
        return _dot(u, w_in_ref[:, _OFF[i]:_OFF[i + 1]])

    gate_cols = ([(w_in_ref, _OFF[4] + jj * MXU_COLS) for jj in range(D_MODEL // MXU_COLS)]
                 + [(w_gates_ref, jj * MXU_COLS) for jj in range(2 * D_MODEL // MXU_COLS)])
    fillers = [functools.partial(lambda w_ref, c0: _dot(u, w_ref[:, c0:c0 + MXU_COLS]), w_ref, c0)
               for w_ref, c0 in gate_cols]
    filled = []

    def issue_fillers(n):
        for _ in range(n):
            if len(filled) < len(fillers):
                filled.append(fillers[len(filled)]())

    a = proj(0)
    q = proj(1) * (GLA_HEAD_K ** -0.5)
    k_all = proj(2)
    v_all = proj(3)
    alpha_low = _dot(u, w_rank_ref[...])

    part = _dot_nt(w_r_ref[...], u2_hi)
    logits = (part[:N_EXPERTS] + part[N_EXPERTS:] + _dot_nt(w_r_ref[0:N_EXPERTS, :], u2_lo)
              + b_r_ref[...])
    issue_fillers(2)

    a_ext[POOL_HALO:POOL_HALO + tile, :] = a
    t_glob = s_idx * tile + lax.broadcasted_iota(jnp.int32, (tile, 1), 0)
    mapped = []
    for g, w in enumerate(POOL_WINDOWS):
        lo, hi = g * POOL_GROUP_WIDTH, (g + 1) * POOL_GROUP_WIDTH
        win = a_ext[:, lo:hi]
        k = 1
        while k < w:
            win = win + pltpu.roll(win, k, 0)
            k *= 2
        inv_cnt = 1.0 / jnp.minimum(t_glob + 1, w).astype(F32)
        pooled = win[POOL_HALO:, :] * inv_cnt - a[:, lo:hi]
        mapped.append(_dot(pooled.astype(BF16), w_pool_ref[g]))
    a_ext[0:POOL_HALO, :] = a[tile - POOL_HALO:tile, :]
    ya = _dot((jnp.concatenate(mapped, axis=1) * pool_scale_ref[...]).astype(BF16), w_a_ref[...])

    erow = lax.broadcasted_iota(jnp.int32, (N_EXPERTS, tile), 0).astype(F32)
    work = logits
    sel = jnp.zeros((N_EXPERTS, tile), F32)
    vals, hits = [], []
    for _ in range(TOP_K):
        m = jnp.max(work, axis=0, keepdims=True)
        idx = jnp.min(jnp.where(work == m, erow, float(N_EXPERTS)), axis=0, keepdims=True)
        hit = erow == idx
        vals.append(m)
        hits.append((idx, hit))
        sel = jnp.where(hit, 1.0, sel)
        work = jnp.where(hit, -jnp.inf, work)
    exps = [jnp.exp(v - vals[0]) for v in vals]
    inv_den = 1.0 / (exps[0] + exps[1] + exps[2] + exps[3])
    issue_fillers(1)

    z = _dot(alpha_low.astype(BF16), w_al_ref[...]) + b_al_ref[...]
    log_a = (jnp.minimum(z, 0.0) - jnp.log1p(jnp.exp(-jnp.abs(z)))) * (1.0 / GLA_TAU)

    ri = lax.broadcasted_iota(jnp.int32, (CHUNK, CHUNK), 0)
    ci = lax.broadcasted_iota(jnp.int32, (CHUNK, CHUNK), 1)
    causal = ci <= ri
    cum_mat = causal.astype(BF16)
    la_hi = log_a.astype(BF16)
    la_split = jnp.concatenate([la_hi, (log_a - la_hi.astype(F32)).astype(BF16)], axis=1)

    n_chunks = tile // CHUNK
    pairs = [(c, h) for c in range(n_chunks) for h in range(GLA_HEADS)]
    rows_of = lambda c: slice(c * CHUNK, (c + 1) * CHUNK)
    ks_of = lambda h: slice(h * GLA_HEAD_K, (h + 1) * GLA_HEAD_K)
    vs_of = lambda h: slice(h * GLA_HEAD_V, (h + 1) * GLA_HEAD_V)

    b_cum, ref_pts, b_ref_pt = [], [], []
    for c in range(n_chunks):
        cum = _dot(cum_mat, la_split[rows_of(c)])
        b_cum.append(cum[:, :GLA_KEY_DIM] + cum[:, GLA_KEY_DIM:])
        ref_pts.append([jnp.zeros((1, GLA_KEY_DIM), F32)]
                       + [b_cum[c][i * SUB - 1:i * SUB, :] for i in range(1, N_SUB)])
        b_ref_pt.append(jnp.concatenate([jnp.broadcast_to(p, (SUB, GLA_KEY_DIM)) for p in ref_pts[c]], axis=0))

    rt = lax.broadcasted_iota(jnp.int32, (tile, tile), 0)
    ct = lax.broadcasted_iota(jnp.int32, (tile, tile), 1)
    base = _dot(sel.astype(BF16), (rt < ct).astype(BF16)) + cnt_ref[:, 0:1]

    for n_pair, (c, h) in enumerate(pairs):
        if n_pair % 3 == 0:
            issue_fillers(1)
        qh, kh, bh = q[rows_of(c), ks_of(h)], k_all[rows_of(c), ks_of(h)], b_cum[c][:, ks_of(h)]
        q_dec = (qh * jnp.exp(bh - b_ref_pt[c][:, ks_of(h)])).astype(BF16)
        k_dec = jnp.concatenate(
            [(kh * jnp.exp(jnp.minimum(ref_pts[c][i][:, ks_of(h)] - bh, EXP_CAP))).astype(BF16)
             for i in range(N_SUB)], axis=0)
        s_all = _dot_nt(q_dec, k_dec)
        sc_ref[n_pair] = jnp.concatenate(
            [s_all[i * SUB:(i + 1) * SUB, i * CHUNK:(i + 1) * CHUNK] for i in range(N_SUB)], axis=0)

    sub_decay = b_ref_pt[0] - b_cum[0]
    for c in range(1, n_chunks):
        sub_decay = jnp.maximum(sub_decay, b_ref_pt[c] - b_cum[c])

    @pl.when(jnp.max(sub_decay) > SAFE_SUB_DECAY)
    def _():
        o_ref[:, 0:GLA_KEY_DIM] = q
        o_ref[:, GLA_KEY_DIM:2 * GLA_KEY_DIM] = jnp.concatenate(b_cum, axis=0)
        for n_pair, (c, h) in enumerate(pairs):
            kh, bh = k_all[rows_of(c), ks_of(h)], b_cum[c][:, ks_of(h)]

            def score_rows(g, carry):
                first = pl.multiple_of(g * 8, 8)
                q_8 = o_ref[pl.ds(c * CHUNK + first, 8), ks_of(h)]
                b_8 = o_ref[pl.ds(c * CHUNK + first, 8),
                            GLA_KEY_DIM + h * GLA_HEAD_K:GLA_KEY_DIM + (h + 1) * GLA_HEAD_K]
                rows = []
                for r in range(8):
                    k_i = (kh * jnp.exp(jnp.minimum(b_8[r:r + 1] - bh, 0.0))).astype(BF16)
                    q_i = jnp.broadcast_to(q_8[r:r + 1], (8, GLA_HEAD_K)).astype(BF16)
                    rows.append(_dot_nt(q_i, k_i)[0:1])
                sc_ref[n_pair, pl.ds(first, 8), :] = jnp.concatenate(rows, axis=0)
                return carry

            lax.fori_loop(0, CHUNK // 8, score_rows, 0)

    o_intra, kv, decay_last = {}, {}, {}
    for n_pair, (c, h) in enumerate(pairs):
        kh, bh = k_all[rows_of(c), ks_of(h)], b_cum[c][:, ks_of(h)]
        vh = v_all[rows_of(c), vs_of(h)].astype(BF16)
        o_intra[c, h] = _dot(jnp.where(causal, sc_ref[n_pair], 0.0).astype(BF16), vh)
        b_last = bh[CHUNK - 1:CHUNK, :]
        kv[c, h] = _dot_tn(vh, (kh * jnp.exp(b_last - bh)).astype(BF16))
        decay_last[c, h] = jnp.exp(b_last)

    orow = lax.broadcasted_iota(jnp.int32, (ROUTE_ROWS, tile), 0)
    route = jnp.zeros((ROUTE_ROWS, tile), F32)
    for kk in range(TOP_K):
        idx, hit = hits[kk]
        rank = jnp.sum(jnp.where(hit, base, 0.0), axis=0, keepdims=True)
        route = jnp.where(orow == kk, idx, route)
        route = jnp.where(orow == TOP_K + kk, rank, route)
        route = jnp.where(orow == 2 * TOP_K + kk, exps[kk] * inv_den, route)
    route_ref[...] = route
    cnt_ref[...] = cnt_ref[...] + (j > 0).astype(F32) * jnp.sum(sel, axis=1, keepdims=True)
    counts_ref[...] = cnt_ref[...]

    o_inter = {}
    state_t = [s_ref[h] for h in range(GLA_HEADS)]
    for c, h in pairs:
        q_in = (q[rows_of(c), ks_of(h)] * jnp.exp(b_cum[c][:, ks_of(h)])).astype(BF16)
        o_inter[c, h] = _dot_nt(q_in, state_t[h].astype(BF16))
        state_t[h] = state_t[h] * decay_last[c, h] + kv[c, h]
    for h in range(GLA_HEADS):
        s_ref[h] = state_t[h]

    for c, h in pairs:
        o = o_intra[c, h] + o_inter[c, h]
        o = o * lax.rsqrt(jnp.mean(o * o, axis=-1, keepdims=True) + RMS_EPS) * gain_ref[...]
        o_ref[rows_of(c), vs_of(h)] = o

    issue_fillers(len(fillers))
    per_proj = D_MODEL // MXU_COLS
    r, gate_a, gate_b = (jnp.concatenate(filled[n * per_proj:(n + 1) * per_proj], axis=1) for n in range(3))
    yb = _dot((o_ref[...] * (r * _sigmoid(r))).astype(BF16), w_b_ref[...])

    merged = _sigmoid(gate_a) * ya + _sigmoid(gate_b) * yb
    y = _dot(merged.astype(BF16), w_out_ref[...])
    z_ref[...] = DEEPNORM_ALPHA * x + g_m * y


def _mixer(x, mod, w_in_p, w_rank_p, w_gates, w_pool, pool_scale, w_a, w_al_p, b_alpha, gain, w_b, w_out,
           ln_g, ln_b, w_r_p, b_r_p):
    bsz, seq, d = x.shape
    tile = SEQ_TILE
    n_s = seq // tile
    n_tiles = bsz * n_s

    def const(shape):
        nd = len(shape)
        return pl.BlockSpec(shape, lambda j: (0,) * nd, pipeline_mode=pl.Buffered(1))

    cur = lambda j: jnp.minimum(j, n_tiles - 1)
    prev = lambda j: jnp.maximum(j - 1, 0)
    return pl.pallas_call(
        functools.partial(_mixer_kernel, tiles_per_seq=n_s, n_tiles=n_tiles),
        grid=(n_tiles + 1,),
        in_specs=[
            pl.BlockSpec((1, tile, d), lambda j: (cur(j) // n_s, cur(j) % n_s, 0)),
            pl.BlockSpec((1, 6, d), lambda j: (cur(j) // n_s, 0, 0)),
            pl.BlockSpec((1, 6, d), lambda j: (prev(j) // n_s, 0, 0)),
            const(w_in_p.shape), const(w_rank_p.shape), const(w_gates.shape),
            const(w_pool.shape), const(pool_scale.shape), const(w_a.shape),
            const(w_al_p.shape), const(b_alpha.shape), const(gain.shape), const(w_b.shape),
            const(w_out.shape), const(ln_g.shape), const(ln_b.shape), const(w_r_p.shape),
            const(b_r_p.shape),
        ],
        out_specs=[pl.BlockSpec((1, tile, d), lambda j: (prev(j) // n_s, prev(j) % n_s, 0)),
                   pl.BlockSpec((tile * ROW_SUB, LANES), lambda j: (prev(j), 0)),
                   pl.BlockSpec((ROUTE_ROWS, tile), lambda j: (0, prev(j))),
                   pl.BlockSpec((N_EXPERTS, LANES), lambda j: (0, 0))],
        out_shape=[
            jax.ShapeDtypeStruct((bsz, seq, d), F32),
            jax.ShapeDtypeStruct((bsz * seq * ROW_SUB, LANES), F32),
            jax.ShapeDtypeStruct((ROUTE_ROWS, bsz * seq), F32),
            jax.ShapeDtypeStruct((N_EXPERTS, LANES), F32),
        ],
        scratch_shapes=[
            pltpu.VMEM((POOL_HALO + tile, d), F32),
            pltpu.VMEM((GLA_HEADS, GLA_HEAD_V, GLA_HEAD_K), F32),
            pltpu.VMEM((N_EXPERTS, LANES), F32),
            pltpu.VMEM((tile, d), F32),
            pltpu.VMEM((tile, d), F32),
            pltpu.VMEM((tile // CHUNK * GLA_HEADS, CHUNK, CHUNK), F32),
        ],
        compiler_params=pltpu.CompilerParams(
            dimension_semantics=("arbitrary",), vmem_limit_bytes=VMEM_LIMIT),
        name="mixer",
    )(x, mod, mod, w_in_p, w_rank_p, w_gates, w_pool, pool_scale, w_a, w_al_p, b_alpha, gain, w_b, w_out,
      ln_g, ln_b, w_r_p, b_r_p)


def _row_at(ref, row8):
    return ref.at[pl.ds(pl.multiple_of(row8, ROW_SUB), ROW_SUB)]


def _per_tile(dest8, tile):
    n_t = dest8.shape[1] // tile
    return dest8.reshape(TOP_K, n_t, tile).transpose(1, 0, 2).reshape(n_t, 1, TOP_K * tile)


def _dispatch_kernel(fill_ref, dest_ref, u2_ref, rows_hbm, zbuf, sem_fill, sem_rows):
    i = pl.program_id(0)
    tile = u2_ref.shape[0] // ROW_SUB
    blk8 = EXPERT_BLOCK * ROW_SUB

    def zero_fills(phase):
        def piece(first_row, n_rows):
            dst = rows_hbm.at[pl.ds(pl.multiple_of(first_row * ROW_SUB, ROW_SUB), n_rows * ROW_SUB)]
            getattr(pltpu.make_async_copy(zbuf.at[pl.ds(0, n_rows * ROW_SUB)], dst, sem_fill.at[0]), phase)()

        for e in range(N_EXPERTS):
            first_row, gap = fill_ref[e], fill_ref[N_EXPERTS + e]
            n_rows = EXPERT_BLOCK // 2
            while n_rows >= 1:
                @pl.when(jnp.bitwise_and(gap, n_rows) != 0)
                def _(first_row=first_row, n_rows=n_rows):
                    piece(first_row, n_rows)
                first_row = first_row + jnp.bitwise_and(gap, n_rows)
                n_rows //= 2

        def unused(window, carry):
            piece(window * EXPERT_BLOCK, EXPERT_BLOCK)
            return carry

        lax.fori_loop(fill_ref[2 * N_EXPERTS], rows_hbm.shape[0] // blk8, unused, 0)

    @pl.when(i == 0)
    def _():
        zbuf[...] = jnp.zeros_like(zbuf)
        zero_fills("start")

    def body(r, carry):
        for kk in range(TOP_K):
            pltpu.make_async_copy(_row_at(u2_ref, r * ROW_SUB), _row_at(rows_hbm, dest_ref[0, 0, kk * tile + r]),
                                  sem_rows.at[0]).start(priority=kk % 2)
        return carry

    lax.fori_loop(0, tile, body, 0, unroll=DMA_UNROLL)
    for _ in range(TOP_K):
        pltpu.make_async_copy(u2_ref, rows_hbm.at[pl.ds(0, tile * ROW_SUB)], sem_rows.at[0]).wait()

    @pl.when(i == pl.num_programs(0) - 1)
    def _():
        zero_fills("wait")


def _dispatch(fill_start, dest8, u2_rows, n_rows):
    n_tok = u2_rows.shape[0] // ROW_SUB
    tile = DISPATCH_TILE
    n_t = n_tok // tile
    grid_spec = pltpu.PrefetchScalarGridSpec(
        num_scalar_prefetch=1,
        grid=(n_t,),
        in_specs=[
            pl.BlockSpec((1, 1, tile * TOP_K), lambda i, fs: (i, 0, 0), memory_space=pltpu.SMEM),
            pl.BlockSpec((tile * ROW_SUB, LANES), lambda i, fs: (i, 0)),
        ],
        out_specs=pl.BlockSpec(memory_space=pl.ANY),
        scratch_shapes=[
            pltpu.VMEM((EXPERT_BLOCK * ROW_SUB, LANES), F32),
            pltpu.SemaphoreType.DMA((1,)),
            pltpu.SemaphoreType.DMA((1,)),
        ],
    )
    return pl.pallas_call(
        _dispatch_kernel,
        grid_spec=grid_spec,
        out_shape=jax.ShapeDtypeStruct(((n_rows + EXPERT_BLOCK) * ROW_SUB, LANES), F32),
        compiler_params=pltpu.CompilerParams(dimension_semantics=("arbitrary",)),
        name="dispatch",
    )(fill_start, _per_tile(dest8, tile), u2_rows)


def _moe_kernel(be_ref, nxt_ref, par_ref, full_ref, nb_ref, x_ref, bg_ref, bu_ref, bd_ref, wgu_hbm, wd_hbm, y_ref,
                wgu_buf, wd_buf, wg_s, wu_s, wd_s, sem_gu, sem_d):
    i = pl.program_id(0)
    new_expert = (i == 0) | (be_ref[i] != be_ref[jnp.maximum(i - 1, 0)])

    def fetch(e, s):
        return (pltpu.make_async_copy(wgu_hbm.at[e], wgu_buf.at[s], sem_gu.at[s]),
                pltpu.make_async_copy(wd_hbm.at[e], wd_buf.at[s], sem_d.at[s]))

    @pl.when(new_expert & (i < nb_ref[0]))
    def _():
        slot = par_ref[i]

        @pl.when(i == 0)
        def _():
            for cp in fetch(be_ref[0], slot):
                cp.start()

        @pl.when(nxt_ref[i] >= 0)
        def _():
            for cp in fetch(nxt_ref[i], 1 - slot):
                cp.start()

        for cp in fetch(be_ref[i], slot):
            cp.wait()
        src = lax.broadcasted_iota(jnp.int32, (MXU_COLS, MXU_COLS), 0)
        col = lax.broadcasted_iota(jnp.int32, (MXU_COLS, MXU_COLS), 1)
        half = MXU_COLS // 2
        want = jnp.where(col < half, 2 * col, 2 * (col - half) + 1)
        unzip = (src == want).astype(BF16)
        for g in range(wgu_buf.shape[2] // MXU_COLS):
            blk = wgu_buf[slot, :, g * MXU_COLS:(g + 1) * MXU_COLS].astype(BF16)
            sep = _dot(blk, unzip)
            wg_s[:, g * half:(g + 1) * half] = sep[:, :half].astype(BF16)
            wu_s[:, g * half:(g + 1) * half] = sep[:, half:].astype(BF16)
        wd_s[...] = wd_buf[slot].astype(BF16)

    def expert_mlp(n_rows):
        xb = _load_rows(x_ref, n_rows).astype(BF16)
        gate = jnp.minimum(_dot(xb, wg_s[...]) + bg_ref[0], SWIGLU_LIMIT)
        up = jnp.clip(_dot(xb, wu_s[...]) + bu_ref[0], -SWIGLU_LIMIT, SWIGLU_LIMIT)
        glu = gate * _sigmoid(gate * SWIGLU_ALPHA)
        _store_rows(y_ref, _dot(((up + 1.0) * glu).astype(BF16), wd_s[...]) + bd_ref[0])

    active = i < nb_ref[0]
    both_halves = full_ref[i] == 1

    @pl.when(active & both_halves)
    def _():
        expert_mlp(EXPERT_BLOCK)

    @pl.when(active & jnp.logical_not(both_halves))
    def _():
        expert_mlp(EXPERT_HALF)
        y_ref[EXPERT_HALF * ROW_SUB:, :] = jnp.zeros((EXPERT_HALF * ROW_SUB, LANES), F32)

    @pl.when(jnp.logical_not(active))
    def _():
        y_ref[...] = jnp.zeros_like(y_ref)


def _moe(block_expert, next_expert, slot_parity, both_halves, n_active, x_rows, w_gate_up, bg, bu, w_down, bd):
    n_blocks = block_expert.shape[0]
    _, d, f2 = w_gate_up.shape
    f = f2 // 2
    blk8 = EXPERT_BLOCK * ROW_SUB
    rows_in = pl.BlockSpec((blk8, LANES),
                           lambda i, be, nx, pr, fl, nb: (jnp.maximum(jnp.minimum(i, nb[0] - 1), 0), 0))
    rows_out = pl.BlockSpec((blk8, LANES), lambda i, be, nx, pr, fl, nb: (i, 0))
    per_expert = lambda shape: pl.BlockSpec((1,) + shape, lambda i, be, nx, pr, fl, nb: (be[i], 0, 0))
    hbm = pl.BlockSpec(memory_space=pl.ANY)
    grid_spec = pltpu.PrefetchScalarGridSpec(
        num_scalar_prefetch=5,
        grid=(n_blocks,),
        in_specs=[rows_in, per_expert((1, f)), per_expert((1, f)), per_expert((1, d)), hbm, hbm],
        out_specs=rows_out,
        scratch_shapes=[pltpu.VMEM((2, d, f2), F32), pltpu.VMEM((2, f, d), F32),
                        pltpu.VMEM((d, f), BF16), pltpu.VMEM((d, f), BF16), pltpu.VMEM((f, d), BF16),
                        pltpu.SemaphoreType.DMA((2,)), pltpu.SemaphoreType.DMA((2,))],
    )
    return pl.pallas_call(
        _moe_kernel,
        grid_spec=grid_spec,
        out_shape=jax.ShapeDtypeStruct((n_blocks * blk8, LANES), F32),
        compiler_params=pltpu.CompilerParams(
            dimension_semantics=("arbitrary",), vmem_limit_bytes=VMEM_LIMIT),
        name="moe",
    )(block_expert, next_expert, slot_parity, both_halves, n_active, x_rows, bg, bu, bd, w_gate_up, w_down)


def _combine_kernel(dest_cur_ref, dest_nxt_ref, x1_ref, mod_ref, route_ref, ln_g_ref, ln_b_ref, y_hbm,
                    o_ref, ybuf, sem):
    i = pl.program_id(0)
    n_steps = pl.num_programs(0)
    tile = x1_ref.shape[0] // 2

    def start_row(dest_ref, pair_row, s, r):
        for kk in range(TOP_K):
            pltpu.make_async_copy(_row_at(y_hbm, dest_ref[pair_row, 0, kk * tile + r]),
                                  _row_at(ybuf.at[s, kk], r * ROW_SUB), sem.at[s]).start(priority=kk % 2)

    @pl.when(i == 0)
    def _():
        def body(r, carry):
            start_row(dest_cur_ref, 0, 0, r)
            return carry
        lax.fori_loop(0, tile, body, 0, unroll=DMA_UNROLL)

    def wait_slot(s):
        for kk in range(TOP_K):
            pltpu.make_async_copy(y_hbm.at[pl.ds(0, tile * ROW_SUB)], ybuf.at[s, kk], sem.at[s]).wait()

    def finish_tile(s):
        rows = slice(s * tile, (s + 1) * tile)
        weight = route_ref[rows, :]
        y = jnp.zeros((tile, D_MODEL), F32)
        for kk in range(TOP_K):
            y = y + weight[:, kk:kk + 1] * _load_rows(ybuf, tile, (s, kk))
        o_ref[rows, :] = _layer_norm(DEEPNORM_ALPHA * x1_ref[rows, :] + mod_ref[0][5:6] * y,
                                     ln_g_ref[...], ln_b_ref[...])

    for r in range(tile):
        start_row(dest_cur_ref, 1, 1, r)
    wait_slot(0)
    finish_tile(0)
    for r in range(tile):
        start_row(dest_nxt_ref, 0, 0, r)
    wait_slot(1)
    finish_tile(1)

    @pl.when(i == n_steps - 1)
    def _():
        wait_slot(0)


def _combine(dest8, x1, mod, route, ln_g, ln_b, y_rows, seq):
    n_tok, d = x1.shape
    tile = COMBINE_TILE
    n_pairs = n_tok // (2 * tile)
    per_seq = seq // (2 * tile)
    dest3 = _per_tile(dest8, tile)
    smem_blk = lambda fn: pl.BlockSpec((2, 1, tile * TOP_K), fn, memory_space=pltpu.SMEM)
    row = lambda width: pl.BlockSpec((2 * tile, width), lambda i: (i, 0))
    vec = pl.BlockSpec((1, d), lambda i: (0, 0))
    return pl.pallas_call(
        _combine_kernel,
        grid=(n_pairs,),
        in_specs=[smem_blk(lambda i: (i, 0, 0)), smem_blk(lambda i: (jnp.minimum(i + 1, n_pairs - 1), 0, 0)),
                  row(d), pl.BlockSpec((1, 6, d), lambda i: (i // per_seq, 0, 0)), row(TOP_K), vec, vec,
                  pl.BlockSpec(memory_space=pl.ANY)],
        out_specs=row(d),
        out_shape=jax.ShapeDtypeStruct((n_tok, d), F32),
        scratch_shapes=[pltpu.VMEM((2, TOP_K, tile * ROW_SUB, LANES), F32), pltpu.SemaphoreType.DMA((2,))],
        compiler_params=pltpu.CompilerParams(
            dimension_semantics=("arbitrary",), vmem_limit_bytes=VMEM_LIMIT),
        name="combine",
    )(dest3, dest3, x1, mod, route, ln_g, ln_b, y_rows)


def _split_in_proj(w_in):
    rank_lo = _OFF[5]
    rank_hi = rank_lo + GLA_GATE_RANK
    pad = jnp.zeros((w_in.shape[0], RANK_PAD - GLA_GATE_RANK), BF16)
    return (w_in[:, :rank_lo].astype(BF16),
            jnp.concatenate([w_in[:, rank_lo:rank_hi].astype(BF16), pad], axis=1),
            w_in[:, rank_hi:].astype(BF16))


def _layer(x, c, w_ada, b_ada, w_in, w_pool_group, pool_scale, w_branch_a, w_alpha_up, b_alpha,
           gla_norm_gain, w_branch_b, w_out, ln1_gain, ln1_bias, w_router, b_router,
           w_gate_up, b_gate_up, w_down, b_down, ln2_gain, ln2_bias):
    bsz, seq, d = x.shape
    n_tok = bsz * seq
    n_assign = n_tok * TOP_K
    row2 = lambda v: v.reshape(1, -1)

    mod = _ada(c, w_ada, b_ada).reshape(bsz, 6, d)

    w_in_p, w_rank_p, w_gates = _split_in_proj(w_in)
    w_al_p = jnp.concatenate(
        [w_alpha_up, jnp.zeros((RANK_PAD - GLA_GATE_RANK, GLA_KEY_DIM), w_alpha_up.dtype)], axis=0).astype(BF16)
    w_r_t = w_router.T
    w_r_hi = w_r_t.astype(BF16)
    w_r_split = jnp.concatenate([w_r_hi, (w_r_t - w_r_hi.astype(F32)).astype(BF16)], axis=0)
    x1, u2_rows, route_t, counts = _mixer(
        x, mod, w_in_p, w_rank_p, w_gates, w_pool_group.astype(BF16), row2(pool_scale),
        w_branch_a.astype(BF16), w_al_p,
        row2(b_alpha), row2(gla_norm_gain), w_branch_b.astype(BF16), w_out.astype(BF16), row2(ln1_gain),
        row2(ln1_bias), w_r_split, b_router.reshape(N_EXPERTS, 1))

    top_idx = route_t[0:TOP_K].astype(jnp.int32)
    rank = route_t[TOP_K:2 * TOP_K].astype(jnp.int32)
    route = route_t[2 * TOP_K:3 * TOP_K].T
    counts = counts[:, 0].astype(jnp.int32)
    padded = (counts + EXPERT_BLOCK - 1) // EXPERT_BLOCK * EXPERT_BLOCK
    pad_end = jnp.cumsum(padded)
    pad_start = pad_end - padded
    n_rows = (n_assign + N_EXPERTS * (EXPERT_BLOCK - 1) + EXPERT_BLOCK - 1) // EXPERT_BLOCK * EXPERT_BLOCK
    n_blocks = n_rows // EXPERT_BLOCK
    n_active = (pad_end[-1] // EXPERT_BLOCK).astype(jnp.int32)
    start_of = jnp.sum(jnp.where(top_idx[..., None] == jnp.arange(N_EXPERTS, dtype=jnp.int32), pad_start, 0),
                       axis=-1)
    dest8 = (start_of + rank) * ROW_SUB
    blk_row = jnp.arange(n_blocks, dtype=jnp.int32)[:, None] * EXPERT_BLOCK
    block_expert = jnp.minimum(jnp.sum(pad_end[None, :] <= blk_row, axis=1), N_EXPERTS - 1).astype(jnp.int32)

    fill_table = jnp.concatenate([pad_start + counts, padded - counts, n_active.reshape(1)]).astype(jnp.int32)
    x_rows = _dispatch(fill_table, dest8, u2_rows, n_rows)
    f = w_down.shape[1]
    bg = b_gate_up[:, 0::2].reshape(N_EXPERTS, 1, f)
    bu = b_gate_up[:, 1::2].reshape(N_EXPERTS, 1, f)
    eid = jnp.arange(N_EXPERTS, dtype=jnp.int32)
    owns = counts > 0
    later = jnp.where((eid[None, :] > eid[:, None]) & owns[None, :], eid[None, :], N_EXPERTS)
    next_of = jnp.min(later, axis=1)
    next_of = jnp.where(next_of == N_EXPERTS, -1, next_of).astype(jnp.int32)
    parity_of = ((jnp.cumsum(owns.astype(jnp.int32)) - 1) % 2).astype(jnp.int32)
    is_block_expert = block_expert[:, None] == eid[None, :]
    per_block = lambda v: jnp.sum(jnp.where(is_block_expert, v[None, :], 0), axis=1).astype(jnp.int32)
    left = per_block(pad_start + counts) - blk_row[:, 0]
    both_halves = (left > EXPERT_HALF).astype(jnp.int32)
    y_rows = _moe(block_expert, per_block(next_of), per_block(parity_of), both_halves, n_active.reshape(1),
                  x_rows, w_gate_up, bg, bu, w_down, b_down.reshape(N_EXPERTS, 1, d))
    out = _combine(dest8, x1.reshape(n_tok, d), mod, route, row2(ln2_gain), row2(ln2_bias), y_rows, seq)
    return out.reshape(bsz, seq, d)


def kernel(x, c, w_ada, b_ada, w_in, w_pool_group, pool_scale, w_branch_a, w_alpha_up, b_alpha, gla_norm_gain,
           w_branch_b, w_out, ln1_gain, ln1_bias, w_router, b_router, w_gate_up, b_gate_up, w_down, b_down,
           ln2_gain, ln2_bias):
    for l in range(DEPTH):
        x = _layer(x, c, w_ada[l], b_ada[l], w_in[l], w_pool_group[l], pool_scale[l], w_branch_a[l],
                   w_alpha_up[l], b_alpha[l], gla_norm_gain[l], w_branch_b[l], w_out[l], ln1_gain[l],
                   ln1_bias[l], w_router[l], b_router[l], w_gate_up[l], b_gate_up[l], w_down[l], b_down[l],
                   ln2_gain[l], ln2_bias[l])
    return x
```

```python
import functools

import jax
import jax.numpy as jnp
from jax import lax
from jax.experimental import pallas as pl
from jax.experimental.pallas import tpu as pltpu

D_MODEL = 1024
CHUNK = 64
SUB = 16
N_SUB = CHUNK // SUB
POOL_WINDOWS = (2, 4, 8, 16)
POOL_GROUP_WIDTH = D_MODEL // len(POOL_WINDOWS)
POOL_HALO = 16
GLA_HEADS = 4
GLA_KEY_DIM = D_MODEL // 2
GLA_HEAD_K = GLA_KEY_DIM // GLA_HEADS
GLA_HEAD_V = D_MODEL // GLA_HEADS
GLA_GATE_RANK = 16
GLA_TAU = 16.0
N_EXPERTS = 32
TOP_K = 4
SWIGLU_ALPHA = 1.702
SWIGLU_LIMIT = 7.0
EXPERT_BLOCK = 512
EXPERT_HALF = EXPERT_BLOCK // 2
LN_EPS = 1e-5
RMS_EPS = 1e-6
DEPTH = 1
DEEPNORM_ALPHA = (2.0 * DEPTH) ** 0.25

LANES = 128
MXU_COLS = 256
RANK_PAD = LANES
_W = (D_MODEL, GLA_KEY_DIM, GLA_KEY_DIM, D_MODEL, D_MODEL, RANK_PAD, D_MODEL, D_MODEL)
_OFF = tuple(sum(_W[:i]) for i in range(len(_W) + 1))
EXP_CAP = 60.0
SAFE_SUB_DECAY = 40.0
ROUTE_ROWS = 16

SEQ_TILE = 512
DISPATCH_TILE = 2048
COMBINE_TILE = 256
DMA_UNROLL = 8
VMEM_LIMIT = 60 * 1024 * 1024

F32 = jnp.float32
BF16 = jnp.bfloat16
HI = lax.Precision.HIGHEST


def _dot(a, b):
    return jnp.dot(a, b, preferred_element_type=F32)


def _dot_nt(a, b):
    return lax.dot_general(a, b, (((1,), (1,)), ((), ())), preferred_element_type=F32)


def _dot_tn(a, b):
    return lax.dot_general(a, b, (((0,), (0,)), ((), ())), preferred_element_type=F32)


def _sigmoid(v):
    return 0.5 * jnp.tanh(0.5 * v) + 0.5


ROW_SUB = D_MODEL // LANES


def _load_rows(ref, n_rows, lead=()):
    return jnp.concatenate(
        [ref[lead + (pl.ds(j, n_rows, stride=ROW_SUB), slice(None))] for j in range(ROW_SUB)], axis=1)


def _store_rows(ref, val):
    for j in range(ROW_SUB):
        ref[pl.ds(j, val.shape[0], stride=ROW_SUB), :] = val[:, j * LANES:(j + 1) * LANES]


def _ada_kernel(c_ref, w_ref, b_ref, o_ref):
    c = c_ref[...]
    s = c * jax.nn.sigmoid(c)
    o_ref[...] = jnp.dot(s, w_ref[...], precision=HI, preferred_element_type=F32) + b_ref[...]


def _ada(c, w_ada, b_ada):
    bsz, d = c.shape
    n = w_ada.shape[1]
    tn = 1024
    return pl.pallas_call(
        _ada_kernel,
        grid=(n // tn,),
        in_specs=[
            pl.BlockSpec((bsz, d), lambda j: (0, 0)),
            pl.BlockSpec((d, tn), lambda j: (0, j)),
            pl.BlockSpec((1, tn), lambda j: (0, j)),
        ],
        out_specs=pl.BlockSpec((bsz, tn), lambda j: (0, j)),
        out_shape=jax.ShapeDtypeStruct((bsz, n), F32),
        name="ada",
    )(c, w_ada, b_ada.reshape(1, n))


def _layer_norm(z, gain, bias):
    mu = jnp.mean(z, axis=-1, keepdims=True)
    zc = z - mu
    var = jnp.mean(zc * zc, axis=-1, keepdims=True)
    return zc * lax.rsqrt(var + LN_EPS) * gain + bias


def _mixer_kernel(x_ref, mod_ref, mod_prev_ref, w_in_ref, w_rank_ref, w_gates_ref, w_pool_ref, pool_scale_ref,
                  w_a_ref, w_al_ref,
                  b_al_ref, gain_ref, w_b_ref, w_out_ref, ln_g_ref, ln_b_ref, w_r_ref, b_r_ref,
                  x1_ref, u2_ref, route_ref, counts_ref, decay_ref,
                  a_ext, s_ref, cnt_ref, o_ref, z_ref, sc_ref, *, tiles_per_seq, n_tiles, direct_scores):
    j = pl.program_id(0)
    s_idx = lax.rem(jnp.minimum(j, n_tiles - 1), tiles_per_seq)
    tile = x_ref.shape[1]

    @pl.when(s_idx == 0)
    def _():
        s_ref[...] = jnp.zeros_like(s_ref)
        a_ext[0:POOL_HALO, :] = jnp.zeros((POOL_HALO, D_MODEL), F32)

    @pl.when(j == 0)
    def _():
        cnt_ref[...] = jnp.zeros_like(cnt_ref)
        z_ref[...] = jnp.zeros_like(z_ref)
        decay_ref[...] = jnp.zeros_like(decay_ref)


    mod_prev = mod_prev_ref[0]
    x1 = _layer_norm(z_ref[...], ln_g_ref[...], ln_b_ref[...])
    x1_ref[0] = x1
    u2 = x1 * (1.0 + mod_prev[4:5]) + mod_prev[3:4]
    _store_rows(u2_ref, u2)
    u2_hi = u2.astype(BF16)
    u2_lo = (u2 - u2_hi.astype(F32)).astype(BF16)

    mod = mod_ref[0]
    sh_m, sc_m, g_m = mod[0:1], mod[1:2], mod[2:3]
    x = x_ref[0]
    u = (x * (1.0 + sc_m) + sh_m).astype(BF16)

    def proj(i):
        return _dot(u, w_in_ref[:, _OFF[i]:_OFF[i + 1]])

    gate_cols = ([(w_in_ref, _OFF[4] + jj * MXU_COLS) for jj in range(D_MODEL // MXU_COLS)]
                 + [(w_gates_ref, jj * MXU_COLS) for jj in range(2 * D_MODEL // MXU_COLS)])
    fillers = [functools.partial(lambda w_ref, c0: _dot(u, w_ref[:, c0:c0 + MXU_COLS]), w_ref, c0)
               for w_ref, c0 in gate_cols]
    filled = []

    def issue_fillers(n):
        for _ in range(n):
            if len(filled) < len(fillers):
                filled.append(fillers[len(filled)]())

    a = proj(0)
    q = proj(1) * (GLA_HEAD_K ** -0.5)
    k_all = proj(2)
    v_all = proj(3)
    alpha_low = _dot(u, w_rank_ref[...])

    part = _dot_nt(w_r_ref[...], u2_hi)
    logits = (part[:N_EXPERTS] + part[N_EXPERTS:] + _dot_nt(w_r_ref[0:N_EXPERTS, :], u2_lo)
              + b_r_ref[...])
    issue_fillers(2)

    a_ext[POOL_HALO:POOL_HALO + tile, :] = a
    t_glob = s_idx * tile + lax.broadcasted_iota(jnp.int32, (tile, 1), 0)
    mapped = []
    for g, w in enumerate(POOL_WINDOWS):
        lo, hi = g * POOL_GROUP_WIDTH, (g + 1) * POOL_GROUP_WIDTH
        win = a_ext[:, lo:hi]
        k = 1
        while k < w:
            win = win + pltpu.roll(win, k, 0)
            k *= 2
        inv_cnt = 1.0 / jnp.minimum(t_glob + 1, w).astype(F32)
        pooled = win[POOL_HALO:, :] * inv_cnt - a[:, lo:hi]
        mapped.append(_dot(pooled.astype(BF16), w_pool_ref[g]))
    a_ext[0:POOL_HALO, :] = a[tile - POOL_HALO:tile, :]
    ya = _dot((jnp.concatenate(mapped, axis=1) * pool_scale_ref[...]).astype(BF16), w_a_ref[...])

    erow = lax.broadcasted_iota(jnp.int32, (N_EXPERTS, tile), 0).astype(F32)
    work = logits
    sel = jnp.zeros((N_EXPERTS, tile), F32)
    vals, hits = [], []
    for _ in range(TOP_K):
        m = jnp.max(work, axis=0, keepdims=True)
        idx = jnp.min(jnp.where(work == m, erow, float(N_EXPERTS)), axis=0, keepdims=True)
        hit = erow == idx
        vals.append(m)
        hits.append((idx, hit))
        sel = jnp.where(hit, 1.0, sel)
        work = jnp.where(hit, -jnp.inf, work)
    exps = [jnp.exp(v - vals[0]) for v in vals]
    inv_den = 1.0 / (exps[0] + exps[1] + exps[2] + exps[3])
    issue_fillers(1)

    z = _dot(alpha_low.astype(BF16), w_al_ref[...]) + b_al_ref[...]
    log_a = (jnp.minimum(z, 0.0) - jnp.log1p(jnp.exp(-jnp.abs(z)))) * (1.0 / GLA_TAU)

    ri = lax.broadcasted_iota(jnp.int32, (CHUNK, CHUNK), 0)
    ci = lax.broadcasted_iota(jnp.int32, (CHUNK, CHUNK), 1)
    causal = ci <= ri
    cum_mat = causal.astype(BF16)
    la_hi = log_a.astype(BF16)
    la_split = jnp.concatenate([la_hi, (log_a - la_hi.astype(F32)).astype(BF16)], axis=1)

    n_chunks = tile // CHUNK
    pairs = [(c, h) for c in range(n_chunks) for h in range(GLA_HEADS)]
    rows_of = lambda c: slice(c * CHUNK, (c + 1) * CHUNK)
    ks_of = lambda h: slice(h * GLA_HEAD_K, (h + 1) * GLA_HEAD_K)
    vs_of = lambda h: slice(h * GLA_HEAD_V, (h + 1) * GLA_HEAD_V)

    b_cum, ref_pts, b_ref_pt = [], [], []
    for c in range(n_chunks):
        cum = _dot(cum_mat, la_split[rows_of(c)])
        b_cum.append(cum[:, :GLA_KEY_DIM] + cum[:, GLA_KEY_DIM:])
        ref_pts.append([jnp.zeros((1, GLA_KEY_DIM), F32)]
                       + [b_cum[c][i * SUB - 1:i * SUB, :] for i in range(1, N_SUB)])
        b_ref_pt.append(jnp.concatenate([jnp.broadcast_to(p, (SUB, GLA_KEY_DIM)) for p in ref_pts[c]], axis=0))

    rt = lax.broadcasted_iota(jnp.int32, (tile, tile), 0)
    ct = lax.broadcasted_iota(jnp.int32, (tile, tile), 1)
    base = _dot(sel.astype(BF16), (rt < ct).astype(BF16)) + cnt_ref[:, 0:1]

    sub_decay = b_ref_pt[0] - b_cum[0]
    for c in range(1, n_chunks):
        sub_decay = jnp.maximum(sub_decay, b_ref_pt[c] - b_cum[c])
    decay_ref[...] = jnp.maximum(decay_ref[...], jnp.max(sub_decay, keepdims=True))

    scores = {}
    if not direct_scores:
        for n_pair, (c, h) in enumerate(pairs):
            if n_pair % 3 == 0:
                issue_fillers(1)
            qh, kh, bh = q[rows_of(c), ks_of(h)], k_all[rows_of(c), ks_of(h)], b_cum[c][:, ks_of(h)]
            q_dec = (qh * jnp.exp(bh - b_ref_pt[c][:, ks_of(h)])).astype(BF16)
            k_dec = jnp.concatenate(
                [(kh * jnp.exp(jnp.minimum(ref_pts[c][i][:, ks_of(h)] - bh, EXP_CAP))).astype(BF16)
                 for i in range(N_SUB)], axis=0)
            s_all = _dot_nt(q_dec, k_dec)
            scores[c, h] = jnp.concatenate(
                [s_all[i * SUB:(i + 1) * SUB, i * CHUNK:(i + 1) * CHUNK] for i in range(N_SUB)], axis=0)
    else:
        o_ref[:, 0:GLA_KEY_DIM] = q
        o_ref[:, GLA_KEY_DIM:2 * GLA_KEY_DIM] = jnp.concatenate(b_cum, axis=0)
        for n_pair, (c, h) in enumerate(pairs):
            kh, bh = k_all[rows_of(c), ks_of(h)], b_cum[c][:, ks_of(h)]

            def score_rows(g, carry):
                first = pl.multiple_of(g * 8, 8)
                q_8 = o_ref[pl.ds(c * CHUNK + first, 8), ks_of(h)]
                b_8 = o_ref[pl.ds(c * CHUNK + first, 8),
                            GLA_KEY_DIM + h * GLA_HEAD_K:GLA_KEY_DIM + (h + 1) * GLA_HEAD_K]
                rows = []
                for r in range(8):
                    k_i = (kh * jnp.exp(jnp.minimum(b_8[r:r + 1] - bh, 0.0))).astype(BF16)
                    q_i = jnp.broadcast_to(q_8[r:r + 1], (8, GLA_HEAD_K)).astype(BF16)
                    rows.append(_dot_nt(q_i, k_i)[0:1])
                sc_ref[n_pair, pl.ds(first, 8), :] = jnp.concatenate(rows, axis=0)
                return carry

            lax.fori_loop(0, CHUNK // 8, score_rows, 0)

    o_intra, kv, decay_last = {}, {}, {}
    for n_pair, (c, h) in enumerate(pairs):
        kh, bh = k_all[rows_of(c), ks_of(h)], b_cum[c][:, ks_of(h)]
        vh = v_all[rows_of(c), vs_of(h)].astype(BF16)
        chunk_scores = sc_ref[n_pair] if direct_scores else scores[c, h]
        o_intra[c, h] = _dot(jnp.where(causal, chunk_scores, 0.0).astype(BF16), vh)
        b_last = bh[CHUNK - 1:CHUNK, :]
        kv[c, h] = _dot_tn(vh, (kh * jnp.exp(b_last - bh)).astype(BF16))
        decay_last[c, h] = jnp.exp(b_last)

    orow = lax.broadcasted_iota(jnp.int32, (ROUTE_ROWS, tile), 0)
    route = jnp.zeros((ROUTE_ROWS, tile), F32)
    for kk in range(TOP_K):
        idx, hit = hits[kk]
        rank = jnp.sum(jnp.where(hit, base, 0.0), axis=0, keepdims=True)
        route = jnp.where(orow == kk, idx, route)
        route = jnp.where(orow == TOP_K + kk, rank, route)
        route = jnp.where(orow == 2 * TOP_K + kk, exps[kk] * inv_den, route)
    route_ref[...] = route
    cnt_ref[...] = cnt_ref[...] + (j > 0).astype(F32) * jnp.sum(sel, axis=1, keepdims=True)
    counts_ref[...] = cnt_ref[...]

    o_inter = {}
    state_t = [s_ref[h] for h in range(GLA_HEADS)]
    for c, h in pairs:
        q_in = (q[rows_of(c), ks_of(h)] * jnp.exp(b_cum[c][:, ks_of(h)])).astype(BF16)
        o_inter[c, h] = _dot_nt(q_in, state_t[h].astype(BF16))
        state_t[h] = state_t[h] * decay_last[c, h] + kv[c, h]
    for h in range(GLA_HEADS):
        s_ref[h] = state_t[h]

    for c, h in pairs:
        o = o_intra[c, h] + o_inter[c, h]
        o = o * lax.rsqrt(jnp.mean(o * o, axis=-1, keepdims=True) + RMS_EPS) * gain_ref[...]
        o_ref[rows_of(c), vs_of(h)] = o

    issue_fillers(len(fillers))
    per_proj = D_MODEL // MXU_COLS
    r, gate_a, gate_b = (jnp.concatenate(filled[n * per_proj:(n + 1) * per_proj], axis=1) for n in range(3))
    yb = _dot((o_ref[...] * (r * _sigmoid(r))).astype(BF16), w_b_ref[...])

    merged = _sigmoid(gate_a) * ya + _sigmoid(gate_b) * yb
    y = _dot(merged.astype(BF16), w_out_ref[...])
    z_ref[...] = DEEPNORM_ALPHA * x + g_m * y


def _mixer(direct_scores, x, mod, w_in_p, w_rank_p, w_gates, w_pool, pool_scale, w_a, w_al_p, b_alpha, gain, w_b,
           w_out, ln_g, ln_b, w_r_p, b_r_p):
    bsz, seq, d = x.shape
    tile = SEQ_TILE
    n_s = seq // tile
    n_tiles = bsz * n_s

    def const(shape):
        nd = len(shape)
        return pl.BlockSpec(shape, lambda j: (0,) * nd, pipeline_mode=pl.Buffered(1))

    cur = lambda j: jnp.minimum(j, n_tiles - 1)
    prev = lambda j: jnp.maximum(j - 1, 0)
    return pl.pallas_call(
        functools.partial(_mixer_kernel, tiles_per_seq=n_s, n_tiles=n_tiles, direct_scores=direct_scores),
        grid=(n_tiles + 1,),
        in_specs=[
            pl.BlockSpec((1, tile, d), lambda j: (cur(j) // n_s, cur(j) % n_s, 0)),
            pl.BlockSpec((1, 6, d), lambda j: (cur(j) // n_s, 0, 0)),
            pl.BlockSpec((1, 6, d), lambda j: (prev(j) // n_s, 0, 0)),
            const(w_in_p.shape), const(w_rank_p.shape), const(w_gates.shape),
            const(w_pool.shape), const(pool_scale.shape), const(w_a.shape),
            const(w_al_p.shape), const(b_alpha.shape), const(gain.shape), const(w_b.shape),
            const(w_out.shape), const(ln_g.shape), const(ln_b.shape), const(w_r_p.shape),
            const(b_r_p.shape),
        ],
        out_specs=[pl.BlockSpec((1, tile, d), lambda j: (prev(j) // n_s, prev(j) % n_s, 0)),
                   pl.BlockSpec((tile * ROW_SUB, LANES), lambda j: (prev(j), 0)),
                   pl.BlockSpec((ROUTE_ROWS, tile), lambda j: (0, prev(j))),
                   pl.BlockSpec((N_EXPERTS, LANES), lambda j: (0, 0)),
                   pl.BlockSpec((8, LANES), lambda j: (0, 0))],
        out_shape=[
            jax.ShapeDtypeStruct((bsz, seq, d), F32),
            jax.ShapeDtypeStruct((bsz * seq * ROW_SUB, LANES), F32),
            jax.ShapeDtypeStruct((ROUTE_ROWS, bsz * seq), F32),
            jax.ShapeDtypeStruct((N_EXPERTS, LANES), F32),
            jax.ShapeDtypeStruct((8, LANES), F32),
        ],
        scratch_shapes=[
            pltpu.VMEM((POOL_HALO + tile, d), F32),
            pltpu.VMEM((GLA_HEADS, GLA_HEAD_V, GLA_HEAD_K), F32),
            pltpu.VMEM((N_EXPERTS, LANES), F32),
            pltpu.VMEM((tile, d), F32),
            pltpu.VMEM((tile, d), F32),
            pltpu.VMEM((tile // CHUNK * GLA_HEADS, CHUNK, CHUNK), F32),
        ],
        compiler_params=pltpu.CompilerParams(
            dimension_semantics=("arbitrary",), vmem_limit_bytes=VMEM_LIMIT),
        name="mixer",
    )(x, mod, mod, w_in_p, w_rank_p, w_gates, w_pool, pool_scale, w_a, w_al_p, b_alpha, gain, w_b, w_out,
      ln_g, ln_b, w_r_p, b_r_p)


def _row_at(ref, row8):
    return ref.at[pl.ds(pl.multiple_of(row8, ROW_SUB), ROW_SUB)]


def _per_tile(dest8, tile):
    n_t = dest8.shape[1] // tile
    return dest8.reshape(TOP_K, n_t, tile).transpose(1, 0, 2).reshape(n_t, 1, TOP_K * tile)


def _dispatch_kernel(fill_ref, dest_ref, u2_ref, rows_hbm, zbuf, sem_fill, sem_rows):
    i = pl.program_id(0)
    tile = u2_ref.shape[0] // ROW_SUB
    blk8 = EXPERT_BLOCK * ROW_SUB

    def zero_fills(phase):
        def piece(first_row, n_rows):
            dst = rows_hbm.at[pl.ds(pl.multiple_of(first_row * ROW_SUB, ROW_SUB), n_rows * ROW_SUB)]
            getattr(pltpu.make_async_copy(zbuf.at[pl.ds(0, n_rows * ROW_SUB)], dst, sem_fill.at[0]), phase)()

        for e in range(N_EXPERTS):
            first_row, gap = fill_ref[e], fill_ref[N_EXPERTS + e]
            n_rows = EXPERT_BLOCK // 2
            while n_rows >= 1:
                @pl.when(jnp.bitwise_and(gap, n_rows) != 0)
                def _(first_row=first_row, n_rows=n_rows):
                    piece(first_row, n_rows)
                first_row = first_row + jnp.bitwise_and(gap, n_rows)
                n_rows //= 2

        def unused(window, carry):
            piece(window * EXPERT_BLOCK, EXPERT_BLOCK)
            return carry

        lax.fori_loop(fill_ref[2 * N_EXPERTS], rows_hbm.shape[0] // blk8, unused, 0)

    @pl.when(i == 0)
    def _():
        zbuf[...] = jnp.zeros_like(zbuf)
        zero_fills("start")

    def body(r, carry):
        for kk in range(TOP_K):
            pltpu.make_async_copy(_row_at(u2_ref, r * ROW_SUB), _row_at(rows_hbm, dest_ref[0, 0, kk * tile + r]),
                                  sem_rows.at[0]).start(priority=kk % 2)
        return carry

    lax.fori_loop(0, tile, body, 0, unroll=DMA_UNROLL)
    for _ in range(TOP_K):
        pltpu.make_async_copy(u2_ref, rows_hbm.at[pl.ds(0, tile * ROW_SUB)], sem_rows.at[0]).wait()

    @pl.when(i == pl.num_programs(0) - 1)
    def _():
        zero_fills("wait")


def _dispatch(fill_start, dest8, u2_rows, n_rows):
    n_tok = u2_rows.shape[0] // ROW_SUB
    tile = DISPATCH_TILE
    n_t = n_tok // tile
    grid_spec = pltpu.PrefetchScalarGridSpec(
        num_scalar_prefetch=1,
        grid=(n_t,),
        in_specs=[
            pl.BlockSpec((1, 1, tile * TOP_K), lambda i, fs: (i, 0, 0), memory_space=pltpu.SMEM),
            pl.BlockSpec((tile * ROW_SUB, LANES), lambda i, fs: (i, 0)),
        ],
        out_specs=pl.BlockSpec(memory_space=pl.ANY),
        scratch_shapes=[
            pltpu.VMEM((EXPERT_BLOCK * ROW_SUB, LANES), F32),
            pltpu.SemaphoreType.DMA((1,)),
            pltpu.SemaphoreType.DMA((1,)),
        ],
    )
    return pl.pallas_call(
        _dispatch_kernel,
        grid_spec=grid_spec,
        out_shape=jax.ShapeDtypeStruct(((n_rows + EXPERT_BLOCK) * ROW_SUB, LANES), F32),
        compiler_params=pltpu.CompilerParams(dimension_semantics=("arbitrary",)),
        name="dispatch",
    )(fill_start, _per_tile(dest8, tile), u2_rows)


def _moe_kernel(be_ref, nxt_ref, par_ref, full_ref, nb_ref, x_ref, bg_ref, bu_ref, bd_ref, wgu_hbm, wd_hbm, y_ref,
                wgu_buf, wd_buf, wg_s, wu_s, wd_s, sem_gu, sem_d):
    i = pl.program_id(0)
    new_expert = (i == 0) | (be_ref[i] != be_ref[jnp.maximum(i - 1, 0)])

    def fetch(e, s):
        return (pltpu.make_async_copy(wgu_hbm.at[e], wgu_buf.at[s], sem_gu.at[s]),
                pltpu.make_async_copy(wd_hbm.at[e], wd_buf.at[s], sem_d.at[s]))

    @pl.when(new_expert & (i < nb_ref[0]))
    def _():
        slot = par_ref[i]

        @pl.when(i == 0)
        def _():
            for cp in fetch(be_ref[0], slot):
                cp.start()

        @pl.when(nxt_ref[i] >= 0)
        def _():
            for cp in fetch(nxt_ref[i], 1 - slot):
                cp.start()

        for cp in fetch(be_ref[i], slot):
            cp.wait()
        src = lax.broadcasted_iota(jnp.int32, (MXU_COLS, MXU_COLS), 0)
        col = lax.broadcasted_iota(jnp.int32, (MXU_COLS, MXU_COLS), 1)
        half = MXU_COLS // 2
        want = jnp.where(col < half, 2 * col, 2 * (col - half) + 1)
        unzip = (src == want).astype(BF16)
        for g in range(wgu_buf.shape[2] // MXU_COLS):
            blk = wgu_buf[slot, :, g * MXU_COLS:(g + 1) * MXU_COLS].astype(BF16)
            sep = _dot(blk, unzip)
            wg_s[:, g * half:(g + 1) * half] = sep[:, :half].astype(BF16)
            wu_s[:, g * half:(g + 1) * half] = sep[:, half:].astype(BF16)
        wd_s[...] = wd_buf[slot].astype(BF16)

    def expert_mlp(n_rows):
        xb = _load_rows(x_ref, n_rows).astype(BF16)
        gate = jnp.minimum(_dot(xb, wg_s[...]) + bg_ref[0], SWIGLU_LIMIT)
        up = jnp.clip(_dot(xb, wu_s[...]) + bu_ref[0], -SWIGLU_LIMIT, SWIGLU_LIMIT)
        glu = gate * _sigmoid(gate * SWIGLU_ALPHA)
        _store_rows(y_ref, _dot(((up + 1.0) * glu).astype(BF16), wd_s[...]) + bd_ref[0])

    active = i < nb_ref[0]
    both_halves = full_ref[i] == 1

    @pl.when(active & both_halves)
    def _():
        expert_mlp(EXPERT_BLOCK)

    @pl.when(active & jnp.logical_not(both_halves))
    def _():
        expert_mlp(EXPERT_HALF)
        y_ref[EXPERT_HALF * ROW_SUB:, :] = jnp.zeros((EXPERT_HALF * ROW_SUB, LANES), F32)

    @pl.when(jnp.logical_not(active))
    def _():
        y_ref[...] = jnp.zeros_like(y_ref)


def _moe(block_expert, next_expert, slot_parity, both_halves, n_active, x_rows, w_gate_up, bg, bu, w_down, bd):
    n_blocks = block_expert.shape[0]
    _, d, f2 = w_gate_up.shape
    f = f2 // 2
    blk8 = EXPERT_BLOCK * ROW_SUB
    rows_in = pl.BlockSpec((blk8, LANES),
                           lambda i, be, nx, pr, fl, nb: (jnp.maximum(jnp.minimum(i, nb[0] - 1), 0), 0))
    rows_out = pl.BlockSpec((blk8, LANES), lambda i, be, nx, pr, fl, nb: (i, 0))
    per_expert = lambda shape: pl.BlockSpec((1,) + shape, lambda i, be, nx, pr, fl, nb: (be[i], 0, 0))
    hbm = pl.BlockSpec(memory_space=pl.ANY)
    grid_spec = pltpu.PrefetchScalarGridSpec(
        num_scalar_prefetch=5,
        grid=(n_blocks,),
        in_specs=[rows_in, per_expert((1, f)), per_expert((1, f)), per_expert((1, d)), hbm, hbm],
        out_specs=rows_out,
        scratch_shapes=[pltpu.VMEM((2, d, f2), F32), pltpu.VMEM((2, f, d), F32),
                        pltpu.VMEM((d, f), BF16), pltpu.VMEM((d, f), BF16), pltpu.VMEM((f, d), BF16),
                        pltpu.SemaphoreType.DMA((2,)), pltpu.SemaphoreType.DMA((2,))],
    )
    return pl.pallas_call(
        _moe_kernel,
        grid_spec=grid_spec,
        out_shape=jax.ShapeDtypeStruct((n_blocks * blk8, LANES), F32),
        compiler_params=pltpu.CompilerParams(
            dimension_semantics=("arbitrary",), vmem_limit_bytes=VMEM_LIMIT),
        name="moe",
    )(block_expert, next_expert, slot_parity, both_halves, n_active, x_rows, bg, bu, bd, w_gate_up, w_down)


def _combine_kernel(dest_cur_ref, dest_nxt_ref, x1_ref, mod_ref, route_ref, ln_g_ref, ln_b_ref, y_hbm,
                    o_ref, ybuf, sem):
    i = pl.program_id(0)
    n_steps = pl.num_programs(0)
    tile = x1_ref.shape[0]
    slot = lax.rem(i, 2)

    def gather(dest_ref, s):
        def body(r, carry):
            for kk in range(TOP_K):
                pltpu.make_async_copy(_row_at(y_hbm, dest_ref[0, 0, kk * tile + r]),
                                      _row_at(ybuf.at[s, kk], r * ROW_SUB), sem.at[s]).start(priority=kk % 2)
            return carry
        lax.fori_loop(0, tile, body, 0, unroll=DMA_UNROLL)

    @pl.when(i == 0)
    def _():
        gather(dest_cur_ref, 0)

    def wait_slot(s):
        for kk in range(TOP_K):
            pltpu.make_async_copy(y_hbm.at[pl.ds(0, tile * ROW_SUB)], ybuf.at[s, kk], sem.at[s]).wait()

    for r in range(tile):
        for kk in range(TOP_K):
            pltpu.make_async_copy(_row_at(y_hbm, dest_nxt_ref[0, 0, kk * tile + r]),
                                  _row_at(ybuf.at[1 - slot, kk], r * ROW_SUB),
                                  sem.at[1 - slot]).start(priority=kk % 2)
    wait_slot(slot)

    g_f = mod_ref[0][5:6]
    weight = route_ref[...]
    y = jnp.zeros(x1_ref.shape, F32)
    for kk in range(TOP_K):
        y = y + weight[:, kk:kk + 1] * _load_rows(ybuf, tile, (slot, kk))
    o_ref[...] = _layer_norm(DEEPNORM_ALPHA * x1_ref[...] + g_f * y, ln_g_ref[...], ln_b_ref[...])

    @pl.when(i == n_steps - 1)
    def _():
        wait_slot(1 - slot)


def _combine(dest8, x1, mod, route, ln_g, ln_b, y_rows, seq):
    n_tok, d = x1.shape
    tile = COMBINE_TILE
    n_t = n_tok // tile
    per_seq = seq // tile
    dest3 = _per_tile(dest8, tile)
    smem_blk = lambda fn: pl.BlockSpec((1, 1, tile * TOP_K), fn, memory_space=pltpu.SMEM)
    row = lambda width: pl.BlockSpec((tile, width), lambda i: (i, 0))
    vec = pl.BlockSpec((1, d), lambda i: (0, 0))
    return pl.pallas_call(
        _combine_kernel,
        grid=(n_t,),
        in_specs=[smem_blk(lambda i: (i, 0, 0)), smem_blk(lambda i: (jnp.minimum(i + 1, n_t - 1), 0, 0)),
                  row(d), pl.BlockSpec((1, 6, d), lambda i: (i // per_seq, 0, 0)), row(TOP_K), vec, vec,
                  pl.BlockSpec(memory_space=pl.ANY)],
        out_specs=row(d),
        out_shape=jax.ShapeDtypeStruct((n_tok, d), F32),
        scratch_shapes=[pltpu.VMEM((2, TOP_K, tile * ROW_SUB, LANES), F32), pltpu.SemaphoreType.DMA((2,))],
        compiler_params=pltpu.CompilerParams(
            dimension_semantics=("arbitrary",), vmem_limit_bytes=VMEM_LIMIT),
        name="combine",
    )(dest3, dest3, x1, mod, route, ln_g, ln_b, y_rows)


def _split_in_proj(w_in):
    rank_lo = _OFF[5]
    rank_hi = rank_lo + GLA_GATE_RANK
    pad = jnp.zeros((w_in.shape[0], RANK_PAD - GLA_GATE_RANK), BF16)
    return (w_in[:, :rank_lo].astype(BF16),
            jnp.concatenate([w_in[:, rank_lo:rank_hi].astype(BF16), pad], axis=1),
            w_in[:, rank_hi:].astype(BF16))


def _layer(x, c, w_ada, b_ada, w_in, w_pool_group, pool_scale, w_branch_a, w_alpha_up, b_alpha,
           gla_norm_gain, w_branch_b, w_out, ln1_gain, ln1_bias, w_router, b_router,
           w_gate_up, b_gate_up, w_down, b_down, ln2_gain, ln2_bias):
    bsz, seq, d = x.shape
    n_tok = bsz * seq
    n_assign = n_tok * TOP_K
    row2 = lambda v: v.reshape(1, -1)

    mod = _ada(c, w_ada, b_ada).reshape(bsz, 6, d)

    w_in_p, w_rank_p, w_gates = _split_in_proj(w_in)
    w_al_p = jnp.concatenate(
        [w_alpha_up, jnp.zeros((RANK_PAD - GLA_GATE_RANK, GLA_KEY_DIM), w_alpha_up.dtype)], axis=0).astype(BF16)
    w_r_t = w_router.T
    w_r_hi = w_r_t.astype(BF16)
    w_r_split = jnp.concatenate([w_r_hi, (w_r_t - w_r_hi.astype(F32)).astype(BF16)], axis=0)
    mixer_args = (x, mod, w_in_p, w_rank_p, w_gates, w_pool_group.astype(BF16), row2(pool_scale),
                  w_branch_a.astype(BF16), w_al_p, row2(b_alpha), row2(gla_norm_gain), w_branch_b.astype(BF16),
                  w_out.astype(BF16), row2(ln1_gain), row2(ln1_bias), w_r_split, b_router.reshape(N_EXPERTS, 1))
    *mixed, sub_decay = _mixer(False, *mixer_args)

    def routed_half(x1, u2_rows, route_t, counts):
        return _routed_moe(x1, u2_rows, route_t, counts, mod, w_gate_up, b_gate_up, w_down, b_down, ln2_gain,
                           ln2_bias)

    return lax.cond(sub_decay[0, 0] > SAFE_SUB_DECAY,
                    lambda: routed_half(*_mixer(True, *mixer_args)[:4]),
                    lambda: routed_half(*mixed))


def _routed_moe(x1, u2_rows, route_t, counts, mod, w_gate_up, b_gate_up, w_down, b_down, ln2_gain, ln2_bias):
    bsz, seq, d = x1.shape
    n_tok = bsz * seq
    n_assign = n_tok * TOP_K
    row2 = lambda v: v.reshape(1, -1)

    top_idx = route_t[0:TOP_K].astype(jnp.int32)
    rank = route_t[TOP_K:2 * TOP_K].astype(jnp.int32)
    route = route_t[2 * TOP_K:3 * TOP_K].T
    counts = counts[:, 0].astype(jnp.int32)
    padded = (counts + EXPERT_BLOCK - 1) // EXPERT_BLOCK * EXPERT_BLOCK
    pad_end = jnp.cumsum(padded)
    pad_start = pad_end - padded
    n_rows = (n_assign + N_EXPERTS * (EXPERT_BLOCK - 1) + EXPERT_BLOCK - 1) // EXPERT_BLOCK * EXPERT_BLOCK
    n_blocks = n_rows // EXPERT_BLOCK
    n_active = (pad_end[-1] // EXPERT_BLOCK).astype(jnp.int32)
    start_of = jnp.sum(jnp.where(top_idx[..., None] == jnp.arange(N_EXPERTS, dtype=jnp.int32), pad_start, 0),
                       axis=-1)
    dest8 = (start_of + rank) * ROW_SUB
    blk_row = jnp.arange(n_blocks, dtype=jnp.int32)[:, None] * EXPERT_BLOCK
    block_expert = jnp.minimum(jnp.sum(pad_end[None, :] <= blk_row, axis=1), N_EXPERTS - 1).astype(jnp.int32)

    fill_table = jnp.concatenate([pad_start + counts, padded - counts, n_active.reshape(1)]).astype(jnp.int32)
    x_rows = _dispatch(fill_table, dest8, u2_rows, n_rows)
    f = w_down.shape[1]
    bg = b_gate_up[:, 0::2].reshape(N_EXPERTS, 1, f)
    bu = b_gate_up[:, 1::2].reshape(N_EXPERTS, 1, f)
    eid = jnp.arange(N_EXPERTS, dtype=jnp.int32)
    owns = counts > 0
    later = jnp.where((eid[None, :] > eid[:, None]) & owns[None, :], eid[None, :], N_EXPERTS)
    next_of = jnp.min(later, axis=1)
    next_of = jnp.where(next_of == N_EXPERTS, -1, next_of).astype(jnp.int32)
    parity_of = ((jnp.cumsum(owns.astype(jnp.int32)) - 1) % 2).astype(jnp.int32)
    is_block_expert = block_expert[:, None] == eid[None, :]
    per_block = lambda v: jnp.sum(jnp.where(is_block_expert, v[None, :], 0), axis=1).astype(jnp.int32)
    left = per_block(pad_start + counts) - blk_row[:, 0]
    both_halves = (left > EXPERT_HALF).astype(jnp.int32)
    y_rows = _moe(block_expert, per_block(next_of), per_block(parity_of), both_halves, n_active.reshape(1),
                  x_rows, w_gate_up, bg, bu, w_down, b_down.reshape(N_EXPERTS, 1, d))
    out = _combine(dest8, x1.reshape(n_tok, d), mod, route, row2(ln2_gain), row2(ln2_bias), y_rows, seq)
    return out.reshape(bsz, seq, d)


def kernel(x, c, w_ada, b_ada, w_in, w_pool_group, pool_scale, w_branch_a, w_alpha_up, b_alpha, gla_norm_gain,
           w_branch_b, w_out, ln1_gain, ln1_bias, w_router, b_router, w_gate_up, b_gate_up, w_down, b_down,
           ln2_gain, ln2_bias):
    for l in range(DEPTH):
        x = _layer(x, c, w_ada[l], b_ada[l], w_in[l], w_pool_group[l], pool_scale[l], w_branch_a[l],
                   w_alpha_up[l], b_alpha[l], gla_norm_gain[l], w_branch_b[l], w_out[l], ln1_gain[l],
                   ln1_bias[l], w_router[l], b_router[l], w_gate_up[l], b_gate_up[l], w_down[l], b_down[l],
                   ln2_gain[l], ln2_bias[l])
    return x
```

```python
import functools

import jax
import jax.numpy as jnp
from jax import lax
from jax.experimental import pallas as pl
from jax.experimental.pallas import tpu as pltpu

D_MODEL = 1024
CHUNK = 64
SUB = 16
N_SUB = CHUNK // SUB
POOL_WINDOWS = (2, 4, 8, 16)
POOL_GROUP_WIDTH = D_MODEL // len(POOL_WINDOWS)
POOL_HALO = 16
GLA_HEADS = 4
GLA_KEY_DIM = D_MODEL // 2
GLA_HEAD_K = GLA_KEY_DIM // GLA_HEADS
GLA_HEAD_V = D_MODEL // GLA_HEADS
GLA_GATE_RANK = 16
GLA_TAU = 16.0
N_EXPERTS = 32
TOP_K = 4
SWIGLU_ALPHA = 1.702
SWIGLU_LIMIT = 7.0
EXPERT_BLOCK = 512
EXPERT_HALF = EXPERT_BLOCK // 2
LN_EPS = 1e-5
RMS_EPS = 1e-6
DEPTH = 1
DEEPNORM_ALPHA = (2.0 * DEPTH) ** 0.25

LANES = 128
MXU_COLS = 256
RANK_PAD = LANES
_W = (D_MODEL, GLA_KEY_DIM, GLA_KEY_DIM, D_MODEL, D_MODEL, RANK_PAD, D_MODEL, D_MODEL)
_OFF = tuple(sum(_W[:i]) for i in range(len(_W) + 1))
EXP_CAP = 60.0
SAFE_SUB_DECAY = 40.0
ROUTE_ROWS = 16

SEQ_TILE = 512
DISPATCH_TILE = 2048
COMBINE_TILE = 256
DMA_UNROLL = 8
VMEM_LIMIT = 60 * 1024 * 1024

F32 = jnp.float32
BF16 = jnp.bfloat16
HI = lax.Precision.HIGHEST


def _dot(a, b):
    return jnp.dot(a, b, preferred_element_type=F32)


def _dot_nt(a, b):
    return lax.dot_general(a, b, (((1,), (1,)), ((), ())), preferred_element_type=F32)


def _dot_tn(a, b):
    return lax.dot_general(a, b, (((0,), (0,)), ((), ())), preferred_element_type=F32)


def _sigmoid(v):
    return 0.5 * jnp.tanh(0.5 * v) + 0.5


ROW_SUB = D_MODEL // LANES


def _load_rows(ref, n_rows, lead=()):
    return jnp.concatenate(
        [ref[lead + (pl.ds(j, n_rows, stride=ROW_SUB), slice(None))] for j in range(ROW_SUB)], axis=1)


def _store_rows(ref, val):
    for j in range(ROW_SUB):
        ref[pl.ds(j, val.shape[0], stride=ROW_SUB), :] = val[:, j * LANES:(j + 1) * LANES]


def _ada_kernel(c_ref, w_ref, b_ref, o_ref):
    c = c_ref[...]
    s = c * jax.nn.sigmoid(c)
    o_ref[...] = jnp.dot(s, w_ref[...], precision=HI, preferred_element_type=F32) + b_ref[...]


def _ada(c, w_ada, b_ada):
    bsz, d = c.shape
    n = w_ada.shape[1]
    tn = 1024
    return pl.pallas_call(
        _ada_kernel,
        grid=(n // tn,),
        in_specs=[
            pl.BlockSpec((bsz, d), lambda j: (0, 0)),
            pl.BlockSpec((d, tn), lambda j: (0, j)),
            pl.BlockSpec((1, tn), lambda j: (0, j)),
        ],
        out_specs=pl.BlockSpec((bsz, tn), lambda j: (0, j)),
        out_shape=jax.ShapeDtypeStruct((bsz, n), F32),
        name="ada",
    )(c, w_ada, b_ada.reshape(1, n))


def _layer_norm(z, gain, bias):
    mu = jnp.mean(z, axis=-1, keepdims=True)
    zc = z - mu
    var = jnp.mean(zc * zc, axis=-1, keepdims=True)
    return zc * lax.rsqrt(var + LN_EPS) * gain + bias


def _mixer_kernel(x_ref, mod_ref, mod_prev_ref, w_in_ref, w_rank_ref, w_gates_ref, w_pool_ref, pool_scale_ref,
                  w_a_ref, w_al_ref,
                  b_al_ref, gain_ref, w_b_ref, w_out_ref, ln_g_ref, ln_b_ref, w_r_ref, b_r_ref,
                  x1_ref, u2_ref, route_ref, counts_ref, decay_ref,
                  a_ext, s_ref, cnt_ref, o_ref, z_ref, sc_ref, *, tiles_per_seq, n_tiles, direct_scores):
    j = pl.program_id(0)
    s_idx = lax.rem(jnp.minimum(j, n_tiles - 1), tiles_per_seq)
    tile = x_ref.shape[1]

    @pl.when(s_idx == 0)
    def _():
        s_ref[...] = jnp.zeros_like(s_ref)
        a_ext[0:POOL_HALO, :] = jnp.zeros((POOL_HALO, D_MODEL), F32)

    @pl.when(j == 0)
    def _():
        cnt_ref[...] = jnp.zeros_like(cnt_ref)
        z_ref[...] = jnp.zeros_like(z_ref)
        decay_ref[...] = jnp.zeros_like(decay_ref)


    mod_prev = mod_prev_ref[0]
    x1 = _layer_norm(z_ref[...], ln_g_ref[...], ln_b_ref[...])
    x1_ref[0] = x1
    u2 = x1 * (1.0 + mod_prev[4:5]) + mod_prev[3:4]
    _store_rows(u2_ref, u2)
    u2_hi = u2.astype(BF16)
    u2_lo = (u2 - u2_hi.astype(F32)).astype(BF16)

    mod = mod_ref[0]
    sh_m, sc_m, g_m = mod[0:1], mod[1:2], mod[2:3]
    x = x_ref[0]
    u = (x * (1.0 + sc_m) + sh_m).astype(BF16)

    def proj(i):
        return _dot(u, w_in_ref[:, _OFF[i]:_OFF[i + 1]])

    gate_cols = ([(w_in_ref, _OFF[4] + jj * MXU_COLS) for jj in range(D_MODEL // MXU_COLS)]
                 + [(w_gates_ref, jj * MXU_COLS) for jj in range(2 * D_MODEL // MXU_COLS)])
    fillers = [functools.partial(lambda w_ref, c0: _dot(u, w_ref[:, c0:c0 + MXU_COLS]), w_ref, c0)
               for w_ref, c0 in gate_cols]
    filled = []

    def issue_fillers(n):
        for _ in range(n):
            if len(filled) < len(fillers):
                filled.append(fillers[len(filled)]())

    a = proj(0)
    q = proj(1) * (GLA_HEAD_K ** -0.5)
    k_all = proj(2)
    v_all = proj(3)
    alpha_low = _dot(u, w_rank_ref[...])

    part = _dot_nt(w_r_ref[...], u2_hi)
    logits = (part[:N_EXPERTS] + part[N_EXPERTS:] + _dot_nt(w_r_ref[0:N_EXPERTS, :], u2_lo)
              + b_r_ref[...])
    issue_fillers(2)

    a_ext[POOL_HALO:POOL_HALO + tile, :] = a
    t_glob = s_idx * tile + lax.broadcasted_iota(jnp.int32, (tile, 1), 0)
    mapped = []
    for g, w in enumerate(POOL_WINDOWS):
        lo, hi = g * POOL_GROUP_WIDTH, (g + 1) * POOL_GROUP_WIDTH
        win = a_ext[:, lo:hi]
        k = 1
        while k < w:
            win = win + pltpu.roll(win, k, 0)
            k *= 2
        inv_cnt = 1.0 / jnp.minimum(t_glob + 1, w).astype(F32)
        pooled = win[POOL_HALO:, :] * inv_cnt - a[:, lo:hi]
        mapped.append(_dot(pooled.astype(BF16), w_pool_ref[g]))
    a_ext[0:POOL_HALO, :] = a[tile - POOL_HALO:tile, :]
    ya = _dot((jnp.concatenate(mapped, axis=1) * pool_scale_ref[...]).astype(BF16), w_a_ref[...])

    erow = lax.broadcasted_iota(jnp.int32, (N_EXPERTS, tile), 0).astype(F32)
    work = logits
    sel = jnp.zeros((N_EXPERTS, tile), F32)
    vals, hits = [], []
    for _ in range(TOP_K):
        m = jnp.max(work, axis=0, keepdims=True)
        idx = jnp.min(jnp.where(work == m, erow, float(N_EXPERTS)), axis=0, keepdims=True)
        hit = erow == idx
        vals.append(m)
        hits.append((idx, hit))
        sel = jnp.where(hit, 1.0, sel)
        work = jnp.where(hit, -jnp.inf, work)
    exps = [jnp.exp(v - vals[0]) for v in vals]
    inv_den = 1.0 / (exps[0] + exps[1] + exps[2] + exps[3])
    issue_fillers(1)

    z = _dot(alpha_low.astype(BF16), w_al_ref[...]) + b_al_ref[...]
    log_a = (jnp.minimum(z, 0.0) - jnp.log1p(jnp.exp(-jnp.abs(z)))) * (1.0 / GLA_TAU)

    ri = lax.broadcasted_iota(jnp.int32, (CHUNK, CHUNK), 0)
    ci = lax.broadcasted_iota(jnp.int32, (CHUNK, CHUNK), 1)
    causal = ci <= ri
    cum_mat = causal.astype(BF16)
    la_hi = log_a.astype(BF16)
    la_split = jnp.concatenate([la_hi, (log_a - la_hi.astype(F32)).astype(BF16)], axis=1)

    n_chunks = tile // CHUNK
    pairs = [(c, h) for c in range(n_chunks) for h in range(GLA_HEADS)]
    rows_of = lambda c: slice(c * CHUNK, (c + 1) * CHUNK)
    ks_of = lambda h: slice(h * GLA_HEAD_K, (h + 1) * GLA_HEAD_K)
    vs_of = lambda h: slice(h * GLA_HEAD_V, (h + 1) * GLA_HEAD_V)

    b_cum, ref_pts, b_ref_pt = [], [], []
    for c in range(n_chunks):
        cum = _dot(cum_mat, la_split[rows_of(c)])
        b_cum.append(cum[:, :GLA_KEY_DIM] + cum[:, GLA_KEY_DIM:])
        ref_pts.append([jnp.zeros((1, GLA_KEY_DIM), F32)]
                       + [b_cum[c][i * SUB - 1:i * SUB, :] for i in range(1, N_SUB)])
        b_ref_pt.append(jnp.concatenate([jnp.broadcast_to(p, (SUB, GLA_KEY_DIM)) for p in ref_pts[c]], axis=0))

    rt = lax.broadcasted_iota(jnp.int32, (tile, tile), 0)
    ct = lax.broadcasted_iota(jnp.int32, (tile, tile), 1)
    base = _dot(sel.astype(BF16), (rt < ct).astype(BF16)) + cnt_ref[:, 0:1]

    sub_decay = b_ref_pt[0] - b_cum[0]
    for c in range(1, n_chunks):
        sub_decay = jnp.maximum(sub_decay, b_ref_pt[c] - b_cum[c])
    decay_ref[...] = jnp.maximum(decay_ref[...], jnp.max(sub_decay, keepdims=True))

    scores = {}
    if not direct_scores:
        for n_pair, (c, h) in enumerate(pairs):
            if n_pair % 3 == 0:
                issue_fillers(1)
            qh, kh, bh = q[rows_of(c), ks_of(h)], k_all[rows_of(c), ks_of(h)], b_cum[c][:, ks_of(h)]
            q_dec = (qh * jnp.exp(bh - b_ref_pt[c][:, ks_of(h)])).astype(BF16)
            k_dec = jnp.concatenate(
                [(kh * jnp.exp(jnp.minimum(ref_pts[c][i][:, ks_of(h)] - bh, EXP_CAP))).astype(BF16)
                 for i in range(N_SUB)], axis=0)
            s_all = _dot_nt(q_dec, k_dec)
            scores[c, h] = jnp.concatenate(
                [s_all[i * SUB:(i + 1) * SUB, i * CHUNK:(i + 1) * CHUNK] for i in range(N_SUB)], axis=0)
    else:
        o_ref[:, 0:GLA_KEY_DIM] = q
        o_ref[:, GLA_KEY_DIM:2 * GLA_KEY_DIM] = jnp.concatenate(b_cum, axis=0)
        for n_pair, (c, h) in enumerate(pairs):
            kh, bh = k_all[rows_of(c), ks_of(h)], b_cum[c][:, ks_of(h)]

            def score_rows(g, carry):
                first = pl.multiple_of(g * 8, 8)
                q_8 = o_ref[pl.ds(c * CHUNK + first, 8), ks_of(h)]
                b_8 = o_ref[pl.ds(c * CHUNK + first, 8),
                            GLA_KEY_DIM + h * GLA_HEAD_K:GLA_KEY_DIM + (h + 1) * GLA_HEAD_K]
                rows = []
                for r in range(8):
                    k_i = (kh * jnp.exp(jnp.minimum(b_8[r:r + 1] - bh, 0.0))).astype(BF16)
                    q_i = jnp.broadcast_to(q_8[r:r + 1], (8, GLA_HEAD_K)).astype(BF16)
                    rows.append(_dot_nt(q_i, k_i)[0:1])
                sc_ref[n_pair, pl.ds(first, 8), :] = jnp.concatenate(rows, axis=0)
                return carry

            lax.fori_loop(0, CHUNK // 8, score_rows, 0)

    o_intra, kv, decay_last = {}, {}, {}
    for n_pair, (c, h) in enumerate(pairs):
        kh, bh = k_all[rows_of(c), ks_of(h)], b_cum[c][:, ks_of(h)]
        vh = v_all[rows_of(c), vs_of(h)].astype(BF16)
        chunk_scores = sc_ref[n_pair] if direct_scores else scores[c, h]
        o_intra[c, h] = _dot(jnp.where(causal, chunk_scores, 0.0).astype(BF16), vh)
        b_last = bh[CHUNK - 1:CHUNK, :]
        kv[c, h] = _dot_tn(vh, (kh * jnp.exp(b_last - bh)).astype(BF16))
        decay_last[c, h] = jnp.exp(b_last)

    orow = lax.broadcasted_iota(jnp.int32, (ROUTE_ROWS, tile), 0)
    route = jnp.zeros((ROUTE_ROWS, tile), F32)
    for kk in range(TOP_K):
        idx, hit = hits[kk]
        rank = jnp.sum(jnp.where(hit, base, 0.0), axis=0, keepdims=True)
        route = jnp.where(orow == kk, idx, route)
        route = jnp.where(orow == TOP_K + kk, rank, route)
        route = jnp.where(orow == 2 * TOP_K + kk, exps[kk] * inv_den, route)
    route_ref[...] = route
    cnt_ref[...] = cnt_ref[...] + (j > 0).astype(F32) * jnp.sum(sel, axis=1, keepdims=True)
    counts_ref[...] = cnt_ref[...]

    o_inter = {}
    state_t = [s_ref[h] for h in range(GLA_HEADS)]
    for c, h in pairs:
        q_in = (q[rows_of(c), ks_of(h)] * jnp.exp(b_cum[c][:, ks_of(h)])).astype(BF16)
        o_inter[c, h] = _dot_nt(q_in, state_t[h].astype(BF16))
        state_t[h] = state_t[h] * decay_last[c, h] + kv[c, h]
    for h in range(GLA_HEADS):
        s_ref[h] = state_t[h]

    for c, h in pairs:
        o = o_intra[c, h] + o_inter[c, h]
        o = o * lax.rsqrt(jnp.mean(o * o, axis=-1, keepdims=True) + RMS_EPS) * gain_ref[...]
        o_ref[rows_of(c), vs_of(h)] = o

    issue_fillers(len(fillers))
    per_proj = D_MODEL // MXU_COLS
    r, gate_a, gate_b = (jnp.concatenate(filled[n * per_proj:(n + 1) * per_proj], axis=1) for n in range(3))
    yb = _dot((o_ref[...] * (r * _sigmoid(r))).astype(BF16), w_b_ref[...])

    merged = _sigmoid(gate_a) * ya + _sigmoid(gate_b) * yb
    y = _dot(merged.astype(BF16), w_out_ref[...])
    z_ref[...] = DEEPNORM_ALPHA * x + g_m * y


def _mixer(direct_scores, x, mod, w_in_p, w_rank_p, w_gates, w_pool, pool_scale, w_a, w_al_p, b_alpha, gain, w_b,
           w_out, ln_g, ln_b, w_r_p, b_r_p):
    bsz, seq, d = x.shape
    tile = SEQ_TILE
    n_s = seq // tile
    n_tiles = bsz * n_s

    def const(shape):
        nd = len(shape)
        return pl.BlockSpec(shape, lambda j: (0,) * nd, pipeline_mode=pl.Buffered(1))

    cur = lambda j: jnp.minimum(j, n_tiles - 1)
    prev = lambda j: jnp.maximum(j - 1, 0)
    return pl.pallas_call(
        functools.partial(_mixer_kernel, tiles_per_seq=n_s, n_tiles=n_tiles, direct_scores=direct_scores),
        grid=(n_tiles + 1,),
        in_specs=[
            pl.BlockSpec((1, tile, d), lambda j: (cur(j) // n_s, cur(j) % n_s, 0)),
            pl.BlockSpec((1, 6, d), lambda j: (cur(j) // n_s, 0, 0)),
            pl.BlockSpec((1, 6, d), lambda j: (prev(j) // n_s, 0, 0)),
            const(w_in_p.shape), const(w_rank_p.shape), const(w_gates.shape),
            const(w_pool.shape), const(pool_scale.shape), const(w_a.shape),
            const(w_al_p.shape), const(b_alpha.shape), const(gain.shape), const(w_b.shape),
            const(w_out.shape), const(ln_g.shape), const(ln_b.shape), const(w_r_p.shape),
            const(b_r_p.shape),
        ],
        out_specs=[pl.BlockSpec((1, tile, d), lambda j: (prev(j) // n_s, prev(j) % n_s, 0)),
                   pl.BlockSpec((tile * ROW_SUB, LANES), lambda j: (prev(j), 0)),
                   pl.BlockSpec((ROUTE_ROWS, tile), lambda j: (0, prev(j))),
                   pl.BlockSpec((N_EXPERTS, LANES), lambda j: (0, 0)),
                   pl.BlockSpec((8, LANES), lambda j: (0, 0))],
        out_shape=[
            jax.ShapeDtypeStruct((bsz, seq, d), F32),
            jax.ShapeDtypeStruct((bsz * seq * ROW_SUB, LANES), F32),
            jax.ShapeDtypeStruct((ROUTE_ROWS, bsz * seq), F32),
            jax.ShapeDtypeStruct((N_EXPERTS, LANES), F32),
            jax.ShapeDtypeStruct((8, LANES), F32),
        ],
        scratch_shapes=[
            pltpu.VMEM((POOL_HALO + tile, d), F32),
            pltpu.VMEM((GLA_HEADS, GLA_HEAD_V, GLA_HEAD_K), F32),
            pltpu.VMEM((N_EXPERTS, LANES), F32),
            pltpu.VMEM((tile, d), F32),
            pltpu.VMEM((tile, d), F32),
            pltpu.VMEM((tile // CHUNK * GLA_HEADS, CHUNK, CHUNK), F32),
        ],
        compiler_params=pltpu.CompilerParams(
            dimension_semantics=("arbitrary",), vmem_limit_bytes=VMEM_LIMIT),
        name="mixer",
    )(x, mod, mod, w_in_p, w_rank_p, w_gates, w_pool, pool_scale, w_a, w_al_p, b_alpha, gain, w_b, w_out,
      ln_g, ln_b, w_r_p, b_r_p)


def _row_at(ref, row8):
    return ref.at[pl.ds(pl.multiple_of(row8, ROW_SUB), ROW_SUB)]


def _per_tile(dest8, tile):
    n_t = dest8.shape[1] // tile
    return dest8.reshape(TOP_K, n_t, tile).transpose(1, 0, 2).reshape(n_t, 1, TOP_K * tile)


def _dispatch_kernel(fill_ref, dest_ref, u2_ref, rows_hbm, zbuf, sem_fill, sem_rows):
    i = pl.program_id(0)
    tile = u2_ref.shape[0] // ROW_SUB
    blk8 = EXPERT_BLOCK * ROW_SUB

    def zero_fills(phase):
        def piece(first_row, n_rows):
            dst = rows_hbm.at[pl.ds(pl.multiple_of(first_row * ROW_SUB, ROW_SUB), n_rows * ROW_SUB)]
            getattr(pltpu.make_async_copy(zbuf.at[pl.ds(0, n_rows * ROW_SUB)], dst, sem_fill.at[0]), phase)()

        for e in range(N_EXPERTS):
            first_row, gap = fill_ref[e], fill_ref[N_EXPERTS + e]
            n_rows = EXPERT_BLOCK // 2
            while n_rows >= 1:
                @pl.when(jnp.bitwise_and(gap, n_rows) != 0)
                def _(first_row=first_row, n_rows=n_rows):
                    piece(first_row, n_rows)
                first_row = first_row + jnp.bitwise_and(gap, n_rows)
                n_rows //= 2

        def unused(window, carry):
            piece(window * EXPERT_BLOCK, EXPERT_BLOCK)
            return carry

        lax.fori_loop(fill_ref[2 * N_EXPERTS], rows_hbm.shape[0] // blk8, unused, 0)

    @pl.when(i == 0)
    def _():
        zbuf[...] = jnp.zeros_like(zbuf)
        zero_fills("start")

    def body(r, carry):
        for kk in range(TOP_K):
            pltpu.make_async_copy(_row_at(u2_ref, r * ROW_SUB), _row_at(rows_hbm, dest_ref[0, 0, kk * tile + r]),
                                  sem_rows.at[0]).start(priority=kk % 2)
        return carry

    lax.fori_loop(0, tile, body, 0, unroll=DMA_UNROLL)
    for _ in range(TOP_K):
        pltpu.make_async_copy(u2_ref, rows_hbm.at[pl.ds(0, tile * ROW_SUB)], sem_rows.at[0]).wait()

    @pl.when(i == pl.num_programs(0) - 1)
    def _():
        zero_fills("wait")


def _dispatch(fill_start, dest8, u2_rows, n_rows):
    n_tok = u2_rows.shape[0] // ROW_SUB
    tile = DISPATCH_TILE
    n_t = n_tok // tile
    grid_spec = pltpu.PrefetchScalarGridSpec(
        num_scalar_prefetch=1,
        grid=(n_t,),
        in_specs=[
            pl.BlockSpec((1, 1, tile * TOP_K), lambda i, fs: (i, 0, 0), memory_space=pltpu.SMEM),
            pl.BlockSpec((tile * ROW_SUB, LANES), lambda i, fs: (i, 0)),
        ],
        out_specs=pl.BlockSpec(memory_space=pl.ANY),
        scratch_shapes=[
            pltpu.VMEM((EXPERT_BLOCK * ROW_SUB, LANES), F32),
            pltpu.SemaphoreType.DMA((1,)),
            pltpu.SemaphoreType.DMA((1,)),
        ],
    )
    return pl.pallas_call(
        _dispatch_kernel,
        grid_spec=grid_spec,
        out_shape=jax.ShapeDtypeStruct(((n_rows + EXPERT_BLOCK) * ROW_SUB, LANES), F32),
        compiler_params=pltpu.CompilerParams(dimension_semantics=("arbitrary",)),
        name="dispatch",
    )(fill_start, _per_tile(dest8, tile), u2_rows)


def _moe_kernel(be_ref, nxt_ref, par_ref, full_ref, nb_ref, x_ref, bg_ref, bu_ref, bd_ref, wgu_hbm, wd_hbm, y_ref,
                wgu_buf, wd_buf, wg_s, wu_s, wd_s, sem_gu, sem_d):
    i = pl.program_id(0)
    new_expert = (i == 0) | (be_ref[i] != be_ref[jnp.maximum(i - 1, 0)])

    def fetch(e, s):
        return (pltpu.make_async_copy(wgu_hbm.at[e], wgu_buf.at[s], sem_gu.at[s]),
                pltpu.make_async_copy(wd_hbm.at[e], wd_buf.at[s], sem_d.at[s]))

    @pl.when(new_expert & (i < nb_ref[0]))
    def _():
        slot = par_ref[i]

        @pl.when(i == 0)
        def _():
            for cp in fetch(be_ref[0], slot):
                cp.start()

        @pl.when(nxt_ref[i] >= 0)
        def _():
            for cp in fetch(nxt_ref[i], 1 - slot):
                cp.start()

        for cp in fetch(be_ref[i], slot):
            cp.wait()
        src = lax.broadcasted_iota(jnp.int32, (MXU_COLS, MXU_COLS), 0)
        col = lax.broadcasted_iota(jnp.int32, (MXU_COLS, MXU_COLS), 1)
        half = MXU_COLS // 2
        want = jnp.where(col < half, 2 * col, 2 * (col - half) + 1)
        unzip = (src == want).astype(BF16)
        for g in range(wgu_buf.shape[2] // MXU_COLS):
            blk = wgu_buf[slot, :, g * MXU_COLS:(g + 1) * MXU_COLS].astype(BF16)
            sep = _dot(blk, unzip)
            wg_s[:, g * half:(g + 1) * half] = sep[:, :half].astype(BF16)
            wu_s[:, g * half:(g + 1) * half] = sep[:, half:].astype(BF16)
        wd_s[...] = wd_buf[slot].astype(BF16)

    def expert_mlp(n_rows):
        xb = _load_rows(x_ref, n_rows).astype(BF16)
        gate = jnp.minimum(_dot(xb, wg_s[...]) + bg_ref[0], SWIGLU_LIMIT)
        up = jnp.clip(_dot(xb, wu_s[...]) + bu_ref[0], -SWIGLU_LIMIT, SWIGLU_LIMIT)
        glu = gate * _sigmoid(gate * SWIGLU_ALPHA)
        _store_rows(y_ref, _dot(((up + 1.0) * glu).astype(BF16), wd_s[...]) + bd_ref[0])

    active = i < nb_ref[0]
    both_halves = full_ref[i] == 1

    @pl.when(active & both_halves)
    def _():
        expert_mlp(EXPERT_BLOCK)

    @pl.when(active & jnp.logical_not(both_halves))
    def _():
        expert_mlp(EXPERT_HALF)
        y_ref[EXPERT_HALF * ROW_SUB:, :] = jnp.zeros((EXPERT_HALF * ROW_SUB, LANES), F32)

    @pl.when(jnp.logical_not(active))
    def _():
        y_ref[...] = jnp.zeros_like(y_ref)


def _moe(block_expert, next_expert, slot_parity, both_halves, n_active, x_rows, w_gate_up, bg, bu, w_down, bd):
    n_blocks = block_expert.shape[0]
    _, d, f2 = w_gate_up.shape
    f = f2 // 2
    blk8 = EXPERT_BLOCK * ROW_SUB
    rows_in = pl.BlockSpec((blk8, LANES),
                           lambda i, be, nx, pr, fl, nb: (jnp.maximum(jnp.minimum(i, nb[0] - 1), 0), 0))
    rows_out = pl.BlockSpec((blk8, LANES), lambda i, be, nx, pr, fl, nb: (i, 0))
    per_expert = lambda shape: pl.BlockSpec((1,) + shape, lambda i, be, nx, pr, fl, nb: (be[i], 0, 0))
    hbm = pl.BlockSpec(memory_space=pl.ANY)
    grid_spec = pltpu.PrefetchScalarGridSpec(
        num_scalar_prefetch=5,
        grid=(n_blocks,),
        in_specs=[rows_in, per_expert((1, f)), per_expert((1, f)), per_expert((1, d)), hbm, hbm],
        out_specs=rows_out,
        scratch_shapes=[pltpu.VMEM((2, d, f2), F32), pltpu.VMEM((2, f, d), F32),
                        pltpu.VMEM((d, f), BF16), pltpu.VMEM((d, f), BF16), pltpu.VMEM((f, d), BF16),
                        pltpu.SemaphoreType.DMA((2,)), pltpu.SemaphoreType.DMA((2,))],
    )
    return pl.pallas_call(
        _moe_kernel,
        grid_spec=grid_spec,
        out_shape=jax.ShapeDtypeStruct((n_blocks * blk8, LANES), F32),
        compiler_params=pltpu.CompilerParams(
            dimension_semantics=("arbitrary",), vmem_limit_bytes=VMEM_LIMIT),
        name="moe",
    )(block_expert, next_expert, slot_parity, both_halves, n_active, x_rows, bg, bu, bd, w_gate_up, w_down)


def _combine_kernel(dest_cur_ref, dest_nxt_ref, x1_ref, mod_ref, route_ref, ln_g_ref, ln_b_ref, y_hbm,
                    o_ref, ybuf, sem):
    i = pl.program_id(0)
    n_steps = pl.num_programs(0)
    tile = x1_ref.shape[0]
    slot = lax.rem(i, 2)

    def gather(dest_ref, s):
        def body(r, carry):
            for kk in range(TOP_K):
                pltpu.make_async_copy(_row_at(y_hbm, dest_ref[0, 0, kk * tile + r]),
                                      _row_at(ybuf.at[s, kk], r * ROW_SUB), sem.at[s]).start(priority=kk % 2)
            return carry
        lax.fori_loop(0, tile, body, 0, unroll=DMA_UNROLL)

    @pl.when(i == 0)
    def _():
        gather(dest_cur_ref, 0)

    def wait_slot(s):
        for kk in range(TOP_K):
            pltpu.make_async_copy(y_hbm.at[pl.ds(0, tile * ROW_SUB)], ybuf.at[s, kk], sem.at[s]).wait()

    for r in range(tile):
        for kk in range(TOP_K):
            pltpu.make_async_copy(_row_at(y_hbm, dest_nxt_ref[0, 0, kk * tile + r]),
                                  _row_at(ybuf.at[1 - slot, kk], r * ROW_SUB),
                                  sem.at[1 - slot]).start(priority=kk % 2)
    wait_slot(slot)

    g_f = mod_ref[0][5:6]
    weight = route_ref[...]
    y = jnp.zeros(x1_ref.shape, F32)
    for kk in range(TOP_K):
        y = y + weight[:, kk:kk + 1] * _load_rows(ybuf, tile, (slot, kk))
    o_ref[...] = _layer_norm(DEEPNORM_ALPHA * x1_ref[...] + g_f * y, ln_g_ref[...], ln_b_ref[...])

    @pl.when(i == n_steps - 1)
    def _():
        wait_slot(1 - slot)


def _combine(dest8, x1, mod, route, ln_g, ln_b, y_rows, seq):
    n_tok, d = x1.shape
    tile = COMBINE_TILE
    n_t = n_tok // tile
    per_seq = seq // tile
    dest3 = _per_tile(dest8, tile)
    smem_blk = lambda fn: pl.BlockSpec((1, 1, tile * TOP_K), fn, memory_space=pltpu.SMEM)
    row = lambda width: pl.BlockSpec((tile, width), lambda i: (i, 0))
    vec = pl.BlockSpec((1, d), lambda i: (0, 0))
    return pl.pallas_call(
        _combine_kernel,
        grid=(n_t,),
        in_specs=[smem_blk(lambda i: (i, 0, 0)), smem_blk(lambda i: (jnp.minimum(i + 1, n_t - 1), 0, 0)),
                  row(d), pl.BlockSpec((1, 6, d), lambda i: (i // per_seq, 0, 0)), row(TOP_K), vec, vec,
                  pl.BlockSpec(memory_space=pl.ANY)],
        out_specs=row(d),
        out_shape=jax.ShapeDtypeStruct((n_tok, d), F32),
        scratch_shapes=[pltpu.VMEM((2, TOP_K, tile * ROW_SUB, LANES), F32), pltpu.SemaphoreType.DMA((2,))],
        compiler_params=pltpu.CompilerParams(
            dimension_semantics=("arbitrary",), vmem_limit_bytes=VMEM_LIMIT),
        name="combine",
    )(dest3, dest3, x1, mod, route, ln_g, ln_b, y_rows)


def _split_kernel(w_ref, main_ref, rank_ref, gates_ref):
    rank_lo = _OFF[5]
    rank_hi = rank_lo + GLA_GATE_RANK
    rows = w_ref.shape[0]
    main_ref[...] = w_ref[:, 0:rank_lo].astype(BF16)
    pad = jnp.zeros((rows, RANK_PAD - GLA_GATE_RANK), F32)
    rank_ref[...] = jnp.concatenate([w_ref[:, rank_lo:rank_hi], pad], axis=1).astype(BF16)
    gates_ref[...] = w_ref[:, rank_hi:rank_hi + 2 * D_MODEL].astype(BF16)


def _split_in_proj(w_in):
    d, n = w_in.shape
    rows = 128
    piece = lambda width: pl.BlockSpec((rows, width), lambda i: (i, 0))
    return pl.pallas_call(
        _split_kernel,
        grid=(d // rows,),
        in_specs=[piece(n)],
        out_specs=[piece(_OFF[5]), piece(RANK_PAD), piece(2 * D_MODEL)],
        out_shape=[jax.ShapeDtypeStruct((d, _OFF[5]), BF16), jax.ShapeDtypeStruct((d, RANK_PAD), BF16),
                   jax.ShapeDtypeStruct((d, 2 * D_MODEL), BF16)],
        name="split_in_proj",
    )(w_in)


def _layer(x, c, w_ada, b_ada, w_in, w_pool_group, pool_scale, w_branch_a, w_alpha_up, b_alpha,
           gla_norm_gain, w_branch_b, w_out, ln1_gain, ln1_bias, w_router, b_router,
           w_gate_up, b_gate_up, w_down, b_down, ln2_gain, ln2_bias):
    bsz, seq, d = x.shape
    n_tok = bsz * seq
    n_assign = n_tok * TOP_K
    row2 = lambda v: v.reshape(1, -1)

    mod = _ada(c, w_ada, b_ada).reshape(bsz, 6, d)

    w_in_p, w_rank_p, w_gates = _split_in_proj(w_in)
    w_al_p = jnp.concatenate(
        [w_alpha_up, jnp.zeros((RANK_PAD - GLA_GATE_RANK, GLA_KEY_DIM), w_alpha_up.dtype)], axis=0).astype(BF16)
    w_r_t = w_router.T
    w_r_hi = w_r_t.astype(BF16)
    w_r_split = jnp.concatenate([w_r_hi, (w_r_t - w_r_hi.astype(F32)).astype(BF16)], axis=0)
    mixer_args = (x, mod, w_in_p, w_rank_p, w_gates, w_pool_group.astype(BF16), row2(pool_scale),
                  w_branch_a.astype(BF16), w_al_p, row2(b_alpha), row2(gla_norm_gain), w_branch_b.astype(BF16),
                  w_out.astype(BF16), row2(ln1_gain), row2(ln1_bias), w_r_split, b_router.reshape(N_EXPERTS, 1))
    *mixed, sub_decay = _mixer(False, *mixer_args)

    def routed_half(x1, u2_rows, route_t, counts):
        return _routed_moe(x1, u2_rows, route_t, counts, mod, w_gate_up, b_gate_up, w_down, b_down, ln2_gain,
                           ln2_bias)

    return lax.cond(sub_decay[0, 0] > SAFE_SUB_DECAY,
                    lambda: routed_half(*_mixer(True, *mixer_args)[:4]),
                    lambda: routed_half(*mixed))


def _routed_moe(x1, u2_rows, route_t, counts, mod, w_gate_up, b_gate_up, w_down, b_down, ln2_gain, ln2_bias):
    bsz, seq, d = x1.shape
    n_tok = bsz * seq
    n_assign = n_tok * TOP_K
    row2 = lambda v: v.reshape(1, -1)

    top_idx = route_t[0:TOP_K].astype(jnp.int32)
    rank = route_t[TOP_K:2 * TOP_K].astype(jnp.int32)
    route = route_t[2 * TOP_K:3 * TOP_K].T
    counts = counts[:, 0].astype(jnp.int32)
    padded = (counts + EXPERT_BLOCK - 1) // EXPERT_BLOCK * EXPERT_BLOCK
    pad_end = jnp.cumsum(padded)
    pad_start = pad_end - padded
    n_rows = (n_assign + N_EXPERTS * (EXPERT_BLOCK - 1) + EXPERT_BLOCK - 1) // EXPERT_BLOCK * EXPERT_BLOCK
    n_blocks = n_rows // EXPERT_BLOCK
    n_active = (pad_end[-1] // EXPERT_BLOCK).astype(jnp.int32)
    start_of = jnp.sum(jnp.where(top_idx[..., None] == jnp.arange(N_EXPERTS, dtype=jnp.int32), pad_start, 0),
                       axis=-1)
    dest8 = (start_of + rank) * ROW_SUB
    blk_row = jnp.arange(n_blocks, dtype=jnp.int32)[:, None] * EXPERT_BLOCK
    block_expert = jnp.minimum(jnp.sum(pad_end[None, :] <= blk_row, axis=1), N_EXPERTS - 1).astype(jnp.int32)

    fill_table = jnp.concatenate([pad_start + counts, padded - counts, n_active.reshape(1)]).astype(jnp.int32)
    x_rows = _dispatch(fill_table, dest8, u2_rows, n_rows)
    f = w_down.shape[1]
    bg = b_gate_up[:, 0::2].reshape(N_EXPERTS, 1, f)
    bu = b_gate_up[:, 1::2].reshape(N_EXPERTS, 1, f)
    eid = jnp.arange(N_EXPERTS, dtype=jnp.int32)
    owns = counts > 0
    later = jnp.where((eid[None, :] > eid[:, None]) & owns[None, :], eid[None, :], N_EXPERTS)
    next_of = jnp.min(later, axis=1)
    next_of = jnp.where(next_of == N_EXPERTS, -1, next_of).astype(jnp.int32)
    parity_of = ((jnp.cumsum(owns.astype(jnp.int32)) - 1) % 2).astype(jnp.int32)
    is_block_expert = block_expert[:, None] == eid[None, :]
    per_block = lambda v: jnp.sum(jnp.where(is_block_expert, v[None, :], 0), axis=1).astype(jnp.int32)
    left = per_block(pad_start + counts) - blk_row[:, 0]
    both_halves = (left > EXPERT_HALF).astype(jnp.int32)
    y_rows = _moe(block_expert, per_block(next_of), per_block(parity_of), both_halves, n_active.reshape(1),
                  x_rows, w_gate_up, bg, bu, w_down, b_down.reshape(N_EXPERTS, 1, d))
    out = _combine(dest8, x1.reshape(n_tok, d), mod, route, row2(ln2_gain), row2(ln2_bias), y_rows, seq)
    return out.reshape(bsz, seq, d)


def kernel(x, c, w_ada, b_ada, w_in, w_pool_group, pool_scale, w_branch_a, w_alpha_up, b_alpha, gla_norm_gain,
           w_branch_b, w_out, ln1_gain, ln1_bias, w_router, b_router, w_gate_up, b_gate_up, w_down, b_down,
           ln2_gain, ln2_bias):
    for l in range(DEPTH):
        x = _layer(x, c, w_ada[l], b_ada[l], w_in[l], w_pool_group[l], pool_scale[l], w_branch_a[l],
                   w_alpha_up[l], b_alpha[l], gla_norm_gain[l], w_branch_b[l], w_out[l], ln1_gain[l],
                   ln1_bias[l], w_router[l], b_router[l], w_gate_up[l], b_gate_up[l], w_down[l], b_down[l],
                   ln2_gain[l], ln2_bias[l])
    return x
```

```python
import functools

import jax
import jax.numpy as jnp
from jax import lax
from jax.experimental import pallas as pl
from jax.experimental.pallas import tpu as pltpu

D_MODEL = 1024
CHUNK = 64
SUB = 16
N_SUB = CHUNK // SUB
POOL_WINDOWS = (2, 4, 8, 16)
POOL_GROUP_WIDTH = D_MODEL // len(POOL_WINDOWS)
POOL_HALO = 16
GLA_HEADS = 4
GLA_KEY_DIM = D_MODEL // 2
GLA_HEAD_K = GLA_KEY_DIM // GLA_HEADS
GLA_HEAD_V = D_MODEL // GLA_HEADS
GLA_GATE_RANK = 16
GLA_TAU = 16.0
N_EXPERTS = 32
TOP_K = 4
SWIGLU_ALPHA = 1.702
SWIGLU_LIMIT = 7.0
EXPERT_BLOCK = 512
EXPERT_HALF = EXPERT_BLOCK // 2
LN_EPS = 1e-5
RMS_EPS = 1e-6
DEPTH = 1
DEEPNORM_ALPHA = (2.0 * DEPTH) ** 0.25

LANES = 128
MXU_COLS = 256
RANK_PAD = LANES
_W = (D_MODEL, GLA_KEY_DIM, GLA_KEY_DIM, D_MODEL, D_MODEL, RANK_PAD, D_MODEL, D_MODEL)
_OFF = tuple(sum(_W[:i]) for i in range(len(_W) + 1))
EXP_CAP = 60.0
SAFE_SUB_DECAY = 40.0
ROUTE_ROWS = 16

SEQ_TILE = 512
DISPATCH_TILE = 2048
COMBINE_TILE = 256
DMA_UNROLL = 8
VMEM_LIMIT = 60 * 1024 * 1024

F32 = jnp.float32
BF16 = jnp.bfloat16
HI = lax.Precision.HIGHEST


def _dot(a, b):
    return jnp.dot(a, b, preferred_element_type=F32)


def _dot_nt(a, b):
    return lax.dot_general(a, b, (((1,), (1,)), ((), ())), preferred_element_type=F32)


def _dot_tn(a, b):
    return lax.dot_general(a, b, (((0,), (0,)), ((), ())), preferred_element_type=F32)


def _sigmoid(v):
    return 0.5 * jnp.tanh(0.5 * v) + 0.5


ROW_SUB = D_MODEL // LANES


def _load_rows(ref, n_rows, lead=()):
    return jnp.concatenate(
        [ref[lead + (pl.ds(j, n_rows, stride=ROW_SUB), slice(None))] for j in range(ROW_SUB)], axis=1)


def _store_rows(ref, val):
    for j in range(ROW_SUB):
        ref[pl.ds(j, val.shape[0], stride=ROW_SUB), :] = val[:, j * LANES:(j + 1) * LANES]


def _ada_kernel(c_ref, w_ref, b_ref, o_ref):
    c = c_ref[...]
    s = c * jax.nn.sigmoid(c)
    o_ref[...] = jnp.dot(s, w_ref[...], precision=HI, preferred_element_type=F32) + b_ref[...]


def _ada(c, w_ada, b_ada):
    bsz, d = c.shape
    n = w_ada.shape[1]
    tn = 1024
    return pl.pallas_call(
        _ada_kernel,
        grid=(n // tn,),
        in_specs=[
            pl.BlockSpec((bsz, d), lambda j: (0, 0)),
            pl.BlockSpec((d, tn), lambda j: (0, j)),
            pl.BlockSpec((1, tn), lambda j: (0, j)),
        ],
        out_specs=pl.BlockSpec((bsz, tn), lambda j: (0, j)),
        out_shape=jax.ShapeDtypeStruct((bsz, n), F32),
        name="ada",
    )(c, w_ada, b_ada.reshape(1, n))


def _layer_norm(z, gain, bias):
    mu = jnp.mean(z, axis=-1, keepdims=True)
    zc = z - mu
    var = jnp.mean(zc * zc, axis=-1, keepdims=True)
    return zc * lax.rsqrt(var + LN_EPS) * gain + bias


def _mixer_kernel(x_ref, mod_ref, mod_prev_ref, w_in_ref, w_rank_ref, w_gates_ref, w_pool_ref, pool_scale_ref,
                  w_a_ref, w_al_ref,
                  b_al_ref, gain_ref, w_b_ref, w_out_ref, ln_g_ref, ln_b_ref, w_r_ref, b_r_ref,
                  x1_ref, u2_ref, route_ref, counts_ref, decay_ref,
                  a_ext, s_ref, cnt_ref, o_ref, z_ref, sc_ref, *, tiles_per_seq, n_tiles, direct_scores):
    j = pl.program_id(0)
    s_idx = lax.rem(jnp.minimum(j, n_tiles - 1), tiles_per_seq)
    tile = x_ref.shape[1]

    @pl.when(s_idx == 0)
    def _():
        s_ref[...] = jnp.zeros_like(s_ref)
        a_ext[0:POOL_HALO, :] = jnp.zeros((POOL_HALO, D_MODEL), F32)

    @pl.when(j == 0)
    def _():
        cnt_ref[...] = jnp.zeros_like(cnt_ref)
        z_ref[...] = jnp.zeros_like(z_ref)
        decay_ref[...] = jnp.zeros_like(decay_ref)


    mod_prev = mod_prev_ref[0]
    x1 = _layer_norm(z_ref[...], ln_g_ref[...], ln_b_ref[...])
    x1_ref[0] = x1
    u2 = x1 * (1.0 + mod_prev[4:5]) + mod_prev[3:4]
    _store_rows(u2_ref, u2)
    u2_hi = u2.astype(BF16)
    u2_lo = (u2 - u2_hi.astype(F32)).astype(BF16)

    mod = mod_ref[0]
    sh_m, sc_m, g_m = mod[0:1], mod[1:2], mod[2:3]
    x = x_ref[0]
    u = (x * (1.0 + sc_m) + sh_m).astype(BF16)

    def proj(i):
        return _dot_nt(u, w_in_ref[_OFF[i]:_OFF[i + 1], :])

    gate_cols = ([(w_in_ref, _OFF[4] + jj * MXU_COLS) for jj in range(D_MODEL // MXU_COLS)]
                 + [(w_gates_ref, jj * MXU_COLS) for jj in range(2 * D_MODEL // MXU_COLS)])
    fillers = [functools.partial(lambda w_ref, c0: _dot_nt(u, w_ref[c0:c0 + MXU_COLS, :]), w_ref, c0)
               for w_ref, c0 in gate_cols]
    filled = []

    def issue_fillers(n):
        for _ in range(n):
            if len(filled) < len(fillers):
                filled.append(fillers[len(filled)]())

    a = proj(0)
    q = proj(1) * (GLA_HEAD_K ** -0.5)
    k_all = proj(2)
    v_all = proj(3)
    alpha_low = _dot_nt(u, w_rank_ref[...])

    part = _dot_nt(w_r_ref[...], u2_hi)
    logits = (part[:N_EXPERTS] + part[N_EXPERTS:] + _dot_nt(w_r_ref[0:N_EXPERTS, :], u2_lo)
              + b_r_ref[...])
    issue_fillers(2)

    a_ext[POOL_HALO:POOL_HALO + tile, :] = a
    t_glob = s_idx * tile + lax.broadcasted_iota(jnp.int32, (tile, 1), 0)
    mapped = []
    for g, w in enumerate(POOL_WINDOWS):
        lo, hi = g * POOL_GROUP_WIDTH, (g + 1) * POOL_GROUP_WIDTH
        win = a_ext[:, lo:hi]
        k = 1
        while k < w:
            win = win + pltpu.roll(win, k, 0)
            k *= 2
        inv_cnt = 1.0 / jnp.minimum(t_glob + 1, w).astype(F32)
        pooled = win[POOL_HALO:, :] * inv_cnt - a[:, lo:hi]
        mapped.append(_dot(pooled.astype(BF16), w_pool_ref[g]))
    a_ext[0:POOL_HALO, :] = a[tile - POOL_HALO:tile, :]
    ya = _dot((jnp.concatenate(mapped, axis=1) * pool_scale_ref[...]).astype(BF16), w_a_ref[...])

    erow = lax.broadcasted_iota(jnp.int32, (N_EXPERTS, tile), 0).astype(F32)
    work = logits
    sel = jnp.zeros((N_EXPERTS, tile), F32)
    vals, hits = [], []
    for _ in range(TOP_K):
        m = jnp.max(work, axis=0, keepdims=True)
        idx = jnp.min(jnp.where(work == m, erow, float(N_EXPERTS)), axis=0, keepdims=True)
        hit = erow == idx
        vals.append(m)
        hits.append((idx, hit))
        sel = jnp.where(hit, 1.0, sel)
        work = jnp.where(hit, -jnp.inf, work)
    exps = [jnp.exp(v - vals[0]) for v in vals]
    inv_den = 1.0 / (exps[0] + exps[1] + exps[2] + exps[3])
    issue_fillers(1)

    z = _dot(alpha_low.astype(BF16), w_al_ref[...]) + b_al_ref[...]
    log_a = (jnp.minimum(z, 0.0) - jnp.log1p(jnp.exp(-jnp.abs(z)))) * (1.0 / GLA_TAU)

    ri = lax.broadcasted_iota(jnp.int32, (CHUNK, CHUNK), 0)
    ci = lax.broadcasted_iota(jnp.int32, (CHUNK, CHUNK), 1)
    causal = ci <= ri
    cum_mat = causal.astype(BF16)
    la_hi = log_a.astype(BF16)
    la_split = jnp.concatenate([la_hi, (log_a - la_hi.astype(F32)).astype(BF16)], axis=1)

    n_chunks = tile // CHUNK
    pairs = [(c, h) for c in range(n_chunks) for h in range(GLA_HEADS)]
    rows_of = lambda c: slice(c * CHUNK, (c + 1) * CHUNK)
    ks_of = lambda h: slice(h * GLA_HEAD_K, (h + 1) * GLA_HEAD_K)
    vs_of = lambda h: slice(h * GLA_HEAD_V, (h + 1) * GLA_HEAD_V)

    b_cum, ref_pts, b_ref_pt = [], [], []
    for c in range(n_chunks):
        cum = _dot(cum_mat, la_split[rows_of(c)])
        b_cum.append(cum[:, :GLA_KEY_DIM] + cum[:, GLA_KEY_DIM:])
        ref_pts.append([jnp.zeros((1, GLA_KEY_DIM), F32)]
                       + [b_cum[c][i * SUB - 1:i * SUB, :] for i in range(1, N_SUB)])
        b_ref_pt.append(jnp.concatenate([jnp.broadcast_to(p, (SUB, GLA_KEY_DIM)) for p in ref_pts[c]], axis=0))

    rt = lax.broadcasted_iota(jnp.int32, (tile, tile), 0)
    ct = lax.broadcasted_iota(jnp.int32, (tile, tile), 1)
    base = _dot(sel.astype(BF16), (rt < ct).astype(BF16)) + cnt_ref[:, 0:1]

    sub_decay = b_ref_pt[0] - b_cum[0]
    for c in range(1, n_chunks):
        sub_decay = jnp.maximum(sub_decay, b_ref_pt[c] - b_cum[c])
    decay_ref[...] = jnp.maximum(decay_ref[...], jnp.max(sub_decay, keepdims=True))

    scores = {}
    if not direct_scores:
        for n_pair, (c, h) in enumerate(pairs):
            if n_pair % 3 == 0:
                issue_fillers(1)
            qh, kh, bh = q[rows_of(c), ks_of(h)], k_all[rows_of(c), ks_of(h)], b_cum[c][:, ks_of(h)]
            q_dec = (qh * jnp.exp(bh - b_ref_pt[c][:, ks_of(h)])).astype(BF16)
            k_dec = jnp.concatenate(
                [(kh * jnp.exp(jnp.minimum(ref_pts[c][i][:, ks_of(h)] - bh, EXP_CAP))).astype(BF16)
                 for i in range(N_SUB)], axis=0)
            s_all = _dot_nt(q_dec, k_dec)
            scores[c, h] = jnp.concatenate(
                [s_all[i * SUB:(i + 1) * SUB, i * CHUNK:(i + 1) * CHUNK] for i in range(N_SUB)], axis=0)
    else:
        o_ref[:, 0:GLA_KEY_DIM] = q
        o_ref[:, GLA_KEY_DIM:2 * GLA_KEY_DIM] = jnp.concatenate(b_cum, axis=0)
        for n_pair, (c, h) in enumerate(pairs):
            kh, bh = k_all[rows_of(c), ks_of(h)], b_cum[c][:, ks_of(h)]

            def score_rows(g, carry):
                first = pl.multiple_of(g * 8, 8)
                q_8 = o_ref[pl.ds(c * CHUNK + first, 8), ks_of(h)]
                b_8 = o_ref[pl.ds(c * CHUNK + first, 8),
                            GLA_KEY_DIM + h * GLA_HEAD_K:GLA_KEY_DIM + (h + 1) * GLA_HEAD_K]
                rows = []
                for r in range(8):
                    k_i = (kh * jnp.exp(jnp.minimum(b_8[r:r + 1] - bh, 0.0))).astype(BF16)
                    q_i = jnp.broadcast_to(q_8[r:r + 1], (8, GLA_HEAD_K)).astype(BF16)
                    rows.append(_dot_nt(q_i, k_i)[0:1])
                sc_ref[n_pair, pl.ds(first, 8), :] = jnp.concatenate(rows, axis=0)
                return carry

            lax.fori_loop(0, CHUNK // 8, score_rows, 0)

    o_intra, kv, decay_last = {}, {}, {}
    for n_pair, (c, h) in enumerate(pairs):
        kh, bh = k_all[rows_of(c), ks_of(h)], b_cum[c][:, ks_of(h)]
        vh = v_all[rows_of(c), vs_of(h)].astype(BF16)
        chunk_scores = sc_ref[n_pair] if direct_scores else scores[c, h]
        o_intra[c, h] = _dot(jnp.where(causal, chunk_scores, 0.0).astype(BF16), vh)
        b_last = bh[CHUNK - 1:CHUNK, :]
        kv[c, h] = _dot_tn(vh, (kh * jnp.exp(b_last - bh)).astype(BF16))
        decay_last[c, h] = jnp.exp(b_last)

    orow = lax.broadcasted_iota(jnp.int32, (ROUTE_ROWS, tile), 0)
    route = jnp.zeros((ROUTE_ROWS, tile), F32)
    for kk in range(TOP_K):
        idx, hit = hits[kk]
        rank = jnp.sum(jnp.where(hit, base, 0.0), axis=0, keepdims=True)
        route = jnp.where(orow == kk, idx, route)
        route = jnp.where(orow == TOP_K + kk, rank, route)
        route = jnp.where(orow == 2 * TOP_K + kk, exps[kk] * inv_den, route)
    route_ref[...] = route
    cnt_ref[...] = cnt_ref[...] + (j > 0).astype(F32) * jnp.sum(sel, axis=1, keepdims=True)
    counts_ref[...] = cnt_ref[...]

    o_inter = {}
    state_t = [s_ref[h] for h in range(GLA_HEADS)]
    for c, h in pairs:
        q_in = (q[rows_of(c), ks_of(h)] * jnp.exp(b_cum[c][:, ks_of(h)])).astype(BF16)
        o_inter[c, h] = _dot_nt(q_in, state_t[h].astype(BF16))
        state_t[h] = state_t[h] * decay_last[c, h] + kv[c, h]
    for h in range(GLA_HEADS):
        s_ref[h] = state_t[h]

    for c, h in pairs:
        o = o_intra[c, h] + o_inter[c, h]
        o = o * lax.rsqrt(jnp.mean(o * o, axis=-1, keepdims=True) + RMS_EPS) * gain_ref[...]
        o_ref[rows_of(c), vs_of(h)] = o

    issue_fillers(len(fillers))
    per_proj = D_MODEL // MXU_COLS
    r, gate_a, gate_b = (jnp.concatenate(filled[n * per_proj:(n + 1) * per_proj], axis=1) for n in range(3))
    yb = _dot((o_ref[...] * (r * _sigmoid(r))).astype(BF16), w_b_ref[...])

    merged = _sigmoid(gate_a) * ya + _sigmoid(gate_b) * yb
    y = _dot(merged.astype(BF16), w_out_ref[...])
    z_ref[...] = DEEPNORM_ALPHA * x + g_m * y


def _mixer(direct_scores, x, mod, w_in_p, w_rank_p, w_gates, w_pool, pool_scale, w_a, w_al_p, b_alpha, gain, w_b,
           w_out, ln_g, ln_b, w_r_p, b_r_p):
    bsz, seq, d = x.shape
    tile = SEQ_TILE
    n_s = seq // tile
    n_tiles = bsz * n_s

    def const(shape):
        nd = len(shape)
        return pl.BlockSpec(shape, lambda j: (0,) * nd, pipeline_mode=pl.Buffered(1))

    cur = lambda j: jnp.minimum(j, n_tiles - 1)
    prev = lambda j: jnp.maximum(j - 1, 0)
    return pl.pallas_call(
        functools.partial(_mixer_kernel, tiles_per_seq=n_s, n_tiles=n_tiles, direct_scores=direct_scores),
        grid=(n_tiles + 1,),
        in_specs=[
            pl.BlockSpec((1, tile, d), lambda j: (cur(j) // n_s, cur(j) % n_s, 0)),
            pl.BlockSpec((1, 6, d), lambda j: (cur(j) // n_s, 0, 0)),
            pl.BlockSpec((1, 6, d), lambda j: (prev(j) // n_s, 0, 0)),
            const(w_in_p.shape), const(w_rank_p.shape), const(w_gates.shape),
            const(w_pool.shape), const(pool_scale.shape), const(w_a.shape),
            const(w_al_p.shape), const(b_alpha.shape), const(gain.shape), const(w_b.shape),
            const(w_out.shape), const(ln_g.shape), const(ln_b.shape), const(w_r_p.shape),
            const(b_r_p.shape),
        ],
        out_specs=[pl.BlockSpec((1, tile, d), lambda j: (prev(j) // n_s, prev(j) % n_s, 0)),
                   pl.BlockSpec((tile * ROW_SUB, LANES), lambda j: (prev(j), 0)),
                   pl.BlockSpec((ROUTE_ROWS, tile), lambda j: (0, prev(j))),
                   pl.BlockSpec((N_EXPERTS, LANES), lambda j: (0, 0)),
                   pl.BlockSpec((8, LANES), lambda j: (0, 0))],
        out_shape=[
            jax.ShapeDtypeStruct((bsz, seq, d), F32),
            jax.ShapeDtypeStruct((bsz * seq * ROW_SUB, LANES), F32),
            jax.ShapeDtypeStruct((ROUTE_ROWS, bsz * seq), F32),
            jax.ShapeDtypeStruct((N_EXPERTS, LANES), F32),
            jax.ShapeDtypeStruct((8, LANES), F32),
        ],
        scratch_shapes=[
            pltpu.VMEM((POOL_HALO + tile, d), F32),
            pltpu.VMEM((GLA_HEADS, GLA_HEAD_V, GLA_HEAD_K), F32),
            pltpu.VMEM((N_EXPERTS, LANES), F32),
            pltpu.VMEM((tile, d), F32),
            pltpu.VMEM((tile, d), F32),
            pltpu.VMEM((tile // CHUNK * GLA_HEADS, CHUNK, CHUNK), F32),
        ],
        compiler_params=pltpu.CompilerParams(
            dimension_semantics=("arbitrary",), vmem_limit_bytes=VMEM_LIMIT),
        name="mixer",
    )(x, mod, mod, w_in_p, w_rank_p, w_gates, w_pool, pool_scale, w_a, w_al_p, b_alpha, gain, w_b, w_out,
      ln_g, ln_b, w_r_p, b_r_p)


def _row_at(ref, row8):
    return ref.at[pl.ds(pl.multiple_of(row8, ROW_SUB), ROW_SUB)]


def _per_tile(dest8, tile):
    n_t = dest8.shape[1] // tile
    return dest8.reshape(TOP_K, n_t, tile).transpose(1, 0, 2).reshape(n_t, 1, TOP_K * tile)


def _dispatch_kernel(fill_ref, dest_ref, u2_ref, rows_hbm, zbuf, sem_fill, sem_rows):
    i = pl.program_id(0)
    tile = u2_ref.shape[0] // ROW_SUB
    blk8 = EXPERT_BLOCK * ROW_SUB

    def zero_fills(phase):
        def piece(first_row, n_rows):
            dst = rows_hbm.at[pl.ds(pl.multiple_of(first_row * ROW_SUB, ROW_SUB), n_rows * ROW_SUB)]
            getattr(pltpu.make_async_copy(zbuf.at[pl.ds(0, n_rows * ROW_SUB)], dst, sem_fill.at[0]), phase)()

        for e in range(N_EXPERTS):
            first_row, gap = fill_ref[e], fill_ref[N_EXPERTS + e]
            n_rows = EXPERT_BLOCK // 2
            while n_rows >= 1:
                @pl.when(jnp.bitwise_and(gap, n_rows) != 0)
                def _(first_row=first_row, n_rows=n_rows):
                    piece(first_row, n_rows)
                first_row = first_row + jnp.bitwise_and(gap, n_rows)
                n_rows //= 2

        def unused(window, carry):
            piece(window * EXPERT_BLOCK, EXPERT_BLOCK)
            return carry

        lax.fori_loop(fill_ref[2 * N_EXPERTS], rows_hbm.shape[0] // blk8, unused, 0)

    @pl.when(i == 0)
    def _():
        zbuf[...] = jnp.zeros_like(zbuf)
        zero_fills("start")

    def body(r, carry):
        for kk in range(TOP_K):
            pltpu.make_async_copy(_row_at(u2_ref, r * ROW_SUB), _row_at(rows_hbm, dest_ref[0, 0, kk * tile + r]),
                                  sem_rows.at[0]).start(priority=kk % 2)
        return carry

    lax.fori_loop(0, tile, body, 0, unroll=DMA_UNROLL)
    for _ in range(TOP_K):
        pltpu.make_async_copy(u2_ref, rows_hbm.at[pl.ds(0, tile * ROW_SUB)], sem_rows.at[0]).wait()

    @pl.when(i == pl.num_programs(0) - 1)
    def _():
        zero_fills("wait")


def _dispatch(fill_start, dest8, u2_rows, n_rows):
    n_tok = u2_rows.shape[0] // ROW_SUB
    tile = DISPATCH_TILE
    n_t = n_tok // tile
    grid_spec = pltpu.PrefetchScalarGridSpec(
        num_scalar_prefetch=1,
        grid=(n_t,),
        in_specs=[
            pl.BlockSpec((1, 1, tile * TOP_K), lambda i, fs: (i, 0, 0), memory_space=pltpu.SMEM),
            pl.BlockSpec((tile * ROW_SUB, LANES), lambda i, fs: (i, 0)),
        ],
        out_specs=pl.BlockSpec(memory_space=pl.ANY),
        scratch_shapes=[
            pltpu.VMEM((EXPERT_BLOCK * ROW_SUB, LANES), F32),
            pltpu.SemaphoreType.DMA((1,)),
            pltpu.SemaphoreType.DMA((1,)),
        ],
    )
    return pl.pallas_call(
        _dispatch_kernel,
        grid_spec=grid_spec,
        out_shape=jax.ShapeDtypeStruct(((n_rows + EXPERT_BLOCK) * ROW_SUB, LANES), F32),
        compiler_params=pltpu.CompilerParams(dimension_semantics=("arbitrary",)),
        name="dispatch",
    )(fill_start, _per_tile(dest8, tile), u2_rows)


def _moe_kernel(be_ref, nxt_ref, par_ref, full_ref, nb_ref, x_ref, bg_ref, bu_ref, bd_ref, wgu_hbm, wd_hbm, y_ref,
                wgu_buf, wd_buf, wg_s, wu_s, wd_s, sem_gu, sem_d):
    i = pl.program_id(0)
    new_expert = (i == 0) | (be_ref[i] != be_ref[jnp.maximum(i - 1, 0)])

    def fetch(e, s):
        return (pltpu.make_async_copy(wgu_hbm.at[e], wgu_buf.at[s], sem_gu.at[s]),
                pltpu.make_async_copy(wd_hbm.at[e], wd_buf.at[s], sem_d.at[s]))

    @pl.when(new_expert & (i < nb_ref[0]))
    def _():
        slot = par_ref[i]

        @pl.when(i == 0)
        def _():
            for cp in fetch(be_ref[0], slot):
                cp.start()

        @pl.when(nxt_ref[i] >= 0)
        def _():
            for cp in fetch(nxt_ref[i], 1 - slot):
                cp.start()

        for cp in fetch(be_ref[i], slot):
            cp.wait()
        src = lax.broadcasted_iota(jnp.int32, (MXU_COLS, MXU_COLS), 0)
        col = lax.broadcasted_iota(jnp.int32, (MXU_COLS, MXU_COLS), 1)
        half = MXU_COLS // 2
        want = jnp.where(col < half, 2 * col, 2 * (col - half) + 1)
        unzip = (src == want).astype(BF16)
        for g in range(wgu_buf.shape[2] // MXU_COLS):
            blk = wgu_buf[slot, :, g * MXU_COLS:(g + 1) * MXU_COLS].astype(BF16)
            sep = _dot(blk, unzip)
            wg_s[:, g * half:(g + 1) * half] = sep[:, :half].astype(BF16)
            wu_s[:, g * half:(g + 1) * half] = sep[:, half:].astype(BF16)
        wd_s[...] = wd_buf[slot].astype(BF16)

    def expert_mlp(n_rows):
        xb = _load_rows(x_ref, n_rows).astype(BF16)
        gate = jnp.minimum(_dot(xb, wg_s[...]) + bg_ref[0], SWIGLU_LIMIT)
        up = jnp.clip(_dot(xb, wu_s[...]) + bu_ref[0], -SWIGLU_LIMIT, SWIGLU_LIMIT)
        glu = gate * _sigmoid(gate * SWIGLU_ALPHA)
        _store_rows(y_ref, _dot(((up + 1.0) * glu).astype(BF16), wd_s[...]) + bd_ref[0])

    active = i < nb_ref[0]
    both_halves = full_ref[i] == 1

    @pl.when(active & both_halves)
    def _():
        expert_mlp(EXPERT_BLOCK)

    @pl.when(active & jnp.logical_not(both_halves))
    def _():
        expert_mlp(EXPERT_HALF)
        y_ref[EXPERT_HALF * ROW_SUB:, :] = jnp.zeros((EXPERT_HALF * ROW_SUB, LANES), F32)

    @pl.when(jnp.logical_not(active))
    def _():
        y_ref[...] = jnp.zeros_like(y_ref)


def _moe(block_expert, next_expert, slot_parity, both_halves, n_active, x_rows, w_gate_up, bg, bu, w_down, bd):
    n_blocks = block_expert.shape[0]
    _, d, f2 = w_gate_up.shape
    f = f2 // 2
    blk8 = EXPERT_BLOCK * ROW_SUB
    rows_in = pl.BlockSpec((blk8, LANES),
                           lambda i, be, nx, pr, fl, nb: (jnp.maximum(jnp.minimum(i, nb[0] - 1), 0), 0))
    rows_out = pl.BlockSpec((blk8, LANES), lambda i, be, nx, pr, fl, nb: (i, 0))
    per_expert = lambda shape: pl.BlockSpec((1,) + shape, lambda i, be, nx, pr, fl, nb: (be[i], 0, 0))
    hbm = pl.BlockSpec(memory_space=pl.ANY)
    grid_spec = pltpu.PrefetchScalarGridSpec(
        num_scalar_prefetch=5,
        grid=(n_blocks,),
        in_specs=[rows_in, per_expert((1, f)), per_expert((1, f)), per_expert((1, d)), hbm, hbm],
        out_specs=rows_out,
        scratch_shapes=[pltpu.VMEM((2, d, f2), F32), pltpu.VMEM((2, f, d), F32),
                        pltpu.VMEM((d, f), BF16), pltpu.VMEM((d, f), BF16), pltpu.VMEM((f, d), BF16),
                        pltpu.SemaphoreType.DMA((2,)), pltpu.SemaphoreType.DMA((2,))],
    )
    return pl.pallas_call(
        _moe_kernel,
        grid_spec=grid_spec,
        out_shape=jax.ShapeDtypeStruct((n_blocks * blk8, LANES), F32),
        compiler_params=pltpu.CompilerParams(
            dimension_semantics=("arbitrary",), vmem_limit_bytes=VMEM_LIMIT),
        name="moe",
    )(block_expert, next_expert, slot_parity, both_halves, n_active, x_rows, bg, bu, bd, w_gate_up, w_down)


def _combine_kernel(dest_cur_ref, dest_nxt_ref, x1_ref, mod_ref, route_ref, ln_g_ref, ln_b_ref, y_hbm,
                    o_ref, ybuf, sem):
    i = pl.program_id(0)
    n_steps = pl.num_programs(0)
    tile = x1_ref.shape[0]
    slot = lax.rem(i, 2)

    def gather(dest_ref, s):
        def body(r, carry):
            for kk in range(TOP_K):
                pltpu.make_async_copy(_row_at(y_hbm, dest_ref[0, 0, kk * tile + r]),
                                      _row_at(ybuf.at[s, kk], r * ROW_SUB), sem.at[s]).start(priority=kk % 2)
            return carry
        lax.fori_loop(0, tile, body, 0, unroll=DMA_UNROLL)

    @pl.when(i == 0)
    def _():
        gather(dest_cur_ref, 0)

    def wait_slot(s):
        for kk in range(TOP_K):
            pltpu.make_async_copy(y_hbm.at[pl.ds(0, tile * ROW_SUB)], ybuf.at[s, kk], sem.at[s]).wait()

    for r in range(tile):
        for kk in range(TOP_K):
            pltpu.make_async_copy(_row_at(y_hbm, dest_nxt_ref[0, 0, kk * tile + r]),
                                  _row_at(ybuf.at[1 - slot, kk], r * ROW_SUB),
                                  sem.at[1 - slot]).start(priority=kk % 2)
    wait_slot(slot)

    g_f = mod_ref[0][5:6]
    weight = route_ref[...]
    y = jnp.zeros(x1_ref.shape, F32)
    for kk in range(TOP_K):
        y = y + weight[:, kk:kk + 1] * _load_rows(ybuf, tile, (slot, kk))
    o_ref[...] = _layer_norm(DEEPNORM_ALPHA * x1_ref[...] + g_f * y, ln_g_ref[...], ln_b_ref[...])

    @pl.when(i == n_steps - 1)
    def _():
        wait_slot(1 - slot)


def _combine(dest8, x1, mod, route, ln_g, ln_b, y_rows, seq):
    n_tok, d = x1.shape
    tile = COMBINE_TILE
    n_t = n_tok // tile
    per_seq = seq // tile
    dest3 = _per_tile(dest8, tile)
    smem_blk = lambda fn: pl.BlockSpec((1, 1, tile * TOP_K), fn, memory_space=pltpu.SMEM)
    row = lambda width: pl.BlockSpec((tile, width), lambda i: (i, 0))
    vec = pl.BlockSpec((1, d), lambda i: (0, 0))
    return pl.pallas_call(
        _combine_kernel,
        grid=(n_t,),
        in_specs=[smem_blk(lambda i: (i, 0, 0)), smem_blk(lambda i: (jnp.minimum(i + 1, n_t - 1), 0, 0)),
                  row(d), pl.BlockSpec((1, 6, d), lambda i: (i // per_seq, 0, 0)), row(TOP_K), vec, vec,
                  pl.BlockSpec(memory_space=pl.ANY)],
        out_specs=row(d),
        out_shape=jax.ShapeDtypeStruct((n_tok, d), F32),
        scratch_shapes=[pltpu.VMEM((2, TOP_K, tile * ROW_SUB, LANES), F32), pltpu.SemaphoreType.DMA((2,))],
        compiler_params=pltpu.CompilerParams(
            dimension_semantics=("arbitrary",), vmem_limit_bytes=VMEM_LIMIT),
        name="combine",
    )(dest3, dest3, x1, mod, route, ln_g, ln_b, y_rows)


def _split_kernel(w_ref, main_ref, rank_ref, gates_ref):
    rank_lo = _OFF[5]
    rank_hi = rank_lo + GLA_GATE_RANK
    cols = w_ref.shape[2]
    main_ref[...] = w_ref[0, 0:rank_lo, :].astype(BF16)
    pad = jnp.zeros((RANK_PAD - GLA_GATE_RANK, cols), F32)
    rank_ref[...] = jnp.concatenate([w_ref[0, rank_lo:rank_hi, :], pad], axis=0).astype(BF16)
    gates_ref[...] = w_ref[0, rank_hi:rank_hi + 2 * D_MODEL, :].astype(BF16)


def _split_in_proj(w_in, layer):
    w_t = jnp.swapaxes(w_in, 1, 2)
    _, _, d = w_t.shape
    cols = MXU_COLS
    piece = lambda rows: pl.BlockSpec((rows, cols), lambda i: (0, i))
    return pl.pallas_call(
        _split_kernel,
        grid=(d // cols,),
        in_specs=[pl.BlockSpec((1, w_t.shape[1], cols), lambda i: (layer, 0, i))],
        out_specs=[piece(_OFF[5]), piece(RANK_PAD), piece(2 * D_MODEL)],
        out_shape=[jax.ShapeDtypeStruct((_OFF[5], d), BF16), jax.ShapeDtypeStruct((RANK_PAD, d), BF16),
                   jax.ShapeDtypeStruct((2 * D_MODEL, d), BF16)],
        name="split_in_proj",
    )(w_t)


def _layer(x, c, w_ada, b_ada, w_in, w_pool_group, pool_scale, w_branch_a, w_alpha_up, b_alpha,
           gla_norm_gain, w_branch_b, w_out, ln1_gain, ln1_bias, w_router, b_router,
           w_gate_up, b_gate_up, w_down, b_down, ln2_gain, ln2_bias):
    bsz, seq, d = x.shape
    n_tok = bsz * seq
    n_assign = n_tok * TOP_K
    row2 = lambda v: v.reshape(1, -1)

    mod = _ada(c, w_ada, b_ada).reshape(bsz, 6, d)

    w_in_p, w_rank_p, w_gates = w_in
    w_al_p = jnp.concatenate(
        [w_alpha_up, jnp.zeros((RANK_PAD - GLA_GATE_RANK, GLA_KEY_DIM), w_alpha_up.dtype)], axis=0).astype(BF16)
    w_r_t = w_router.T
    w_r_hi = w_r_t.astype(BF16)
    w_r_split = jnp.concatenate([w_r_hi, (w_r_t - w_r_hi.astype(F32)).astype(BF16)], axis=0)
    mixer_args = (x, mod, w_in_p, w_rank_p, w_gates, w_pool_group.astype(BF16), row2(pool_scale),
                  w_branch_a.astype(BF16), w_al_p, row2(b_alpha), row2(gla_norm_gain), w_branch_b.astype(BF16),
                  w_out.astype(BF16), row2(ln1_gain), row2(ln1_bias), w_r_split, b_router.reshape(N_EXPERTS, 1))
    *mixed, sub_decay = _mixer(False, *mixer_args)

    def routed_half(x1, u2_rows, route_t, counts):
        return _routed_moe(x1, u2_rows, route_t, counts, mod, w_gate_up, b_gate_up, w_down, b_down, ln2_gain,
                           ln2_bias)

    return lax.cond(sub_decay[0, 0] > SAFE_SUB_DECAY,
                    lambda: routed_half(*_mixer(True, *mixer_args)[:4]),
                    lambda: routed_half(*mixed))


def _routed_moe(x1, u2_rows, route_t, counts, mod, w_gate_up, b_gate_up, w_down, b_down, ln2_gain, ln2_bias):
    bsz, seq, d = x1.shape
    n_tok = bsz * seq
    n_assign = n_tok * TOP_K
    row2 = lambda v: v.reshape(1, -1)

    top_idx = route_t[0:TOP_K].astype(jnp.int32)
    rank = route_t[TOP_K:2 * TOP_K].astype(jnp.int32)
    route = route_t[2 * TOP_K:3 * TOP_K].T
    counts = counts[:, 0].astype(jnp.int32)
    padded = (counts + EXPERT_BLOCK - 1) // EXPERT_BLOCK * EXPERT_BLOCK
    pad_end = jnp.cumsum(padded)
    pad_start = pad_end - padded
    n_rows = (n_assign + N_EXPERTS * (EXPERT_BLOCK - 1) + EXPERT_BLOCK - 1) // EXPERT_BLOCK * EXPERT_BLOCK
    n_blocks = n_rows // EXPERT_BLOCK
    n_active = (pad_end[-1] // EXPERT_BLOCK).astype(jnp.int32)
    start_of = jnp.sum(jnp.where(top_idx[..., None] == jnp.arange(N_EXPERTS, dtype=jnp.int32), pad_start, 0),
                       axis=-1)
    dest8 = (start_of + rank) * ROW_SUB
    blk_row = jnp.arange(n_blocks, dtype=jnp.int32)[:, None] * EXPERT_BLOCK
    block_expert = jnp.minimum(jnp.sum(pad_end[None, :] <= blk_row, axis=1), N_EXPERTS - 1).astype(jnp.int32)

    fill_table = jnp.concatenate([pad_start + counts, padded - counts, n_active.reshape(1)]).astype(jnp.int32)
    x_rows = _dispatch(fill_table, dest8, u2_rows, n_rows)
    f = w_down.shape[1]
    bg = b_gate_up[:, 0::2].reshape(N_EXPERTS, 1, f)
    bu = b_gate_up[:, 1::2].reshape(N_EXPERTS, 1, f)
    eid = jnp.arange(N_EXPERTS, dtype=jnp.int32)
    owns = counts > 0
    later = jnp.where((eid[None, :] > eid[:, None]) & owns[None, :], eid[None, :], N_EXPERTS)
    next_of = jnp.min(later, axis=1)
    next_of = jnp.where(next_of == N_EXPERTS, -1, next_of).astype(jnp.int32)
    parity_of = ((jnp.cumsum(owns.astype(jnp.int32)) - 1) % 2).astype(jnp.int32)
    is_block_expert = block_expert[:, None] == eid[None, :]
    per_block = lambda v: jnp.sum(jnp.where(is_block_expert, v[None, :], 0), axis=1).astype(jnp.int32)
    left = per_block(pad_start + counts) - blk_row[:, 0]
    both_halves = (left > EXPERT_HALF).astype(jnp.int32)
    y_rows = _moe(block_expert, per_block(next_of), per_block(parity_of), both_halves, n_active.reshape(1),
                  x_rows, w_gate_up, bg, bu, w_down, b_down.reshape(N_EXPERTS, 1, d))
    out = _combine(dest8, x1.reshape(n_tok, d), mod, route, row2(ln2_gain), row2(ln2_bias), y_rows, seq)
    return out.reshape(bsz, seq, d)


def kernel(x, c, w_ada, b_ada, w_in, w_pool_group, pool_scale, w_branch_a, w_alpha_up, b_alpha, gla_norm_gain,
           w_branch_b, w_out, ln1_gain, ln1_bias, w_router, b_router, w_gate_up, b_gate_up, w_down, b_down,
           ln2_gain, ln2_bias):
    for l in range(DEPTH):
        x = _layer(x, c, w_ada[l], b_ada[l], _split_in_proj(w_in, l), w_pool_group[l], pool_scale[l], w_branch_a[l],
                   w_alpha_up[l], b_alpha[l], gla_norm_gain[l], w_branch_b[l], w_out[l], ln1_gain[l],
                   ln1_bias[l], w_router[l], b_router[l], w_gate_up[l], b_gate_up[l], w_down[l], b_down[l],
                   ln2_gain[l], ln2_bias[l])
    return x
```

```python
import functools

import jax
import jax.numpy as jnp
from jax import lax
from jax.experimental import pallas as pl
from jax.experimental.pallas import tpu as pltpu

D_MODEL = 1024
CHUNK = 64
SUB = 16
N_SUB = CHUNK // SUB
POOL_WINDOWS = (2, 4, 8, 16)
POOL_GROUP_WIDTH = D_MODEL // len(POOL_WINDOWS)
POOL_HALO = 16
GLA_HEADS = 4
GLA_KEY_DIM = D_MODEL // 2
GLA_HEAD_K = GLA_KEY_DIM // GLA_HEADS
GLA_HEAD_V = D_MODEL // GLA_HEADS
GLA_GATE_RANK = 16
GLA_TAU = 16.0
N_EXPERTS = 32
TOP_K = 4
SWIGLU_ALPHA = 1.702
SWIGLU_LIMIT = 7.0
EXPERT_BLOCK = 512
EXPERT_HALF = EXPERT_BLOCK // 2
LN_EPS = 1e-5
RMS_EPS = 1e-6
DEPTH = 1
DEEPNORM_ALPHA = (2.0 * DEPTH) ** 0.25

LANES = 128
MXU_COLS = 256
RANK_PAD = LANES
_W = (D_MODEL, GLA_KEY_DIM, GLA_KEY_DIM, D_MODEL, D_MODEL, RANK_PAD, D_MODEL, D_MODEL)
_OFF = tuple(sum(_W[:i]) for i in range(len(_W) + 1))
EXP_CAP = 60.0
SAFE_SUB_DECAY = 40.0
ROUTE_ROWS = 16

SEQ_TILE = 512
DISPATCH_TILE = 2048
COMBINE_TILE = 256
DMA_UNROLL = 8
VMEM_LIMIT = 60 * 1024 * 1024

F32 = jnp.float32
BF16 = jnp.bfloat16
HI = lax.Precision.HIGHEST


def _dot(a, b):
    return jnp.dot(a, b, preferred_element_type=F32)


def _dot_nt(a, b):
    return lax.dot_general(a, b, (((1,), (1,)), ((), ())), preferred_element_type=F32)


def _dot_tn(a, b):
    return lax.dot_general(a, b, (((0,), (0,)), ((), ())), preferred_element_type=F32)


def _sigmoid(v):
    return 0.5 * jnp.tanh(0.5 * v) + 0.5


ROW_SUB = D_MODEL // LANES


def _load_rows(ref, n_rows, lead=()):
    return jnp.concatenate(
        [ref[lead + (pl.ds(j, n_rows, stride=ROW_SUB), slice(None))] for j in range(ROW_SUB)], axis=1)


def _store_rows(ref, val):
    for j in range(ROW_SUB):
        ref[pl.ds(j, val.shape[0], stride=ROW_SUB), :] = val[:, j * LANES:(j + 1) * LANES]


def _ada_kernel(c_ref, w_ref, b_ref, o_ref):
    c = c_ref[...]
    s = c * jax.nn.sigmoid(c)
    o_ref[...] = jnp.dot(s, w_ref[...], precision=HI, preferred_element_type=F32) + b_ref[...]


def _ada(c, w_ada, b_ada):
    bsz, d = c.shape
    n = w_ada.shape[1]
    tn = 1024
    return pl.pallas_call(
        _ada_kernel,
        grid=(n // tn,),
        in_specs=[
            pl.BlockSpec((bsz, d), lambda j: (0, 0)),
            pl.BlockSpec((d, tn), lambda j: (0, j)),
            pl.BlockSpec((1, tn), lambda j: (0, j)),
        ],
        out_specs=pl.BlockSpec((bsz, tn), lambda j: (0, j)),
        out_shape=jax.ShapeDtypeStruct((bsz, n), F32),
        name="ada",
    )(c, w_ada, b_ada.reshape(1, n))


def _layer_norm(z, gain, bias):
    mu = jnp.mean(z, axis=-1, keepdims=True)
    zc = z - mu
    var = jnp.mean(zc * zc, axis=-1, keepdims=True)
    return zc * lax.rsqrt(var + LN_EPS) * gain + bias


def _mixer_kernel(x_ref, mod_ref, mod_prev_ref, w_in_ref, w_rank_ref, w_gates_ref, w_pool_ref, pool_scale_ref,
                  w_a_ref, w_al_ref,
                  b_al_ref, gain_ref, w_b_ref, w_out_ref, ln_g_ref, ln_b_ref, w_r_ref, b_r_ref,
                  x1_ref, u2_ref, route_ref, counts_ref, decay_ref,
                  a_ext, s_ref, cnt_ref, o_ref, z_ref, sc_ref, *, tiles_per_seq, n_tiles, direct_scores):
    j = pl.program_id(0)
    s_idx = lax.rem(jnp.minimum(j, n_tiles - 1), tiles_per_seq)
    tile = x_ref.shape[1]

    @pl.when(s_idx == 0)
    def _():
        s_ref[...] = jnp.zeros_like(s_ref)
        a_ext[0:POOL_HALO, :] = jnp.zeros((POOL_HALO, D_MODEL), F32)

    @pl.when(j == 0)
    def _():
        cnt_ref[...] = jnp.zeros_like(cnt_ref)
        z_ref[...] = jnp.zeros_like(z_ref)
        decay_ref[...] = jnp.zeros_like(decay_ref)


    mod_prev = mod_prev_ref[0]
    x1 = _layer_norm(z_ref[...], ln_g_ref[...], ln_b_ref[...])
    x1_ref[0] = x1
    u2 = x1 * (1.0 + mod_prev[4:5]) + mod_prev[3:4]
    _store_rows(u2_ref, u2)
    u2_hi = u2.astype(BF16)
    u2_lo = (u2 - u2_hi.astype(F32)).astype(BF16)

    mod = mod_ref[0]
    sh_m, sc_m, g_m = mod[0:1], mod[1:2], mod[2:3]
    x = x_ref[0]
    u = (x * (1.0 + sc_m) + sh_m).astype(BF16)

    def proj(i):
        return _dot(u, w_in_ref[:, _OFF[i]:_OFF[i + 1]])

    gate_cols = ([(w_in_ref, _OFF[4] + jj * MXU_COLS) for jj in range(D_MODEL // MXU_COLS)]
                 + [(w_gates_ref, jj * MXU_COLS) for jj in range(2 * D_MODEL // MXU_COLS)])
    fillers = [functools.partial(lambda w_ref, c0: _dot(u, w_ref[:, c0:c0 + MXU_COLS]), w_ref, c0)
               for w_ref, c0 in gate_cols]
    filled = []

    def issue_fillers(n):
        for _ in range(n):
            if len(filled) < len(fillers):
                filled.append(fillers[len(filled)]())

    a = proj(0)
    q = proj(1) * (GLA_HEAD_K ** -0.5)
    k_all = proj(2)
    v_all = proj(3)
    alpha_low = _dot(u, w_rank_ref[...])

    part = _dot_nt(w_r_ref[...], u2_hi)
    logits = (part[:N_EXPERTS] + part[N_EXPERTS:] + _dot_nt(w_r_ref[0:N_EXPERTS, :], u2_lo)
              + b_r_ref[...])
    issue_fillers(2)

    a_ext[POOL_HALO:POOL_HALO + tile, :] = a
    t_glob = s_idx * tile + lax.broadcasted_iota(jnp.int32, (tile, 1), 0)
    mapped = []
    for g, w in enumerate(POOL_WINDOWS):
        lo, hi = g * POOL_GROUP_WIDTH, (g + 1) * POOL_GROUP_WIDTH
        win = a_ext[:, lo:hi]
        k = 1
        while k < w:
            win = win + pltpu.roll(win, k, 0)
            k *= 2
        inv_cnt = 1.0 / jnp.minimum(t_glob + 1, w).astype(F32)
        pooled = win[POOL_HALO:, :] * inv_cnt - a[:, lo:hi]
        mapped.append(_dot(pooled.astype(BF16), w_pool_ref[g]))
    a_ext[0:POOL_HALO, :] = a[tile - POOL_HALO:tile, :]
    ya = _dot((jnp.concatenate(mapped, axis=1) * pool_scale_ref[...]).astype(BF16), w_a_ref[...])

    erow = lax.broadcasted_iota(jnp.int32, (N_EXPERTS, tile), 0).astype(F32)
    work = logits
    sel = jnp.zeros((N_EXPERTS, tile), F32)
    vals, hits = [], []
    for _ in range(TOP_K):
        m = jnp.max(work, axis=0, keepdims=True)
        idx = jnp.min(jnp.where(work == m, erow, float(N_EXPERTS)), axis=0, keepdims=True)
        hit = erow == idx
        vals.append(m)
        hits.append((idx, hit))
        sel = jnp.where(hit, 1.0, sel)
        work = jnp.where(hit, -jnp.inf, work)
    exps = [jnp.exp(v - vals[0]) for v in vals]
    inv_den = 1.0 / (exps[0] + exps[1] + exps[2] + exps[3])
    issue_fillers(1)

    z = _dot(alpha_low.astype(BF16), w_al_ref[...]) + b_al_ref[...]
    log_a = (jnp.minimum(z, 0.0) - jnp.log1p(jnp.exp(-jnp.abs(z)))) * (1.0 / GLA_TAU)

    ri = lax.broadcasted_iota(jnp.int32, (CHUNK, CHUNK), 0)
    ci = lax.broadcasted_iota(jnp.int32, (CHUNK, CHUNK), 1)
    causal = ci <= ri
    cum_mat = causal.astype(BF16)
    la_hi = log_a.astype(BF16)
    la_split = jnp.concatenate([la_hi, (log_a - la_hi.astype(F32)).astype(BF16)], axis=1)

    n_chunks = tile // CHUNK
    pairs = [(c, h) for c in range(n_chunks) for h in range(GLA_HEADS)]
    rows_of = lambda c: slice(c * CHUNK, (c + 1) * CHUNK)
    ks_of = lambda h: slice(h * GLA_HEAD_K, (h + 1) * GLA_HEAD_K)
    vs_of = lambda h: slice(h * GLA_HEAD_V, (h + 1) * GLA_HEAD_V)

    b_cum, ref_pts, b_ref_pt = [], [], []
    for c in range(n_chunks):
        cum = _dot(cum_mat, la_split[rows_of(c)])
        b_cum.append(cum[:, :GLA_KEY_DIM] + cum[:, GLA_KEY_DIM:])
        ref_pts.append([jnp.zeros((1, GLA_KEY_DIM), F32)]
                       + [b_cum[c][i * SUB - 1:i * SUB, :] for i in range(1, N_SUB)])
        b_ref_pt.append(jnp.concatenate([jnp.broadcast_to(p, (SUB, GLA_KEY_DIM)) for p in ref_pts[c]], axis=0))

    rt = lax.broadcasted_iota(jnp.int32, (tile, tile), 0)
    ct = lax.broadcasted_iota(jnp.int32, (tile, tile), 1)
    base = _dot(sel.astype(BF16), (rt < ct).astype(BF16)) + cnt_ref[:, 0:1]

    sub_decay = b_ref_pt[0] - b_cum[0]
    for c in range(1, n_chunks):
        sub_decay = jnp.maximum(sub_decay, b_ref_pt[c] - b_cum[c])
    decay_ref[...] = jnp.maximum(decay_ref[...], jnp.max(sub_decay, keepdims=True))

    scores = {}
    if not direct_scores:
        for n_pair, (c, h) in enumerate(pairs):
            if n_pair % 3 == 0:
                issue_fillers(1)
            qh, kh, bh = q[rows_of(c), ks_of(h)], k_all[rows_of(c), ks_of(h)], b_cum[c][:, ks_of(h)]
            q_dec = (qh * jnp.exp(bh - b_ref_pt[c][:, ks_of(h)])).astype(BF16)
            k_dec = jnp.concatenate(
                [(kh * jnp.exp(jnp.minimum(ref_pts[c][i][:, ks_of(h)] - bh, EXP_CAP))).astype(BF16)
                 for i in range(N_SUB)], axis=0)
            s_all = _dot_nt(q_dec, k_dec)
            scores[c, h] = jnp.concatenate(
                [s_all[i * SUB:(i + 1) * SUB, i * CHUNK:(i + 1) * CHUNK] for i in range(N_SUB)], axis=0)
    else:
        o_ref[:, 0:GLA_KEY_DIM] = q
        o_ref[:, GLA_KEY_DIM:2 * GLA_KEY_DIM] = jnp.concatenate(b_cum, axis=0)
        for n_pair, (c, h) in enumerate(pairs):
            kh, bh = k_all[rows_of(c), ks_of(h)], b_cum[c][:, ks_of(h)]

            def score_rows(g, carry):
                first = pl.multiple_of(g * 8, 8)
                q_8 = o_ref[pl.ds(c * CHUNK + first, 8), ks_of(h)]
                b_8 = o_ref[pl.ds(c * CHUNK + first, 8),
                            GLA_KEY_DIM + h * GLA_HEAD_K:GLA_KEY_DIM + (h + 1) * GLA_HEAD_K]
                rows = []
                for r in range(8):
                    k_i = (kh * jnp.exp(jnp.minimum(b_8[r:r + 1] - bh, 0.0))).astype(BF16)
                    q_i = jnp.broadcast_to(q_8[r:r + 1], (8, GLA_HEAD_K)).astype(BF16)
                    rows.append(_dot_nt(q_i, k_i)[0:1])
                sc_ref[n_pair, pl.ds(first, 8), :] = jnp.concatenate(rows, axis=0)
                return carry

            lax.fori_loop(0, CHUNK // 8, score_rows, 0)

    o_intra, kv, decay_last = {}, {}, {}
    for n_pair, (c, h) in enumerate(pairs):
        kh, bh = k_all[rows_of(c), ks_of(h)], b_cum[c][:, ks_of(h)]
        vh = v_all[rows_of(c), vs_of(h)].astype(BF16)
        chunk_scores = sc_ref[n_pair] if direct_scores else scores[c, h]
        o_intra[c, h] = _dot(jnp.where(causal, chunk_scores, 0.0).astype(BF16), vh)
        b_last = bh[CHUNK - 1:CHUNK, :]
        kv[c, h] = _dot_tn(vh, (kh * jnp.exp(b_last - bh)).astype(BF16))
        decay_last[c, h] = jnp.exp(b_last)

    orow = lax.broadcasted_iota(jnp.int32, (ROUTE_ROWS, tile), 0)
    route = jnp.zeros((ROUTE_ROWS, tile), F32)
    for kk in range(TOP_K):
        idx, hit = hits[kk]
        rank = jnp.sum(jnp.where(hit, base, 0.0), axis=0, keepdims=True)
        route = jnp.where(orow == kk, idx, route)
        route = jnp.where(orow == TOP_K + kk, rank, route)
        route = jnp.where(orow == 2 * TOP_K + kk, exps[kk] * inv_den, route)
    route_ref[...] = route
    cnt_ref[...] = cnt_ref[...] + (j > 0).astype(F32) * jnp.sum(sel, axis=1, keepdims=True)
    counts_ref[...] = cnt_ref[...]

    o_inter = {}
    state_t = [s_ref[h] for h in range(GLA_HEADS)]
    for c, h in pairs:
        q_in = (q[rows_of(c), ks_of(h)] * jnp.exp(b_cum[c][:, ks_of(h)])).astype(BF16)
        o_inter[c, h] = _dot_nt(q_in, state_t[h].astype(BF16))
        state_t[h] = state_t[h] * decay_last[c, h] + kv[c, h]
    for h in range(GLA_HEADS):
        s_ref[h] = state_t[h]

    for c, h in pairs:
        o = o_intra[c, h] + o_inter[c, h]
        o = o * lax.rsqrt(jnp.mean(o * o, axis=-1, keepdims=True) + RMS_EPS) * gain_ref[...]
        o_ref[rows_of(c), vs_of(h)] = o

    issue_fillers(len(fillers))
    per_proj = D_MODEL // MXU_COLS
    r, gate_a, gate_b = (jnp.concatenate(filled[n * per_proj:(n + 1) * per_proj], axis=1) for n in range(3))
    yb = _dot((o_ref[...] * (r * _sigmoid(r))).astype(BF16), w_b_ref[...])

    merged = _sigmoid(gate_a) * ya + _sigmoid(gate_b) * yb
    y = _dot(merged.astype(BF16), w_out_ref[...])
    z_ref[...] = DEEPNORM_ALPHA * x + g_m * y


def _mixer(direct_scores, x, mod, w_in_p, w_rank_p, w_gates, w_pool, pool_scale, w_a, w_al_p, b_alpha, gain, w_b,
           w_out, ln_g, ln_b, w_r_p, b_r_p):
    bsz, seq, d = x.shape
    tile = SEQ_TILE
    n_s = seq // tile
    n_tiles = bsz * n_s

    def const(shape):
        nd = len(shape)
        return pl.BlockSpec(shape, lambda j: (0,) * nd, pipeline_mode=pl.Buffered(1))

    cur = lambda j: jnp.minimum(j, n_tiles - 1)
    prev = lambda j: jnp.maximum(j - 1, 0)
    return pl.pallas_call(
        functools.partial(_mixer_kernel, tiles_per_seq=n_s, n_tiles=n_tiles, direct_scores=direct_scores),
        grid=(n_tiles + 1,),
        in_specs=[
            pl.BlockSpec((1, tile, d), lambda j: (cur(j) // n_s, cur(j) % n_s, 0)),
            pl.BlockSpec((1, 6, d), lambda j: (cur(j) // n_s, 0, 0)),
            pl.BlockSpec((1, 6, d), lambda j: (prev(j) // n_s, 0, 0)),
            const(w_in_p.shape), const(w_rank_p.shape), const(w_gates.shape),
            const(w_pool.shape), const(pool_scale.shape), const(w_a.shape),
            const(w_al_p.shape), const(b_alpha.shape), const(gain.shape), const(w_b.shape),
            const(w_out.shape), const(ln_g.shape), const(ln_b.shape), const(w_r_p.shape),
            const(b_r_p.shape),
        ],
        out_specs=[pl.BlockSpec((1, tile, d), lambda j: (prev(j) // n_s, prev(j) % n_s, 0)),
                   pl.BlockSpec((tile * ROW_SUB, LANES), lambda j: (prev(j), 0)),
                   pl.BlockSpec((ROUTE_ROWS, tile), lambda j: (0, prev(j))),
                   pl.BlockSpec((N_EXPERTS, LANES), lambda j: (0, 0)),
                   pl.BlockSpec((8, LANES), lambda j: (0, 0))],
        out_shape=[
            jax.ShapeDtypeStruct((bsz, seq, d), F32),
            jax.ShapeDtypeStruct((bsz * seq * ROW_SUB, LANES), F32),
            jax.ShapeDtypeStruct((ROUTE_ROWS, bsz * seq), F32),
            jax.ShapeDtypeStruct((N_EXPERTS, LANES), F32),
            jax.ShapeDtypeStruct((8, LANES), F32),
        ],
        scratch_shapes=[
            pltpu.VMEM((POOL_HALO + tile, d), F32),
            pltpu.VMEM((GLA_HEADS, GLA_HEAD_V, GLA_HEAD_K), F32),
            pltpu.VMEM((N_EXPERTS, LANES), F32),
            pltpu.VMEM((tile, d), F32),
            pltpu.VMEM((tile, d), F32),
            pltpu.VMEM((tile // CHUNK * GLA_HEADS, CHUNK, CHUNK), F32),
        ],
        compiler_params=pltpu.CompilerParams(
            dimension_semantics=("arbitrary",), vmem_limit_bytes=VMEM_LIMIT),
        name="mixer",
    )(x, mod, mod, w_in_p, w_rank_p, w_gates, w_pool, pool_scale, w_a, w_al_p, b_alpha, gain, w_b, w_out,
      ln_g, ln_b, w_r_p, b_r_p)


def _row_at(ref, row8):
    return ref.at[pl.ds(pl.multiple_of(row8, ROW_SUB), ROW_SUB)]


def _per_tile(dest8, tile):
    n_t = dest8.shape[1] // tile
    return dest8.reshape(TOP_K, n_t, tile).transpose(1, 0, 2).reshape(n_t, 1, TOP_K * tile)


def _dispatch_kernel(fill_ref, dest_ref, u2_ref, rows_hbm, zbuf, sem_fill, sem_rows):
    i = pl.program_id(0)
    tile = u2_ref.shape[0] // ROW_SUB
    blk8 = EXPERT_BLOCK * ROW_SUB

    def zero_fills(phase):
        def piece(first_row, n_rows):
            dst = rows_hbm.at[pl.ds(pl.multiple_of(first_row * ROW_SUB, ROW_SUB), n_rows * ROW_SUB)]
            getattr(pltpu.make_async_copy(zbuf.at[pl.ds(0, n_rows * ROW_SUB)], dst, sem_fill.at[0]), phase)()

        for e in range(N_EXPERTS):
            first_row, gap = fill_ref[e], fill_ref[N_EXPERTS + e]
            n_rows = EXPERT_BLOCK // 2
            while n_rows >= 1:
                @pl.when(jnp.bitwise_and(gap, n_rows) != 0)
                def _(first_row=first_row, n_rows=n_rows):
                    piece(first_row, n_rows)
                first_row = first_row + jnp.bitwise_and(gap, n_rows)
                n_rows //= 2

        def unused(window, carry):
            piece(window * EXPERT_BLOCK, EXPERT_BLOCK)
            return carry

        lax.fori_loop(fill_ref[2 * N_EXPERTS], rows_hbm.shape[0] // blk8, unused, 0)

    @pl.when(i == 0)
    def _():
        zbuf[...] = jnp.zeros_like(zbuf)
        zero_fills("start")

    def body(r, carry):
        for kk in range(TOP_K):
            pltpu.make_async_copy(_row_at(u2_ref, r * ROW_SUB), _row_at(rows_hbm, dest_ref[0, 0, kk * tile + r]),
                                  sem_rows.at[0]).start(priority=kk % 2)
        return carry

    lax.fori_loop(0, tile, body, 0, unroll=DMA_UNROLL)
    for _ in range(TOP_K):
        pltpu.make_async_copy(u2_ref, rows_hbm.at[pl.ds(0, tile * ROW_SUB)], sem_rows.at[0]).wait()

    @pl.when(i == pl.num_programs(0) - 1)
    def _():
        zero_fills("wait")


def _dispatch(fill_start, dest8, u2_rows, n_rows):
    n_tok = u2_rows.shape[0] // ROW_SUB
    tile = DISPATCH_TILE
    n_t = n_tok // tile
    grid_spec = pltpu.PrefetchScalarGridSpec(
        num_scalar_prefetch=1,
        grid=(n_t,),
        in_specs=[
            pl.BlockSpec((1, 1, tile * TOP_K), lambda i, fs: (i, 0, 0), memory_space=pltpu.SMEM),
            pl.BlockSpec((tile * ROW_SUB, LANES), lambda i, fs: (i, 0)),
        ],
        out_specs=pl.BlockSpec(memory_space=pl.ANY),
        scratch_shapes=[
            pltpu.VMEM((EXPERT_BLOCK * ROW_SUB, LANES), F32),
            pltpu.SemaphoreType.DMA((1,)),
            pltpu.SemaphoreType.DMA((1,)),
        ],
    )
    return pl.pallas_call(
        _dispatch_kernel,
        grid_spec=grid_spec,
        out_shape=jax.ShapeDtypeStruct(((n_rows + EXPERT_BLOCK) * ROW_SUB, LANES), F32),
        compiler_params=pltpu.CompilerParams(dimension_semantics=("arbitrary",)),
        name="dispatch",
    )(fill_start, _per_tile(dest8, tile), u2_rows)


def _moe_kernel(be_ref, nxt_ref, par_ref, full_ref, nb_ref, x_ref, bg_ref, bu_ref, bd_ref, wgu_hbm, wd_hbm, y_ref,
                wgu_buf, wd_buf, wg_s, wu_s, wd_s, sem_gu, sem_d):
    i = pl.program_id(0)
    new_expert = (i == 0) | (be_ref[i] != be_ref[jnp.maximum(i - 1, 0)])

    def fetch(e, s):
        return (pltpu.make_async_copy(wgu_hbm.at[e], wgu_buf.at[s], sem_gu.at[s]),
                pltpu.make_async_copy(wd_hbm.at[e], wd_buf.at[s], sem_d.at[s]))

    @pl.when(new_expert & (i < nb_ref[0]))
    def _():
        slot = par_ref[i]

        @pl.when(i == 0)
        def _():
            for cp in fetch(be_ref[0], slot):
                cp.start()

        @pl.when(nxt_ref[i] >= 0)
        def _():
            for cp in fetch(nxt_ref[i], 1 - slot):
                cp.start()

        for cp in fetch(be_ref[i], slot):
            cp.wait()
        src = lax.broadcasted_iota(jnp.int32, (MXU_COLS, MXU_COLS), 0)
        col = lax.broadcasted_iota(jnp.int32, (MXU_COLS, MXU_COLS), 1)
        half = MXU_COLS // 2
        want = jnp.where(col < half, 2 * col, 2 * (col - half) + 1)
        unzip = (src == want).astype(BF16)
        for g in range(wgu_buf.shape[2] // MXU_COLS):
            blk = wgu_buf[slot, :, g * MXU_COLS:(g + 1) * MXU_COLS].astype(BF16)
            sep = _dot(blk, unzip)
            wg_s[:, g * half:(g + 1) * half] = sep[:, :half].astype(BF16)
            wu_s[:, g * half:(g + 1) * half] = sep[:, half:].astype(BF16)
        wd_s[...] = wd_buf[slot].astype(BF16)

    def expert_mlp(n_rows):
        xb = _load_rows(x_ref, n_rows).astype(BF16)
        gate = jnp.minimum(_dot(xb, wg_s[...]) + bg_ref[0], SWIGLU_LIMIT)
        up = jnp.clip(_dot(xb, wu_s[...]) + bu_ref[0], -SWIGLU_LIMIT, SWIGLU_LIMIT)
        glu = gate * _sigmoid(gate * SWIGLU_ALPHA)
        _store_rows(y_ref, _dot(((up + 1.0) * glu).astype(BF16), wd_s[...]) + bd_ref[0])

    active = i < nb_ref[0]
    both_halves = full_ref[i] == 1

    @pl.when(active & both_halves)
    def _():
        expert_mlp(EXPERT_BLOCK)

    @pl.when(active & jnp.logical_not(both_halves))
    def _():
        expert_mlp(EXPERT_HALF)
        y_ref[EXPERT_HALF * ROW_SUB:, :] = jnp.zeros((EXPERT_HALF * ROW_SUB, LANES), F32)

    @pl.when(jnp.logical_not(active))
    def _():
        y_ref[...] = jnp.zeros_like(y_ref)


def _moe(block_expert, next_expert, slot_parity, both_halves, n_active, x_rows, w_gate_up, bg, bu, w_down, bd):
    n_blocks = block_expert.shape[0]
    _, d, f2 = w_gate_up.shape
    f = f2 // 2
    blk8 = EXPERT_BLOCK * ROW_SUB
    rows_in = pl.BlockSpec((blk8, LANES),
                           lambda i, be, nx, pr, fl, nb: (jnp.maximum(jnp.minimum(i, nb[0] - 1), 0), 0))
    rows_out = pl.BlockSpec((blk8, LANES), lambda i, be, nx, pr, fl, nb: (i, 0))
    per_expert = lambda shape: pl.BlockSpec((1,) + shape, lambda i, be, nx, pr, fl, nb: (be[i], 0, 0))
    hbm = pl.BlockSpec(memory_space=pl.ANY)
    grid_spec = pltpu.PrefetchScalarGridSpec(
        num_scalar_prefetch=5,
        grid=(n_blocks,),
        in_specs=[rows_in, per_expert((1, f)), per_expert((1, f)), per_expert((1, d)), hbm, hbm],
        out_specs=rows_out,
        scratch_shapes=[pltpu.VMEM((2, d, f2), F32), pltpu.VMEM((2, f, d), F32),
                        pltpu.VMEM((d, f), BF16), pltpu.VMEM((d, f), BF16), pltpu.VMEM((f, d), BF16),
                        pltpu.SemaphoreType.DMA((2,)), pltpu.SemaphoreType.DMA((2,))],
    )
    return pl.pallas_call(
        _moe_kernel,
        grid_spec=grid_spec,
        out_shape=jax.ShapeDtypeStruct((n_blocks * blk8, LANES), F32),
        compiler_params=pltpu.CompilerParams(
            dimension_semantics=("arbitrary",), vmem_limit_bytes=VMEM_LIMIT),
        name="moe",
    )(block_expert, next_expert, slot_parity, both_halves, n_active, x_rows, bg, bu, bd, w_gate_up, w_down)


def _combine_kernel(dest_cur_ref, dest_nxt_ref, x1_ref, mod_ref, route_ref, ln_g_ref, ln_b_ref, y_hbm,
                    o_ref, ybuf, sem):
    i = pl.program_id(0)
    n_steps = pl.num_programs(0)
    tile = x1_ref.shape[0]
    slot = lax.rem(i, 2)

    def gather(dest_ref, s):
        def body(r, carry):
            for kk in range(TOP_K):
                pltpu.make_async_copy(_row_at(y_hbm, dest_ref[0, 0, kk * tile + r]),
                                      _row_at(ybuf.at[s, kk], r * ROW_SUB), sem.at[s]).start(priority=kk % 2)
            return carry
        lax.fori_loop(0, tile, body, 0, unroll=DMA_UNROLL)

    @pl.when(i == 0)
    def _():
        gather(dest_cur_ref, 0)

    def wait_slot(s):
        for kk in range(TOP_K):
            pltpu.make_async_copy(y_hbm.at[pl.ds(0, tile * ROW_SUB)], ybuf.at[s, kk], sem.at[s]).wait()

    for r in range(tile):
        for kk in range(TOP_K):
            pltpu.make_async_copy(_row_at(y_hbm, dest_nxt_ref[0, 0, kk * tile + r]),
                                  _row_at(ybuf.at[1 - slot, kk], r * ROW_SUB),
                                  sem.at[1 - slot]).start(priority=kk % 2)
    wait_slot(slot)

    g_f = mod_ref[0][5:6]
    weight = route_ref[...]
    y = jnp.zeros(x1_ref.shape, F32)
    for kk in range(TOP_K):
        y = y + weight[:, kk:kk + 1] * _load_rows(ybuf, tile, (slot, kk))
    o_ref[...] = _layer_norm(DEEPNORM_ALPHA * x1_ref[...] + g_f * y, ln_g_ref[...], ln_b_ref[...])

    @pl.when(i == n_steps - 1)
    def _():
        wait_slot(1 - slot)


def _combine(dest8, x1, mod, route, ln_g, ln_b, y_rows, seq):
    n_tok, d = x1.shape
    tile = COMBINE_TILE
    n_t = n_tok // tile
    per_seq = seq // tile
    dest3 = _per_tile(dest8, tile)
    smem_blk = lambda fn: pl.BlockSpec((1, 1, tile * TOP_K), fn, memory_space=pltpu.SMEM)
    row = lambda width: pl.BlockSpec((tile, width), lambda i: (i, 0))
    vec = pl.BlockSpec((1, d), lambda i: (0, 0))
    return pl.pallas_call(
        _combine_kernel,
        grid=(n_t,),
        in_specs=[smem_blk(lambda i: (i, 0, 0)), smem_blk(lambda i: (jnp.minimum(i + 1, n_t - 1), 0, 0)),
                  row(d), pl.BlockSpec((1, 6, d), lambda i: (i // per_seq, 0, 0)), row(TOP_K), vec, vec,
                  pl.BlockSpec(memory_space=pl.ANY)],
        out_specs=row(d),
        out_shape=jax.ShapeDtypeStruct((n_tok, d), F32),
        scratch_shapes=[pltpu.VMEM((2, TOP_K, tile * ROW_SUB, LANES), F32), pltpu.SemaphoreType.DMA((2,))],
        compiler_params=pltpu.CompilerParams(
            dimension_semantics=("arbitrary",), vmem_limit_bytes=VMEM_LIMIT),
        name="combine",
    )(dest3, dest3, x1, mod, route, ln_g, ln_b, y_rows)


def _split_kernel(w_ref, main_ref, rank_ref, gates_ref):
    rank_lo = _OFF[5]
    rank_hi = rank_lo + GLA_GATE_RANK
    cols = w_ref.shape[2]
    main_ref[...] = w_ref[0, 0:rank_lo, :].T.astype(BF16)
    pad = jnp.zeros((RANK_PAD - GLA_GATE_RANK, cols), F32)
    rank_ref[...] = jnp.concatenate([w_ref[0, rank_lo:rank_hi, :], pad], axis=0).T.astype(BF16)
    gates_ref[...] = w_ref[0, rank_hi:rank_hi + 2 * D_MODEL, :].T.astype(BF16)


def _split_in_proj(w_in, layer):
    w_t = jnp.swapaxes(w_in, 1, 2)
    _, _, d = w_t.shape
    cols = MXU_COLS
    piece = lambda width: pl.BlockSpec((cols, width), lambda i: (i, 0))
    return pl.pallas_call(
        _split_kernel,
        grid=(d // cols,),
        in_specs=[pl.BlockSpec((1, w_t.shape[1], cols), lambda i: (layer, 0, i))],
        out_specs=[piece(_OFF[5]), piece(RANK_PAD), piece(2 * D_MODEL)],
        out_shape=[jax.ShapeDtypeStruct((d, _OFF[5]), BF16), jax.ShapeDtypeStruct((d, RANK_PAD), BF16),
                   jax.ShapeDtypeStruct((d, 2 * D_MODEL), BF16)],
        name="split_in_proj",
    )(w_t)


def _layer(x, c, w_ada, b_ada, w_in, w_pool_group, pool_scale, w_branch_a, w_alpha_up, b_alpha,
           gla_norm_gain, w_branch_b, w_out, ln1_gain, ln1_bias, w_router, b_router,
           w_gate_up, b_gate_up, w_down, b_down, ln2_gain, ln2_bias):
    bsz, seq, d = x.shape
    n_tok = bsz * seq
    n_assign = n_tok * TOP_K
    row2 = lambda v: v.reshape(1, -1)

    mod = _ada(c, w_ada, b_ada).reshape(bsz, 6, d)

    w_in_p, w_rank_p, w_gates = w_in
    w_al_p = jnp.concatenate(
        [w_alpha_up, jnp.zeros((RANK_PAD - GLA_GATE_RANK, GLA_KEY_DIM), w_alpha_up.dtype)], axis=0).astype(BF16)
    w_r_t = w_router.T
    w_r_hi = w_r_t.astype(BF16)
    w_r_split = jnp.concatenate([w_r_hi, (w_r_t - w_r_hi.astype(F32)).astype(BF16)], axis=0)
    mixer_args = (x, mod, w_in_p, w_rank_p, w_gates, w_pool_group.astype(BF16), row2(pool_scale),
                  w_branch_a.astype(BF16), w_al_p, row2(b_alpha), row2(gla_norm_gain), w_branch_b.astype(BF16),
                  w_out.astype(BF16), row2(ln1_gain), row2(ln1_bias), w_r_split, b_router.reshape(N_EXPERTS, 1))
    *mixed, sub_decay = _mixer(False, *mixer_args)

    def routed_half(x1, u2_rows, route_t, counts):
        return _routed_moe(x1, u2_rows, route_t, counts, mod, w_gate_up, b_gate_up, w_down, b_down, ln2_gain,
                           ln2_bias)

    return lax.cond(sub_decay[0, 0] > SAFE_SUB_DECAY,
                    lambda: routed_half(*_mixer(True, *mixer_args)[:4]),
                    lambda: routed_half(*mixed))


def _routed_moe(x1, u2_rows, route_t, counts, mod, w_gate_up, b_gate_up, w_down, b_down, ln2_gain, ln2_bias):
    bsz, seq, d = x1.shape
    n_tok = bsz * seq
    n_assign = n_tok * TOP_K
    row2 = lambda v: v.reshape(1, -1)

    top_idx = route_t[0:TOP_K].astype(jnp.int32)
    rank = route_t[TOP_K:2 * TOP_K].astype(jnp.int32)
    route = route_t[2 * TOP_K:3 * TOP_K].T
    counts = counts[:, 0].astype(jnp.int32)
    padded = (counts + EXPERT_BLOCK - 1) // EXPERT_BLOCK * EXPERT_BLOCK
    pad_end = jnp.cumsum(padded)
    pad_start = pad_end - padded
    n_rows = (n_assign + N_EXPERTS * (EXPERT_BLOCK - 1) + EXPERT_BLOCK - 1) // EXPERT_BLOCK * EXPERT_BLOCK
    n_blocks = n_rows // EXPERT_BLOCK
    n_active = (pad_end[-1] // EXPERT_BLOCK).astype(jnp.int32)
    start_of = jnp.sum(jnp.where(top_idx[..., None] == jnp.arange(N_EXPERTS, dtype=jnp.int32), pad_start, 0),
                       axis=-1)
    dest8 = (start_of + rank) * ROW_SUB
    blk_row = jnp.arange(n_blocks, dtype=jnp.int32)[:, None] * EXPERT_BLOCK
    block_expert = jnp.minimum(jnp.sum(pad_end[None, :] <= blk_row, axis=1), N_EXPERTS - 1).astype(jnp.int32)

    fill_table = jnp.concatenate([pad_start + counts, padded - counts, n_active.reshape(1)]).astype(jnp.int32)
    x_rows = _dispatch(fill_table, dest8, u2_rows, n_rows)
    f = w_down.shape[1]
    bg = b_gate_up[:, 0::2].reshape(N_EXPERTS, 1, f)
    bu = b_gate_up[:, 1::2].reshape(N_EXPERTS, 1, f)
    eid = jnp.arange(N_EXPERTS, dtype=jnp.int32)
    owns = counts > 0
    later = jnp.where((eid[None, :] > eid[:, None]) & owns[None, :], eid[None, :], N_EXPERTS)
    next_of = jnp.min(later, axis=1)
    next_of = jnp.where(next_of == N_EXPERTS, -1, next_of).astype(jnp.int32)
    parity_of = ((jnp.cumsum(owns.astype(jnp.int32)) - 1) % 2).astype(jnp.int32)
    is_block_expert = block_expert[:, None] == eid[None, :]
    per_block = lambda v: jnp.sum(jnp.where(is_block_expert, v[None, :], 0), axis=1).astype(jnp.int32)
    left = per_block(pad_start + counts) - blk_row[:, 0]
    both_halves = (left > EXPERT_HALF).astype(jnp.int32)
    y_rows = _moe(block_expert, per_block(next_of), per_block(parity_of), both_halves, n_active.reshape(1),
                  x_rows, w_gate_up, bg, bu, w_down, b_down.reshape(N_EXPERTS, 1, d))
    out = _combine(dest8, x1.reshape(n_tok, d), mod, route, row2(ln2_gain), row2(ln2_bias), y_rows, seq)
    return out.reshape(bsz, seq, d)


def kernel(x, c, w_ada, b_ada, w_in, w_pool_group, pool_scale, w_branch_a, w_alpha_up, b_alpha, gla_norm_gain,
           w_branch_b, w_out, ln1_gain, ln1_bias, w_router, b_router, w_gate_up, b_gate_up, w_down, b_down,
           ln2_gain, ln2_bias):
    for l in range(DEPTH):
        x = _layer(x, c, w_ada[l], b_ada[l], _split_in_proj(w_in, l), w_pool_group[l], pool_scale[l], w_branch_a[l],
                   w_alpha_up[l], b_alpha[l], gla_norm_gain[l], w_branch_b[l], w_out[l], ln1_gain[l],
                   ln1_bias[l], w_router[l], b_router[l], w_gate_up[l], b_gate_up[l], w_down[l], b_down[l],
                   ln2_gain[l], ln2_bias[l])
    return x
```

```python
import functools

import jax
import jax.numpy as jnp
from jax import lax
from jax.experimental import pallas as pl
from jax.experimental.pallas import tpu as pltpu

D_MODEL = 1024
CHUNK = 64
SUB = 16
N_SUB = CHUNK // SUB
POOL_WINDOWS = (2, 4, 8, 16)
POOL_GROUP_WIDTH = D_MODEL // len(POOL_WINDOWS)
POOL_HALO = 16
GLA_HEADS = 4
GLA_KEY_DIM = D_MODEL // 2
GLA_HEAD_K = GLA_KEY_DIM // GLA_HEADS
GLA_HEAD_V = D_MODEL // GLA_HEADS
GLA_GATE_RANK = 16
GLA_TAU = 16.0
N_EXPERTS = 32
TOP_K = 4
SWIGLU_ALPHA = 1.702
SWIGLU_LIMIT = 7.0
EXPERT_BLOCK = 512
EXPERT_HALF = EXPERT_BLOCK // 2
LN_EPS = 1e-5
RMS_EPS = 1e-6
DEPTH = 1
DEEPNORM_ALPHA = (2.0 * DEPTH) ** 0.25

LANES = 128
MXU_COLS = 256
RANK_PAD = LANES
_W = (D_MODEL, GLA_KEY_DIM, GLA_KEY_DIM, D_MODEL, D_MODEL, RANK_PAD, D_MODEL, D_MODEL)
_OFF = tuple(sum(_W[:i]) for i in range(len(_W) + 1))
EXP_CAP = 60.0
SAFE_SUB_DECAY = 40.0
ROUTE_ROWS = 16

SEQ_TILE = 512
DISPATCH_TILE = 2048
COMBINE_TILE = 256
DMA_UNROLL = 8
VMEM_LIMIT = 60 * 1024 * 1024

F32 = jnp.float32
BF16 = jnp.bfloat16
HI = lax.Precision.HIGHEST


def _dot(a, b):
    return jnp.dot(a, b, preferred_element_type=F32)


def _dot_nt(a, b):
    return lax.dot_general(a, b, (((1,), (1,)), ((), ())), preferred_element_type=F32)


def _dot_tn(a, b):
    return lax.dot_general(a, b, (((0,), (0,)), ((), ())), preferred_element_type=F32)


def _sigmoid(v):
    return 0.5 * jnp.tanh(0.5 * v) + 0.5


ROW_SUB = D_MODEL // LANES


def _load_rows(ref, n_rows, lead=()):
    return jnp.concatenate(
        [ref[lead + (pl.ds(j, n_rows, stride=ROW_SUB), slice(None))] for j in range(ROW_SUB)], axis=1)


def _store_rows(ref, val):
    for j in range(ROW_SUB):
        ref[pl.ds(j, val.shape[0], stride=ROW_SUB), :] = val[:, j * LANES:(j + 1) * LANES]


def _ada_kernel(c_ref, w_ref, b_ref, o_ref):
    c = c_ref[...]
    s = c * jax.nn.sigmoid(c)
    o_ref[...] = jnp.dot(s, w_ref[...], precision=HI, preferred_element_type=F32) + b_ref[...]


def _ada(c, w_ada, b_ada):
    bsz, d = c.shape
    n = w_ada.shape[1]
    tn = 1024
    return pl.pallas_call(
        _ada_kernel,
        grid=(n // tn,),
        in_specs=[
            pl.BlockSpec((bsz, d), lambda j: (0, 0)),
            pl.BlockSpec((d, tn), lambda j: (0, j)),
            pl.BlockSpec((1, tn), lambda j: (0, j)),
        ],
        out_specs=pl.BlockSpec((bsz, tn), lambda j: (0, j)),
        out_shape=jax.ShapeDtypeStruct((bsz, n), F32),
        name="ada",
    )(c, w_ada, b_ada.reshape(1, n))


def _layer_norm(z, gain, bias):
    mu = jnp.mean(z, axis=-1, keepdims=True)
    zc = z - mu
    var = jnp.mean(zc * zc, axis=-1, keepdims=True)
    return zc * lax.rsqrt(var + LN_EPS) * gain + bias


def _mixer_kernel(x_ref, mod_ref, mod_prev_ref, w_in_ref, w_rank_ref, w_gates_ref, w_pool_ref, pool_scale_ref,
                  w_a_ref, w_al_ref,
                  b_al_ref, gain_ref, w_b_ref, w_out_ref, ln_g_ref, ln_b_ref, w_r_ref, b_r_ref,
                  x1_ref, u2_ref, route_ref, counts_ref, decay_ref,
                  a_ext, s_ref, cnt_ref, o_ref, z_ref, sc_ref, *, tiles_per_seq, n_tiles, direct_scores):
    j = pl.program_id(0)
    s_idx = lax.rem(jnp.minimum(j, n_tiles - 1), tiles_per_seq)
    tile = x_ref.shape[1]

    @pl.when(s_idx == 0)
    def _():
        s_ref[...] = jnp.zeros_like(s_ref)
        a_ext[0:POOL_HALO, :] = jnp.zeros((POOL_HALO, D_MODEL), F32)

    @pl.when(j == 0)
    def _():
        cnt_ref[...] = jnp.zeros_like(cnt_ref)
        z_ref[...] = jnp.zeros_like(z_ref)
        decay_ref[...] = jnp.zeros_like(decay_ref)


    mod_prev = mod_prev_ref[0]
    x1 = _layer_norm(z_ref[...], ln_g_ref[...], ln_b_ref[...])
    x1_ref[0] = x1
    u2 = x1 * (1.0 + mod_prev[4:5]) + mod_prev[3:4]
    _store_rows(u2_ref, u2)
    u2_hi = u2.astype(BF16)
    u2_lo = (u2 - u2_hi.astype(F32)).astype(BF16)

    mod = mod_ref[0]
    sh_m, sc_m, g_m = mod[0:1], mod[1:2], mod[2:3]
    x = x_ref[0]
    u = (x * (1.0 + sc_m) + sh_m).astype(BF16)

    def proj(i):
        return _dot(u, w_in_ref[:, _OFF[i]:_OFF[i + 1]])

    gate_cols = ([(w_in_ref, _OFF[4] + jj * MXU_COLS) for jj in range(D_MODEL // MXU_COLS)]
                 + [(w_gates_ref, jj * MXU_COLS) for jj in range(2 * D_MODEL // MXU_COLS)])
    fillers = [functools.partial(lambda w_ref, c0: _dot(u, w_ref[:, c0:c0 + MXU_COLS]), w_ref, c0)
               for w_ref, c0 in gate_cols]
    filled = []

    def issue_fillers(n):
        for _ in range(n):
            if len(filled) < len(fillers):
                filled.append(fillers[len(filled)]())

    a = proj(0)
    q = proj(1) * (GLA_HEAD_K ** -0.5)
    k_all = proj(2)
    v_all = proj(3)
    alpha_low = _dot(u, w_rank_ref[...])

    part = _dot_nt(w_r_ref[...], u2_hi)
    logits = (part[:N_EXPERTS] + part[N_EXPERTS:] + _dot_nt(w_r_ref[0:N_EXPERTS, :], u2_lo)
              + b_r_ref[...])
    issue_fillers(2)

    a_ext[POOL_HALO:POOL_HALO + tile, :] = a
    t_glob = s_idx * tile + lax.broadcasted_iota(jnp.int32, (tile, 1), 0)
    mapped = []
    for g, w in enumerate(POOL_WINDOWS):
        lo, hi = g * POOL_GROUP_WIDTH, (g + 1) * POOL_GROUP_WIDTH
        win = a_ext[:, lo:hi]
        k = 1
        while k < w:
            win = win + pltpu.roll(win, k, 0)
            k *= 2
        inv_cnt = 1.0 / jnp.minimum(t_glob + 1, w).astype(F32)
        pooled = win[POOL_HALO:, :] * inv_cnt - a[:, lo:hi]
        mapped.append(_dot(pooled.astype(BF16), w_pool_ref[g]))
    a_ext[0:POOL_HALO, :] = a[tile - POOL_HALO:tile, :]
    ya = _dot((jnp.concatenate(mapped, axis=1) * pool_scale_ref[...]).astype(BF16), w_a_ref[...])

    erow = lax.broadcasted_iota(jnp.int32, (N_EXPERTS, tile), 0).astype(F32)
    work = logits
    sel = jnp.zeros((N_EXPERTS, tile), F32)
    vals, hits = [], []
    for _ in range(TOP_K):
        m = jnp.max(work, axis=0, keepdims=True)
        idx = jnp.min(jnp.where(work == m, erow, float(N_EXPERTS)), axis=0, keepdims=True)
        hit = erow == idx
        vals.append(m)
        hits.append((idx, hit))
        sel = jnp.where(hit, 1.0, sel)
        work = jnp.where(hit, -jnp.inf, work)
    exps = [jnp.exp(v - vals[0]) for v in vals]
    inv_den = 1.0 / (exps[0] + exps[1] + exps[2] + exps[3])
    issue_fillers(1)

    z = _dot(alpha_low.astype(BF16), w_al_ref[...]) + b_al_ref[...]
    log_a = (jnp.minimum(z, 0.0) - jnp.log1p(jnp.exp(-jnp.abs(z)))) * (1.0 / GLA_TAU)

    ri = lax.broadcasted_iota(jnp.int32, (CHUNK, CHUNK), 0)
    ci = lax.broadcasted_iota(jnp.int32, (CHUNK, CHUNK), 1)
    causal = ci <= ri
    cum_mat = causal.astype(BF16)
    la_hi = log_a.astype(BF16)
    la_split = jnp.concatenate([la_hi, (log_a - la_hi.astype(F32)).astype(BF16)], axis=1)

    n_chunks = tile // CHUNK
    pairs = [(c, h) for c in range(n_chunks) for h in range(GLA_HEADS)]
    rows_of = lambda c: slice(c * CHUNK, (c + 1) * CHUNK)
    ks_of = lambda h: slice(h * GLA_HEAD_K, (h + 1) * GLA_HEAD_K)
    vs_of = lambda h: slice(h * GLA_HEAD_V, (h + 1) * GLA_HEAD_V)

    b_cum, ref_pts, b_ref_pt = [], [], []
    for c in range(n_chunks):
        cum = _dot(cum_mat, la_split[rows_of(c)])
        b_cum.append(cum[:, :GLA_KEY_DIM] + cum[:, GLA_KEY_DIM:])
        ref_pts.append([jnp.zeros((1, GLA_KEY_DIM), F32)]
                       + [b_cum[c][i * SUB - 1:i * SUB, :] for i in range(1, N_SUB)])
        b_ref_pt.append(jnp.concatenate([jnp.broadcast_to(p, (SUB, GLA_KEY_DIM)) for p in ref_pts[c]], axis=0))

    rt = lax.broadcasted_iota(jnp.int32, (tile, tile), 0)
    ct = lax.broadcasted_iota(jnp.int32, (tile, tile), 1)
    base = _dot(sel.astype(BF16), (rt < ct).astype(BF16)) + cnt_ref[:, 0:1]

    sub_decay = b_ref_pt[0] - b_cum[0]
    for c in range(1, n_chunks):
        sub_decay = jnp.maximum(sub_decay, b_ref_pt[c] - b_cum[c])
    decay_ref[...] = jnp.maximum(decay_ref[...], jnp.max(sub_decay, keepdims=True))

    scores = {}
    if not direct_scores:
        for n_pair, (c, h) in enumerate(pairs):
            if n_pair % 3 == 0:
                issue_fillers(1)
            qh, kh, bh = q[rows_of(c), ks_of(h)], k_all[rows_of(c), ks_of(h)], b_cum[c][:, ks_of(h)]
            q_dec = (qh * jnp.exp(bh - b_ref_pt[c][:, ks_of(h)])).astype(BF16)
            k_dec = jnp.concatenate(
                [(kh * jnp.exp(jnp.minimum(ref_pts[c][i][:, ks_of(h)] - bh, EXP_CAP))).astype(BF16)
                 for i in range(N_SUB)], axis=0)
            s_all = _dot_nt(q_dec, k_dec)
            scores[c, h] = jnp.concatenate(
                [s_all[i * SUB:(i + 1) * SUB, i * CHUNK:(i + 1) * CHUNK] for i in range(N_SUB)], axis=0)
    else:
        o_ref[:, 0:GLA_KEY_DIM] = q
        o_ref[:, GLA_KEY_DIM:2 * GLA_KEY_DIM] = jnp.concatenate(b_cum, axis=0)
        for n_pair, (c, h) in enumerate(pairs):
            kh, bh = k_all[rows_of(c), ks_of(h)], b_cum[c][:, ks_of(h)]

            def score_rows(g, carry):
                first = pl.multiple_of(g * 8, 8)
                q_8 = o_ref[pl.ds(c * CHUNK + first, 8), ks_of(h)]
                b_8 = o_ref[pl.ds(c * CHUNK + first, 8),
                            GLA_KEY_DIM + h * GLA_HEAD_K:GLA_KEY_DIM + (h + 1) * GLA_HEAD_K]
                rows = []
                for r in range(8):
                    k_i = (kh * jnp.exp(jnp.minimum(b_8[r:r + 1] - bh, 0.0))).astype(BF16)
                    q_i = jnp.broadcast_to(q_8[r:r + 1], (8, GLA_HEAD_K)).astype(BF16)
                    rows.append(_dot_nt(q_i, k_i)[0:1])
                sc_ref[n_pair, pl.ds(first, 8), :] = jnp.concatenate(rows, axis=0)
                return carry

            lax.fori_loop(0, CHUNK // 8, score_rows, 0)

    o_intra, kv, decay_last = {}, {}, {}
    for n_pair, (c, h) in enumerate(pairs):
        kh, bh = k_all[rows_of(c), ks_of(h)], b_cum[c][:, ks_of(h)]
        vh = v_all[rows_of(c), vs_of(h)].astype(BF16)
        chunk_scores = sc_ref[n_pair] if direct_scores else scores[c, h]
        o_intra[c, h] = _dot(jnp.where(causal, chunk_scores, 0.0).astype(BF16), vh)
        b_last = bh[CHUNK - 1:CHUNK, :]
        kv[c, h] = _dot_tn(vh, (kh * jnp.exp(b_last - bh)).astype(BF16))
        decay_last[c, h] = jnp.exp(b_last)

    orow = lax.broadcasted_iota(jnp.int32, (ROUTE_ROWS, tile), 0)
    route = jnp.zeros((ROUTE_ROWS, tile), F32)
    for kk in range(TOP_K):
        idx, hit = hits[kk]
        rank = jnp.sum(jnp.where(hit, base, 0.0), axis=0, keepdims=True)
        route = jnp.where(orow == kk, idx, route)
        route = jnp.where(orow == TOP_K + kk, rank, route)
        route = jnp.where(orow == 2 * TOP_K + kk, exps[kk] * inv_den, route)
    route_ref[...] = route
    cnt_ref[...] = cnt_ref[...] + (j > 0).astype(F32) * jnp.sum(sel, axis=1, keepdims=True)
    counts_ref[...] = cnt_ref[...]

    o_inter = {}
    state_t = [s_ref[h] for h in range(GLA_HEADS)]
    for c, h in pairs:
        q_in = (q[rows_of(c), ks_of(h)] * jnp.exp(b_cum[c][:, ks_of(h)])).astype(BF16)
        o_inter[c, h] = _dot_nt(q_in, state_t[h].astype(BF16))
        state_t[h] = state_t[h] * decay_last[c, h] + kv[c, h]
    for h in range(GLA_HEADS):
        s_ref[h] = state_t[h]

    for c, h in pairs:
        o = o_intra[c, h] + o_inter[c, h]
        o = o * lax.rsqrt(jnp.mean(o * o, axis=-1, keepdims=True) + RMS_EPS) * gain_ref[...]
        o_ref[rows_of(c), vs_of(h)] = o

    issue_fillers(len(fillers))
    per_proj = D_MODEL // MXU_COLS
    r, gate_a, gate_b = (jnp.concatenate(filled[n * per_proj:(n + 1) * per_proj], axis=1) for n in range(3))
    yb = _dot((o_ref[...] * (r * _sigmoid(r))).astype(BF16), w_b_ref[...])

    merged = _sigmoid(gate_a) * ya + _sigmoid(gate_b) * yb
    y = _dot(merged.astype(BF16), w_out_ref[...])
    z_ref[...] = DEEPNORM_ALPHA * x + g_m * y


def _mixer(direct_scores, x, mod, w_in_p, w_rank_p, w_gates, w_pool, pool_scale, w_a, w_al_p, b_alpha, gain, w_b,
           w_out, ln_g, ln_b, w_r_p, b_r_p):
    bsz, seq, d = x.shape
    tile = SEQ_TILE
    n_s = seq // tile
    n_tiles = bsz * n_s

    def const(shape):
        nd = len(shape)
        return pl.BlockSpec(shape, lambda j: (0,) * nd, pipeline_mode=pl.Buffered(1))

    cur = lambda j: jnp.minimum(j, n_tiles - 1)
    prev = lambda j: jnp.maximum(j - 1, 0)
    return pl.pallas_call(
        functools.partial(_mixer_kernel, tiles_per_seq=n_s, n_tiles=n_tiles, direct_scores=direct_scores),
        grid=(n_tiles + 1,),
        in_specs=[
            pl.BlockSpec((1, tile, d), lambda j: (cur(j) // n_s, cur(j) % n_s, 0)),
            pl.BlockSpec((1, 6, d), lambda j: (cur(j) // n_s, 0, 0)),
            pl.BlockSpec((1, 6, d), lambda j: (prev(j) // n_s, 0, 0)),
            const(w_in_p.shape), const(w_rank_p.shape), const(w_gates.shape),
            const(w_pool.shape), const(pool_scale.shape), const(w_a.shape),
            const(w_al_p.shape), const(b_alpha.shape), const(gain.shape), const(w_b.shape),
            const(w_out.shape), const(ln_g.shape), const(ln_b.shape), const(w_r_p.shape),
            const(b_r_p.shape),
        ],
        out_specs=[pl.BlockSpec((1, tile, d), lambda j: (prev(j) // n_s, prev(j) % n_s, 0)),
                   pl.BlockSpec((tile * ROW_SUB, LANES), lambda j: (prev(j), 0)),
                   pl.BlockSpec((ROUTE_ROWS, tile), lambda j: (0, prev(j))),
                   pl.BlockSpec((N_EXPERTS, LANES), lambda j: (0, 0)),
                   pl.BlockSpec((8, LANES), lambda j: (0, 0))],
        out_shape=[
            jax.ShapeDtypeStruct((bsz, seq, d), F32),
            jax.ShapeDtypeStruct((bsz * seq * ROW_SUB, LANES), F32),
            jax.ShapeDtypeStruct((ROUTE_ROWS, bsz * seq), F32),
            jax.ShapeDtypeStruct((N_EXPERTS, LANES), F32),
            jax.ShapeDtypeStruct((8, LANES), F32),
        ],
        scratch_shapes=[
            pltpu.VMEM((POOL_HALO + tile, d), F32),
            pltpu.VMEM((GLA_HEADS, GLA_HEAD_V, GLA_HEAD_K), F32),
            pltpu.VMEM((N_EXPERTS, LANES), F32),
            pltpu.VMEM((tile, d), F32),
            pltpu.VMEM((tile, d), F32),
            pltpu.VMEM((tile // CHUNK * GLA_HEADS, CHUNK, CHUNK), F32),
        ],
        compiler_params=pltpu.CompilerParams(
            dimension_semantics=("arbitrary",), vmem_limit_bytes=VMEM_LIMIT),
        name="mixer",
    )(x, mod, mod, w_in_p, w_rank_p, w_gates, w_pool, pool_scale, w_a, w_al_p, b_alpha, gain, w_b, w_out,
      ln_g, ln_b, w_r_p, b_r_p)


def _row_at(ref, row8):
    return ref.at[pl.ds(pl.multiple_of(row8, ROW_SUB), ROW_SUB)]


def _per_tile(dest8, tile):
    n_t = dest8.shape[1] // tile
    return dest8.reshape(TOP_K, n_t, tile).transpose(1, 0, 2).reshape(n_t, 1, TOP_K * tile)


def _dispatch_kernel(fill_ref, dest_ref, u2_hbm, rows_hbm, ubuf, zbuf, sem_fill, sem_load, sem_rows):
    i = pl.program_id(0)
    tile = ubuf.shape[1] // ROW_SUB
    blk8 = EXPERT_BLOCK * ROW_SUB

    def zero_fills(phase):
        def piece(first_row, n_rows):
            dst = rows_hbm.at[pl.ds(pl.multiple_of(first_row * ROW_SUB, ROW_SUB), n_rows * ROW_SUB)]
            getattr(pltpu.make_async_copy(zbuf.at[pl.ds(0, n_rows * ROW_SUB)], dst, sem_fill.at[0]), phase)()

        for e in range(N_EXPERTS):
            first_row, gap = fill_ref[e], fill_ref[N_EXPERTS + e]
            n_rows = EXPERT_BLOCK // 2
            while n_rows >= 1:
                @pl.when(jnp.bitwise_and(gap, n_rows) != 0)
                def _(first_row=first_row, n_rows=n_rows):
                    piece(first_row, n_rows)
                first_row = first_row + jnp.bitwise_and(gap, n_rows)
                n_rows //= 2

        def unused(window, carry):
            piece(window * EXPERT_BLOCK, EXPERT_BLOCK)
            return carry

        lax.fori_loop(fill_ref[2 * N_EXPERTS], rows_hbm.shape[0] // blk8, unused, 0)

    n_steps = pl.num_programs(0)
    slot = lax.rem(i, 2)

    def load(step, s):
        src = u2_hbm.at[pl.ds(pl.multiple_of(step * tile * ROW_SUB, ROW_SUB), tile * ROW_SUB)]
        return pltpu.make_async_copy(src, ubuf.at[s], sem_load.at[s])

    def wait_rows(s):
        for _ in range(TOP_K):
            pltpu.make_async_copy(ubuf.at[s], rows_hbm.at[pl.ds(0, tile * ROW_SUB)], sem_rows.at[s]).wait()

    @pl.when(i == 0)
    def _():
        zbuf[...] = jnp.zeros_like(zbuf)
        zero_fills("start")
        load(0, 0).start()

    load(i, slot).wait()

    def body(r, carry):
        for kk in range(TOP_K):
            pltpu.make_async_copy(_row_at(ubuf.at[slot], r * ROW_SUB),
                                  _row_at(rows_hbm, dest_ref[0, 0, kk * tile + r]),
                                  sem_rows.at[slot]).start(priority=kk % 2)
        return carry

    lax.fori_loop(0, tile, body, 0, unroll=DMA_UNROLL)

    @pl.when(i >= 1)
    def _():
        wait_rows(1 - slot)

    @pl.when(i + 1 < n_steps)
    def _():
        load(i + 1, 1 - slot).start()

    @pl.when(i == n_steps - 1)
    def _():
        wait_rows(slot)
        zero_fills("wait")


def _dispatch(fill_start, dest8, u2_rows, n_rows):
    n_tok = u2_rows.shape[0] // ROW_SUB
    tile = DISPATCH_TILE
    n_t = n_tok // tile
    grid_spec = pltpu.PrefetchScalarGridSpec(
        num_scalar_prefetch=1,
        grid=(n_t,),
        in_specs=[
            pl.BlockSpec((1, 1, tile * TOP_K), lambda i, fs: (i, 0, 0), memory_space=pltpu.SMEM),
            pl.BlockSpec(memory_space=pl.ANY),
        ],
        out_specs=pl.BlockSpec(memory_space=pl.ANY),
        scratch_shapes=[
            pltpu.VMEM((2, tile * ROW_SUB, LANES), F32),
            pltpu.VMEM((EXPERT_BLOCK * ROW_SUB, LANES), F32),
            pltpu.SemaphoreType.DMA((1,)),
            pltpu.SemaphoreType.DMA((2,)),
            pltpu.SemaphoreType.DMA((2,)),
        ],
    )
    return pl.pallas_call(
        _dispatch_kernel,
        grid_spec=grid_spec,
        out_shape=jax.ShapeDtypeStruct(((n_rows + EXPERT_BLOCK) * ROW_SUB, LANES), F32),
        compiler_params=pltpu.CompilerParams(dimension_semantics=("arbitrary",), vmem_limit_bytes=VMEM_LIMIT),
        name="dispatch",
    )(fill_start, _per_tile(dest8, tile), u2_rows)


def _moe_kernel(be_ref, nxt_ref, par_ref, full_ref, nb_ref, x_ref, bg_ref, bu_ref, bd_ref, wgu_hbm, wd_hbm, y_ref,
                wgu_buf, wd_buf, wg_s, wu_s, wd_s, sem_gu, sem_d):
    i = pl.program_id(0)
    new_expert = (i == 0) | (be_ref[i] != be_ref[jnp.maximum(i - 1, 0)])

    def fetch(e, s):
        return (pltpu.make_async_copy(wgu_hbm.at[e], wgu_buf.at[s], sem_gu.at[s]),
                pltpu.make_async_copy(wd_hbm.at[e], wd_buf.at[s], sem_d.at[s]))

    @pl.when(new_expert & (i < nb_ref[0]))
    def _():
        slot = par_ref[i]

        @pl.when(i == 0)
        def _():
            for cp in fetch(be_ref[0], slot):
                cp.start()

        @pl.when(nxt_ref[i] >= 0)
        def _():
            for cp in fetch(nxt_ref[i], 1 - slot):
                cp.start()

        for cp in fetch(be_ref[i], slot):
            cp.wait()
        src = lax.broadcasted_iota(jnp.int32, (MXU_COLS, MXU_COLS), 0)
        col = lax.broadcasted_iota(jnp.int32, (MXU_COLS, MXU_COLS), 1)
        half = MXU_COLS // 2
        want = jnp.where(col < half, 2 * col, 2 * (col - half) + 1)
        unzip = (src == want).astype(BF16)
        for g in range(wgu_buf.shape[2] // MXU_COLS):
            blk = wgu_buf[slot, :, g * MXU_COLS:(g + 1) * MXU_COLS].astype(BF16)
            sep = _dot(blk, unzip)
            wg_s[:, g * half:(g + 1) * half] = sep[:, :half].astype(BF16)
            wu_s[:, g * half:(g + 1) * half] = sep[:, half:].astype(BF16)
        wd_s[...] = wd_buf[slot].astype(BF16)

    def expert_mlp(n_rows):
        xb = _load_rows(x_ref, n_rows).astype(BF16)
        gate = jnp.minimum(_dot(xb, wg_s[...]) + bg_ref[0], SWIGLU_LIMIT)
        up = jnp.clip(_dot(xb, wu_s[...]) + bu_ref[0], -SWIGLU_LIMIT, SWIGLU_LIMIT)
        glu = gate * _sigmoid(gate * SWIGLU_ALPHA)
        _store_rows(y_ref, _dot(((up + 1.0) * glu).astype(BF16), wd_s[...]) + bd_ref[0])

    active = i < nb_ref[0]
    both_halves = full_ref[i] == 1

    @pl.when(active & both_halves)
    def _():
        expert_mlp(EXPERT_BLOCK)

    @pl.when(active & jnp.logical_not(both_halves))
    def _():
        expert_mlp(EXPERT_HALF)
        y_ref[EXPERT_HALF * ROW_SUB:, :] = jnp.zeros((EXPERT_HALF * ROW_SUB, LANES), F32)

    @pl.when(jnp.logical_not(active))
    def _():
        y_ref[...] = jnp.zeros_like(y_ref)


def _moe(block_expert, next_expert, slot_parity, both_halves, n_active, x_rows, w_gate_up, bg, bu, w_down, bd):
    n_blocks = block_expert.shape[0]
    _, d, f2 = w_gate_up.shape
    f = f2 // 2
    blk8 = EXPERT_BLOCK * ROW_SUB
    rows_in = pl.BlockSpec((blk8, LANES),
                           lambda i, be, nx, pr, fl, nb: (jnp.maximum(jnp.minimum(i, nb[0] - 1), 0), 0))
    rows_out = pl.BlockSpec((blk8, LANES), lambda i, be, nx, pr, fl, nb: (i, 0))
    per_expert = lambda shape: pl.BlockSpec((1,) + shape, lambda i, be, nx, pr, fl, nb: (be[i], 0, 0))
    hbm = pl.BlockSpec(memory_space=pl.ANY)
    grid_spec = pltpu.PrefetchScalarGridSpec(
        num_scalar_prefetch=5,
        grid=(n_blocks,),
        in_specs=[rows_in, per_expert((1, f)), per_expert((1, f)), per_expert((1, d)), hbm, hbm],
        out_specs=rows_out,
        scratch_shapes=[pltpu.VMEM((2, d, f2), F32), pltpu.VMEM((2, f, d), F32),
                        pltpu.VMEM((d, f), BF16), pltpu.VMEM((d, f), BF16), pltpu.VMEM((f, d), BF16),
                        pltpu.SemaphoreType.DMA((2,)), pltpu.SemaphoreType.DMA((2,))],
    )
    return pl.pallas_call(
        _moe_kernel,
        grid_spec=grid_spec,
        out_shape=jax.ShapeDtypeStruct((n_blocks * blk8, LANES), F32),
        compiler_params=pltpu.CompilerParams(
            dimension_semantics=("arbitrary",), vmem_limit_bytes=VMEM_LIMIT),
        name="moe",
    )(block_expert, next_expert, slot_parity, both_halves, n_active, x_rows, bg, bu, bd, w_gate_up, w_down)


def _combine_kernel(dest_cur_ref, dest_nxt_ref, x1_ref, mod_ref, route_ref, ln_g_ref, ln_b_ref, y_hbm,
                    o_ref, ybuf, sem):
    i = pl.program_id(0)
    n_steps = pl.num_programs(0)
    tile = x1_ref.shape[0]
    slot = lax.rem(i, 2)

    def gather(dest_ref, s):
        def body(r, carry):
            for kk in range(TOP_K):
                pltpu.make_async_copy(_row_at(y_hbm, dest_ref[0, 0, kk * tile + r]),
                                      _row_at(ybuf.at[s, kk], r * ROW_SUB), sem.at[s]).start(priority=kk % 2)
            return carry
        lax.fori_loop(0, tile, body, 0, unroll=DMA_UNROLL)

    @pl.when(i == 0)
    def _():
        gather(dest_cur_ref, 0)

    def wait_slot(s):
        for kk in range(TOP_K):
            pltpu.make_async_copy(y_hbm.at[pl.ds(0, tile * ROW_SUB)], ybuf.at[s, kk], sem.at[s]).wait()

    for r in range(tile):
        for kk in range(TOP_K):
            pltpu.make_async_copy(_row_at(y_hbm, dest_nxt_ref[0, 0, kk * tile + r]),
                                  _row_at(ybuf.at[1 - slot, kk], r * ROW_SUB),
                                  sem.at[1 - slot]).start(priority=kk % 2)
    wait_slot(slot)

    g_f = mod_ref[0][5:6]
    weight = route_ref[...]
    y = jnp.zeros(x1_ref.shape, F32)
    for kk in range(TOP_K):
        y = y + weight[:, kk:kk + 1] * _load_rows(ybuf, tile, (slot, kk))
    o_ref[...] = _layer_norm(DEEPNORM_ALPHA * x1_ref[...] + g_f * y, ln_g_ref[...], ln_b_ref[...])

    @pl.when(i == n_steps - 1)
    def _():
        wait_slot(1 - slot)


def _combine(dest8, x1, mod, route, ln_g, ln_b, y_rows, seq):
    n_tok, d = x1.shape
    tile = COMBINE_TILE
    n_t = n_tok // tile
    per_seq = seq // tile
    dest3 = _per_tile(dest8, tile)
    smem_blk = lambda fn: pl.BlockSpec((1, 1, tile * TOP_K), fn, memory_space=pltpu.SMEM)
    row = lambda width: pl.BlockSpec((tile, width), lambda i: (i, 0))
    vec = pl.BlockSpec((1, d), lambda i: (0, 0))
    return pl.pallas_call(
        _combine_kernel,
        grid=(n_t,),
        in_specs=[smem_blk(lambda i: (i, 0, 0)), smem_blk(lambda i: (jnp.minimum(i + 1, n_t - 1), 0, 0)),
                  row(d), pl.BlockSpec((1, 6, d), lambda i: (i // per_seq, 0, 0)), row(TOP_K), vec, vec,
                  pl.BlockSpec(memory_space=pl.ANY)],
        out_specs=row(d),
        out_shape=jax.ShapeDtypeStruct((n_tok, d), F32),
        scratch_shapes=[pltpu.VMEM((2, TOP_K, tile * ROW_SUB, LANES), F32), pltpu.SemaphoreType.DMA((2,))],
        compiler_params=pltpu.CompilerParams(
            dimension_semantics=("arbitrary",), vmem_limit_bytes=VMEM_LIMIT),
        name="combine",
    )(dest3, dest3, x1, mod, route, ln_g, ln_b, y_rows)


def _split_kernel(w_ref, main_ref, rank_ref, gates_ref):
    rank_lo = _OFF[5]
    rank_hi = rank_lo + GLA_GATE_RANK
    cols = w_ref.shape[2]
    main_ref[...] = w_ref[0, 0:rank_lo, :].T.astype(BF16)
    pad = jnp.zeros((RANK_PAD - GLA_GATE_RANK, cols), F32)
    rank_ref[...] = jnp.concatenate([w_ref[0, rank_lo:rank_hi, :], pad], axis=0).T.astype(BF16)
    gates_ref[...] = w_ref[0, rank_hi:rank_hi + 2 * D_MODEL, :].T.astype(BF16)


def _split_in_proj(w_in, layer):
    w_t = jnp.swapaxes(w_in, 1, 2)
    _, _, d = w_t.shape
    cols = MXU_COLS
    piece = lambda width: pl.BlockSpec((cols, width), lambda i: (i, 0))
    return pl.pallas_call(
        _split_kernel,
        grid=(d // cols,),
        in_specs=[pl.BlockSpec((1, w_t.shape[1], cols), lambda i: (layer, 0, i))],
        out_specs=[piece(_OFF[5]), piece(RANK_PAD), piece(2 * D_MODEL)],
        out_shape=[jax.ShapeDtypeStruct((d, _OFF[5]), BF16), jax.ShapeDtypeStruct((d, RANK_PAD), BF16),
                   jax.ShapeDtypeStruct((d, 2 * D_MODEL), BF16)],
        name="split_in_proj",
    )(w_t)


def _layer(x, c, w_ada, b_ada, w_in, w_pool_group, pool_scale, w_branch_a, w_alpha_up, b_alpha,
           gla_norm_gain, w_branch_b, w_out, ln1_gain, ln1_bias, w_router, b_router,
           w_gate_up, b_gate_up, w_down, b_down, ln2_gain, ln2_bias):
    bsz, seq, d = x.shape
    n_tok = bsz * seq
    n_assign = n_tok * TOP_K
    row2 = lambda v: v.reshape(1, -1)

    mod = _ada(c, w_ada, b_ada).reshape(bsz, 6, d)

    w_in_p, w_rank_p, w_gates = w_in
    w_al_p = jnp.concatenate(
        [w_alpha_up, jnp.zeros((RANK_PAD - GLA_GATE_RANK, GLA_KEY_DIM), w_alpha_up.dtype)], axis=0).astype(BF16)
    w_r_t = w_router.T
    w_r_hi = w_r_t.astype(BF16)
    w_r_split = jnp.concatenate([w_r_hi, (w_r_t - w_r_hi.astype(F32)).astype(BF16)], axis=0)
    mixer_args = (x, mod, w_in_p, w_rank_p, w_gates, w_pool_group.astype(BF16), row2(pool_scale),
                  w_branch_a.astype(BF16), w_al_p, row2(b_alpha), row2(gla_norm_gain), w_branch_b.astype(BF16),
                  w_out.astype(BF16), row2(ln1_gain), row2(ln1_bias), w_r_split, b_router.reshape(N_EXPERTS, 1))
    *mixed, sub_decay = _mixer(False, *mixer_args)

    def routed_half(x1, u2_rows, route_t, counts):
        return _routed_moe(x1, u2_rows, route_t, counts, mod, w_gate_up, b_gate_up, w_down, b_down, ln2_gain,
                           ln2_bias)

    return lax.cond(sub_decay[0, 0] > SAFE_SUB_DECAY,
                    lambda: routed_half(*_mixer(True, *mixer_args)[:4]),
                    lambda: routed_half(*mixed))


def _routed_moe(x1, u2_rows, route_t, counts, mod, w_gate_up, b_gate_up, w_down, b_down, ln2_gain, ln2_bias):
    bsz, seq, d = x1.shape
    n_tok = bsz * seq
    n_assign = n_tok * TOP_K
    row2 = lambda v: v.reshape(1, -1)

    top_idx = route_t[0:TOP_K].astype(jnp.int32)
    rank = route_t[TOP_K:2 * TOP_K].astype(jnp.int32)
    route = route_t[2 * TOP_K:3 * TOP_K].T
    counts = counts[:, 0].astype(jnp.int32)
    padded = (counts + EXPERT_BLOCK - 1) // EXPERT_BLOCK * EXPERT_BLOCK
    pad_end = jnp.cumsum(padded)
    pad_start = pad_end - padded
    n_rows = (n_assign + N_EXPERTS * (EXPERT_BLOCK - 1) + EXPERT_BLOCK - 1) // EXPERT_BLOCK * EXPERT_BLOCK
    n_blocks = n_rows // EXPERT_BLOCK
    n_active = (pad_end[-1] // EXPERT_BLOCK).astype(jnp.int32)
    start_of = jnp.sum(jnp.where(top_idx[..., None] == jnp.arange(N_EXPERTS, dtype=jnp.int32), pad_start, 0),
                       axis=-1)
    dest8 = (start_of + rank) * ROW_SUB
    blk_row = jnp.arange(n_blocks, dtype=jnp.int32)[:, None] * EXPERT_BLOCK
    block_expert = jnp.minimum(jnp.sum(pad_end[None, :] <= blk_row, axis=1), N_EXPERTS - 1).astype(jnp.int32)

    fill_table = jnp.concatenate([pad_start + counts, padded - counts, n_active.reshape(1)]).astype(jnp.int32)
    x_rows = _dispatch(fill_table, dest8, u2_rows, n_rows)
    f = w_down.shape[1]
    bg = b_gate_up[:, 0::2].reshape(N_EXPERTS, 1, f)
    bu = b_gate_up[:, 1::2].reshape(N_EXPERTS, 1, f)
    eid = jnp.arange(N_EXPERTS, dtype=jnp.int32)
    owns = counts > 0
    later = jnp.where((eid[None, :] > eid[:, None]) & owns[None, :], eid[None, :], N_EXPERTS)
    next_of = jnp.min(later, axis=1)
    next_of = jnp.where(next_of == N_EXPERTS, -1, next_of).astype(jnp.int32)
    parity_of = ((jnp.cumsum(owns.astype(jnp.int32)) - 1) % 2).astype(jnp.int32)
    is_block_expert = block_expert[:, None] == eid[None, :]
    per_block = lambda v: jnp.sum(jnp.where(is_block_expert, v[None, :], 0), axis=1).astype(jnp.int32)
    left = per_block(pad_start + counts) - blk_row[:, 0]
    both_halves = (left > EXPERT_HALF).astype(jnp.int32)
    y_rows = _moe(block_expert, per_block(next_of), per_block(parity_of), both_halves, n_active.reshape(1),
                  x_rows, w_gate_up, bg, bu, w_down, b_down.reshape(N_EXPERTS, 1, d))
    out = _combine(dest8, x1.reshape(n_tok, d), mod, route, row2(ln2_gain), row2(ln2_bias), y_rows, seq)
    return out.reshape(bsz, seq, d)


def kernel(x, c, w_ada, b_ada, w_in, w_pool_group, pool_scale, w_branch_a, w_alpha_up, b_alpha, gla_norm_gain,
           w_branch_b, w_out, ln1_gain, ln1_bias, w_router, b_router, w_gate_up, b_gate_up, w_down, b_down,
           ln2_gain, ln2_bias):
    for l in range(DEPTH):
        x = _layer(x, c, w_ada[l], b_ada[l], _split_in_proj(w_in, l), w_pool_group[l], pool_scale[l], w_branch_a[l],
                   w_alpha_up[l], b_alpha[l], gla_norm_gain[l], w_branch_b[l], w_out[l], ln1_gain[l],
                   ln1_bias[l], w_router[l], b_router[l], w_gate_up[l], b_gate_up[l], w_down[l], b_down[l],
                   ln2_gain[l], ln2_bias[l])
    return x
```

```python
import functools

import jax
import jax.numpy as jnp
from jax import lax
from jax.experimental import pallas as pl
from jax.experimental.pallas import tpu as pltpu

D_MODEL = 1024
CHUNK = 64
SUB = 16
N_SUB = CHUNK // SUB
POOL_WINDOWS = (2, 4, 8, 16)
POOL_GROUP_WIDTH = D_MODEL // len(POOL_WINDOWS)
POOL_HALO = 16
GLA_HEADS = 4
GLA_KEY_DIM = D_MODEL // 2
GLA_HEAD_K = GLA_KEY_DIM // GLA_HEADS
GLA_HEAD_V = D_MODEL // GLA_HEADS
GLA_GATE_RANK = 16
GLA_TAU = 16.0
N_EXPERTS = 32
TOP_K = 4
SWIGLU_ALPHA = 1.702
SWIGLU_LIMIT = 7.0
EXPERT_BLOCK = 512
EXPERT_HALF = EXPERT_BLOCK // 2
LN_EPS = 1e-5
RMS_EPS = 1e-6
DEPTH = 1
DEEPNORM_ALPHA = (2.0 * DEPTH) ** 0.25

LANES = 128
MXU_COLS = 256
RANK_PAD = LANES
_W = (D_MODEL, GLA_KEY_DIM, GLA_KEY_DIM, D_MODEL, D_MODEL, RANK_PAD, D_MODEL, D_MODEL)
_OFF = tuple(sum(_W[:i]) for i in range(len(_W) + 1))
EXP_CAP = 60.0
SAFE_SUB_DECAY = 40.0
ROUTE_ROWS = 16

SEQ_TILE = 512
DISPATCH_TILE = 2048
COMBINE_TILE = 256
DMA_UNROLL = 8
VMEM_LIMIT = 60 * 1024 * 1024

F32 = jnp.float32
BF16 = jnp.bfloat16
HI = lax.Precision.HIGHEST


def _dot(a, b):
    return jnp.dot(a, b, preferred_element_type=F32)


def _dot_nt(a, b):
    return lax.dot_general(a, b, (((1,), (1,)), ((), ())), preferred_element_type=F32)


def _dot_tn(a, b):
    return lax.dot_general(a, b, (((0,), (0,)), ((), ())), preferred_element_type=F32)


def _sigmoid(v):
    return 0.5 * jnp.tanh(0.5 * v) + 0.5


ROW_SUB = D_MODEL // LANES


def _load_rows(ref, n_rows, lead=()):
    return jnp.concatenate(
        [ref[lead + (pl.ds(j, n_rows, stride=ROW_SUB), slice(None))] for j in range(ROW_SUB)], axis=1)


def _store_rows(ref, val):
    for j in range(ROW_SUB):
        ref[pl.ds(j, val.shape[0], stride=ROW_SUB), :] = val[:, j * LANES:(j + 1) * LANES]


def _ada_kernel(c_ref, w_ref, b_ref, o_ref):
    c = c_ref[...]
    s = c * jax.nn.sigmoid(c)
    o_ref[...] = jnp.dot(s, w_ref[...], precision=HI, preferred_element_type=F32) + b_ref[...]


def _ada(c, w_ada, b_ada):
    bsz, d = c.shape
    n = w_ada.shape[1]
    tn = 1024
    return pl.pallas_call(
        _ada_kernel,
        grid=(n // tn,),
        in_specs=[
            pl.BlockSpec((bsz, d), lambda j: (0, 0)),
            pl.BlockSpec((d, tn), lambda j: (0, j)),
            pl.BlockSpec((1, tn), lambda j: (0, j)),
        ],
        out_specs=pl.BlockSpec((bsz, tn), lambda j: (0, j)),
        out_shape=jax.ShapeDtypeStruct((bsz, n), F32),
        name="ada",
    )(c, w_ada, b_ada.reshape(1, n))


def _layer_norm(z, gain, bias):
    mu = jnp.mean(z, axis=-1, keepdims=True)
    zc = z - mu
    var = jnp.mean(zc * zc, axis=-1, keepdims=True)
    return zc * lax.rsqrt(var + LN_EPS) * gain + bias


def _mixer_kernel(x_ref, mod_ref, mod_prev_ref, w_in_ref, w_rank_ref, w_gates_ref, w_pool_ref, pool_scale_ref,
                  w_a_ref, w_al_ref,
                  b_al_ref, gain_ref, w_b_ref, w_out_ref, ln_g_ref, ln_b_ref, w_r_ref, b_r_ref,
                  x1_ref, u2_ref, route_ref, counts_ref, decay_ref,
                  a_ext, s_ref, cnt_ref, o_ref, z_ref, sc_ref, *, tiles_per_seq, n_tiles, direct_scores):
    j = pl.program_id(0)
    s_idx = lax.rem(jnp.minimum(j, n_tiles - 1), tiles_per_seq)
    tile = x_ref.shape[1]

    @pl.when(s_idx == 0)
    def _():
        s_ref[...] = jnp.zeros_like(s_ref)
        a_ext[0:POOL_HALO, :] = jnp.zeros((POOL_HALO, D_MODEL), F32)

    @pl.when(j == 0)
    def _():
        cnt_ref[...] = jnp.zeros_like(cnt_ref)
        z_ref[...] = jnp.zeros_like(z_ref)
        decay_ref[...] = jnp.zeros_like(decay_ref)


    mod_prev = mod_prev_ref[0]
    x1 = _layer_norm(z_ref[...], ln_g_ref[...], ln_b_ref[...])
    x1_ref[0] = x1
    u2 = x1 * (1.0 + mod_prev[4:5]) + mod_prev[3:4]
    _store_rows(u2_ref, u2)
    u2_hi = u2.astype(BF16)
    u2_lo = (u2 - u2_hi.astype(F32)).astype(BF16)

    mod = mod_ref[0]
    sh_m, sc_m, g_m = mod[0:1], mod[1:2], mod[2:3]
    x = x_ref[0]
    u = (x * (1.0 + sc_m) + sh_m).astype(BF16)

    def proj(i):
        return _dot(u, w_in_ref[:, _OFF[i]:_OFF[i + 1]])

    gate_cols = ([(w_in_ref, _OFF[4] + jj * MXU_COLS) for jj in range(D_MODEL // MXU_COLS)]
                 + [(w_gates_ref, jj * MXU_COLS) for jj in range(2 * D_MODEL // MXU_COLS)])
    fillers = [functools.partial(lambda w_ref, c0: _dot(u, w_ref[:, c0:c0 + MXU_COLS]), w_ref, c0)
               for w_ref, c0 in gate_cols]
    filled = []

    def issue_fillers(n):
        for _ in range(n):
            if len(filled) < len(fillers):
                filled.append(fillers[len(filled)]())

    a = proj(0)
    q = proj(1) * (GLA_HEAD_K ** -0.5)
    k_all = proj(2)
    v_all = proj(3)
    alpha_low = _dot(u, w_rank_ref[...])

    part = _dot_nt(w_r_ref[...], u2_hi)
    logits = (part[:N_EXPERTS] + part[N_EXPERTS:] + _dot_nt(w_r_ref[0:N_EXPERTS, :], u2_lo)
              + b_r_ref[...])
    issue_fillers(2)

    a_ext[POOL_HALO:POOL_HALO + tile, :] = a
    t_glob = s_idx * tile + lax.broadcasted_iota(jnp.int32, (tile, 1), 0)
    mapped = []
    for g, w in enumerate(POOL_WINDOWS):
        lo, hi = g * POOL_GROUP_WIDTH, (g + 1) * POOL_GROUP_WIDTH
        win = a_ext[:, lo:hi]
        k = 1
        while k < w:
            win = win + pltpu.roll(win, k, 0)
            k *= 2
        inv_cnt = 1.0 / jnp.minimum(t_glob + 1, w).astype(F32)
        pooled = win[POOL_HALO:, :] * inv_cnt - a[:, lo:hi]
        mapped.append(_dot(pooled.astype(BF16), w_pool_ref[g]))
    a_ext[0:POOL_HALO, :] = a[tile - POOL_HALO:tile, :]
    ya = _dot((jnp.concatenate(mapped, axis=1) * pool_scale_ref[...]).astype(BF16), w_a_ref[...])

    erow = lax.broadcasted_iota(jnp.int32, (N_EXPERTS, tile), 0).astype(F32)
    work = logits
    sel = jnp.zeros((N_EXPERTS, tile), F32)
    vals, hits = [], []
    for _ in range(TOP_K):
        m = jnp.max(work, axis=0, keepdims=True)
        idx = jnp.min(jnp.where(work == m, erow, float(N_EXPERTS)), axis=0, keepdims=True)
        hit = erow == idx
        vals.append(m)
        hits.append((idx, hit))
        sel = jnp.where(hit, 1.0, sel)
        work = jnp.where(hit, -jnp.inf, work)
    exps = [jnp.exp(v - vals[0]) for v in vals]
    inv_den = 1.0 / (exps[0] + exps[1] + exps[2] + exps[3])
    issue_fillers(1)

    z = _dot(alpha_low.astype(BF16), w_al_ref[...]) + b_al_ref[...]
    log_a = (jnp.minimum(z, 0.0) - jnp.log1p(jnp.exp(-jnp.abs(z)))) * (1.0 / GLA_TAU)

    ri = lax.broadcasted_iota(jnp.int32, (CHUNK, CHUNK), 0)
    ci = lax.broadcasted_iota(jnp.int32, (CHUNK, CHUNK), 1)
    causal = ci <= ri
    cum_mat = causal.astype(BF16)
    la_hi = log_a.astype(BF16)
    la_split = jnp.concatenate([la_hi, (log_a - la_hi.astype(F32)).astype(BF16)], axis=1)

    n_chunks = tile // CHUNK
    pairs = [(c, h) for c in range(n_chunks) for h in range(GLA_HEADS)]
    rows_of = lambda c: slice(c * CHUNK, (c + 1) * CHUNK)
    ks_of = lambda h: slice(h * GLA_HEAD_K, (h + 1) * GLA_HEAD_K)
    vs_of = lambda h: slice(h * GLA_HEAD_V, (h + 1) * GLA_HEAD_V)

    b_cum, ref_pts, b_ref_pt = [], [], []
    for c in range(n_chunks):
        cum = _dot(cum_mat, la_split[rows_of(c)])
        b_cum.append(cum[:, :GLA_KEY_DIM] + cum[:, GLA_KEY_DIM:])
        ref_pts.append([jnp.zeros((1, GLA_KEY_DIM), F32)]
                       + [b_cum[c][i * SUB - 1:i * SUB, :] for i in range(1, N_SUB)])
        b_ref_pt.append(jnp.concatenate([jnp.broadcast_to(p, (SUB, GLA_KEY_DIM)) for p in ref_pts[c]], axis=0))

    rt = lax.broadcasted_iota(jnp.int32, (tile, tile), 0)
    ct = lax.broadcasted_iota(jnp.int32, (tile, tile), 1)
    base = _dot(sel.astype(BF16), (rt < ct).astype(BF16)) + cnt_ref[:, 0:1]

    sub_decay = b_ref_pt[0] - b_cum[0]
    for c in range(1, n_chunks):
        sub_decay = jnp.maximum(sub_decay, b_ref_pt[c] - b_cum[c])
    decay_ref[...] = jnp.maximum(decay_ref[...], jnp.max(sub_decay, keepdims=True))

    scores = {}
    if not direct_scores:
        for n_pair, (c, h) in enumerate(pairs):
            if n_pair % 3 == 0:
                issue_fillers(1)
            qh, kh, bh = q[rows_of(c), ks_of(h)], k_all[rows_of(c), ks_of(h)], b_cum[c][:, ks_of(h)]
            q_dec = (qh * jnp.exp(bh - b_ref_pt[c][:, ks_of(h)])).astype(BF16)
            k_dec = jnp.concatenate(
                [(kh * jnp.exp(jnp.minimum(ref_pts[c][i][:, ks_of(h)] - bh, EXP_CAP))).astype(BF16)
                 for i in range(N_SUB)], axis=0)
            s_all = _dot_nt(q_dec, k_dec)
            scores[c, h] = jnp.concatenate(
                [s_all[i * SUB:(i + 1) * SUB, i * CHUNK:(i + 1) * CHUNK] for i in range(N_SUB)], axis=0)
    else:
        o_ref[:, 0:GLA_KEY_DIM] = q
        o_ref[:, GLA_KEY_DIM:2 * GLA_KEY_DIM] = jnp.concatenate(b_cum, axis=0)
        for n_pair, (c, h) in enumerate(pairs):
            kh, bh = k_all[rows_of(c), ks_of(h)], b_cum[c][:, ks_of(h)]

            def score_rows(g, carry):
                first = pl.multiple_of(g * 8, 8)
                q_8 = o_ref[pl.ds(c * CHUNK + first, 8), ks_of(h)]
                b_8 = o_ref[pl.ds(c * CHUNK + first, 8),
                            GLA_KEY_DIM + h * GLA_HEAD_K:GLA_KEY_DIM + (h + 1) * GLA_HEAD_K]
                rows = []
                for r in range(8):
                    k_i = (kh * jnp.exp(jnp.minimum(b_8[r:r + 1] - bh, 0.0))).astype(BF16)
                    q_i = jnp.broadcast_to(q_8[r:r + 1], (8, GLA_HEAD_K)).astype(BF16)
                    rows.append(_dot_nt(q_i, k_i)[0:1])
                sc_ref[n_pair, pl.ds(first, 8), :] = jnp.concatenate(rows, axis=0)
                return carry

            lax.fori_loop(0, CHUNK // 8, score_rows, 0)

    o_intra, kv, decay_last = {}, {}, {}
    for n_pair, (c, h) in enumerate(pairs):
        kh, bh = k_all[rows_of(c), ks_of(h)], b_cum[c][:, ks_of(h)]
        vh = v_all[rows_of(c), vs_of(h)].astype(BF16)
        chunk_scores = sc_ref[n_pair] if direct_scores else scores[c, h]
        o_intra[c, h] = _dot(jnp.where(causal, chunk_scores, 0.0).astype(BF16), vh)
        b_last = bh[CHUNK - 1:CHUNK, :]
        kv[c, h] = _dot_tn(vh, (kh * jnp.exp(b_last - bh)).astype(BF16))
        decay_last[c, h] = jnp.exp(b_last)

    orow = lax.broadcasted_iota(jnp.int32, (ROUTE_ROWS, tile), 0)
    route = jnp.zeros((ROUTE_ROWS, tile), F32)
    for kk in range(TOP_K):
        idx, hit = hits[kk]
        rank = jnp.sum(jnp.where(hit, base, 0.0), axis=0, keepdims=True)
        route = jnp.where(orow == kk, idx, route)
        route = jnp.where(orow == TOP_K + kk, rank, route)
        route = jnp.where(orow == 2 * TOP_K + kk, exps[kk] * inv_den, route)
    route_ref[...] = route
    cnt_ref[...] = cnt_ref[...] + (j > 0).astype(F32) * jnp.sum(sel, axis=1, keepdims=True)
    counts_ref[...] = cnt_ref[...]

    o_inter = {}
    state_t = [s_ref[h] for h in range(GLA_HEADS)]
    for c, h in pairs:
        q_in = (q[rows_of(c), ks_of(h)] * jnp.exp(b_cum[c][:, ks_of(h)])).astype(BF16)
        o_inter[c, h] = _dot_nt(q_in, state_t[h].astype(BF16))
        state_t[h] = state_t[h] * decay_last[c, h] + kv[c, h]
    for h in range(GLA_HEADS):
        s_ref[h] = state_t[h]

    for c, h in pairs:
        o = o_intra[c, h] + o_inter[c, h]
        o = o * lax.rsqrt(jnp.mean(o * o, axis=-1, keepdims=True) + RMS_EPS) * gain_ref[...]
        o_ref[rows_of(c), vs_of(h)] = o

    issue_fillers(len(fillers))
    per_proj = D_MODEL // MXU_COLS
    r, gate_a, gate_b = (jnp.concatenate(filled[n * per_proj:(n + 1) * per_proj], axis=1) for n in range(3))
    yb = _dot((o_ref[...] * (r * _sigmoid(r))).astype(BF16), w_b_ref[...])

    merged = _sigmoid(gate_a) * ya + _sigmoid(gate_b) * yb
    y = _dot(merged.astype(BF16), w_out_ref[...])
    z_ref[...] = DEEPNORM_ALPHA * x + g_m * y


def _mixer(direct_scores, x, mod, w_in_p, w_rank_p, w_gates, w_pool, pool_scale, w_a, w_al_p, b_alpha, gain, w_b,
           w_out, ln_g, ln_b, w_r_p, b_r_p):
    bsz, seq, d = x.shape
    tile = SEQ_TILE
    n_s = seq // tile
    n_tiles = bsz * n_s

    def const(shape):
        nd = len(shape)
        return pl.BlockSpec(shape, lambda j: (0,) * nd, pipeline_mode=pl.Buffered(1))

    cur = lambda j: jnp.minimum(j, n_tiles - 1)
    prev = lambda j: jnp.maximum(j - 1, 0)
    return pl.pallas_call(
        functools.partial(_mixer_kernel, tiles_per_seq=n_s, n_tiles=n_tiles, direct_scores=direct_scores),
        grid=(n_tiles + 1,),
        in_specs=[
            pl.BlockSpec((1, tile, d), lambda j: (cur(j) // n_s, cur(j) % n_s, 0)),
            pl.BlockSpec((1, 6, d), lambda j: (cur(j) // n_s, 0, 0)),
            pl.BlockSpec((1, 6, d), lambda j: (prev(j) // n_s, 0, 0)),
            const(w_in_p.shape), const(w_rank_p.shape), const(w_gates.shape),
            const(w_pool.shape), const(pool_scale.shape), const(w_a.shape),
            const(w_al_p.shape), const(b_alpha.shape), const(gain.shape), const(w_b.shape),
            const(w_out.shape), const(ln_g.shape), const(ln_b.shape), const(w_r_p.shape),
            const(b_r_p.shape),
        ],
        out_specs=[pl.BlockSpec((1, tile, d), lambda j: (prev(j) // n_s, prev(j) % n_s, 0)),
                   pl.BlockSpec((tile * ROW_SUB, LANES), lambda j: (prev(j), 0)),
                   pl.BlockSpec((ROUTE_ROWS, tile), lambda j: (0, prev(j))),
                   pl.BlockSpec((N_EXPERTS, LANES), lambda j: (0, 0)),
                   pl.BlockSpec((8, LANES), lambda j: (0, 0))],
        out_shape=[
            jax.ShapeDtypeStruct((bsz, seq, d), F32),
            jax.ShapeDtypeStruct((bsz * seq * ROW_SUB, LANES), F32),
            jax.ShapeDtypeStruct((ROUTE_ROWS, bsz * seq), F32),
            jax.ShapeDtypeStruct((N_EXPERTS, LANES), F32),
            jax.ShapeDtypeStruct((8, LANES), F32),
        ],
        scratch_shapes=[
            pltpu.VMEM((POOL_HALO + tile, d), F32),
            pltpu.VMEM((GLA_HEADS, GLA_HEAD_V, GLA_HEAD_K), F32),
            pltpu.VMEM((N_EXPERTS, LANES), F32),
            pltpu.VMEM((tile, d), F32),
            pltpu.VMEM((tile, d), F32),
            pltpu.VMEM((tile // CHUNK * GLA_HEADS, CHUNK, CHUNK), F32),
        ],
        compiler_params=pltpu.CompilerParams(
            dimension_semantics=("arbitrary",), vmem_limit_bytes=VMEM_LIMIT),
        name="mixer",
    )(x, mod, mod, w_in_p, w_rank_p, w_gates, w_pool, pool_scale, w_a, w_al_p, b_alpha, gain, w_b, w_out,
      ln_g, ln_b, w_r_p, b_r_p)


def _row_at(ref, row8):
    return ref.at[pl.ds(pl.multiple_of(row8, ROW_SUB), ROW_SUB)]


def _per_tile(dest8, tile):
    n_t = dest8.shape[1] // tile
    return dest8.reshape(TOP_K, n_t, tile).transpose(1, 0, 2).reshape(n_t, 1, TOP_K * tile)


def _dispatch_kernel(fill_ref, dest_ref, u2_ref, rows_hbm, zbuf, sem_fill, sem_rows):
    i = pl.program_id(0)
    tile = u2_ref.shape[0] // ROW_SUB
    blk8 = EXPERT_BLOCK * ROW_SUB

    def zero_fills(phase):
        def piece(first_row, n_rows):
            dst = rows_hbm.at[pl.ds(pl.multiple_of(first_row * ROW_SUB, ROW_SUB), n_rows * ROW_SUB)]
            getattr(pltpu.make_async_copy(zbuf.at[pl.ds(0, n_rows * ROW_SUB)], dst, sem_fill.at[0]), phase)()

        for e in range(N_EXPERTS):
            first_row, gap = fill_ref[e], fill_ref[N_EXPERTS + e]
            n_rows = EXPERT_BLOCK // 2
            while n_rows >= 1:
                @pl.when(jnp.bitwise_and(gap, n_rows) != 0)
                def _(first_row=first_row, n_rows=n_rows):
                    piece(first_row, n_rows)
                first_row = first_row + jnp.bitwise_and(gap, n_rows)
                n_rows //= 2

        def unused(window, carry):
            piece(window * EXPERT_BLOCK, EXPERT_BLOCK)
            return carry

        lax.fori_loop(fill_ref[2 * N_EXPERTS], rows_hbm.shape[0] // blk8, unused, 0)

    @pl.when(i == 0)
    def _():
        zbuf[...] = jnp.zeros_like(zbuf)
        zero_fills("start")

    def body(r, carry):
        for kk in range(TOP_K):
            pltpu.make_async_copy(_row_at(u2_ref, r * ROW_SUB), _row_at(rows_hbm, dest_ref[0, 0, kk * tile + r]),
                                  sem_rows.at[0]).start(priority=kk % 2)
        return carry

    lax.fori_loop(0, tile, body, 0, unroll=DMA_UNROLL)
    for _ in range(TOP_K):
        pltpu.make_async_copy(u2_ref, rows_hbm.at[pl.ds(0, tile * ROW_SUB)], sem_rows.at[0]).wait()

    @pl.when(i == pl.num_programs(0) - 1)
    def _():
        zero_fills("wait")


def _dispatch(fill_start, dest8, u2_rows, n_rows):
    n_tok = u2_rows.shape[0] // ROW_SUB
    tile = DISPATCH_TILE
    n_t = n_tok // tile
    grid_spec = pltpu.PrefetchScalarGridSpec(
        num_scalar_prefetch=1,
        grid=(n_t,),
        in_specs=[
            pl.BlockSpec((1, 1, tile * TOP_K), lambda i, fs: (i, 0, 0), memory_space=pltpu.SMEM),
            pl.BlockSpec((tile * ROW_SUB, LANES), lambda i, fs: (i, 0)),
        ],
        out_specs=pl.BlockSpec(memory_space=pl.ANY),
        scratch_shapes=[
            pltpu.VMEM((EXPERT_BLOCK * ROW_SUB, LANES), F32),
            pltpu.SemaphoreType.DMA((1,)),
            pltpu.SemaphoreType.DMA((1,)),
        ],
    )
    return pl.pallas_call(
        _dispatch_kernel,
        grid_spec=grid_spec,
        out_shape=jax.ShapeDtypeStruct(((n_rows + EXPERT_BLOCK) * ROW_SUB, LANES), F32),
        compiler_params=pltpu.CompilerParams(dimension_semantics=("arbitrary",)),
        name="dispatch",
    )(fill_start, _per_tile(dest8, tile), u2_rows)


def _moe_kernel(be_ref, nxt_ref, par_ref, full_ref, nb_ref, x_ref, bg_ref, bu_ref, bd_ref, wgu_hbm, wd_hbm, y_ref,
                wgu_buf, wd_buf, wg_s, wu_s, wd_s, sem_gu, sem_d):
    i = pl.program_id(0)
    new_expert = (i == 0) | (be_ref[i] != be_ref[jnp.maximum(i - 1, 0)])

    def fetch(e, s):
        return (pltpu.make_async_copy(wgu_hbm.at[e], wgu_buf.at[s], sem_gu.at[s]),
                pltpu.make_async_copy(wd_hbm.at[e], wd_buf.at[s], sem_d.at[s]))

    @pl.when(new_expert & (i < nb_ref[0]))
    def _():
        slot = par_ref[i]

        @pl.when(i == 0)
        def _():
            for cp in fetch(be_ref[0], slot):
                cp.start()

        @pl.when(nxt_ref[i] >= 0)
        def _():
            for cp in fetch(nxt_ref[i], 1 - slot):
                cp.start()

        for cp in fetch(be_ref[i], slot):
            cp.wait()
        src = lax.broadcasted_iota(jnp.int32, (MXU_COLS, MXU_COLS), 0)
        col = lax.broadcasted_iota(jnp.int32, (MXU_COLS, MXU_COLS), 1)
        half = MXU_COLS // 2
        want = jnp.where(col < half, 2 * col, 2 * (col - half) + 1)
        unzip = (src == want).astype(BF16)
        for g in range(wgu_buf.shape[2] // MXU_COLS):
            blk = wgu_buf[slot, :, g * MXU_COLS:(g + 1) * MXU_COLS].astype(BF16)
            sep = _dot(blk, unzip)
            wg_s[:, g * half:(g + 1) * half] = sep[:, :half].astype(BF16)
            wu_s[:, g * half:(g + 1) * half] = sep[:, half:].astype(BF16)
        wd_s[...] = wd_buf[slot].astype(BF16)

    def expert_mlp(n_rows):
        xb = _load_rows(x_ref, n_rows).astype(BF16)
        gate = jnp.minimum(_dot(xb, wg_s[...]) + bg_ref[0], SWIGLU_LIMIT)
        up = jnp.clip(_dot(xb, wu_s[...]) + bu_ref[0], -SWIGLU_LIMIT, SWIGLU_LIMIT)
        glu = gate * _sigmoid(gate * SWIGLU_ALPHA)
        _store_rows(y_ref, _dot(((up + 1.0) * glu).astype(BF16), wd_s[...]) + bd_ref[0])

    active = i < nb_ref[0]
    both_halves = full_ref[i] == 1

    @pl.when(active & both_halves)
    def _():
        expert_mlp(EXPERT_BLOCK)

    @pl.when(active & jnp.logical_not(both_halves))
    def _():
        expert_mlp(EXPERT_HALF)
        y_ref[EXPERT_HALF * ROW_SUB:, :] = jnp.zeros((EXPERT_HALF * ROW_SUB, LANES), F32)

    @pl.when(jnp.logical_not(active))
    def _():
        y_ref[...] = jnp.zeros_like(y_ref)


def _moe(block_expert, next_expert, slot_parity, both_halves, n_active, x_rows, w_gate_up, bg, bu, w_down, bd):
    n_blocks = block_expert.shape[0]
    _, d, f2 = w_gate_up.shape
    f = f2 // 2
    blk8 = EXPERT_BLOCK * ROW_SUB
    rows_in = pl.BlockSpec((blk8, LANES),
                           lambda i, be, nx, pr, fl, nb: (jnp.maximum(jnp.minimum(i, nb[0] - 1), 0), 0))
    rows_out = pl.BlockSpec((blk8, LANES), lambda i, be, nx, pr, fl, nb: (i, 0))
    per_expert = lambda shape: pl.BlockSpec((1,) + shape, lambda i, be, nx, pr, fl, nb: (be[i], 0, 0))
    hbm = pl.BlockSpec(memory_space=pl.ANY)
    grid_spec = pltpu.PrefetchScalarGridSpec(
        num_scalar_prefetch=5,
        grid=(n_blocks,),
        in_specs=[rows_in, per_expert((1, f)), per_expert((1, f)), per_expert((1, d)), hbm, hbm],
        out_specs=rows_out,
        scratch_shapes=[pltpu.VMEM((2, d, f2), F32), pltpu.VMEM((2, f, d), F32),
                        pltpu.VMEM((d, f), BF16), pltpu.VMEM((d, f), BF16), pltpu.VMEM((f, d), BF16),
                        pltpu.SemaphoreType.DMA((2,)), pltpu.SemaphoreType.DMA((2,))],
    )
    return pl.pallas_call(
        _moe_kernel,
        grid_spec=grid_spec,
        out_shape=jax.ShapeDtypeStruct((n_blocks * blk8, LANES), F32),
        compiler_params=pltpu.CompilerParams(
            dimension_semantics=("arbitrary",), vmem_limit_bytes=VMEM_LIMIT),
        name="moe",
    )(block_expert, next_expert, slot_parity, both_halves, n_active, x_rows, bg, bu, bd, w_gate_up, w_down)


def _combine_kernel(dest_cur_ref, dest_one_ref, dest_nxt_ref, x1_ref, mod_ref, route_ref, ln_g_ref, ln_b_ref, y_hbm,
                    o_ref, ybuf, sem):
    i = pl.program_id(0)
    n_steps = pl.num_programs(0)
    tile = x1_ref.shape[0]
    slot = lax.rem(i, 3)
    ahead = lax.rem(i + 2, 3)

    def gather(dest_ref, s):
        def body(r, carry):
            for kk in range(TOP_K):
                pltpu.make_async_copy(_row_at(y_hbm, dest_ref[0, 0, kk * tile + r]),
                                      _row_at(ybuf.at[s, kk], r * ROW_SUB), sem.at[s]).start(priority=kk % 2)
            return carry
        lax.fori_loop(0, tile, body, 0, unroll=DMA_UNROLL)

    @pl.when(i == 0)
    def _():
        gather(dest_cur_ref, 0)
        gather(dest_one_ref, 1)

    def wait_slot(s):
        for kk in range(TOP_K):
            pltpu.make_async_copy(y_hbm.at[pl.ds(0, tile * ROW_SUB)], ybuf.at[s, kk], sem.at[s]).wait()

    wait_slot(slot)

    g_f = mod_ref[0][5:6]
    weight = route_ref[...]
    y = jnp.zeros(x1_ref.shape, F32)
    for kk in range(TOP_K):
        y = y + weight[:, kk:kk + 1] * _load_rows(ybuf, tile, (slot, kk))
    o_ref[...] = _layer_norm(DEEPNORM_ALPHA * x1_ref[...] + g_f * y, ln_g_ref[...], ln_b_ref[...])

    for r in range(tile):
        for kk in range(TOP_K):
            pltpu.make_async_copy(_row_at(y_hbm, dest_nxt_ref[0, 0, kk * tile + r]),
                                  _row_at(ybuf.at[ahead, kk], r * ROW_SUB),
                                  sem.at[ahead]).start(priority=kk % 2)

    @pl.when(i == n_steps - 1)
    def _():
        wait_slot(lax.rem(i + 1, 3))
        wait_slot(ahead)


def _combine(dest8, x1, mod, route, ln_g, ln_b, y_rows, seq):
    n_tok, d = x1.shape
    tile = COMBINE_TILE
    n_t = n_tok // tile
    per_seq = seq // tile
    dest3 = _per_tile(dest8, tile)
    smem_blk = lambda fn: pl.BlockSpec((1, 1, tile * TOP_K), fn, memory_space=pltpu.SMEM)
    row = lambda width: pl.BlockSpec((tile, width), lambda i: (i, 0))
    vec = pl.BlockSpec((1, d), lambda i: (0, 0))
    return pl.pallas_call(
        _combine_kernel,
        grid=(n_t,),
        in_specs=[smem_blk(lambda i: (i, 0, 0)), smem_blk(lambda i: (jnp.minimum(i + 1, n_t - 1), 0, 0)),
                  smem_blk(lambda i: (jnp.minimum(i + 2, n_t - 1), 0, 0)),
                  row(d), pl.BlockSpec((1, 6, d), lambda i: (i // per_seq, 0, 0)), row(TOP_K), vec, vec,
                  pl.BlockSpec(memory_space=pl.ANY)],
        out_specs=row(d),
        out_shape=jax.ShapeDtypeStruct((n_tok, d), F32),
        scratch_shapes=[pltpu.VMEM((3, TOP_K, tile * ROW_SUB, LANES), F32), pltpu.SemaphoreType.DMA((3,))],
        compiler_params=pltpu.CompilerParams(
            dimension_semantics=("arbitrary",), vmem_limit_bytes=VMEM_LIMIT),
        name="combine",
    )(dest3, dest3, dest3, x1, mod, route, ln_g, ln_b, y_rows)


def _split_kernel(w_ref, main_ref, rank_ref, gates_ref):
    rank_lo = _OFF[5]
    rank_hi = rank_lo + GLA_GATE_RANK
    cols = w_ref.shape[2]
    main_ref[...] = w_ref[0, 0:rank_lo, :].T.astype(BF16)
    pad = jnp.zeros((RANK_PAD - GLA_GATE_RANK, cols), F32)
    rank_ref[...] = jnp.concatenate([w_ref[0, rank_lo:rank_hi, :], pad], axis=0).T.astype(BF16)
    gates_ref[...] = w_ref[0, rank_hi:rank_hi + 2 * D_MODEL, :].T.astype(BF16)


def _split_in_proj(w_in, layer):
    w_t = jnp.swapaxes(w_in, 1, 2)
    _, _, d = w_t.shape
    cols = MXU_COLS
    piece = lambda width: pl.BlockSpec((cols, width), lambda i: (i, 0))
    return pl.pallas_call(
        _split_kernel,
        grid=(d // cols,),
        in_specs=[pl.BlockSpec((1, w_t.shape[1], cols), lambda i: (layer, 0, i))],
        out_specs=[piece(_OFF[5]), piece(RANK_PAD), piece(2 * D_MODEL)],
        out_shape=[jax.ShapeDtypeStruct((d, _OFF[5]), BF16), jax.ShapeDtypeStruct((d, RANK_PAD), BF16),
                   jax.ShapeDtypeStruct((d, 2 * D_MODEL), BF16)],
        name="split_in_proj",
    )(w_t)


def _layer(x, c, w_ada, b_ada, w_in, w_pool_group, pool_scale, w_branch_a, w_alpha_up, b_alpha,
           gla_norm_gain, w_branch_b, w_out, ln1_gain, ln1_bias, w_router, b_router,
           w_gate_up, b_gate_up, w_down, b_down, ln2_gain, ln2_bias):
    bsz, seq, d = x.shape
    n_tok = bsz * seq
    n_assign = n_tok * TOP_K
    row2 = lambda v: v.reshape(1, -1)

    mod = _ada(c, w_ada, b_ada).reshape(bsz, 6, d)

    w_in_p, w_rank_p, w_gates = w_in
    w_al_p = jnp.concatenate(
        [w_alpha_up, jnp.zeros((RANK_PAD - GLA_GATE_RANK, GLA_KEY_DIM), w_alpha_up.dtype)], axis=0).astype(BF16)
    w_r_t = w_router.T
    w_r_hi = w_r_t.astype(BF16)
    w_r_split = jnp.concatenate([w_r_hi, (w_r_t - w_r_hi.astype(F32)).astype(BF16)], axis=0)
    mixer_args = (x, mod, w_in_p, w_rank_p, w_gates, w_pool_group.astype(BF16), row2(pool_scale),
                  w_branch_a.astype(BF16), w_al_p, row2(b_alpha), row2(gla_norm_gain), w_branch_b.astype(BF16),
                  w_out.astype(BF16), row2(ln1_gain), row2(ln1_bias), w_r_split, b_router.reshape(N_EXPERTS, 1))
    *mixed, sub_decay = _mixer(False, *mixer_args)

    def routed_half(x1, u2_rows, route_t, counts):
        return _routed_moe(x1, u2_rows, route_t, counts, mod, w_gate_up, b_gate_up, w_down, b_down, ln2_gain,
                           ln2_bias)

    return lax.cond(sub_decay[0, 0] > SAFE_SUB_DECAY,
                    lambda: routed_half(*_mixer(True, *mixer_args)[:4]),
                    lambda: routed_half(*mixed))


def _routed_moe(x1, u2_rows, route_t, counts, mod, w_gate_up, b_gate_up, w_down, b_down, ln2_gain, ln2_bias):
    bsz, seq, d = x1.shape
    n_tok = bsz * seq
    n_assign = n_tok * TOP_K
    row2 = lambda v: v.reshape(1, -1)

    top_idx = route_t[0:TOP_K].astype(jnp.int32)
    rank = route_t[TOP_K:2 * TOP_K].astype(jnp.int32)
    route = route_t[2 * TOP_K:3 * TOP_K].T
    counts = counts[:, 0].astype(jnp.int32)
    padded = (counts + EXPERT_BLOCK - 1) // EXPERT_BLOCK * EXPERT_BLOCK
    pad_end = jnp.cumsum(padded)
    pad_start = pad_end - padded
    n_rows = (n_assign + N_EXPERTS * (EXPERT_BLOCK - 1) + EXPERT_BLOCK - 1) // EXPERT_BLOCK * EXPERT_BLOCK
    n_blocks = n_rows // EXPERT_BLOCK
    n_active = (pad_end[-1] // EXPERT_BLOCK).astype(jnp.int32)
    start_of = jnp.sum(jnp.where(top_idx[..., None] == jnp.arange(N_EXPERTS, dtype=jnp.int32), pad_start, 0),
                       axis=-1)
    dest8 = (start_of + rank) * ROW_SUB
    blk_row = jnp.arange(n_blocks, dtype=jnp.int32)[:, None] * EXPERT_BLOCK
    block_expert = jnp.minimum(jnp.sum(pad_end[None, :] <= blk_row, axis=1), N_EXPERTS - 1).astype(jnp.int32)

    fill_table = jnp.concatenate([pad_start + counts, padded - counts, n_active.reshape(1)]).astype(jnp.int32)
    x_rows = _dispatch(fill_table, dest8, u2_rows, n_rows)
    f = w_down.shape[1]
    bg = b_gate_up[:, 0::2].reshape(N_EXPERTS, 1, f)
    bu = b_gate_up[:, 1::2].reshape(N_EXPERTS, 1, f)
    eid = jnp.arange(N_EXPERTS, dtype=jnp.int32)
    owns = counts > 0
    later = jnp.where((eid[None, :] > eid[:, None]) & owns[None, :], eid[None, :], N_EXPERTS)
    next_of = jnp.min(later, axis=1)
    next_of = jnp.where(next_of == N_EXPERTS, -1, next_of).astype(jnp.int32)
    parity_of = ((jnp.cumsum(owns.astype(jnp.int32)) - 1) % 2).astype(jnp.int32)
    is_block_expert = block_expert[:, None] == eid[None, :]
    per_block = lambda v: jnp.sum(jnp.where(is_block_expert, v[None, :], 0), axis=1).astype(jnp.int32)
    left = per_block(pad_start + counts) - blk_row[:, 0]
    both_halves = (left > EXPERT_HALF).astype(jnp.int32)
    y_rows = _moe(block_expert, per_block(next_of), per_block(parity_of), both_halves, n_active.reshape(1),
                  x_rows, w_gate_up, bg, bu, w_down, b_down.reshape(N_EXPERTS, 1, d))
    out = _combine(dest8, x1.reshape(n_tok, d), mod, route, row2(ln2_gain), row2(ln2_bias), y_rows, seq)
    return out.reshape(bsz, seq, d)


def kernel(x, c, w_ada, b_ada, w_in, w_pool_group, pool_scale, w_branch_a, w_alpha_up, b_alpha, gla_norm_gain,
           w_branch_b, w_out, ln1_gain, ln1_bias, w_router, b_router, w_gate_up, b_gate_up, w_down, b_down,
           ln2_gain, ln2_bias):
    for l in range(DEPTH):
        x = _layer(x, c, w_ada[l], b_ada[l], _split_in_proj(w_in, l), w_pool_group[l], pool_scale[l], w_branch_a[l],
                   w_alpha_up[l], b_alpha[l], gla_norm_gain[l], w_branch_b[l], w_out[l], ln1_gain[l],
                   ln1_bias[l], w_router[l], b_router[l], w_gate_up[l], b_gate_up[l], w_down[l], b_down[l],
                   ln2_gain[l], ln2_bias[l])
    return x
```
